```python
import math
import jax, jax.numpy as jnp
from jax import lax
import numpy as np

D_MODEL = 1024
BATCH = 2
SEQ = 8192
DEPTH = 2
DEC_BATCH = 32
DEC_SEQ = 2048
PAST_LEN = 128

HEAD_DIM = 64
N_HEADS_A = 8
N_HEADS_B = 8
N_HEADS_C = 8
N_HEADS_D = 8
WIDTH_A = N_HEADS_A * HEAD_DIM
WIDTH_B = N_HEADS_B * HEAD_DIM
WIDTH_C = N_HEADS_C * HEAD_DIM
WIDTH_D = N_HEADS_D * HEAD_DIM
DILATED_BRANCHES = ((128, 1), (512, 4), (2048, 16))
RWKV_DECAY_RANK = 64
RWKV_ICL_RANK = 64
RWKV_GATE_RANK = 128
SSD_STATE = 128
SSD_GROUPS = 2
SSD_CONV = 5
SSD_XBC = WIDTH_D + 2 * SSD_GROUPS * SSD_STATE
CHUNK = 128
D_FF = 4 * D_MODEL
N_EVEN = (DEPTH + 1) // 2
N_ODD = DEPTH // 2
RWKV_COLS = 3 * WIDTH_B + 2 * RWKV_DECAY_RANK + 2 * RWKV_ICL_RANK + RWKV_GATE_RANK
EVEN_IN = 3 * WIDTH_A + RWKV_COLS
ODD_IN = 4 * WIDTH_C + WIDTH_D + SSD_XBC + 2 * N_HEADS_D
MIX_OUT_EVEN = WIDTH_A + WIDTH_B
MIX_OUT_ODD = WIDTH_C + WIDTH_D
NORM_EPS = 1e-6
GN_EPS = 1e-5
RWKV_GN_EPS = 64e-5
ROPE_BASE = 10000.0

kernel_name = 'hybrid_bidir_encoder_dilated_rwkv7_retnet_ssd'


def split_cols(u, sizes):
    return jnp.split(u, [int(c) for c in np.cumsum(sizes)[:-1]], axis=-1)


def rms_norm(x, g):
    xf = x.astype(jnp.float32)
    return xf * lax.rsqrt(jnp.mean(xf * xf, -1, keepdims=True) + NORM_EPS) * g


def head_norm(y, gain, bias, eps):
    mu = jnp.mean(y, -1, keepdims=True)
    var = jnp.mean(jnp.square(y - mu), -1, keepdims=True)
    yn = (y - mu) * lax.rsqrt(var + eps)
    return yn.reshape(*y.shape[:-2], -1) * gain + bias


def alibi_slopes(n_heads):
    return jnp.exp2(-8.0 * (jnp.arange(n_heads, dtype=jnp.float32) + 1.0) / n_heads)


def rotary(u):
    t = u.shape[1]
    half = HEAD_DIM // 2
    inv = ROPE_BASE ** (-jnp.arange(half, dtype=jnp.float32) / half)
    ang = jnp.arange(t, dtype=jnp.float32)[:, None] * inv[None, :]
    cos = jnp.cos(ang)[None, :, None, :]
    sin = jnp.sin(ang)[None, :, None, :]
    u1, u2 = u[..., :half], u[..., half:]
    return jnp.concatenate([u1 * cos - u2 * sin, u1 * sin + u2 * cos], -1)


def centred_shift(u, mu_prev, mu_next):
    prev = jnp.pad(u, ((0, 0), (1, 0), (0, 0)))[:, :-1]
    nxt = jnp.pad(u, ((0, 0), (0, 1), (0, 0)))[:, 1:]
    return u + mu_prev * (prev - u) + mu_next * (nxt - u)


def centred_dwconv(u, w, bias):
    c = u.shape[-1]
    pad = w.shape[0] // 2
    out = lax.conv_general_dilated(u, w[:, None, :].astype(u.dtype), window_strides=(1,),
                                   padding=[(pad, pad)], dimension_numbers=('NWC', 'WIO', 'NWC'),
                                   feature_group_count=c)
    return out + bias


def dilated_branch(q, k, v, slopes, window, dilation):
    b, t, h, dh = q.shape
    r = window // (2 * dilation)
    L = t // dilation
    nb = -(-L // r)
    lp = nb * r

    def to_sub(u):
        return u.reshape(b, L, dilation, h, dh).transpose(0, 2, 1, 3, 4)

    qs = jnp.pad(to_sub(q), ((0, 0), (0, 0), (0, lp - L), (0, 0), (0, 0))).reshape(b, dilation, nb, r, h, dh)
    kpad = ((0, 0), (0, 0), (r, lp - L + r), (0, 0), (0, 0))
    ks = jnp.pad(to_sub(k), kpad).reshape(b, dilation, nb + 2, r, h, dh)
    vs = jnp.pad(to_sub(v), kpad).reshape(b, dilation, nb + 2, r, h, dh)
    kb = jnp.concatenate([ks[:, :, :-2], ks[:, :, 1:-1], ks[:, :, 2:]], axis=3)
    vb = jnp.concatenate([vs[:, :, :-2], vs[:, :, 1:-1], vs[:, :, 2:]], axis=3)
    s = jnp.einsum('bdnqhe,bdnkhe->bdnhqk', qs, kb)
    qpos = jnp.arange(nb)[:, None] * r + jnp.arange(r)[None, :]
    kpos = jnp.arange(nb)[:, None] * r - r + jnp.arange(3 * r)[None, :]
    rel = kpos[:, None, :] - qpos[:, :, None]
    valid = (jnp.abs(rel) <= r) & (kpos[:, None, :] >= 0) & (kpos[:, None, :] < L)
    dist = (jnp.abs(rel) * dilation).astype(jnp.float32)
    alibi = -slopes[None, :, None, None] * dist[:, None]
    s = jnp.where(valid[None, None, :, None], s + alibi[None, None], -jnp.inf)
    m = jnp.max(s, -1, keepdims=True)
    p = jnp.exp(s - m)
    den = jnp.sum(p, -1, keepdims=True)
    o = jnp.einsum('bdnhqk,bdnkhe->bdnqhe', p / den, vb)
    lse = (m + jnp.log(den))[..., 0]
    o = o.reshape(b, dilation, lp, h, dh)[:, :, :L].transpose(0, 2, 1, 3, 4).reshape(b, t, h, dh)
    lse = lse.transpose(0, 1, 2, 4, 3).reshape(b, dilation, lp, h)[:, :, :L]
    lse = lse.transpose(0, 2, 1, 3).reshape(b, t, h)
    return o, lse


def dilated_attention(q, k, v, q_gain, k_gain):
    b, t, _ = q.shape
    hs = lambda z: z.reshape(b, t, N_HEADS_A, HEAD_DIM)
    qh = rms_norm(hs(q), q_gain) * (HEAD_DIM ** -0.5)
    kh = rms_norm(hs(k), k_gain)
    vh = hs(v)
    slopes = alibi_slopes(N_HEADS_A)
    outs, lses = [], []
    for window, dilation in DILATED_BRANCHES:
        o, lse = dilated_branch(qh, kh, vh, slopes, window, dilation)
        outs.append(o)
        lses.append(lse)
    wts = jax.nn.softmax(jnp.stack(lses, 0), axis=0)
    o = jnp.einsum('nbth,nbthe->bthe', wts, jnp.stack(outs, 0))
    return o.reshape(b, t, WIDTH_A)


def rwkv7_scan(r, w, k, v, kk, a):
    b, t, h, n = r.shape

    def step(S, inp):
        r_t, w_t, k_t, v_t, kk_t, a_t = inp
        sa = jnp.einsum('bhvk,bhk->bhv', S, -kk_t)
        S = S * w_t[:, :, None, :] + sa[..., None] * (kk_t * a_t)[:, :, None, :] + v_t[..., None] * k_t[:, :, None, :]
        return S, jnp.einsum('bhvk,bhk->bhv', S, r_t)

    xs = tuple(u.transpose(1, 0, 2, 3) for u in (r, w, k, v, kk, a))
    _, y = lax.scan(step, jnp.zeros((b, h, n, n), jnp.float32), xs)
    return y.transpose(1, 0, 2, 3)


def rwkv7_mixer(cols, mu_prev, mu_next, w0, w2, a0, a2, g2, k_k, k_a, r_k, ln_g, ln_b):
    b, t, _ = cols.shape
    cols = centred_shift(cols, mu_prev, mu_next)
    r, k, v, zw, za, zg = split_cols(cols, (WIDTH_B, WIDTH_B, WIDTH_B, 2 * RWKV_DECAY_RANK, 2 * RWKV_ICL_RANK, RWKV_GATE_RANK))
    hs = lambda z: z.reshape(b, t, N_HEADS_B, HEAD_DIM)
    kk = hs(k * k_k)
    kk = kk / jnp.maximum(jnp.sqrt(jnp.sum(kk * kk, -1, keepdims=True)), 1e-12)
    g = jnp.einsum('btr,re->bte', jax.nn.sigmoid(zg), g2)
    zw = zw.reshape(b, t, 2, RWKV_DECAY_RANK)
    za = za.reshape(b, t, 2, RWKV_ICL_RANK)
    flip = lambda z: jnp.flip(z, 1)
    ys = []
    for d in range(2):
        w_log = -jax.nn.softplus(-(w0[d] + jnp.einsum('btr,re->bte', jnp.tanh(zw[:, :, d]), w2[d]))) - 0.5
        decay = jnp.exp(-jnp.exp(w_log))
        a = jax.nn.sigmoid(a0[d] + jnp.einsum('btr,re->bte', za[:, :, d], a2[d]))
        k_rep = k * (1.0 + (a - 1.0) * k_a)
        args = (hs(r), hs(decay), hs(k_rep), hs(v), kk, hs(a))
        if d == 0:
            ys.append(rwkv7_scan(*args))
        else:
            ys.append(flip(rwkv7_scan(*[flip(z) for z in args])))
    y = head_norm(ys[0] + ys[1], ln_g, ln_b, RWKV_GN_EPS)
    bonus = jnp.sum(hs(r) * hs(k) * r_k, -1, keepdims=True) * hs(v)
    return (y + bonus.reshape(b, t, WIDTH_B)) * g


def chunked_decay_attention(q, k, v, log_a):
    b, t, h, dk = q.shape
    dv = v.shape[-1]
    nc = t // CHUNK
    qc = q.reshape(b, nc, CHUNK, h, dk)
    kc = k.reshape(b, nc, CHUNK, h, dk)
    vc = v.reshape(b, nc, CHUNK, h, dv)
    cum = jnp.cumsum(log_a.reshape(b, nc, CHUNK, h), axis=2)
    causal = jnp.tril(jnp.ones((CHUNK, CHUNK), bool))
    seg = cum.transpose(0, 1, 3, 2)
    diff = seg[..., :, None] - seg[..., None, :]
    decay = jnp.where(causal, jnp.exp(jnp.where(causal, diff, 0.0)), 0.0)
    s = jnp.einsum('bclhk,bcshk->bchls', qc, kc) * decay
    y_intra = jnp.einsum('bchls,bcshv->bclhv', s, vc)
    w_end = jnp.exp(cum[:, :, -1:] - cum)
    states = jnp.einsum('bcshk,bcsh,bcshv->bchkv', kc, w_end, vc)
    total = jnp.exp(cum[:, :, -1])

    def step(hprev, inp):
        st, tot = inp
        return hprev * tot[..., None, None] + st, hprev

    _, hprevs = lax.scan(step, jnp.zeros((b, h, dk, dv), jnp.float32),
                         (states.transpose(1, 0, 2, 3, 4), total.transpose(1, 0, 2)))
    hprevs = hprevs.transpose(1, 0, 2, 3, 4)
    y_inter = jnp.einsum('bclhk,bclh,bchkv->bclhv', qc, jnp.exp(cum), hprevs)
    return (y_intra + y_inter).reshape(b, t, h, dv)


def bidirectional_decay_attention(q, k, v_f, la_f, v_b, la_b):
    flip = lambda z: jnp.flip(z, 1)
    y_f = chunked_decay_attention(q, k, v_f, la_f)
    y_b = flip(chunked_decay_attention(flip(q), flip(k), flip(v_b), flip(la_b)))
    return y_f + y_b


def retention_mixer(q, k, v, g, decay_exp, gn_g, gn_b):
    b, t, _ = q.shape
    hs = lambda z: z.reshape(b, t, N_HEADS_C, HEAD_DIM)
    qh = rotary(hs(q))
    kh = rotary(hs(k)) * (HEAD_DIM ** -0.5)
    vh = hs(v)
    log_gamma = jnp.log1p(-jnp.exp2(-decay_exp.astype(jnp.float32)))
    la_f = jnp.broadcast_to(log_gamma[0], (b, t, N_HEADS_C))
    la_b = jnp.broadcast_to(log_gamma[1], (b, t, N_HEADS_C))
    y = bidirectional_decay_attention(qh, kh, vh, la_f, vh, la_b)
    return head_norm(y, gn_g, gn_b, GN_EPS) * jax.nn.silu(g)


def ssd_mixer(z, xbc, dt, conv_w, conv_b, dt_bias, a_log, d_skip, norm_g):
    b, t, _ = z.shape
    xbc = jax.nn.silu(centred_dwconv(xbc, conv_w, conv_b))
    xs, bm, cm = split_cols(xbc, (WIDTH_D, SSD_GROUPS * SSD_STATE, SSD_GROUPS * SSD_STATE))
    rep = N_HEADS_D // SSD_GROUPS
    xh = xs.reshape(b, t, N_HEADS_D, HEAD_DIM)
    bh = jnp.repeat(bm.reshape(b, t, SSD_GROUPS, SSD_STATE), rep, axis=2)
    ch = jnp.repeat(cm.reshape(b, t, SSD_GROUPS, SSD_STATE), rep, axis=2)
    dts = jax.nn.softplus(dt.reshape(b, t, 2, N_HEADS_D) + dt_bias)
    la = dts * (-jnp.exp(a_log.astype(jnp.float32)))
    y = bidirectional_decay_attention(ch, bh, xh * dts[:, :, 0, :, None], la[:, :, 0],
                                      xh * dts[:, :, 1, :, None], la[:, :, 1])
    y = (y + d_skip[:, None] * xh).reshape(b, t, WIDTH_D) * jax.nn.silu(z)
    yg = y.reshape(b, t, SSD_GROUPS, -1)
    yg = yg * lax.rsqrt(jnp.mean(yg * yg, -1, keepdims=True) + NORM_EPS)
    return yg.reshape(b, t, WIDTH_D) * norm_g


def even_mixer(u, p, i):
    proj = jnp.einsum('btd,de->bte', u, p['even_in_w'][i]).astype(jnp.float32)
    qa, ka, va, rcols = split_cols(proj, (WIDTH_A, WIDTH_A, WIDTH_A, RWKV_COLS))
    y_a = dilated_attention(qa, ka, va, p['attn_q_gain'][i], p['attn_k_gain'][i])
    y_b = rwkv7_mixer(rcols, p['rwkv_mu_prev'][i], p['rwkv_mu_next'][i], p['rwkv_w0'][i], p['rwkv_w2'][i],
                      p['rwkv_a0'][i], p['rwkv_a2'][i], p['rwkv_g2'][i], p['rwkv_k_k'][i], p['rwkv_k_a'][i],
                      p['rwkv_r_k'][i], p['rwkv_ln_g'][i], p['rwkv_ln_b'][i])
    y = jnp.concatenate([y_a, y_b], -1)
    return jnp.einsum('bte,ed->btd', y, p['even_out_w'][i])


def odd_mixer(u, p, i):
    proj = jnp.einsum('btd,de->bte', u, p['odd_in_w'][i]).astype(jnp.float32)
    qc, kc, vc, gc, z, xbc, dt = split_cols(proj, (WIDTH_C, WIDTH_C, WIDTH_C, WIDTH_C, WIDTH_D, SSD_XBC, 2 * N_HEADS_D))
    y_c = retention_mixer(qc, kc, vc, gc, p['ret_decay_exp'][i], p['ret_gn_g'][i], p['ret_gn_b'][i])
    y_d = ssd_mixer(z, xbc, dt, p['ssd_conv_w'][i], p['ssd_conv_b'][i], p['ssd_dt_bias'][i],
                    p['ssd_a_log'][i], p['ssd_d'][i], p['ssd_norm_g'][i])
    y = jnp.concatenate([y_c, y_d], -1)
    return jnp.einsum('bte,ed->btd', y, p['odd_out_w'][i])


def squared_relu_mlp(u, w1, w2):
    hdn = jnp.square(jax.nn.relu(jnp.einsum('btd,df->btf', u, w1)))
    return jnp.einsum('btf,fd->btd', hdn, w2)


def trunk(x, p):
    h = x
    for layer in range(DEPTH):
        u = rms_norm(h, p['norm_mix'][layer])
        i = layer // 2
        if layer % 2 == 0:
            mix = even_mixer(u, p, i)
        else:
            mix = odd_mixer(u, p, i)
        h = h + mix
        u = rms_norm(h, p['norm_mlp'][layer])
        h = h + squared_relu_mlp(u, p['mlp_w1'][layer], p['mlp_w2'][layer])
    return h.astype(x.dtype)


def setup_inputs(seed: int = 0) -> dict:
    key = jax.random.key(seed)
    ks = jax.random.split(key, 48)
    it = iter(range(48))
    f32 = jnp.float32

    def nrm(shape, scale):
        return scale * jax.random.normal(ks[next(it)], shape, f32)

    def gain(shape):
        return 1.0 + 0.02 * jax.random.normal(ks[next(it)], shape, f32)

    def unif(shape, lo, hi):
        return jax.random.uniform(ks[next(it)], shape, f32, lo, hi)

    dt0 = jnp.exp(unif((N_ODD, 2, N_HEADS_D), math.log(1e-3), math.log(1e-1)))
    return {
        'x_prompt': nrm((BATCH, SEQ, D_MODEL), 1.0),
        'x_sample': nrm((DEC_BATCH, DEC_SEQ, D_MODEL), 1.0),
        'norm_mix': gain((DEPTH, D_MODEL)),
        'norm_mlp': gain((DEPTH, D_MODEL)),
        'mlp_w1': nrm((DEPTH, D_MODEL, D_FF), D_MODEL ** -0.5),
        'mlp_w2': nrm((DEPTH, D_FF, D_MODEL), 0.5 * D_FF ** -0.5),
        'even_in_w': nrm((N_EVEN, D_MODEL, EVEN_IN), D_MODEL ** -0.5),
        'even_out_w': nrm((N_EVEN, MIX_OUT_EVEN, D_MODEL), 0.5 * MIX_OUT_EVEN ** -0.5),
        'attn_q_gain': gain((N_EVEN, HEAD_DIM)),
        'attn_k_gain': gain((N_EVEN, HEAD_DIM)),
        'rwkv_mu_prev': unif((N_EVEN, RWKV_COLS), 0.0, 0.5),
        'rwkv_mu_next': unif((N_EVEN, RWKV_COLS), 0.0, 0.5),
        'rwkv_w0': unif((N_EVEN, 2, WIDTH_B), -6.0, -1.0),
        'rwkv_w2': nrm((N_EVEN, 2, RWKV_DECAY_RANK, WIDTH_B), 0.1 * RWKV_DECAY_RANK ** -0.5),
        'rwkv_a0': nrm((N_EVEN, 2, WIDTH_B), 0.1),
        'rwkv_a2': nrm((N_EVEN, 2, RWKV_ICL_RANK, WIDTH_B), 0.5 * RWKV_ICL_RANK ** -0.5),
        'rwkv_g2': nrm((N_EVEN, RWKV_GATE_RANK, WIDTH_B), RWKV_GATE_RANK ** -0.5),
        'rwkv_k_k': 0.85 + nrm((N_EVEN, WIDTH_B), 0.05),
        'rwkv_k_a': 1.0 + nrm((N_EVEN, WIDTH_B), 0.05),
        'rwkv_r_k': nrm((N_EVEN, N_HEADS_B, HEAD_DIM), 0.1),
        'rwkv_ln_g': gain((N_EVEN, WIDTH_B)),
        'rwkv_ln_b': nrm((N_EVEN, WIDTH_B), 0.01),
        'odd_in_w': nrm((N_ODD, D_MODEL, ODD_IN), D_MODEL ** -0.5),
        'odd_out_w': nrm((N_ODD, MIX_OUT_ODD, D_MODEL), 0.5 * MIX_OUT_ODD ** -0.5),
        'ret_decay_exp': 5.0 + jnp.arange(N_HEADS_C, dtype=f32)[None, None, :] + nrm((N_ODD, 2, N_HEADS_C), 0.1),
        'ret_gn_g': gain((N_ODD, WIDTH_C)),
        'ret_gn_b': nrm((N_ODD, WIDTH_C), 0.01),
        'ssd_conv_w': nrm((N_ODD, SSD_CONV, SSD_XBC), SSD_CONV ** -0.5),
        'ssd_conv_b': nrm((N_ODD, SSD_XBC), 0.01),
        'ssd_dt_bias': dt0 + jnp.log(-jnp.expm1(-dt0)),
        'ssd_a_log': jnp.log(unif((N_ODD, 2, N_HEADS_D), 1.0, 16.0)),
        'ssd_d': 1.0 + nrm((N_ODD, N_HEADS_D), 0.1),
        'ssd_norm_g': gain((N_ODD, WIDTH_D)),
    }


def reference(x_prompt, x_sample, norm_mix, norm_mlp, mlp_w1, mlp_w2, even_in_w, even_out_w,
              attn_q_gain, attn_k_gain, rwkv_mu_prev, rwkv_mu_next, rwkv_w0, rwkv_w2, rwkv_a0, rwkv_a2,
              rwkv_g2, rwkv_k_k, rwkv_k_a, rwkv_r_k, rwkv_ln_g, rwkv_ln_b, odd_in_w, odd_out_w,
              ret_decay_exp, ret_gn_g, ret_gn_b, ssd_conv_w, ssd_conv_b, ssd_dt_bias, ssd_a_log, ssd_d,
              ssd_norm_g):
    p = dict(norm_mix=norm_mix, norm_mlp=norm_mlp, mlp_w1=mlp_w1, mlp_w2=mlp_w2,
             even_in_w=even_in_w, even_out_w=even_out_w, attn_q_gain=attn_q_gain, attn_k_gain=attn_k_gain,
             rwkv_mu_prev=rwkv_mu_prev, rwkv_mu_next=rwkv_mu_next, rwkv_w0=rwkv_w0, rwkv_w2=rwkv_w2,
             rwkv_a0=rwkv_a0, rwkv_a2=rwkv_a2, rwkv_g2=rwkv_g2, rwkv_k_k=rwkv_k_k, rwkv_k_a=rwkv_k_a,
             rwkv_r_k=rwkv_r_k, rwkv_ln_g=rwkv_ln_g, rwkv_ln_b=rwkv_ln_b,
             odd_in_w=odd_in_w, odd_out_w=odd_out_w, ret_decay_exp=ret_decay_exp, ret_gn_g=ret_gn_g,
             ret_gn_b=ret_gn_b, ssd_conv_w=ssd_conv_w, ssd_conv_b=ssd_conv_b, ssd_dt_bias=ssd_dt_bias,
             ssd_a_log=ssd_a_log, ssd_d=ssd_d, ssd_norm_g=ssd_norm_g)
    y_prompt = trunk(x_prompt, p)
    y_sample = trunk(x_sample, p)
    return (y_prompt, y_sample)
```

```python
import functools
import math

import numpy as np
import jax
import jax.numpy as jnp
from jax import lax
from jax.experimental import pallas as pl
from jax.experimental.pallas import tpu as pltpu

F32 = jnp.float32
BF16 = jnp.bfloat16

D_MODEL = 1024
HEAD_DIM = 64
N_HEADS = 8
WIDTH = N_HEADS * HEAD_DIM
N_PAIRS = N_HEADS // 2
LANES = 128
DILATED_BRANCHES = ((128, 1), (512, 4), (2048, 16))
ATT_R = 64
RWKV_DECAY_RANK = 64
RWKV_ICL_RANK = 64
RWKV_GATE_RANK = 128
RWKV_COLS = 3 * WIDTH + 2 * RWKV_DECAY_RANK + 2 * RWKV_ICL_RANK + RWKV_GATE_RANK
SSD_STATE = 128
SSD_GROUPS = 2
SSD_CONV = 5
SSD_XBC = WIDTH + 2 * SSD_GROUPS * SSD_STATE
SSD_COLS = 13 * LANES
D_FF = 4 * D_MODEL
NORM_EPS = 1e-6
GN_EPS = 1e-5
RWKV_GN_EPS = 64e-5
ROPE_BASE = 10000.0
RWKV_CHUNK = 64
LIN_CHUNK = 128
NEG_BIG = -1e30
VMEM_LIMIT = 56 * 1024 * 1024


def _cparams(*sem):
    return pltpu.CompilerParams(dimension_semantics=sem, vmem_limit_bytes=VMEM_LIMIT)


def _iota(shape, dim):
    return lax.broadcasted_iota(jnp.int32, shape, dim)


def _lane_lo(shape):
    return (_iota(shape, len(shape) - 1) & HEAD_DIM) == 0


def _bd(x):
    lo = _lane_lo(x.shape)
    return jnp.concatenate([jnp.where(lo, x, 0.0), jnp.where(lo, 0.0, x)], axis=0)


def _mm(a, b):
    return jnp.dot(a.astype(BF16), b.astype(BF16), preferred_element_type=F32)


def _mm_nt(a, b):
    return lax.dot_general(a.astype(BF16), b.astype(BF16), (((1,), (1,)), ((), ())),
                           preferred_element_type=F32)


def _split3(x):
    x1 = x.astype(BF16)
    r1 = x - x1.astype(F32)
    x2 = r1.astype(BF16)
    x3 = (r1 - x2.astype(F32)).astype(BF16)
    return x1, x2, x3


def _mm_exact_left(t, x):
    x1, x2, x3 = _split3(x)
    d = lambda b: jnp.dot(t, b, preferred_element_type=F32)
    return d(x1) + d(x2) + d(x3)


def _mm_exact_right(x, t):
    x1, x2, x3 = _split3(x)
    d = lambda a: jnp.dot(a, t, preferred_element_type=F32)
    return d(x1) + d(x2) + d(x3)


def _group_sum(x, ones_bd):
    hi = x.astype(BF16)
    lo = (x - hi.astype(F32)).astype(BF16)
    return (jnp.dot(hi, ones_bd, preferred_element_type=F32)
            + jnp.dot(lo, ones_bd, preferred_element_type=F32))


def _rms_rows(x, g):
    return x * lax.rsqrt(jnp.mean(x * x, -1, keepdims=True) + NORM_EPS) * g


def _sigmoid(x):
    return 1.0 / (1.0 + jnp.exp(-x))


def _silu(x):
    return x * _sigmoid(x)


def _softplus(x):
    return jnp.maximum(x, 0.0) + jnp.log1p(jnp.exp(-jnp.abs(x)))


def _even_in_kernel(x_ref, g_ref, wa_ref, wr_ref, ones_ref, qg_ref, kg_ref,
                    q_out, k_out, v_out, r_out):
    ub = _rms_rows(x_ref[...], g_ref[...]).astype(BF16)
    a = jnp.dot(ub, wa_ref[...], preferred_element_type=F32)
    q = a[:, :WIDTH]
    k = a[:, WIDTH:2 * WIDTH]
    ones = ones_ref[...]
    inv = 1.0 / HEAD_DIM
    qn = q * lax.rsqrt(_group_sum(q * q, ones) * inv + NORM_EPS) * qg_ref[...]
    kn = k * lax.rsqrt(_group_sum(k * k, ones) * inv + NORM_EPS) * kg_ref[...]
    q_out[...] = (qn * (HEAD_DIM ** -0.5)).astype(BF16)
    k_out[...] = kn.astype(BF16)
    v_out[...] = a[:, 2 * WIDTH:].astype(BF16)
    r_out[...] = jnp.dot(ub, wr_ref[...], preferred_element_type=F32)


def _even_in(x2, g, wa, wr, ones_bd, qg, kg, tm=512):
    m = x2.shape[0]
    const = lambda shape: pl.BlockSpec(shape, lambda i: (0, 0))
    row = lambda n: pl.BlockSpec((tm, n), lambda i: (i, 0))
    return pl.pallas_call(
        _even_in_kernel,
        grid=(m // tm,),
        in_specs=[row(D_MODEL), const((1, D_MODEL)), const((D_MODEL, 3 * WIDTH)),
                  const((D_MODEL, RWKV_COLS)), const((WIDTH, WIDTH)),
                  const((1, WIDTH)), const((1, WIDTH))],
        out_specs=[row(WIDTH), row(WIDTH), row(WIDTH), row(RWKV_COLS)],
        out_shape=[jax.ShapeDtypeStruct((m, WIDTH), BF16)] * 3
        + [jax.ShapeDtypeStruct((m, RWKV_COLS), F32)],
        compiler_params=_cparams("parallel"),
        name="even_in",
    )(x2, g, wa, wr, ones_bd, qg, kg)


def _dil_attn_kernel(q_ref, k_ref, kl_ref, kr_ref, v_ref, vl_ref, vr_ref, o_ref, lse_ref,
                     kext, vext, *, tq, sub_len, dil):
    r = ATT_R
    i = pl.program_id(2)
    kext[0:r] = kl_ref[0]
    kext[r:r + tq] = k_ref[0]
    kext[r + tq:] = kr_ref[0]
    vext[0:r] = vl_ref[0]
    vext[r:r + tq] = v_ref[0]
    vext[r + tq:] = vr_ref[0]

    c_io = _iota((r, 3 * r), 1)
    q_io = _iota((r, 3 * r), 0)
    rel = c_io - r - q_io
    near = jnp.abs(rel) <= r
    dist = jnp.abs(rel).astype(F32) * float(dil)
    lo = _lane_lo((r, LANES))

    def body(j, carry):
        row0 = pl.multiple_of(j * r, r)
        kpos = i * tq + j * r + c_io - r
        valid = near & (kpos >= 0) & (kpos < sub_len)
        for p in range(N_PAIRS):
            cols = slice(p * LANES, (p + 1) * LANES)
            qp = q_ref[0, pl.ds(row0, r), cols]
            kp = kext[pl.ds(row0, 3 * r), cols]
            vp = vext[pl.ds(row0, 3 * r), cols]
            outs, lses = [], []
            for hh in range(2):
                slope = 2.0 ** (-(2 * p + hh + 1))
                qm = jnp.where(lo if hh == 0 else jnp.logical_not(lo), qp, jnp.zeros_like(qp))
                s = lax.dot_general(qm, kp, (((1,), (1,)), ((), ())), preferred_element_type=F32)
                s = jnp.where(valid, s - slope * dist, NEG_BIG)
                m = jnp.max(s, -1, keepdims=True)
                e = jnp.exp(s - m)
                den = jnp.sum(e, -1, keepdims=True)
                outs.append(jnp.dot(e.astype(BF16), vp, preferred_element_type=F32) / den)
                lses.append(jnp.broadcast_to(m + jnp.log(den), (r, LANES)))
            o_ref[0, pl.ds(row0, r), cols] = jnp.where(lo, outs[0], outs[1])
            lse_ref[0, pl.ds(row0, r), cols] = jnp.where(lo, lses[0], lses[1])
        return carry

    lax.fori_loop(0, tq // r, body, 0)


def _dil_attn(qn, kn, vb, dil):
    b, t, _ = qn.shape
    sub_len = t // dil
    tq = min(sub_len, 256)
    nblk = sub_len // ATT_R
    per = tq // ATT_R
    view = lambda u: u.reshape(b, sub_len, dil * WIDTH)
    main = pl.BlockSpec((1, tq, WIDTH), lambda bb, rr, i: (bb, i, rr))
    left = pl.BlockSpec((1, ATT_R, WIDTH), lambda bb, rr, i: (bb, jnp.maximum(i * per - 1, 0), rr))
    right = pl.BlockSpec((1, ATT_R, WIDTH),
                         lambda bb, rr, i: (bb, jnp.minimum((i + 1) * per, nblk - 1), rr))
    o, lse = pl.pallas_call(
        functools.partial(_dil_attn_kernel, tq=tq, sub_len=sub_len, dil=dil),
        grid=(b, dil, sub_len // tq),
        in_specs=[main, main, left, right, main, left, right],
        out_specs=[main, main],
        out_shape=[jax.ShapeDtypeStruct((b, sub_len, dil * WIDTH), F32)] * 2,
        scratch_shapes=[pltpu.VMEM((tq + 2 * ATT_R, WIDTH), BF16)] * 2,
        compiler_params=_cparams("parallel", "parallel", "parallel"),
        name=f"dil_attn_d{dil}",
    )(view(qn), view(kn), view(kn), view(kn), view(vb), view(vb), view(vb))
    return o.reshape(b, t, WIDTH), lse.reshape(b, t, WIDTH)


def _rwkv_prep_kernel(x_ref, xp_ref, xn_ref, mup_ref, mun_ref, w0_ref, w2_ref, a0_ref, a2_ref,
                      g2_ref, kk_ref, ka_ref, rk_ref, ones_ref,
                      r_out, v_out, kkn_out, g_out, bonus_out, lw_out, krep_out, b_out, *, tm):
    i = pl.program_id(1)
    nt = pl.num_programs(1)
    x = x_ref[0]
    prow = jnp.where(i > 0, xp_ref[0, 7:8, :], 0.0)
    nrow = jnp.where(i < nt - 1, xn_ref[0, 0:1, :], 0.0)
    rid = _iota((tm, 1), 0)
    prev = jnp.where(rid == 0, prow, pltpu.roll(x, 1, 0))
    nxt = jnp.where(rid == tm - 1, nrow, pltpu.roll(x, tm - 1, 0))
    xs = x + mup_ref[...] * (prev - x) + mun_ref[...] * (nxt - x)
    r = xs[:, 0:WIDTH]
    k = xs[:, WIDTH:2 * WIDTH]
    v = xs[:, 2 * WIDTH:3 * WIDTH]
    zw = xs[:, 3 * WIDTH:3 * WIDTH + LANES]
    za = xs[:, 3 * WIDTH + LANES:3 * WIDTH + 2 * LANES]
    zg = xs[:, 3 * WIDTH + 2 * LANES:]
    ones = ones_ref[...]
    kk = k * kk_ref[...]
    kkn = kk / jnp.maximum(jnp.sqrt(_group_sum(kk * kk, ones)), 1e-12)
    g_out[0] = _mm(_sigmoid(zg), g2_ref[...])
    r_out[0] = r
    v_out[0] = v
    kkn_out[0] = kkn
    bonus_out[0] = _group_sum(r * k * rk_ref[...], ones) * v
    wx = w0_ref[...] + _mm(jnp.tanh(zw), w2_ref[...])
    ax = a0_ref[...] + _mm(za, a2_ref[...])
    for d in range(2):
        cols = slice(d * WIDTH, (d + 1) * WIDTH)
        lw_out[d, 0] = -math.exp(-0.5) * _sigmoid(wx[:, cols])
        a = _sigmoid(ax[:, cols])
        krep_out[d, 0] = k * (1.0 + (a - 1.0) * ka_ref[...])
        b_out[d, 0] = kkn * a


def _rwkv_prep(rcols, p, tm=256):
    b, t, _ = rcols.shape
    nt = t // tm
    per = tm // 8
    const = lambda shape: pl.BlockSpec(shape, lambda bb, i: (0,) * len(shape))
    main = lambda n: pl.BlockSpec((1, tm, n), lambda bb, i: (bb, i, 0))
    dirs = pl.BlockSpec((2, 1, tm, WIDTH), lambda bb, i: (0, bb, i, 0))
    prev = pl.BlockSpec((1, 8, RWKV_COLS), lambda bb, i: (bb, jnp.maximum(i * per - 1, 0), 0))
    nxt = pl.BlockSpec((1, 8, RWKV_COLS),
                       lambda bb, i: (bb, jnp.minimum((i + 1) * per, t // 8 - 1), 0))
    one = jax.ShapeDtypeStruct((b, t, WIDTH), F32)
    two = jax.ShapeDtypeStruct((2, b, t, WIDTH), F32)
    return pl.pallas_call(
        functools.partial(_rwkv_prep_kernel, tm=tm),
        grid=(b, nt),
        in_specs=[main(RWKV_COLS), prev, nxt, const((1, RWKV_COLS)), const((1, RWKV_COLS)),
                  const((1, 2 * WIDTH)), const((LANES, 2 * WIDTH)),
                  const((1, 2 * WIDTH)), const((LANES, 2 * WIDTH)),
                  const((LANES, WIDTH)), const((1, WIDTH)), const((1, WIDTH)), const((1, WIDTH)),
                  const((WIDTH, WIDTH))],
        out_specs=[main(WIDTH)] * 5 + [dirs] * 3,
        out_shape=[one] * 5 + [two] * 3,
        compiler_params=_cparams("parallel", "parallel"),
        name="rwkv_prep",
    )(rcols, rcols, rcols, p["mu_prev"], p["mu_next"], p["w0"], p["w2"], p["a0"], p["a2"],
      p["g2"], p["k_k"], p["k_a"], p["r_k"], p["ones_bd"])


def _rwkv_scan_kernel(r_ref, v_ref, kk_ref, lw_ref, k_ref, b_ref, tri_ref, y_ref, s_ref,
                      *, nc, reverse):
    c_len = RWKV_CHUNK

    @pl.when(pl.program_id(2) == 0)
    def _():
        s_ref[...] = jnp.zeros_like(s_ref)

    tri = tri_ref[...]
    row = _iota((c_len, LANES), 0)
    li = _iota((c_len, LANES), 1) & (HEAD_DIM - 1)
    strict = (li > row) if reverse else (li < row)
    incl = (li >= row) if reverse else (li <= row)
    r2 = _iota((LANES, LANES), 0)
    c2 = _iota((LANES, LANES), 1)
    same_head = (r2 & HEAD_DIM) == (c2 & HEAD_DIM)
    diag = r2 == c2
    zeros = jnp.zeros((c_len, LANES), F32)

    s = s_ref[...]
    order = range(nc - 1, -1, -1) if reverse else range(nc)
    for c in order:
        rows = slice(c * c_len, (c + 1) * c_len)
        r = r_ref[0, rows, :]
        v = v_ref[0, rows, :]
        kk = kk_ref[0, rows, :]
        lw = lw_ref[0, 0, rows, :]
        k = k_ref[0, 0, rows, :]
        b = b_ref[0, 0, rows, :]

        g_in = _mm_exact_left(tri, lw)
        g_ex = g_in - lw
        g_tot = g_in[0:1] if reverse else g_in[c_len - 1:c_len]
        at = -kk * jnp.exp(g_ex)
        rt = r * jnp.exp(g_in)
        inv = jnp.exp(-g_in)
        to_end = jnp.exp(g_tot - g_in)

        w = _mm_nt(jnp.concatenate([at, rt], 0),
                   jnp.concatenate([_bd(b * inv), _bd(k * inv)], 0))
        m_ab = jnp.where(strict, w[:c_len, :LANES], 0.0)
        m_ak = jnp.where(strict, w[:c_len, LANES:], 0.0)
        m_rb = jnp.where(incl, w[c_len:, :LANES], 0.0)
        m_rk = jnp.where(incl, w[c_len:, LANES:], 0.0)

        bdv = _bd(v).astype(BF16)
        x = jnp.concatenate([at, _mm(m_ak, bdv)], axis=1)
        mj = m_ab
        for j in range(6):
            mjb = mj.astype(BF16)
            x = x + _mm(mjb, _bd(x))
            if j < 5:
                mj = _mm(mjb, _bd(mj))

        t1 = _mm(m_rb, _bd(x))
        rp = rt + t1[:, :LANES]
        y0 = t1[:, LANES:] + _mm(m_rk, bdv)

        bk_t = jnp.concatenate([b * to_end, k * to_end], 0).T
        t2 = _mm(bk_t, jnp.concatenate([x, jnp.concatenate([zeros, v], axis=1)], 0))
        g_low = jnp.where(same_head, t2[:, :LANES], 0.0)
        h_new = jnp.where(same_head, t2[:, LANES:], 0.0)
        g_col = jnp.sum(jnp.where(diag, jnp.broadcast_to(g_tot, (LANES, LANES)), 0.0),
                        axis=1, keepdims=True)

        sb = s.astype(BF16)
        y_ref[0, rows, :] = _mm(rp, sb) + y0
        s = s * jnp.exp(g_col) + _mm(g_low, sb) + h_new
    s_ref[...] = s


def _rwkv_scan(r, v, kkn, lw, krep, bvec, tri, direction, tb=256):
    b, t, _ = r.shape
    nt = t // tb
    reverse = direction == 1
    tmap = (lambda i: nt - 1 - i) if reverse else (lambda i: i)
    one = pl.BlockSpec((1, tb, LANES), lambda bb, p, i: (bb, tmap(i), p))
    two = pl.BlockSpec((1, 1, tb, LANES), lambda bb, p, i: (direction, bb, tmap(i), p))
    return pl.pallas_call(
        functools.partial(_rwkv_scan_kernel, nc=tb // RWKV_CHUNK, reverse=reverse),
        grid=(b, N_PAIRS, nt),
        in_specs=[one, one, one, two, two, two,
                  pl.BlockSpec((RWKV_CHUNK, RWKV_CHUNK), lambda bb, p, i: (0, 0))],
        out_specs=one,
        out_shape=jax.ShapeDtypeStruct((b, t, WIDTH), F32),
        scratch_shapes=[pltpu.VMEM((LANES, LANES), F32)],
        compiler_params=_cparams("parallel", "parallel", "arbitrary"),
        name=f"rwkv_scan_{'bwd' if reverse else 'fwd'}",
    )(r, v, kkn, lw, krep, bvec, tri)


def _even_out_kernel(h_ref, o1, o2, o3, l1, l2, l3, yf, yb, bonus, g, lng, lnb, ones_ref, w_ref,
                     out_ref):
    la, lb, lc = l1[...], l2[...], l3[...]
    mx = jnp.maximum(jnp.maximum(la, lb), lc)
    wa, wb, wc = jnp.exp(la - mx), jnp.exp(lb - mx), jnp.exp(lc - mx)
    y_a = (wa * o1[...] + wb * o2[...] + wc * o3[...]) / (wa + wb + wc)
    ones = ones_ref[...]
    inv = 1.0 / HEAD_DIM
    y = yf[...] + yb[...]
    yc = y - _group_sum(y, ones) * inv
    var = _group_sum(yc * yc, ones) * inv
    yn = yc * lax.rsqrt(var + RWKV_GN_EPS) * lng[...] + lnb[...]
    y_b = (yn + bonus[...]) * g[...]
    out_ref[...] = (h_ref[...] + _mm(y_a, w_ref[0:WIDTH, :]) + _mm(y_b, w_ref[WIDTH:, :]))


def _even_out(h2, parts, lng, lnb, ones_bd, w, tm=256):
    m = h2.shape[0]
    const = lambda shape: pl.BlockSpec(shape, lambda i: (0, 0))
    row = lambda n: pl.BlockSpec((tm, n), lambda i: (i, 0))
    return pl.pallas_call(
        _even_out_kernel,
        grid=(m // tm,),
        in_specs=[row(D_MODEL)] + [row(WIDTH)] * 10
        + [const((1, WIDTH)), const((1, WIDTH)), const((WIDTH, WIDTH)), const((D_MODEL, D_MODEL))],
        out_specs=row(D_MODEL),
        out_shape=jax.ShapeDtypeStruct((m, D_MODEL), F32),
        compiler_params=_cparams("parallel"),
        name="even_out",
    )(h2, *parts, lng, lnb, ones_bd, w)


def _mlp_kernel(h_ref, g_ref, w1_ref, w2_ref, out_ref, u_ref):
    f = pl.program_id(1)

    @pl.when(f == 0)
    def _():
        x = h_ref[...]
        u_ref[...] = _rms_rows(x, g_ref[...]).astype(BF16)
        out_ref[...] = x

    hdn = jnp.dot(u_ref[...], w1_ref[...], preferred_element_type=F32)
    hdn = jnp.square(jnp.maximum(hdn, 0.0))
    out_ref[...] += jnp.dot(hdn.astype(BF16), w2_ref[...], preferred_element_type=F32)


def _mlp(h2, g, w1, w2, tm=1024, tf=512):
    m = h2.shape[0]
    return pl.pallas_call(
        _mlp_kernel,
        grid=(m // tm, D_FF // tf),
        in_specs=[pl.BlockSpec((tm, D_MODEL), lambda i, f: (i, 0)),
                  pl.BlockSpec((1, D_MODEL), lambda i, f: (0, 0)),
                  pl.BlockSpec((D_MODEL, tf), lambda i, f: (0, f)),
                  pl.BlockSpec((tf, D_MODEL), lambda i, f: (f, 0))],
        out_specs=pl.BlockSpec((tm, D_MODEL), lambda i, f: (i, 0)),
        out_shape=jax.ShapeDtypeStruct((m, D_MODEL), F32),
        scratch_shapes=[pltpu.VMEM((tm, D_MODEL), BF16)],
        compiler_params=_cparams("parallel", "arbitrary"),
        name="mlp",
    )(h2, g, w1, w2)


def _odd_in_kernel(x_ref, g_ref, wr_ref, ws_ref, ret_out, ssd_out):
    ub = _rms_rows(x_ref[...], g_ref[...]).astype(BF16)
    ret_out[...] = jnp.dot(ub, wr_ref[...], preferred_element_type=F32)
    ssd_out[...] = jnp.dot(ub, ws_ref[...], preferred_element_type=F32)


def _odd_in(x2, g, wr, ws, tm=512):
    m = x2.shape[0]
    const = lambda shape: pl.BlockSpec(shape, lambda i: (0, 0))
    row = lambda n: pl.BlockSpec((tm, n), lambda i: (i, 0))
    return pl.pallas_call(
        _odd_in_kernel,
        grid=(m // tm,),
        in_specs=[row(D_MODEL), const((1, D_MODEL)), const((D_MODEL, 4 * WIDTH)),
                  const((D_MODEL, SSD_COLS))],
        out_specs=[row(4 * WIDTH), row(SSD_COLS)],
        out_shape=[jax.ShapeDtypeStruct((m, 4 * WIDTH), F32),
                   jax.ShapeDtypeStruct((m, SSD_COLS), F32)],
        compiler_params=_cparams("parallel"),
        name="odd_in",
    )(x2, g, wr, ws)


def _rotary_pair(u, cos, sin_signed):
    first = (_iota(u.shape, 1) & (HEAD_DIM - 1)) < HEAD_DIM // 2
    swapped = jnp.where(first, pltpu.roll(u, LANES - HEAD_DIM // 2, 1),
                        pltpu.roll(u, HEAD_DIM // 2, 1))
    return u * cos + swapped * sin_signed


def _ret_kernel(q_ref, k_ref, v_ref, cos_ref, sin_ref, lg64_ref, lg128_ref, y_ref, s_ref,
                *, nc, reverse):
    c_len = LIN_CHUNK

    @pl.when(pl.program_id(2) == 0)
    def _():
        s_ref[...] = jnp.zeros_like(s_ref)

    lg64 = lg64_ref[0]
    lg128 = lg128_ref[0]
    l_io = _iota((c_len, 2 * c_len), 0)
    s_io = _iota((c_len, 2 * c_len), 1) & (c_len - 1)
    delta = (s_io - l_io) if reverse else (l_io - s_io)
    causal = delta >= 0
    decay = jnp.where(causal, jnp.exp(lg128 * jnp.where(causal, delta, 0).astype(F32)), 0.0)
    pos = _iota((c_len, LANES), 0)
    if reverse:
        pos = c_len - 1 - pos
    pos = pos.astype(F32)
    from_start = jnp.exp(lg64 * (pos + 1.0))
    to_end = jnp.exp(lg64 * (float(c_len - 1) - pos))
    whole = jnp.exp(lg64 * float(c_len))
    r2 = _iota((LANES, LANES), 0)
    c2 = _iota((LANES, LANES), 1)
    same_head = (r2 & HEAD_DIM) == (c2 & HEAD_DIM)

    s = s_ref[...]
    order = range(nc - 1, -1, -1) if reverse else range(nc)
    for c in order:
        rows = slice(c * c_len, (c + 1) * c_len)
        cos = cos_ref[rows, :]
        sin = sin_ref[rows, :]
        q = _rotary_pair(q_ref[0, rows, :], cos, sin)
        k = _rotary_pair(k_ref[0, rows, :], cos, sin) * (HEAD_DIM ** -0.5)
        v = v_ref[0, rows, :]
        sc = _mm_nt(q, _bd(k)) * decay
        sb = s.astype(BF16)
        y_ref[0, rows, :] = _mm(sc, _bd(v)) + _mm(q * from_start, sb)
        upd = _mm((k * to_end).T, v)
        s = s * whole + jnp.where(same_head, upd, 0.0)
    s_ref[...] = s


def _retention(ret, cos, sin, lg64, lg128, direction, tb=512):
    b, t, _ = ret.shape
    nt = t // tb
    reverse = direction == 1
    tmap = (lambda i: nt - 1 - i) if reverse else (lambda i: i)
    col = lambda off: pl.BlockSpec((1, tb, LANES), lambda bb, p, i: (bb, tmap(i), off + p))
    tab = pl.BlockSpec((tb, LANES), lambda bb, p, i: (tmap(i), 0))
    return pl.pallas_call(
        functools.partial(_ret_kernel, nc=tb // LIN_CHUNK, reverse=reverse),
        grid=(b, N_PAIRS, nt),
        in_specs=[col(0), col(N_PAIRS), col(2 * N_PAIRS), tab, tab,
                  pl.BlockSpec((1, 1, LANES), lambda bb, p, i: (direction, 0, p)),
                  pl.BlockSpec((1, 1, 2 * LANES), lambda bb, p, i: (direction, 0, p))],
        out_specs=col(0),
        out_shape=jax.ShapeDtypeStruct((b, t, WIDTH), F32),
        scratch_shapes=[pltpu.VMEM((LANES, LANES), F32)],
        compiler_params=_cparams("parallel", "parallel", "arbitrary"),
        name=f"retention_{'bwd' if reverse else 'fwd'}",
    )(ret, ret, ret, cos, sin, lg64, lg128)


def _ssd_conv_kernel(x_ref, xp_ref, xn_ref, w_ref, b_ref, xs_out, bc_out, *, tm):
    i = pl.program_id(1)
    nt = pl.num_programs(1)
    prev = jnp.where(i > 0, xp_ref[0], 0.0)
    nxt = jnp.where(i < nt - 1, xn_ref[0], 0.0)
    xe = jnp.concatenate([prev, x_ref[0], nxt], axis=0)
    ext = tm + 16
    pad = SSD_CONV // 2
    acc = jnp.zeros((tm, SSD_XBC), F32) + b_ref[...]
    for j in range(SSD_CONV):
        shift = (pad - j) % ext
        xj = xe if shift == 0 else pltpu.roll(xe, shift, 0)
        acc = acc + xj[8:8 + tm] * w_ref[j:j + 1, :]
    y = _silu(acc)
    xs_out[0] = y[:, :WIDTH]
    bc_out[0] = y[:, WIDTH:].astype(BF16)


def _ssd_conv(ssd, w, bias, tm=256):
    b, t, _ = ssd.shape
    nt = t // tm
    per = tm // 8
    const = lambda shape: pl.BlockSpec(shape, lambda bb, i: (0, 0))
    main = pl.BlockSpec((1, tm, SSD_XBC), lambda bb, i: (bb, i, 0))
    prev = pl.BlockSpec((1, 8, SSD_XBC), lambda bb, i: (bb, jnp.maximum(i * per - 1, 0), 0))
    nxt = pl.BlockSpec((1, 8, SSD_XBC),
                       lambda bb, i: (bb, jnp.minimum((i + 1) * per, t // 8 - 1), 0))
    out = pl.BlockSpec((1, tm, WIDTH), lambda bb, i: (bb, i, 0))
    return pl.pallas_call(
        functools.partial(_ssd_conv_kernel, tm=tm),
        grid=(b, nt),
        in_specs=[main, prev, nxt, const((8, SSD_XBC)), const((1, SSD_XBC))],
        out_specs=[out, out],
        out_shape=[jax.ShapeDtypeStruct((b, t, WIDTH), F32),
                   jax.ShapeDtypeStruct((b, t, WIDTH), BF16)],
        compiler_params=_cparams("parallel", "parallel"),
        name="ssd_conv",
    )(ssd, ssd, ssd, w, bias)


def _ssd_kernel(xs_ref, bm_ref, cm_ref, dtc_ref, dtr_ref, bc_ref, br_ref, ac_ref, ar_ref,
                tri_ref, trit_ref, y_ref, s_ref, *, nc, reverse):
    c_len = LIN_CHUNK
    per_group = N_HEADS // SSD_GROUPS

    @pl.when(pl.program_id(2) == 0)
    def _():
        s_ref[...] = jnp.zeros_like(s_ref)

    tri = tri_ref[...]
    trit = trit_ref[...]
    dts_c = _softplus(dtc_ref[0, 0, 0] + bc_ref[0, 0])
    la_c = dts_c * ac_ref[0, 0]
    dts_r = _softplus(dtr_ref[0, 0, 0] + br_ref[0, 0])
    la_r = dts_r * ar_ref[0, 0]
    l_io = _iota((c_len, c_len), 0)
    s_io = _iota((c_len, c_len), 1)
    causal = (s_io >= l_io) if reverse else (s_io <= l_io)
    lo = _lane_lo((c_len, LANES))

    def by_head128(cols):
        return jnp.concatenate([jnp.broadcast_to(cols[:, h:h + 1], (c_len, LANES))
                                for h in range(per_group)], axis=1)

    def to64(wide):
        h = [wide[:, j * LANES:(j + 1) * LANES] for j in range(per_group)]
        return jnp.concatenate([jnp.where(lo, h[0], h[1]), jnp.where(lo, h[2], h[3])], axis=1)

    s = s_ref[...]
    order = range(nc - 1, -1, -1) if reverse else range(nc)
    for c in order:
        rows = slice(c * c_len, (c + 1) * c_len)
        xs = xs_ref[0, rows, :]
        bm = bm_ref[0, rows, :]
        cm = cm_ref[0, rows, :]
        cum = _mm_exact_left(tri, by_head128(la_c[rows]))
        cum_r = _mm_exact_right(la_r[:, rows], trit)
        last = cum[0:1] if reverse else cum[c_len - 1:c_len]
        dts64 = to64(by_head128(dts_c[rows]))
        xdt = xs * dts64
        gm = _mm_nt(cm, bm)
        parts = []
        for pp in range(per_group // 2):
            wide = []
            for hh in (2 * pp, 2 * pp + 1):
                diff = cum[:, hh * LANES:(hh + 1) * LANES] - cum_r[hh:hh + 1, :]
                dec = jnp.where(causal, jnp.exp(jnp.where(causal, diff, 0.0)), 0.0)
                wide.append((gm * dec).astype(BF16))
            parts.append(_mm(jnp.concatenate(wide, axis=1), _bd(xdt[:, pp * LANES:(pp + 1) * LANES])))
        y_intra = jnp.concatenate(parts, axis=1)
        y_inter = _mm(cm, s) * to64(jnp.exp(cum))
        y_ref[0, rows, :] = y_intra + y_inter
        upd = _mm(bm.astype(F32).T, xdt * to64(jnp.exp(last - cum)))
        s = s * to64(jnp.exp(last)) + upd
    s_ref[...] = s


def _ssd_scan(xs, bc, dtc, dtr, sp, direction, tb=256):
    b, t, _ = xs.shape
    nt = t // tb
    reverse = direction == 1
    tmap = (lambda i: nt - 1 - i) if reverse else (lambda i: i)
    gw = WIDTH // SSD_GROUPS
    par = lambda shape: pl.BlockSpec((1, 1) + shape, lambda bb, g, i: (direction, g, 0, 0))
    return pl.pallas_call(
        functools.partial(_ssd_kernel, nc=tb // LIN_CHUNK, reverse=reverse),
        grid=(b, SSD_GROUPS, nt),
        in_specs=[pl.BlockSpec((1, tb, gw), lambda bb, g, i: (bb, tmap(i), g)),
                  pl.BlockSpec((1, tb, SSD_STATE), lambda bb, g, i: (bb, tmap(i), g)),
                  pl.BlockSpec((1, tb, SSD_STATE), lambda bb, g, i: (bb, tmap(i), SSD_GROUPS + g)),
                  pl.BlockSpec((1, 1, 1, tb, 4), lambda bb, g, i: (direction, g, bb, tmap(i), 0)),
                  pl.BlockSpec((1, 1, 1, 8, tb), lambda bb, g, i: (direction, g, bb, 0, tmap(i))),
                  par((1, 4)), par((8, 1)), par((1, 4)), par((8, 1)),
                  pl.BlockSpec((LIN_CHUNK, LIN_CHUNK), lambda bb, g, i: (0, 0)),
                  pl.BlockSpec((LIN_CHUNK, LIN_CHUNK), lambda bb, g, i: (0, 0))],
        out_specs=pl.BlockSpec((1, tb, gw), lambda bb, g, i: (bb, tmap(i), g)),
        out_shape=jax.ShapeDtypeStruct((b, t, WIDTH), F32),
        scratch_shapes=[pltpu.VMEM((SSD_STATE, gw), F32)],
        compiler_params=_cparams("parallel", "parallel", "arbitrary"),
        name=f"ssd_scan_{'bwd' if reverse else 'fwd'}",
    )(xs, bc, bc, dtc, dtr, sp["bias_c"], sp["bias_r"], sp["nega_c"], sp["nega_r"],
      sp["tri_rev" if reverse else "tri_fwd"], sp["trit_rev" if reverse else "trit_fwd"])


def _odd_out_kernel(h_ref, rf, rb, gate, sf, sb, xs, z, gng, gnb, dsk, ng, ones_ref, w_ref, out_ref):
    ones = ones_ref[...]
    inv = 1.0 / HEAD_DIM
    y = rf[...] + rb[...]
    yc = y - _group_sum(y, ones) * inv
    var = _group_sum(yc * yc, ones) * inv
    y_c = (yc * lax.rsqrt(var + GN_EPS) * gng[...] + gnb[...]) * _silu(gate[...])
    yd = (sf[...] + sb[...] + dsk[...] * xs[...]) * _silu(z[...])
    gw = WIDTH // SSD_GROUPS
    halves = []
    for gi in range(SSD_GROUPS):
        yg = yd[:, gi * gw:(gi + 1) * gw]
        halves.append(yg * lax.rsqrt(jnp.mean(yg * yg, -1, keepdims=True) + NORM_EPS))
    y_d = jnp.concatenate(halves, axis=1) * ng[...]
    out_ref[...] = h_ref[...] + _mm(y_c, w_ref[0:WIDTH, :]) + _mm(y_d, w_ref[WIDTH:, :])


def _odd_out(h2, rf, rb, ret2, sf, sb, xs, ssd2, gng, gnb, dsk, ng, ones_bd, w, tm=256):
    m = h2.shape[0]
    const = lambda shape: pl.BlockSpec(shape, lambda i: (0, 0))
    row = lambda n: pl.BlockSpec((tm, n), lambda i: (i, 0))
    vec = const((1, WIDTH))
    return pl.pallas_call(
        _odd_out_kernel,
        grid=(m // tm,),
        in_specs=[row(D_MODEL), row(WIDTH), row(WIDTH),
                  pl.BlockSpec((tm, WIDTH), lambda i: (i, 3)),
                  row(WIDTH), row(WIDTH), row(WIDTH),
                  pl.BlockSpec((tm, WIDTH), lambda i: (i, 2)),
                  vec, vec, vec, vec, const((WIDTH, WIDTH)), const((D_MODEL, D_MODEL))],
        out_specs=row(D_MODEL),
        out_shape=jax.ShapeDtypeStruct((m, D_MODEL), F32),
        compiler_params=_cparams("parallel"),
        name="odd_out",
    )(h2, rf, rb, ret2, sf, sb, xs, ssd2, gng, gnb, dsk, ng, ones_bd, w)


def _tri(n, reverse):
    i = np.arange(n)
    m = (i[None, :] >= i[:, None]) if reverse else (i[None, :] <= i[:, None])
    return m.astype(np.float32)


def _prepare(norm_mix, norm_mlp, mlp_w1, mlp_w2, even_in_w, even_out_w, attn_q_gain, attn_k_gain,
             rwkv_mu_prev, rwkv_mu_next, rwkv_w0, rwkv_w2, rwkv_a0, rwkv_a2, rwkv_g2, rwkv_k_k,
             rwkv_k_a, rwkv_r_k, rwkv_ln_g, rwkv_ln_b, odd_in_w, odd_out_w, ret_decay_exp, ret_gn_g,
             ret_gn_b, ssd_conv_w, ssd_conv_b, ssd_dt_bias, ssd_a_log, ssd_d, ssd_norm_g):
    row = lambda v: v.reshape(1, -1).astype(F32)
    heads = np.arange(WIDTH) // HEAD_DIM
    ones_bd = jnp.asarray(heads[:, None] == heads[None, :], BF16)
    p = {"ones_bd": ones_bd, "norm_mix": norm_mix, "norm_mlp": norm_mlp,
         "mlp_w1": mlp_w1.astype(BF16), "mlp_w2": mlp_w2.astype(BF16)}

    w_in = even_in_w[0]
    p["even_wa"] = w_in[:, :3 * WIDTH].astype(BF16)
    p["even_wr"] = w_in[:, 3 * WIDTH:].astype(BF16)
    p["even_out_w"] = even_out_w[0].astype(BF16)
    p["q_gain"] = row(jnp.tile(attn_q_gain[0], N_HEADS))
    p["k_gain"] = row(jnp.tile(attn_k_gain[0], N_HEADS))

    def two_dir_lowrank(w):
        z = jnp.zeros_like(w[0])
        return jnp.concatenate([jnp.concatenate([w[0], z], 1), jnp.concatenate([z, w[1]], 1)], 0)

    p["rwkv"] = {
        "mu_prev": row(rwkv_mu_prev[0]), "mu_next": row(rwkv_mu_next[0]),
        "w0": row(rwkv_w0[0]), "w2": two_dir_lowrank(rwkv_w2[0]).astype(BF16),
        "a0": row(rwkv_a0[0]), "a2": two_dir_lowrank(rwkv_a2[0]).astype(BF16),
        "g2": rwkv_g2[0].astype(BF16), "k_k": row(rwkv_k_k[0]), "k_a": row(rwkv_k_a[0]),
        "r_k": row(rwkv_r_k[0]), "ones_bd": ones_bd,
    }
    p["rwkv_tri"] = [jnp.asarray(_tri(RWKV_CHUNK, False), BF16), jnp.asarray(_tri(RWKV_CHUNK, True), BF16)]
    p["rwkv_ln_g"] = row(rwkv_ln_g[0])
    p["rwkv_ln_b"] = row(rwkv_ln_b[0])

    w_odd = odd_in_w[0]
    p["odd_wr"] = w_odd[:, :4 * WIDTH].astype(BF16)
    z_w = w_odd[:, 4 * WIDTH:5 * WIDTH]
    xbc_w = w_odd[:, 5 * WIDTH:5 * WIDTH + SSD_XBC]
    dt_w = w_odd[:, 5 * WIDTH + SSD_XBC:]
    pad = jnp.zeros((D_MODEL, SSD_COLS - SSD_XBC - WIDTH - 2 * N_HEADS), F32)
    p["odd_ws"] = jnp.concatenate([xbc_w, z_w, dt_w, pad], axis=1).astype(BF16)
    p["odd_out_w"] = odd_out_w[0].astype(BF16)

    log_gamma = jnp.log1p(-jnp.exp2(-ret_decay_exp[0].astype(F32)))
    p["lg64"] = jnp.repeat(log_gamma, HEAD_DIM, axis=1).reshape(2, 1, WIDTH)
    p["lg128"] = jnp.repeat(log_gamma, LIN_CHUNK, axis=1).reshape(2, 1, N_HEADS * LIN_CHUNK)
    p["ret_gn_g"] = row(ret_gn_g[0])
    p["ret_gn_b"] = row(ret_gn_b[0])

    p["conv_w"] = jnp.concatenate([ssd_conv_w[0], jnp.zeros((8 - SSD_CONV, SSD_XBC), F32)], 0)
    p["conv_b"] = row(ssd_conv_b[0])
    per_group = N_HEADS // SSD_GROUPS
    bias = ssd_dt_bias[0].reshape(2, SSD_GROUPS, per_group).astype(F32)
    nega = (-jnp.exp(ssd_a_log[0].astype(F32))).reshape(2, SSD_GROUPS, per_group)
    as_rows = lambda v: jnp.concatenate([v, jnp.zeros_like(v)], -1)[..., None]
    p["ssd"] = {
        "bias_c": bias[:, :, None, :], "nega_c": nega[:, :, None, :],
        "bias_r": as_rows(bias), "nega_r": as_rows(nega),
        "tri_fwd": jnp.asarray(_tri(LIN_CHUNK, False), BF16),
        "tri_rev": jnp.asarray(_tri(LIN_CHUNK, True), BF16),
        "trit_fwd": jnp.asarray(_tri(LIN_CHUNK, False).T, BF16),
        "trit_rev": jnp.asarray(_tri(LIN_CHUNK, True).T, BF16),
    }
    p["ssd_d"] = row(jnp.repeat(ssd_d[0], HEAD_DIM))
    p["ssd_norm_g"] = row(ssd_norm_g[0])
    return p


def _rope_tables(t):
    half = HEAD_DIM // 2
    inv = ROPE_BASE ** (-jnp.arange(half, dtype=F32) / half)
    ang = jnp.arange(t, dtype=F32)[:, None] * inv[None, :]
    cos = jnp.tile(jnp.cos(ang), (1, LANES // half))
    sin = jnp.sin(ang)
    sin_signed = jnp.tile(jnp.concatenate([-sin, sin], axis=1), (1, LANES // HEAD_DIM))
    return cos, sin_signed


def _even_mixer(h2, b, t, p):
    qn, kn, vb, rcols = _even_in(h2, p["norm_mix"][0:1], p["even_wa"], p["even_wr"], p["ones_bd"],
                                 p["q_gain"], p["k_gain"])
    shape3 = lambda u: u.reshape(b, t, -1)
    flat = lambda u: u.reshape(b * t, -1)
    outs, lses = [], []
    for _, dil in DILATED_BRANCHES:
        o, lse = _dil_attn(shape3(qn), shape3(kn), shape3(vb), dil)
        outs.append(flat(o))
        lses.append(flat(lse))
    r, v, kkn, g, bonus, lw, krep, bvec = _rwkv_prep(shape3(rcols), p["rwkv"])
    ys = [_rwkv_scan(r, v, kkn, lw, krep, bvec, p["rwkv_tri"][d], d) for d in range(2)]
    parts = outs + lses + [flat(ys[0]), flat(ys[1]), flat(bonus), flat(g)]
    return _even_out(h2, parts, p["rwkv_ln_g"], p["rwkv_ln_b"], p["ones_bd"], p["even_out_w"])


def _odd_mixer(h2, b, t, p):
    ret2, ssd2 = _odd_in(h2, p["norm_mix"][1:2], p["odd_wr"], p["odd_ws"])
    ret = ret2.reshape(b, t, -1)
    ssd = ssd2.reshape(b, t, -1)
    cos, sin = _rope_tables(t)
    rets = [_retention(ret, cos, sin, p["lg64"], p["lg128"], d) for d in range(2)]
    xs, bc = _ssd_conv(ssd, p["conv_w"], p["conv_b"])
    per_group = N_HEADS // SSD_GROUPS
    dt = ssd[:, :, SSD_XBC + WIDTH:SSD_XBC + WIDTH + 2 * N_HEADS]
    dtc = dt.reshape(b, t, 2, SSD_GROUPS, per_group).transpose(2, 3, 0, 1, 4)
    dtr = dtc.transpose(0, 1, 2, 4, 3)
    dtr = jnp.concatenate([dtr, jnp.zeros_like(dtr)], axis=3)
    ssds = [_ssd_scan(xs, bc, dtc, dtr, p["ssd"], d) for d in range(2)]
    flat = lambda u: u.reshape(b * t, -1)
    return _odd_out(h2, flat(rets[0]), flat(rets[1]), ret2, flat(ssds[0]), flat(ssds[1]), flat(xs),
                    ssd2, p["ret_gn_g"], p["ret_gn_b"], p["ssd_d"], p["ssd_norm_g"], p["ones_bd"],
                    p["odd_out_w"])


def _trunk(x, p):
    b, t, _ = x.shape
    h2 = x.reshape(b * t, D_MODEL)
    h2 = _even_mixer(h2, b, t, p)
    h2 = _mlp(h2, p["norm_mlp"][0:1], p["mlp_w1"][0], p["mlp_w2"][0])
    h2 = _odd_mixer(h2, b, t, p)
    h2 = _mlp(h2, p["norm_mlp"][1:2], p["mlp_w1"][1], p["mlp_w2"][1])
    return h2.reshape(b, t, D_MODEL)


def kernel(x_prompt, x_sample, norm_mix, norm_mlp, mlp_w1, mlp_w2, even_in_w, even_out_w, attn_q_gain, attn_k_gain, rwkv_mu_prev, rwkv_mu_next, rwkv_w0, rwkv_w2, rwkv_a0, rwkv_a2, rwkv_g2, rwkv_k_k, rwkv_k_a, rwkv_r_k, rwkv_ln_g, rwkv_ln_b, odd_in_w, odd_out_w, ret_decay_exp, ret_gn_g, ret_gn_b, ssd_conv_w, ssd_conv_b, ssd_dt_bias, ssd_a_log, ssd_d, ssd_norm_g):
    p = _prepare(norm_mix, norm_mlp, mlp_w1, mlp_w2, even_in_w, even_out_w, attn_q_gain, attn_k_gain,
                 rwkv_mu_prev, rwkv_mu_next, rwkv_w0, rwkv_w2, rwkv_a0, rwkv_a2, rwkv_g2, rwkv_k_k,
                 rwkv_k_a, rwkv_r_k, rwkv_ln_g, rwkv_ln_b, odd_in_w, odd_out_w, ret_decay_exp,
                 ret_gn_g, ret_gn_b, ssd_conv_w, ssd_conv_b, ssd_dt_bias, ssd_a_log, ssd_d, ssd_norm_g)
    return (_trunk(x_prompt, p), _trunk(x_sample, p))
```

```python
import functools
import math

import numpy as np
import jax
import jax.numpy as jnp
from jax import lax
from jax.experimental import pallas as pl
from jax.experimental.pallas import tpu as pltpu

F32 = jnp.float32
BF16 = jnp.bfloat16

D_MODEL = 1024
HEAD_DIM = 64
N_HEADS = 8
WIDTH = N_HEADS * HEAD_DIM
N_PAIRS = N_HEADS // 2
LANES = 128
DILATED_BRANCHES = ((128, 1), (512, 4), (2048, 16))
ATT_R = 64
RWKV_DECAY_RANK = 64
RWKV_ICL_RANK = 64
RWKV_GATE_RANK = 128
RWKV_COLS = 3 * WIDTH + 2 * RWKV_DECAY_RANK + 2 * RWKV_ICL_RANK + RWKV_GATE_RANK
SSD_STATE = 128
SSD_GROUPS = 2
SSD_CONV = 5
SSD_XBC = WIDTH + 2 * SSD_GROUPS * SSD_STATE
SSD_COLS = 13 * LANES
D_FF = 4 * D_MODEL
NORM_EPS = 1e-6
GN_EPS = 1e-5
RWKV_GN_EPS = 64e-5
ROPE_BASE = 10000.0
RWKV_CHUNK = 64
LIN_CHUNK = 128
NEG_BIG = -1e30
VMEM_LIMIT = 56 * 1024 * 1024


def _cparams(*sem):
    return pltpu.CompilerParams(dimension_semantics=sem, vmem_limit_bytes=VMEM_LIMIT)


def _iota(shape, dim):
    return lax.broadcasted_iota(jnp.int32, shape, dim)


def _lane_lo(shape):
    return (_iota(shape, len(shape) - 1) & HEAD_DIM) == 0


def _bd(x):
    lo = _lane_lo(x.shape)
    return jnp.concatenate([jnp.where(lo, x, 0.0), jnp.where(lo, 0.0, x)], axis=0)


def _mm(a, b):
    return jnp.dot(a.astype(BF16), b.astype(BF16), preferred_element_type=F32)


def _mm_nt(a, b):
    return lax.dot_general(a.astype(BF16), b.astype(BF16), (((1,), (1,)), ((), ())),
                           preferred_element_type=F32)


def _split3(x):
    x1 = x.astype(BF16)
    r1 = x - x1.astype(F32)
    x2 = r1.astype(BF16)
    x3 = (r1 - x2.astype(F32)).astype(BF16)
    return x1, x2, x3


def _mm_exact_left(t, x):
    x1, x2, x3 = _split3(x)
    d = lambda b: jnp.dot(t, b, preferred_element_type=F32)
    return d(x1) + d(x2) + d(x3)


def _mm_exact_right(x, t):
    x1, x2, x3 = _split3(x)
    d = lambda a: jnp.dot(a, t, preferred_element_type=F32)
    return d(x1) + d(x2) + d(x3)


def _group_sum(x, ones_bd):
    hi = x.astype(BF16)
    lo = (x - hi.astype(F32)).astype(BF16)
    return (jnp.dot(hi, ones_bd, preferred_element_type=F32)
            + jnp.dot(lo, ones_bd, preferred_element_type=F32))


def _rms_rows(x, g):
    return x * lax.rsqrt(jnp.mean(x * x, -1, keepdims=True) + NORM_EPS) * g


def _sigmoid(x):
    return 1.0 / (1.0 + jnp.exp(-x))


def _silu(x):
    return x * _sigmoid(x)


def _softplus(x):
    return jnp.maximum(x, 0.0) + jnp.log1p(jnp.exp(-jnp.abs(x)))


def _even_in_kernel(x_ref, g_ref, wa_ref, wr_ref, ones_ref, qg_ref, kg_ref,
                    qkv1_out, qkv4_out, qkv16_out, r_out, qkv_sc, *, tm):
    ub = _rms_rows(x_ref[0], g_ref[...]).astype(BF16)
    a = jnp.dot(ub, wa_ref[...], preferred_element_type=F32)
    q = a[:, :WIDTH]
    k = a[:, WIDTH:2 * WIDTH]
    ones = ones_ref[...]
    inv = 1.0 / HEAD_DIM
    qn = q * lax.rsqrt(_group_sum(q * q, ones) * inv + NORM_EPS) * qg_ref[...]
    kn = k * lax.rsqrt(_group_sum(k * k, ones) * inv + NORM_EPS) * kg_ref[...]
    qkv = jnp.concatenate([qn * (HEAD_DIM ** -0.5), kn, a[:, 2 * WIDTH:]], axis=1)
    qkv1_out[0, 0] = qkv.astype(BF16)
    n_tiles = 3 * WIDTH // LANES
    for c in range(n_tiles):
        qkv_sc[c] = qkv[:, c * LANES:(c + 1) * LANES]
    for dil, out in ((4, qkv4_out), (16, qkv16_out)):
        for rho in range(dil):
            for c in range(n_tiles):
                out[0, rho, :, c * LANES:(c + 1) * LANES] = (
                    qkv_sc[c, pl.ds(rho, tm // dil, stride=dil), :].astype(BF16))
    r_out[0] = jnp.dot(ub, wr_ref[...], preferred_element_type=F32)


def _even_in(x, g, wa, wr, ones_bd, qg, kg, tm=512):
    b, t, _ = x.shape
    const = lambda shape: pl.BlockSpec(shape, lambda bb, i: (0, 0))
    row = lambda n: pl.BlockSpec((1, tm, n), lambda bb, i: (bb, i, 0))
    res = lambda d: pl.BlockSpec((1, d, tm // d, 3 * WIDTH), lambda bb, i: (bb, 0, i, 0))
    dils = [d for _, d in DILATED_BRANCHES]
    return pl.pallas_call(
        functools.partial(_even_in_kernel, tm=tm),
        grid=(b, t // tm),
        in_specs=[row(D_MODEL), const((1, D_MODEL)), const((D_MODEL, 3 * WIDTH)),
                  const((D_MODEL, RWKV_COLS)), const((WIDTH, WIDTH)),
                  const((1, WIDTH)), const((1, WIDTH))],
        out_specs=[res(d) for d in dils] + [row(RWKV_COLS)],
        out_shape=[jax.ShapeDtypeStruct((b, d, t // d, 3 * WIDTH), BF16) for d in dils]
        + [jax.ShapeDtypeStruct((b, t, RWKV_COLS), F32)],
        scratch_shapes=[pltpu.VMEM((3 * WIDTH // LANES, tm, LANES), F32)],
        compiler_params=_cparams("parallel", "parallel"),
        name="even_in",
    )(x, g, wa, wr, ones_bd, qg, kg)


def _dil_attn_kernel(q_ref, k_ref, kl_ref, kr_ref, v_ref, vl_ref, vr_ref, o_ref, lse_ref,
                     kext, vext, *, tq, sub_len, dil):
    r = ATT_R
    i = pl.program_id(2)
    kext[0:r] = kl_ref[0, 0]
    kext[r:r + tq] = k_ref[0, 0]
    kext[r + tq:] = kr_ref[0, 0]
    vext[0:r] = vl_ref[0, 0]
    vext[r:r + tq] = v_ref[0, 0]
    vext[r + tq:] = vr_ref[0, 0]

    c_io = _iota((r, 3 * r), 1)
    q_io = _iota((r, 3 * r), 0)
    rel = c_io - r - q_io
    near = jnp.abs(rel) <= r
    dist = jnp.abs(rel).astype(F32) * float(dil)
    lo = _lane_lo((r, LANES))
    hi = jnp.logical_not(lo)

    for j in range(tq // r):
        kpos = i * tq + j * r + c_io - r
        valid = near & (kpos >= 0) & (kpos < sub_len)
        heads = [(p, hh) for p in range(N_PAIRS) for hh in range(2)]
        cols = [slice(p * LANES, (p + 1) * LANES) for p in range(N_PAIRS)]
        kp = [kext[j * r:(j + 3) * r, c] for c in cols]
        vp = [vext[j * r:(j + 3) * r, c] for c in cols]
        s = []
        for p, hh in heads:
            qp = q_ref[0, 0, j * r:(j + 1) * r, cols[p]]
            qm = jnp.where(lo if hh == 0 else hi, qp, jnp.zeros_like(qp))
            sc = lax.dot_general(qm, kp[p], (((1,), (1,)), ((), ())), preferred_element_type=F32)
            s.append(jnp.where(valid, sc - 2.0 ** (-(2 * p + hh + 1)) * dist, NEG_BIG))
        m = [jnp.max(u, -1, keepdims=True) for u in s]
        e = [jnp.exp(u - mm) for u, mm in zip(s, m)]
        den = [jnp.sum(u, -1, keepdims=True) for u in e]
        pv = [jnp.dot(e[n].astype(BF16), vp[p], preferred_element_type=F32)
              for n, (p, hh) in enumerate(heads)]
        for p in range(N_PAIRS):
            a, b = 2 * p, 2 * p + 1
            o_ref[0, 0, j * r:(j + 1) * r, cols[p]] = jnp.where(lo, pv[a] / den[a], pv[b] / den[b])
            lse_ref[0, 0, j * r:(j + 1) * r, cols[p]] = jnp.where(
                lo, jnp.broadcast_to(m[a] + jnp.log(den[a]), (r, LANES)),
                jnp.broadcast_to(m[b] + jnp.log(den[b]), (r, LANES)))


def _dil_attn(qkv, dil):
    b, _, sub_len, _ = qkv.shape
    tq = min(sub_len, 256)
    nblk = sub_len // ATT_R
    per = tq // ATT_R
    main = lambda col: pl.BlockSpec((1, 1, tq, WIDTH), lambda bb, rr, i: (bb, rr, i, col))
    left = lambda col: pl.BlockSpec((1, 1, ATT_R, WIDTH),
                                    lambda bb, rr, i: (bb, rr, jnp.maximum(i * per - 1, 0), col))
    right = lambda col: pl.BlockSpec((1, 1, ATT_R, WIDTH),
                                     lambda bb, rr, i: (bb, rr, jnp.minimum((i + 1) * per, nblk - 1), col))
    return pl.pallas_call(
        functools.partial(_dil_attn_kernel, tq=tq, sub_len=sub_len, dil=dil),
        grid=(b, dil, sub_len // tq),
        in_specs=[main(0), main(1), left(1), right(1), main(2), left(2), right(2)],
        out_specs=[main(0), main(0)],
        out_shape=[jax.ShapeDtypeStruct((b, dil, sub_len, WIDTH), F32)] * 2,
        scratch_shapes=[pltpu.VMEM((tq + 2 * ATT_R, WIDTH), BF16)] * 2,
        compiler_params=_cparams("parallel", "parallel", "parallel"),
        name=f"dil_attn_d{dil}",
    )(qkv, qkv, qkv, qkv, qkv, qkv, qkv)


def _rwkv_prep_kernel(x_ref, xp_ref, xn_ref, mup_ref, mun_ref, w0_ref, w2_ref, a0_ref, a2_ref,
                      g2_ref, kk_ref, ka_ref, rk_ref, ones_ref,
                      r_out, v_out, kkn_out, g_out, bonus_out, lw_out, krep_out, b_out, *, tm):
    i = pl.program_id(1)
    nt = pl.num_programs(1)
    x = x_ref[0]
    prow = jnp.where(i > 0, xp_ref[0, 7:8, :], 0.0)
    nrow = jnp.where(i < nt - 1, xn_ref[0, 0:1, :], 0.0)
    rid = _iota((tm, 1), 0)
    prev = jnp.where(rid == 0, prow, pltpu.roll(x, 1, 0))
    nxt = jnp.where(rid == tm - 1, nrow, pltpu.roll(x, tm - 1, 0))
    xs = x + mup_ref[...] * (prev - x) + mun_ref[...] * (nxt - x)
    r = xs[:, 0:WIDTH]
    k = xs[:, WIDTH:2 * WIDTH]
    v = xs[:, 2 * WIDTH:3 * WIDTH]
    zw = xs[:, 3 * WIDTH:3 * WIDTH + LANES]
    za = xs[:, 3 * WIDTH + LANES:3 * WIDTH + 2 * LANES]
    zg = xs[:, 3 * WIDTH + 2 * LANES:]
    ones = ones_ref[...]
    kk = k * kk_ref[...]
    kkn = kk / jnp.maximum(jnp.sqrt(_group_sum(kk * kk, ones)), 1e-12)
    g_out[0] = _mm(_sigmoid(zg), g2_ref[...])
    r_out[0] = r
    v_out[0] = v
    kkn_out[0] = kkn
    bonus_out[0] = _group_sum(r * k * rk_ref[...], ones) * v
    wx = w0_ref[...] + _mm(jnp.tanh(zw), w2_ref[...])
    ax = a0_ref[...] + _mm(za, a2_ref[...])
    for d in range(2):
        cols = slice(d * WIDTH, (d + 1) * WIDTH)
        lw_out[d, 0] = -math.exp(-0.5) * _sigmoid(wx[:, cols])
        a = _sigmoid(ax[:, cols])
        krep_out[d, 0] = k * (1.0 + (a - 1.0) * ka_ref[...])
        b_out[d, 0] = kkn * a


def _rwkv_prep(rcols, p, tm=256):
    b, t, _ = rcols.shape
    nt = t // tm
    per = tm // 8
    const = lambda shape: pl.BlockSpec(shape, lambda bb, i: (0,) * len(shape))
    main = lambda n: pl.BlockSpec((1, tm, n), lambda bb, i: (bb, i, 0))
    dirs = pl.BlockSpec((2, 1, tm, WIDTH), lambda bb, i: (0, bb, i, 0))
    prev = pl.BlockSpec((1, 8, RWKV_COLS), lambda bb, i: (bb, jnp.maximum(i * per - 1, 0), 0))
    nxt = pl.BlockSpec((1, 8, RWKV_COLS),
                       lambda bb, i: (bb, jnp.minimum((i + 1) * per, t // 8 - 1), 0))
    one = jax.ShapeDtypeStruct((b, t, WIDTH), F32)
    two = jax.ShapeDtypeStruct((2, b, t, WIDTH), F32)
    return pl.pallas_call(
        functools.partial(_rwkv_prep_kernel, tm=tm),
        grid=(b, nt),
        in_specs=[main(RWKV_COLS), prev, nxt, const((1, RWKV_COLS)), const((1, RWKV_COLS)),
                  const((1, 2 * WIDTH)), const((LANES, 2 * WIDTH)),
                  const((1, 2 * WIDTH)), const((LANES, 2 * WIDTH)),
                  const((LANES, WIDTH)), const((1, WIDTH)), const((1, WIDTH)), const((1, WIDTH)),
                  const((WIDTH, WIDTH))],
        out_specs=[main(WIDTH)] * 5 + [dirs] * 3,
        out_shape=[one] * 5 + [two] * 3,
        compiler_params=_cparams("parallel", "parallel"),
        name="rwkv_prep",
    )(rcols, rcols, rcols, p["mu_prev"], p["mu_next"], p["w0"], p["w2"], p["a0"], p["a2"],
      p["g2"], p["k_k"], p["k_a"], p["r_k"], p["ones_bd"])


def _rwkv_scan_kernel(r_ref, v_ref, kk_ref, lw_ref, k_ref, b_ref, tri_ref, y_ref, s_ref,
                      *, nc, reverse):
    c_len = RWKV_CHUNK

    @pl.when(pl.program_id(1) == 0)
    def _():
        s_ref[...] = jnp.zeros_like(s_ref)

    tri = tri_ref[...]
    row = _iota((c_len, LANES), 0)
    li = _iota((c_len, LANES), 1) & (HEAD_DIM - 1)
    strict = (li > row) if reverse else (li < row)
    incl = (li >= row) if reverse else (li <= row)
    r2 = _iota((LANES, LANES), 0)
    c2 = _iota((LANES, LANES), 1)
    same_head = (r2 & HEAD_DIM) == (c2 & HEAD_DIM)
    diag = r2 == c2
    zeros = jnp.zeros((c_len, LANES), F32)

    order = list(range(nc - 1, -1, -1) if reverse else range(nc))
    inst = [(p, c) for c in order for p in range(N_PAIRS)]
    n = len(inst)

    def tile(ref, p, c, lead=()):
        return ref[lead + (0, slice(c * c_len, (c + 1) * c_len), slice(p * LANES, (p + 1) * LANES))]

    lw = [tile(lw_ref, p, c, (0,)) for p, c in inst]
    g_in = [_mm_exact_left(tri, lw[i]) for i in range(n)]
    g_tot = [g[0:1] if reverse else g[c_len - 1:c_len] for g in g_in]
    v = [tile(v_ref, p, c) for p, c in inst]
    at, rt, w, bk_end = [], [], [], []
    for i, (p, c) in enumerate(inst):
        b = tile(b_ref, p, c, (0,))
        k = tile(k_ref, p, c, (0,))
        at.append(-tile(kk_ref, p, c) * jnp.exp(g_in[i] - lw[i]))
        rt.append(tile(r_ref, p, c) * jnp.exp(g_in[i]))
        inv = jnp.exp(-g_in[i])
        to_end = jnp.exp(g_tot[i] - g_in[i])
        bk_end.append(jnp.concatenate([b * to_end, k * to_end], 0))
        w.append(_mm_nt(jnp.concatenate([at[i], rt[i]], 0),
                        jnp.concatenate([_bd(b * inv), _bd(k * inv)], 0)))

    bdv = [_bd(u).astype(BF16) for u in v]
    mj = [jnp.where(strict, u[:c_len, :LANES], 0.0).astype(BF16) for u in w]
    x = [jnp.concatenate([at[i], _mm(jnp.where(strict, w[i][:c_len, LANES:], 0.0), bdv[i])], axis=1)
         for i in range(n)]
    for j in range(6):
        x = [x[i] + _mm(mj[i], _bd(x[i])) for i in range(n)]
        if j < 5:
            mj = [_mm(mj[i], _bd(mj[i])).astype(BF16) for i in range(n)]

    t1 = [_mm(jnp.where(incl, w[i][c_len:, :LANES], 0.0), _bd(x[i])) for i in range(n)]
    y0 = [t1[i][:, LANES:] + _mm(jnp.where(incl, w[i][c_len:, LANES:], 0.0), bdv[i]) for i in range(n)]
    rp = [(rt[i] + t1[i][:, :LANES]).astype(BF16) for i in range(n)]
    t2 = [_mm(bk_end[i].T, jnp.concatenate([x[i], jnp.concatenate([zeros, v[i]], axis=1)], 0))
          for i in range(n)]
    g_low = [jnp.where(same_head, u[:, :LANES], 0.0).astype(BF16) for u in t2]
    h_new = [jnp.where(same_head, u[:, LANES:], 0.0) for u in t2]
    g_col = [jnp.exp(jnp.sum(jnp.where(diag, jnp.broadcast_to(g, (LANES, LANES)), 0.0),
                             axis=1, keepdims=True)) for g in g_tot]

    s = [s_ref[p] for p in range(N_PAIRS)]
    for ci, c in enumerate(order):
        sb = [u.astype(BF16) for u in s]
        for p in range(N_PAIRS):
            i = ci * N_PAIRS + p
            y_ref[0, c * c_len:(c + 1) * c_len, p * LANES:(p + 1) * LANES] = _mm(rp[i], sb[p]) + y0[i]
        s = [s[p] * g_col[ci * N_PAIRS + p] + _mm(g_low[ci * N_PAIRS + p], sb[p]) + h_new[ci * N_PAIRS + p]
             for p in range(N_PAIRS)]
    for p in range(N_PAIRS):
        s_ref[p] = s[p]


def _rwkv_scan(r, v, kkn, lw, krep, bvec, tri, direction, tb=256):
    b, t, _ = r.shape
    nt = t // tb
    reverse = direction == 1
    tmap = (lambda i: nt - 1 - i) if reverse else (lambda i: i)
    one = pl.BlockSpec((1, tb, WIDTH), lambda bb, i: (bb, tmap(i), 0))
    two = pl.BlockSpec((1, 1, tb, WIDTH), lambda bb, i: (direction, bb, tmap(i), 0))
    return pl.pallas_call(
        functools.partial(_rwkv_scan_kernel, nc=tb // RWKV_CHUNK, reverse=reverse),
        grid=(b, nt),
        in_specs=[one, one, one, two, two, two,
                  pl.BlockSpec((RWKV_CHUNK, RWKV_CHUNK), lambda bb, i: (0, 0))],
        out_specs=one,
        out_shape=jax.ShapeDtypeStruct((b, t, WIDTH), F32),
        scratch_shapes=[pltpu.VMEM((N_PAIRS, LANES, LANES), F32)],
        compiler_params=_cparams("parallel", "arbitrary"),
        name=f"rwkv_scan_{'bwd' if reverse else 'fwd'}",
    )(r, v, kkn, lw, krep, bvec, tri)


def _even_out_kernel(h_ref, o1, l1, o4, l4, o16, l16, yf, yb, bonus, g, lng, lnb, ones_ref, w_ref,
                     out_ref, tok_sc, *, tm):
    n_tiles = WIDTH // LANES
    for slot, (dil, ref) in enumerate(((4, o4), (4, l4), (16, o16), (16, l16))):
        for rho in range(dil):
            for c in range(n_tiles):
                tok_sc[slot * n_tiles + c, pl.ds(rho, tm // dil, stride=dil), :] = (
                    ref[0, rho, :, c * LANES:(c + 1) * LANES])
    tok = lambda slot: jnp.concatenate([tok_sc[slot * n_tiles + c] for c in range(n_tiles)], axis=1)
    la, lb, lc = l1[0, 0], tok(1), tok(3)
    mx = jnp.maximum(jnp.maximum(la, lb), lc)
    wa, wb, wc = jnp.exp(la - mx), jnp.exp(lb - mx), jnp.exp(lc - mx)
    y_a = (wa * o1[0, 0] + wb * tok(0) + wc * tok(2)) / (wa + wb + wc)
    ones = ones_ref[...]
    inv = 1.0 / HEAD_DIM
    y = yf[0] + yb[0]
    yc = y - _group_sum(y, ones) * inv
    var = _group_sum(yc * yc, ones) * inv
    yn = yc * lax.rsqrt(var + RWKV_GN_EPS) * lng[...] + lnb[...]
    y_b = (yn + bonus[0]) * g[0]
    out_ref[0] = (h_ref[0] + _mm(y_a, w_ref[0:WIDTH, :]) + _mm(y_b, w_ref[WIDTH:, :]))


def _even_out(h, branch, rwkv_parts, lng, lnb, ones_bd, w, tm=256):
    b, t, _ = h.shape
    const = lambda shape: pl.BlockSpec(shape, lambda bb, i: (0, 0))
    row = lambda n: pl.BlockSpec((1, tm, n), lambda bb, i: (bb, i, 0))
    res = lambda d: pl.BlockSpec((1, d, tm // d, WIDTH), lambda bb, i: (bb, 0, i, 0))
    dils = [d for _, d in DILATED_BRANCHES]
    return pl.pallas_call(
        functools.partial(_even_out_kernel, tm=tm),
        grid=(b, t // tm),
        in_specs=[row(D_MODEL)] + [res(d) for d in dils for _ in range(2)] + [row(WIDTH)] * 4
        + [const((1, WIDTH)), const((1, WIDTH)), const((WIDTH, WIDTH)), const((D_MODEL, D_MODEL))],
        out_specs=row(D_MODEL),
        out_shape=jax.ShapeDtypeStruct((b, t, D_MODEL), F32),
        scratch_shapes=[pltpu.VMEM((4 * WIDTH // LANES, tm, LANES), F32)],
        compiler_params=_cparams("parallel", "parallel"),
        name="even_out",
    )(h, *branch, *rwkv_parts, lng, lnb, ones_bd, w)


def _mlp_kernel(h_ref, g_ref, w1_ref, w2_ref, out_ref, u_ref):
    f = pl.program_id(1)

    @pl.when(f == 0)
    def _():
        x = h_ref[...]
        u_ref[...] = _rms_rows(x, g_ref[...]).astype(BF16)
        out_ref[...] = x

    hdn = jnp.dot(u_ref[...], w1_ref[...], preferred_element_type=F32)
    hdn = jnp.square(jnp.maximum(hdn, 0.0))
    out_ref[...] += jnp.dot(hdn.astype(BF16), w2_ref[...], preferred_element_type=F32)


def _mlp(h2, g, w1, w2, tm=1024, tf=512):
    m = h2.shape[0]
    return pl.pallas_call(
        _mlp_kernel,
        grid=(m // tm, D_FF // tf),
        in_specs=[pl.BlockSpec((tm, D_MODEL), lambda i, f: (i, 0)),
                  pl.BlockSpec((1, D_MODEL), lambda i, f: (0, 0)),
                  pl.BlockSpec((D_MODEL, tf), lambda i, f: (0, f)),
                  pl.BlockSpec((tf, D_MODEL), lambda i, f: (f, 0))],
        out_specs=pl.BlockSpec((tm, D_MODEL), lambda i, f: (i, 0)),
        out_shape=jax.ShapeDtypeStruct((m, D_MODEL), F32),
        scratch_shapes=[pltpu.VMEM((tm, D_MODEL), BF16)],
        compiler_params=_cparams("parallel", "arbitrary"),
        name="mlp",
    )(h2, g, w1, w2)


def _odd_in_kernel(x_ref, g_ref, wr_ref, ws_ref, ret_out, ssd_out):
    ub = _rms_rows(x_ref[...], g_ref[...]).astype(BF16)
    ret_out[...] = jnp.dot(ub, wr_ref[...], preferred_element_type=F32)
    ssd_out[...] = jnp.dot(ub, ws_ref[...], preferred_element_type=F32)


def _odd_in(x2, g, wr, ws, tm=512):
    m = x2.shape[0]
    const = lambda shape: pl.BlockSpec(shape, lambda i: (0, 0))
    row = lambda n: pl.BlockSpec((tm, n), lambda i: (i, 0))
    return pl.pallas_call(
        _odd_in_kernel,
        grid=(m // tm,),
        in_specs=[row(D_MODEL), const((1, D_MODEL)), const((D_MODEL, 4 * WIDTH)),
                  const((D_MODEL, SSD_COLS))],
        out_specs=[row(4 * WIDTH), row(SSD_COLS)],
        out_shape=[jax.ShapeDtypeStruct((m, 4 * WIDTH), F32),
                   jax.ShapeDtypeStruct((m, SSD_COLS), F32)],
        compiler_params=_cparams("parallel"),
        name="odd_in",
    )(x2, g, wr, ws)


def _rotary_pair(u, cos, sin_signed):
    first = (_iota(u.shape, 1) & (HEAD_DIM - 1)) < HEAD_DIM // 2
    swapped = jnp.where(first, pltpu.roll(u, LANES - HEAD_DIM // 2, 1),
                        pltpu.roll(u, HEAD_DIM // 2, 1))
    return u * cos + swapped * sin_signed


def _ret_kernel(q_ref, k_ref, v_ref, cos_ref, sin_ref, lg64_ref, lg128_ref, y_ref, s_ref,
                *, nc, reverse):
    c_len = LIN_CHUNK

    @pl.when(pl.program_id(2) == 0)
    def _():
        s_ref[...] = jnp.zeros_like(s_ref)

    lg64 = lg64_ref[0]
    lg128 = lg128_ref[0]
    l_io = _iota((c_len, 2 * c_len), 0)
    s_io = _iota((c_len, 2 * c_len), 1) & (c_len - 1)
    delta = (s_io - l_io) if reverse else (l_io - s_io)
    causal = delta >= 0
    decay = jnp.where(causal, jnp.exp(lg128 * jnp.where(causal, delta, 0).astype(F32)), 0.0)
    pos = _iota((c_len, LANES), 0)
    if reverse:
        pos = c_len - 1 - pos
    pos = pos.astype(F32)
    from_start = jnp.exp(lg64 * (pos + 1.0))
    to_end = jnp.exp(lg64 * (float(c_len - 1) - pos))
    whole = jnp.exp(lg64 * float(c_len))
    r2 = _iota((LANES, LANES), 0)
    c2 = _iota((LANES, LANES), 1)
    same_head = (r2 & HEAD_DIM) == (c2 & HEAD_DIM)

    s = s_ref[...]
    order = range(nc - 1, -1, -1) if reverse else range(nc)
    for c in order:
        rows = slice(c * c_len, (c + 1) * c_len)
        cos = cos_ref[rows, :]
        sin = sin_ref[rows, :]
        q = _rotary_pair(q_ref[0, rows, :], cos, sin)
        k = _rotary_pair(k_ref[0, rows, :], cos, sin) * (HEAD_DIM ** -0.5)
        v = v_ref[0, rows, :]
        sc = _mm_nt(q, _bd(k)) * decay
        sb = s.astype(BF16)
        y_ref[0, rows, :] = _mm(sc, _bd(v)) + _mm(q * from_start, sb)
        upd = _mm((k * to_end).T, v)
        s = s * whole + jnp.where(same_head, upd, 0.0)
    s_ref[...] = s


def _retention(ret, cos, sin, lg64, lg128, direction, tb=512):
    b, t, _ = ret.shape
    nt = t // tb
    reverse = direction == 1
    tmap = (lambda i: nt - 1 - i) if reverse else (lambda i: i)
    col = lambda off: pl.BlockSpec((1, tb, LANES), lambda bb, p, i: (bb, tmap(i), off + p))
    tab = pl.BlockSpec((tb, LANES), lambda bb, p, i: (tmap(i), 0))
    return pl.pallas_call(
        functools.partial(_ret_kernel, nc=tb // LIN_CHUNK, reverse=reverse),
        grid=(b, N_PAIRS, nt),
        in_specs=[col(0), col(N_PAIRS), col(2 * N_PAIRS), tab, tab,
                  pl.BlockSpec((1, 1, LANES), lambda bb, p, i: (direction, 0, p)),
                  pl.BlockSpec((1, 1, 2 * LANES), lambda bb, p, i: (direction, 0, p))],
        out_specs=col(0),
        out_shape=jax.ShapeDtypeStruct((b, t, WIDTH), F32),
        scratch_shapes=[pltpu.VMEM((LANES, LANES), F32)],
        compiler_params=_cparams("parallel", "parallel", "arbitrary"),
        name=f"retention_{'bwd' if reverse else 'fwd'}",
    )(ret, ret, ret, cos, sin, lg64, lg128)


def _ssd_conv_kernel(x_ref, xp_ref, xn_ref, w_ref, b_ref, xs_out, bc_out, *, tm):
    i = pl.program_id(1)
    nt = pl.num_programs(1)
    prev = jnp.where(i > 0, xp_ref[0], 0.0)
    nxt = jnp.where(i < nt - 1, xn_ref[0], 0.0)
    xe = jnp.concatenate([prev, x_ref[0], nxt], axis=0)
    ext = tm + 16
    pad = SSD_CONV // 2
    acc = jnp.zeros((tm, SSD_XBC), F32) + b_ref[...]
    for j in range(SSD_CONV):
        shift = (pad - j) % ext
        xj = xe if shift == 0 else pltpu.roll(xe, shift, 0)
        acc = acc + xj[8:8 + tm] * w_ref[j:j + 1, :]
    y = _silu(acc)
    xs_out[0] = y[:, :WIDTH]
    bc_out[0] = y[:, WIDTH:].astype(BF16)


def _ssd_conv(ssd, w, bias, tm=256):
    b, t, _ = ssd.shape
    nt = t // tm
    per = tm // 8
    const = lambda shape: pl.BlockSpec(shape, lambda bb, i: (0, 0))
    main = pl.BlockSpec((1, tm, SSD_XBC), lambda bb, i: (bb, i, 0))
    prev = pl.BlockSpec((1, 8, SSD_XBC), lambda bb, i: (bb, jnp.maximum(i * per - 1, 0), 0))
    nxt = pl.BlockSpec((1, 8, SSD_XBC),
                       lambda bb, i: (bb, jnp.minimum((i + 1) * per, t // 8 - 1), 0))
    out = pl.BlockSpec((1, tm, WIDTH), lambda bb, i: (bb, i, 0))
    return pl.pallas_call(
        functools.partial(_ssd_conv_kernel, tm=tm),
        grid=(b, nt),
        in_specs=[main, prev, nxt, const((8, SSD_XBC)), const((1, SSD_XBC))],
        out_specs=[out, out],
        out_shape=[jax.ShapeDtypeStruct((b, t, WIDTH), F32),
                   jax.ShapeDtypeStruct((b, t, WIDTH), BF16)],
        compiler_params=_cparams("parallel", "parallel"),
        name="ssd_conv",
    )(ssd, ssd, ssd, w, bias)


def _ssd_kernel(xs_ref, bm_ref, cm_ref, dtc_ref, dtr_ref, bc_ref, br_ref, ac_ref, ar_ref,
                tri_ref, trit_ref, y_ref, s_ref, *, nc, reverse):
    c_len = LIN_CHUNK
    per_group = N_HEADS // SSD_GROUPS

    @pl.when(pl.program_id(2) == 0)
    def _():
        s_ref[...] = jnp.zeros_like(s_ref)

    tri = tri_ref[...]
    trit = trit_ref[...]
    dts_c = _softplus(dtc_ref[0, 0, 0] + bc_ref[0, 0])
    la_c = dts_c * ac_ref[0, 0]
    dts_r = _softplus(dtr_ref[0, 0, 0] + br_ref[0, 0])
    la_r = dts_r * ar_ref[0, 0]
    l_io = _iota((c_len, c_len), 0)
    s_io = _iota((c_len, c_len), 1)
    causal = (s_io >= l_io) if reverse else (s_io <= l_io)
    lo = _lane_lo((c_len, LANES))

    def by_head128(cols):
        return jnp.concatenate([jnp.broadcast_to(cols[:, h:h + 1], (c_len, LANES))
                                for h in range(per_group)], axis=1)

    def to64(wide):
        h = [wide[:, j * LANES:(j + 1) * LANES] for j in range(per_group)]
        return jnp.concatenate([jnp.where(lo, h[0], h[1]), jnp.where(lo, h[2], h[3])], axis=1)

    s = s_ref[...]
    order = range(nc - 1, -1, -1) if reverse else range(nc)
    for c in order:
        rows = slice(c * c_len, (c + 1) * c_len)
        xs = xs_ref[0, rows, :]
        bm = bm_ref[0, rows, :]
        cm = cm_ref[0, rows, :]
        cum = _mm_exact_left(tri, by_head128(la_c[rows]))
        cum_r = _mm_exact_right(la_r[:, rows], trit)
        last = cum[0:1] if reverse else cum[c_len - 1:c_len]
        dts64 = to64(by_head128(dts_c[rows]))
        xdt = xs * dts64
        gm = _mm_nt(cm, bm)
        parts = []
        for pp in range(per_group // 2):
            wide = []
            for hh in (2 * pp, 2 * pp + 1):
                diff = cum[:, hh * LANES:(hh + 1) * LANES] - cum_r[hh:hh + 1, :]
                dec = jnp.where(causal, jnp.exp(jnp.where(causal, diff, 0.0)), 0.0)
                wide.append((gm * dec).astype(BF16))
            parts.append(_mm(jnp.concatenate(wide, axis=1), _bd(xdt[:, pp * LANES:(pp + 1) * LANES])))
        y_intra = jnp.concatenate(parts, axis=1)
        y_inter = _mm(cm, s) * to64(jnp.exp(cum))
        y_ref[0, rows, :] = y_intra + y_inter
        upd = _mm(bm.astype(F32).T, xdt * to64(jnp.exp(last - cum)))
        s = s * to64(jnp.exp(last)) + upd
    s_ref[...] = s


def _ssd_scan(xs, bc, dtc, dtr, sp, direction, tb=256):
    b, t, _ = xs.shape
    nt = t // tb
    reverse = direction == 1
    tmap = (lambda i: nt - 1 - i) if reverse else (lambda i: i)
    gw = WIDTH // SSD_GROUPS
    par = lambda shape: pl.BlockSpec((1, 1) + shape, lambda bb, g, i: (direction, g, 0, 0))
    return pl.pallas_call(
        functools.partial(_ssd_kernel, nc=tb // LIN_CHUNK, reverse=reverse),
        grid=(b, SSD_GROUPS, nt),
        in_specs=[pl.BlockSpec((1, tb, gw), lambda bb, g, i: (bb, tmap(i), g)),
                  pl.BlockSpec((1, tb, SSD_STATE), lambda bb, g, i: (bb, tmap(i), g)),
                  pl.BlockSpec((1, tb, SSD_STATE), lambda bb, g, i: (bb, tmap(i), SSD_GROUPS + g)),
                  pl.BlockSpec((1, 1, 1, tb, 4), lambda bb, g, i: (direction, g, bb, tmap(i), 0)),
                  pl.BlockSpec((1, 1, 1, 8, tb), lambda bb, g, i: (direction, g, bb, 0, tmap(i))),
                  par((1, 4)), par((8, 1)), par((1, 4)), par((8, 1)),
                  pl.BlockSpec((LIN_CHUNK, LIN_CHUNK), lambda bb, g, i: (0, 0)),
                  pl.BlockSpec((LIN_CHUNK, LIN_CHUNK), lambda bb, g, i: (0, 0))],
        out_specs=pl.BlockSpec((1, tb, gw), lambda bb, g, i: (bb, tmap(i), g)),
        out_shape=jax.ShapeDtypeStruct((b, t, WIDTH), F32),
        scratch_shapes=[pltpu.VMEM((SSD_STATE, gw), F32)],
        compiler_params=_cparams("parallel", "parallel", "arbitrary"),
        name=f"ssd_scan_{'bwd' if reverse else 'fwd'}",
    )(xs, bc, bc, dtc, dtr, sp["bias_c"], sp["bias_r"], sp["nega_c"], sp["nega_r"],
      sp["tri_rev" if reverse else "tri_fwd"], sp["trit_rev" if reverse else "trit_fwd"])


def _odd_out_kernel(h_ref, rf, rb, gate, sf, sb, xs, z, gng, gnb, dsk, ng, ones_ref, w_ref, out_ref):
    ones = ones_ref[...]
    inv = 1.0 / HEAD_DIM
    y = rf[...] + rb[...]
    yc = y - _group_sum(y, ones) * inv
    var = _group_sum(yc * yc, ones) * inv
    y_c = (yc * lax.rsqrt(var + GN_EPS) * gng[...] + gnb[...]) * _silu(gate[...])
    yd = (sf[...] + sb[...] + dsk[...] * xs[...]) * _silu(z[...])
    gw = WIDTH // SSD_GROUPS
    halves = []
    for gi in range(SSD_GROUPS):
        yg = yd[:, gi * gw:(gi + 1) * gw]
        halves.append(yg * lax.rsqrt(jnp.mean(yg * yg, -1, keepdims=True) + NORM_EPS))
    y_d = jnp.concatenate(halves, axis=1) * ng[...]
    out_ref[...] = h_ref[...] + _mm(y_c, w_ref[0:WIDTH, :]) + _mm(y_d, w_ref[WIDTH:, :])


def _odd_out(h2, rf, rb, ret2, sf, sb, xs, ssd2, gng, gnb, dsk, ng, ones_bd, w, tm=256):
    m = h2.shape[0]
    const = lambda shape: pl.BlockSpec(shape, lambda i: (0, 0))
    row = lambda n: pl.BlockSpec((tm, n), lambda i: (i, 0))
    vec = const((1, WIDTH))
    return pl.pallas_call(
        _odd_out_kernel,
        grid=(m // tm,),
        in_specs=[row(D_MODEL), row(WIDTH), row(WIDTH),
                  pl.BlockSpec((tm, WIDTH), lambda i: (i, 3)),
                  row(WIDTH), row(WIDTH), row(WIDTH),
                  pl.BlockSpec((tm, WIDTH), lambda i: (i, 2)),
                  vec, vec, vec, vec, const((WIDTH, WIDTH)), const((D_MODEL, D_MODEL))],
        out_specs=row(D_MODEL),
        out_shape=jax.ShapeDtypeStruct((m, D_MODEL), F32),
        compiler_params=_cparams("parallel"),
        name="odd_out",
    )(h2, rf, rb, ret2, sf, sb, xs, ssd2, gng, gnb, dsk, ng, ones_bd, w)


def _tri(n, reverse):
    i = np.arange(n)
    m = (i[None, :] >= i[:, None]) if reverse else (i[None, :] <= i[:, None])
    return m.astype(np.float32)


def _prepare(norm_mix, norm_mlp, mlp_w1, mlp_w2, even_in_w, even_out_w, attn_q_gain, attn_k_gain,
             rwkv_mu_prev, rwkv_mu_next, rwkv_w0, rwkv_w2, rwkv_a0, rwkv_a2, rwkv_g2, rwkv_k_k,
             rwkv_k_a, rwkv_r_k, rwkv_ln_g, rwkv_ln_b, odd_in_w, odd_out_w, ret_decay_exp, ret_gn_g,
             ret_gn_b, ssd_conv_w, ssd_conv_b, ssd_dt_bias, ssd_a_log, ssd_d, ssd_norm_g):
    row = lambda v: v.reshape(1, -1).astype(F32)
    heads = np.arange(WIDTH) // HEAD_DIM
    ones_bd = jnp.asarray(heads[:, None] == heads[None, :], BF16)
    p = {"ones_bd": ones_bd, "norm_mix": norm_mix, "norm_mlp": norm_mlp,
         "mlp_w1": mlp_w1.astype(BF16), "mlp_w2": mlp_w2.astype(BF16)}

    w_in = even_in_w[0]
    p["even_wa"] = w_in[:, :3 * WIDTH].astype(BF16)
    p["even_wr"] = w_in[:, 3 * WIDTH:].astype(BF16)
    p["even_out_w"] = even_out_w[0].astype(BF16)
    p["q_gain"] = row(jnp.tile(attn_q_gain[0], N_HEADS))
    p["k_gain"] = row(jnp.tile(attn_k_gain[0], N_HEADS))

    def two_dir_lowrank(w):
        z = jnp.zeros_like(w[0])
        return jnp.concatenate([jnp.concatenate([w[0], z], 1), jnp.concatenate([z, w[1]], 1)], 0)

    p["rwkv"] = {
        "mu_prev": row(rwkv_mu_prev[0]), "mu_next": row(rwkv_mu_next[0]),
        "w0": row(rwkv_w0[0]), "w2": two_dir_lowrank(rwkv_w2[0]).astype(BF16),
        "a0": row(rwkv_a0[0]), "a2": two_dir_lowrank(rwkv_a2[0]).astype(BF16),
        "g2": rwkv_g2[0].astype(BF16), "k_k": row(rwkv_k_k[0]), "k_a": row(rwkv_k_a[0]),
        "r_k": row(rwkv_r_k[0]), "ones_bd": ones_bd,
    }
    p["rwkv_tri"] = [jnp.asarray(_tri(RWKV_CHUNK, False), BF16), jnp.asarray(_tri(RWKV_CHUNK, True), BF16)]
    p["rwkv_ln_g"] = row(rwkv_ln_g[0])
    p["rwkv_ln_b"] = row(rwkv_ln_b[0])

    w_odd = odd_in_w[0]
    p["odd_wr"] = w_odd[:, :4 * WIDTH].astype(BF16)
    z_w = w_odd[:, 4 * WIDTH:5 * WIDTH]
    xbc_w = w_odd[:, 5 * WIDTH:5 * WIDTH + SSD_XBC]
    dt_w = w_odd[:, 5 * WIDTH + SSD_XBC:]
    pad = jnp.zeros((D_MODEL, SSD_COLS - SSD_XBC - WIDTH - 2 * N_HEADS), F32)
    p["odd_ws"] = jnp.concatenate([xbc_w, z_w, dt_w, pad], axis=1).astype(BF16)
    p["odd_out_w"] = odd_out_w[0].astype(BF16)

    log_gamma = jnp.log1p(-jnp.exp2(-ret_decay_exp[0].astype(F32)))
    p["lg64"] = jnp.repeat(log_gamma, HEAD_DIM, axis=1).reshape(2, 1, WIDTH)
    p["lg128"] = jnp.repeat(log_gamma, LIN_CHUNK, axis=1).reshape(2, 1, N_HEADS * LIN_CHUNK)
    p["ret_gn_g"] = row(ret_gn_g[0])
    p["ret_gn_b"] = row(ret_gn_b[0])

    p["conv_w"] = jnp.concatenate([ssd_conv_w[0], jnp.zeros((8 - SSD_CONV, SSD_XBC), F32)], 0)
    p["conv_b"] = row(ssd_conv_b[0])
    per_group = N_HEADS // SSD_GROUPS
    bias = ssd_dt_bias[0].reshape(2, SSD_GROUPS, per_group).astype(F32)
    nega = (-jnp.exp(ssd_a_log[0].astype(F32))).reshape(2, SSD_GROUPS, per_group)
    as_rows = lambda v: jnp.concatenate([v, jnp.zeros_like(v)], -1)[..., None]
    p["ssd"] = {
        "bias_c": bias[:, :, None, :], "nega_c": nega[:, :, None, :],
        "bias_r": as_rows(bias), "nega_r": as_rows(nega),
        "tri_fwd": jnp.asarray(_tri(LIN_CHUNK, False), BF16),
        "tri_rev": jnp.asarray(_tri(LIN_CHUNK, True), BF16),
        "trit_fwd": jnp.asarray(_tri(LIN_CHUNK, False).T, BF16),
        "trit_rev": jnp.asarray(_tri(LIN_CHUNK, True).T, BF16),
    }
    p["ssd_d"] = row(jnp.repeat(ssd_d[0], HEAD_DIM))
    p["ssd_norm_g"] = row(ssd_norm_g[0])
    return p


def _rope_tables(t):
    half = HEAD_DIM // 2
    inv = ROPE_BASE ** (-jnp.arange(half, dtype=F32) / half)
    ang = jnp.arange(t, dtype=F32)[:, None] * inv[None, :]
    cos = jnp.tile(jnp.cos(ang), (1, LANES // half))
    sin = jnp.sin(ang)
    sin_signed = jnp.tile(jnp.concatenate([-sin, sin], axis=1), (1, LANES // HEAD_DIM))
    return cos, sin_signed


def _even_mixer(h2, b, t, p):
    h = h2.reshape(b, t, D_MODEL)
    *qkvs, rcols = _even_in(h, p["norm_mix"][0:1], p["even_wa"], p["even_wr"], p["ones_bd"],
                            p["q_gain"], p["k_gain"])
    branch = []
    for qkv, (_, dil) in zip(qkvs, DILATED_BRANCHES):
        branch.extend(_dil_attn(qkv, dil))
    r, v, kkn, g, bonus, lw, krep, bvec = _rwkv_prep(rcols, p["rwkv"])
    ys = [_rwkv_scan(r, v, kkn, lw, krep, bvec, p["rwkv_tri"][d], d) for d in range(2)]
    out = _even_out(h, branch, [ys[0], ys[1], bonus, g], p["rwkv_ln_g"], p["rwkv_ln_b"],
                    p["ones_bd"], p["even_out_w"])
    return out.reshape(b * t, D_MODEL)


def _odd_mixer(h2, b, t, p):
    ret2, ssd2 = _odd_in(h2, p["norm_mix"][1:2], p["odd_wr"], p["odd_ws"])
    ret = ret2.reshape(b, t, -1)
    ssd = ssd2.reshape(b, t, -1)
    cos, sin = _rope_tables(t)
    rets = [_retention(ret, cos, sin, p["lg64"], p["lg128"], d) for d in range(2)]
    xs, bc = _ssd_conv(ssd, p["conv_w"], p["conv_b"])
    per_group = N_HEADS // SSD_GROUPS
    dt = ssd[:, :, SSD_XBC + WIDTH:SSD_XBC + WIDTH + 2 * N_HEADS]
    dtc = dt.reshape(b, t, 2, SSD_GROUPS, per_group).transpose(2, 3, 0, 1, 4)
    dtr = dtc.transpose(0, 1, 2, 4, 3)
    dtr = jnp.concatenate([dtr, jnp.zeros_like(dtr)], axis=3)
    ssds = [_ssd_scan(xs, bc, dtc, dtr, p["ssd"], d) for d in range(2)]
    flat = lambda u: u.reshape(b * t, -1)
    return _odd_out(h2, flat(rets[0]), flat(rets[1]), ret2, flat(ssds[0]), flat(ssds[1]), flat(xs),
                    ssd2, p["ret_gn_g"], p["ret_gn_b"], p["ssd_d"], p["ssd_norm_g"], p["ones_bd"],
                    p["odd_out_w"])


def _trunk(x, p):
    b, t, _ = x.shape
    h2 = x.reshape(b * t, D_MODEL)
    h2 = _even_mixer(h2, b, t, p)
    h2 = _mlp(h2, p["norm_mlp"][0:1], p["mlp_w1"][0], p["mlp_w2"][0])
    h2 = _odd_mixer(h2, b, t, p)
    h2 = _mlp(h2, p["norm_mlp"][1:2], p["mlp_w1"][1], p["mlp_w2"][1])
    return h2.reshape(b, t, D_MODEL)


def kernel(x_prompt, x_sample, norm_mix, norm_mlp, mlp_w1, mlp_w2, even_in_w, even_out_w, attn_q_gain, attn_k_gain, rwkv_mu_prev, rwkv_mu_next, rwkv_w0, rwkv_w2, rwkv_a0, rwkv_a2, rwkv_g2, rwkv_k_k, rwkv_k_a, rwkv_r_k, rwkv_ln_g, rwkv_ln_b, odd_in_w, odd_out_w, ret_decay_exp, ret_gn_g, ret_gn_b, ssd_conv_w, ssd_conv_b, ssd_dt_bias, ssd_a_log, ssd_d, ssd_norm_g):
    p = _prepare(norm_mix, norm_mlp, mlp_w1, mlp_w2, even_in_w, even_out_w, attn_q_gain, attn_k_gain,
                 rwkv_mu_prev, rwkv_mu_next, rwkv_w0, rwkv_w2, rwkv_a0, rwkv_a2, rwkv_g2, rwkv_k_k,
                 rwkv_k_a, rwkv_r_k, rwkv_ln_g, rwkv_ln_b, odd_in_w, odd_out_w, ret_decay_exp,
                 ret_gn_g, ret_gn_b, ssd_conv_w, ssd_conv_b, ssd_dt_bias, ssd_a_log, ssd_d, ssd_norm_g)
    return (_trunk(x_prompt, p), _trunk(x_sample, p))
```

```python
import functools
import math

import numpy as np
import jax
import jax.numpy as jnp
from jax import lax
from jax.experimental import pallas as pl
from jax.experimental.pallas import tpu as pltpu

F32 = jnp.float32
BF16 = jnp.bfloat16

D_MODEL = 1024
HEAD_DIM = 64
N_HEADS = 8
WIDTH = N_HEADS * HEAD_DIM
N_PAIRS = N_HEADS // 2
LANES = 128
HALO_ROWS = 16
DILATED_BRANCHES = ((128, 1), (512, 4), (2048, 16))
ATT_R = 64
RWKV_DECAY_RANK = 64
RWKV_ICL_RANK = 64
RWKV_GATE_RANK = 128
RWKV_COLS = 3 * WIDTH + 2 * RWKV_DECAY_RANK + 2 * RWKV_ICL_RANK + RWKV_GATE_RANK
SSD_STATE = 128
SSD_GROUPS = 2
SSD_CONV = 5
SSD_XBC = WIDTH + 2 * SSD_GROUPS * SSD_STATE
D_FF = 4 * D_MODEL
NORM_EPS = 1e-6
GN_EPS = 1e-5
RWKV_GN_EPS = 64e-5
ROPE_BASE = 10000.0
RWKV_CHUNK = 64
LIN_CHUNK = 128
NEG_BIG = -1e30
VMEM_LIMIT = 56 * 1024 * 1024


def _cparams(*sem):
    return pltpu.CompilerParams(dimension_semantics=sem, vmem_limit_bytes=VMEM_LIMIT)


def _iota(shape, dim):
    return lax.broadcasted_iota(jnp.int32, shape, dim)


def _lane_lo(shape):
    return (_iota(shape, len(shape) - 1) & HEAD_DIM) == 0


def _bd(x):
    lo = _lane_lo(x.shape)
    zero = jnp.zeros_like(x)
    return jnp.concatenate([jnp.where(lo, x, zero), jnp.where(lo, zero, x)], axis=0)


def _mm(a, b):
    return jnp.dot(a.astype(BF16), b.astype(BF16), preferred_element_type=F32)


def _mm_nt(a, b):
    return lax.dot_general(a.astype(BF16), b.astype(BF16), (((1,), (1,)), ((), ())),
                           preferred_element_type=F32)


def _split3(x):
    x1 = x.astype(BF16)
    r1 = x - x1.astype(F32)
    x2 = r1.astype(BF16)
    x3 = (r1 - x2.astype(F32)).astype(BF16)
    return x1, x2, x3


def _mm_exact_left(t, x):
    x1, x2, x3 = _split3(x)
    d = lambda b: jnp.dot(t, b, preferred_element_type=F32)
    return d(x1) + d(x2) + d(x3)


def _mm_exact_right(x, t):
    x1, x2, x3 = _split3(x)
    d = lambda a: jnp.dot(a, t, preferred_element_type=F32)
    return d(x1) + d(x2) + d(x3)


def _group_sum(x, ones_bd):
    hi = x.astype(BF16)
    lo = (x - hi.astype(F32)).astype(BF16)
    return (jnp.dot(hi, ones_bd, preferred_element_type=F32)
            + jnp.dot(lo, ones_bd, preferred_element_type=F32))


def _rms_rows(x, g):
    return x * lax.rsqrt(jnp.mean(x * x, -1, keepdims=True) + NORM_EPS) * g


def _sigmoid(x):
    return 1.0 / (1.0 + jnp.exp(-x))


def _silu(x):
    return x * _sigmoid(x)


def _softplus(x):
    return jnp.maximum(x, 0.0) + jnp.log1p(jnp.exp(-jnp.abs(x)))


def _scan_order(nc, reverse):
    return list(range(nc - 1, -1, -1) if reverse else range(nc))


def _even_in_kernel(x_ref, g_ref, wa_ref, wr_ref, ones_ref, qg_ref, kg_ref,
                    qkv1_out, qkv4_out, qkv16_out, r_out, qkv_sc, *, tm):
    ub = _rms_rows(x_ref[0], g_ref[...]).astype(BF16)
    a = jnp.dot(ub, wa_ref[...], preferred_element_type=F32)
    q = a[:, :WIDTH]
    k = a[:, WIDTH:2 * WIDTH]
    ones = ones_ref[...]
    inv = 1.0 / HEAD_DIM
    qn = q * lax.rsqrt(_group_sum(q * q, ones) * inv + NORM_EPS) * qg_ref[...]
    kn = k * lax.rsqrt(_group_sum(k * k, ones) * inv + NORM_EPS) * kg_ref[...]
    qkv = jnp.concatenate([qn * (HEAD_DIM ** -0.5), kn, a[:, 2 * WIDTH:]], axis=1)
    qkv1_out[0, 0] = qkv.astype(BF16)
    n_tiles = 3 * WIDTH // LANES
    for c in range(n_tiles):
        qkv_sc[c] = qkv[:, c * LANES:(c + 1) * LANES]
    for dil, out in ((4, qkv4_out), (16, qkv16_out)):
        for rho in range(dil):
            for c in range(n_tiles):
                out[0, rho, :, c * LANES:(c + 1) * LANES] = (
                    qkv_sc[c, pl.ds(rho, tm // dil, stride=dil), :].astype(BF16))
    r_out[0] = jnp.dot(ub, wr_ref[...], preferred_element_type=F32).astype(BF16)


def _even_in(x, g, wa, wr, ones_bd, qg, kg, tm=512):
    b, t, _ = x.shape
    const = lambda shape: pl.BlockSpec(shape, lambda bb, i: (0, 0))
    row = lambda n: pl.BlockSpec((1, tm, n), lambda bb, i: (bb, i, 0))
    res = lambda d: pl.BlockSpec((1, d, tm // d, 3 * WIDTH), lambda bb, i: (bb, 0, i, 0))
    dils = [d for _, d in DILATED_BRANCHES]
    return pl.pallas_call(
        functools.partial(_even_in_kernel, tm=tm),
        grid=(b, t // tm),
        in_specs=[row(D_MODEL), const((1, D_MODEL)), const((D_MODEL, 3 * WIDTH)),
                  const((D_MODEL, RWKV_COLS)), const((WIDTH, WIDTH)),
                  const((1, WIDTH)), const((1, WIDTH))],
        out_specs=[res(d) for d in dils] + [row(RWKV_COLS)],
        out_shape=[jax.ShapeDtypeStruct((b, d, t // d, 3 * WIDTH), BF16) for d in dils]
        + [jax.ShapeDtypeStruct((b, t, RWKV_COLS), BF16)],
        scratch_shapes=[pltpu.VMEM((3 * WIDTH // LANES, tm, LANES), F32)],
        compiler_params=_cparams("parallel", "parallel"),
        name="even_in",
    )(x, g, wa, wr, ones_bd, qg, kg)


def _dil_attn_kernel(q_ref, k_ref, kl_ref, kr_ref, v_ref, vl_ref, vr_ref, o_ref, lse_ref,
                     kext, vext, *, tq, sub_len, dil):
    r = ATT_R
    i = pl.program_id(2)
    kext[0:r] = kl_ref[0, 0]
    kext[r:r + tq] = k_ref[0, 0]
    kext[r + tq:] = kr_ref[0, 0]
    vext[0:r] = vl_ref[0, 0]
    vext[r:r + tq] = v_ref[0, 0]
    vext[r + tq:] = vr_ref[0, 0]

    c_io = _iota((r, 3 * r), 1)
    q_io = _iota((r, 3 * r), 0)
    rel = c_io - r - q_io
    near = jnp.abs(rel) <= r
    dist = jnp.abs(rel).astype(F32) * float(dil)
    lo = _lane_lo((r, LANES))
    hi = jnp.logical_not(lo)

    for j in range(tq // r):
        kpos = i * tq + j * r + c_io - r
        valid = near & (kpos >= 0) & (kpos < sub_len)
        heads = [(p, hh) for p in range(N_PAIRS) for hh in range(2)]
        cols = [slice(p * LANES, (p + 1) * LANES) for p in range(N_PAIRS)]
        kp = [kext[j * r:(j + 3) * r, c] for c in cols]
        vp = [vext[j * r:(j + 3) * r, c] for c in cols]
        s = []
        for p, hh in heads:
            qp = q_ref[0, 0, j * r:(j + 1) * r, cols[p]]
            qm = jnp.where(lo if hh == 0 else hi, qp, jnp.zeros_like(qp))
            sc = lax.dot_general(qm, kp[p], (((1,), (1,)), ((), ())), preferred_element_type=F32)
            s.append(jnp.where(valid, sc - 2.0 ** (-(2 * p + hh + 1)) * dist, NEG_BIG))
        m = [jnp.max(u, -1, keepdims=True) for u in s]
        e = [jnp.exp(u - mm) for u, mm in zip(s, m)]
        den = [jnp.sum(u, -1, keepdims=True) for u in e]
        pv = [jnp.dot(e[n].astype(BF16), vp[p], preferred_element_type=F32)
              for n, (p, hh) in enumerate(heads)]
        for p in range(N_PAIRS):
            a, b = 2 * p, 2 * p + 1
            o_ref[0, 0, j * r:(j + 1) * r, cols[p]] = jnp.where(
                lo, pv[a] / den[a], pv[b] / den[b]).astype(BF16)
            lse_ref[0, 0, j * r:(j + 1) * r, cols[p]] = jnp.where(
                lo, jnp.broadcast_to(m[a] + jnp.log(den[a]), (r, LANES)),
                jnp.broadcast_to(m[b] + jnp.log(den[b]), (r, LANES)))


def _dil_attn(qkv, dil):
    b, _, sub_len, _ = qkv.shape
    tq = min(sub_len, 256)
    nblk = sub_len // ATT_R
    per = tq // ATT_R
    main = lambda col: pl.BlockSpec((1, 1, tq, WIDTH), lambda bb, rr, i: (bb, rr, i, col))
    left = lambda col: pl.BlockSpec((1, 1, ATT_R, WIDTH),
                                    lambda bb, rr, i: (bb, rr, jnp.maximum(i * per - 1, 0), col))
    right = lambda col: pl.BlockSpec((1, 1, ATT_R, WIDTH),
                                     lambda bb, rr, i: (bb, rr, jnp.minimum((i + 1) * per, nblk - 1), col))
    return pl.pallas_call(
        functools.partial(_dil_attn_kernel, tq=tq, sub_len=sub_len, dil=dil),
        grid=(b, dil, sub_len // tq),
        in_specs=[main(0), main(1), left(1), right(1), main(2), left(2), right(2)],
        out_specs=[main(0), main(0)],
        out_shape=[jax.ShapeDtypeStruct((b, dil, sub_len, WIDTH), BF16),
                   jax.ShapeDtypeStruct((b, dil, sub_len, WIDTH), F32)],
        scratch_shapes=[pltpu.VMEM((tq + 2 * ATT_R, WIDTH), BF16)] * 2,
        compiler_params=_cparams("parallel", "parallel", "parallel"),
        name=f"dil_attn_d{dil}",
    )(qkv, qkv, qkv, qkv, qkv, qkv, qkv)


def _rwkv_prep_kernel(x_ref, xp_ref, xn_ref, mup_ref, mun_ref, w0_ref, w2_ref, a0_ref, a2_ref,
                      g2_ref, kk_ref, ka_ref, rk_ref, ones_ref,
                      r_out, v_out, kkn_out, g_out, bonus_out, lw_out, krep_out, b_out, *, tm):
    i = pl.program_id(1)
    nt = pl.num_programs(1)
    x = x_ref[0].astype(F32)
    prow = jnp.where(i > 0, xp_ref[0, HALO_ROWS - 1:HALO_ROWS, :].astype(F32), 0.0)
    nrow = jnp.where(i < nt - 1, xn_ref[0, 0:1, :].astype(F32), 0.0)
    rid = _iota((tm, 1), 0)
    prev = jnp.where(rid == 0, prow, pltpu.roll(x, 1, 0))
    nxt = jnp.where(rid == tm - 1, nrow, pltpu.roll(x, tm - 1, 0))
    xs = x + mup_ref[...] * (prev - x) + mun_ref[...] * (nxt - x)
    r = xs[:, 0:WIDTH]
    k = xs[:, WIDTH:2 * WIDTH]
    v = xs[:, 2 * WIDTH:3 * WIDTH]
    zw = xs[:, 3 * WIDTH:3 * WIDTH + LANES]
    za = xs[:, 3 * WIDTH + LANES:3 * WIDTH + 2 * LANES]
    zg = xs[:, 3 * WIDTH + 2 * LANES:]
    ones = ones_ref[...]
    kk = k * kk_ref[...]
    kkn = kk / jnp.maximum(jnp.sqrt(_group_sum(kk * kk, ones)), 1e-12)
    g_out[0] = _mm(_sigmoid(zg), g2_ref[...]).astype(BF16)
    r_out[0] = r.astype(BF16)
    v_out[0] = v.astype(BF16)
    kkn_out[0] = kkn.astype(BF16)
    bonus_out[0] = (_group_sum(r * k * rk_ref[...], ones) * v).astype(BF16)
    wx = w0_ref[...] + _mm(jnp.tanh(zw), w2_ref[...])
    ax = a0_ref[...] + _mm(za, a2_ref[...])
    for d in range(2):
        cols = slice(d * WIDTH, (d + 1) * WIDTH)
        lw_out[d, 0] = -math.exp(-0.5) * _sigmoid(wx[:, cols])
        a = _sigmoid(ax[:, cols])
        krep_out[d, 0] = (k * (1.0 + (a - 1.0) * ka_ref[...])).astype(BF16)
        b_out[d, 0] = (kkn * a).astype(BF16)


def _rwkv_prep(rcols, p, tm=256):
    b, t, _ = rcols.shape
    nt = t // tm
    per = tm // HALO_ROWS
    const = lambda shape: pl.BlockSpec(shape, lambda bb, i: (0,) * len(shape))
    main = lambda n: pl.BlockSpec((1, tm, n), lambda bb, i: (bb, i, 0))
    dirs = pl.BlockSpec((2, 1, tm, WIDTH), lambda bb, i: (0, bb, i, 0))
    prev = pl.BlockSpec((1, HALO_ROWS, RWKV_COLS), lambda bb, i: (bb, jnp.maximum(i * per - 1, 0), 0))
    nxt = pl.BlockSpec((1, HALO_ROWS, RWKV_COLS),
                       lambda bb, i: (bb, jnp.minimum((i + 1) * per, t // HALO_ROWS - 1), 0))
    one = jax.ShapeDtypeStruct((b, t, WIDTH), BF16)
    two = lambda dt: jax.ShapeDtypeStruct((2, b, t, WIDTH), dt)
    return pl.pallas_call(
        functools.partial(_rwkv_prep_kernel, tm=tm),
        grid=(b, nt),
        in_specs=[main(RWKV_COLS), prev, nxt, const((1, RWKV_COLS)), const((1, RWKV_COLS)),
                  const((1, 2 * WIDTH)), const((LANES, 2 * WIDTH)),
                  const((1, 2 * WIDTH)), const((LANES, 2 * WIDTH)),
                  const((LANES, WIDTH)), const((1, WIDTH)), const((1, WIDTH)), const((1, WIDTH)),
                  const((WIDTH, WIDTH))],
        out_specs=[main(WIDTH)] * 5 + [dirs] * 3,
        out_shape=[one] * 5 + [two(F32), two(BF16), two(BF16)],
        compiler_params=_cparams("parallel", "parallel"),
        name="rwkv_prep",
    )(rcols, rcols, rcols, p["mu_prev"], p["mu_next"], p["w0"], p["w2"], p["a0"], p["a2"],
      p["g2"], p["k_k"], p["k_a"], p["r_k"], p["ones_bd"])


def _rwkv_scan_kernel(r_ref, v_ref, kk_ref, lw_ref, k_ref, b_ref, tri_ref, y_ref, s_ref,
                      *, nc, reverse):
    c_len = RWKV_CHUNK

    @pl.when(pl.program_id(1) == 0)
    def _():
        s_ref[...] = jnp.zeros_like(s_ref)

    tri = tri_ref[...]
    row = _iota((c_len, LANES), 0)
    li = _iota((c_len, LANES), 1) & (HEAD_DIM - 1)
    strict = (li > row) if reverse else (li < row)
    incl = (li >= row) if reverse else (li <= row)
    r2 = _iota((LANES, LANES), 0)
    c2 = _iota((LANES, LANES), 1)
    same_head = (r2 & HEAD_DIM) == (c2 & HEAD_DIM)
    diag = r2 == c2
    zeros = jnp.zeros((c_len, LANES), F32)

    order = _scan_order(nc, reverse)
    inst = [(p, c) for c in order for p in range(N_PAIRS)]
    n = len(inst)

    def tile(ref, p, c, lead=()):
        return ref[lead + (0, slice(c * c_len, (c + 1) * c_len), slice(p * LANES, (p + 1) * LANES))]

    lw = [tile(lw_ref, p, c, (0,)) for p, c in inst]
    g_in = [_mm_exact_left(tri, lw[i]) for i in range(n)]
    g_tot = [g[0:1] if reverse else g[c_len - 1:c_len] for g in g_in]
    v = [tile(v_ref, p, c) for p, c in inst]
    at, rt, w, bk_end = [], [], [], []
    for i, (p, c) in enumerate(inst):
        b = tile(b_ref, p, c, (0,)).astype(F32)
        k = tile(k_ref, p, c, (0,)).astype(F32)
        at.append(-tile(kk_ref, p, c).astype(F32) * jnp.exp(g_in[i] - lw[i]))
        rt.append(tile(r_ref, p, c).astype(F32) * jnp.exp(g_in[i]))
        inv = jnp.exp(-g_in[i])
        to_end = jnp.exp(g_tot[i] - g_in[i])
        bk_end.append(jnp.concatenate([b * to_end, k * to_end], 0))
        w.append(_mm_nt(jnp.concatenate([at[i], rt[i]], 0),
                        jnp.concatenate([_bd(b * inv), _bd(k * inv)], 0)))

    bdv = [_bd(u) for u in v]
    mj = [jnp.where(strict, u[:c_len, :LANES], 0.0).astype(BF16) for u in w]
    x = [jnp.concatenate([at[i], _mm(jnp.where(strict, w[i][:c_len, LANES:], 0.0), bdv[i])], axis=1)
         for i in range(n)]
    for j in range(6):
        x = [x[i] + _mm(mj[i], _bd(x[i])) for i in range(n)]
        if j < 5:
            mj = [_mm(mj[i], _bd(mj[i])).astype(BF16) for i in range(n)]

    t1 = [_mm(jnp.where(incl, w[i][c_len:, :LANES], 0.0), _bd(x[i])) for i in range(n)]
    y0 = [t1[i][:, LANES:] + _mm(jnp.where(incl, w[i][c_len:, LANES:], 0.0), bdv[i]) for i in range(n)]
    rp = [(rt[i] + t1[i][:, :LANES]).astype(BF16) for i in range(n)]
    t2 = [_mm(bk_end[i].T,
              jnp.concatenate([x[i], jnp.concatenate([zeros, v[i].astype(F32)], axis=1)], 0))
          for i in range(n)]
    g_low = [jnp.where(same_head, u[:, :LANES], 0.0).astype(BF16) for u in t2]
    h_new = [jnp.where(same_head, u[:, LANES:], 0.0) for u in t2]
    g_col = [jnp.exp(jnp.sum(jnp.where(diag, jnp.broadcast_to(g, (LANES, LANES)), 0.0),
                             axis=1, keepdims=True)) for g in g_tot]

    s = [s_ref[p] for p in range(N_PAIRS)]
    for ci, c in enumerate(order):
        sb = [u.astype(BF16) for u in s]
        for p in range(N_PAIRS):
            i = ci * N_PAIRS + p
            y_ref[0, c * c_len:(c + 1) * c_len, p * LANES:(p + 1) * LANES] = (
                _mm(rp[i], sb[p]) + y0[i]).astype(BF16)
        s = [s[p] * g_col[ci * N_PAIRS + p] + _mm(g_low[ci * N_PAIRS + p], sb[p]) + h_new[ci * N_PAIRS + p]
             for p in range(N_PAIRS)]
    for p in range(N_PAIRS):
        s_ref[p] = s[p]


def _rwkv_scan(r, v, kkn, lw, krep, bvec, tri, direction, tb=256):
    b, t, _ = r.shape
    nt = t // tb
    reverse = direction == 1
    tmap = (lambda i: nt - 1 - i) if reverse else (lambda i: i)
    one = pl.BlockSpec((1, tb, WIDTH), lambda bb, i: (bb, tmap(i), 0))
    two = pl.BlockSpec((1, 1, tb, WIDTH), lambda bb, i: (direction, bb, tmap(i), 0))
    return pl.pallas_call(
        functools.partial(_rwkv_scan_kernel, nc=tb // RWKV_CHUNK, reverse=reverse),
        grid=(b, nt),
        in_specs=[one, one, one, two, two, two,
                  pl.BlockSpec((RWKV_CHUNK, RWKV_CHUNK), lambda bb, i: (0, 0))],
        out_specs=one,
        out_shape=jax.ShapeDtypeStruct((b, t, WIDTH), BF16),
        scratch_shapes=[pltpu.VMEM((N_PAIRS, LANES, LANES), F32)],
        compiler_params=_cparams("parallel", "arbitrary"),
        name=f"rwkv_scan_{'bwd' if reverse else 'fwd'}",
    )(r, v, kkn, lw, krep, bvec, tri)


def _even_out_kernel(h_ref, o1, l1, o4, l4, o16, l16, yf, yb, bonus, g, lng, lnb, ones_ref, w_ref,
                     out_ref, tok_sc, *, tm):
    n_tiles = WIDTH // LANES
    for slot, (dil, ref) in enumerate(((4, o4), (4, l4), (16, o16), (16, l16))):
        for rho in range(dil):
            for c in range(n_tiles):
                tok_sc[slot * n_tiles + c, pl.ds(rho, tm // dil, stride=dil), :] = (
                    ref[0, rho, :, c * LANES:(c + 1) * LANES].astype(F32))
    tok = lambda slot: jnp.concatenate([tok_sc[slot * n_tiles + c] for c in range(n_tiles)], axis=1)
    la, lb, lc = l1[0, 0], tok(1), tok(3)
    mx = jnp.maximum(jnp.maximum(la, lb), lc)
    wa, wb, wc = jnp.exp(la - mx), jnp.exp(lb - mx), jnp.exp(lc - mx)
    y_a = (wa * o1[0, 0].astype(F32) + wb * tok(0) + wc * tok(2)) / (wa + wb + wc)
    ones = ones_ref[...]
    inv = 1.0 / HEAD_DIM
    y = yf[0].astype(F32) + yb[0].astype(F32)
    yc = y - _group_sum(y, ones) * inv
    var = _group_sum(yc * yc, ones) * inv
    yn = yc * lax.rsqrt(var + RWKV_GN_EPS) * lng[...] + lnb[...]
    y_b = (yn + bonus[0].astype(F32)) * g[0].astype(F32)
    out_ref[0] = (h_ref[0] + _mm(y_a, w_ref[0:WIDTH, :]) + _mm(y_b, w_ref[WIDTH:, :]))


def _even_out(h, branch, rwkv_parts, lng, lnb, ones_bd, w, tm=256):
    b, t, _ = h.shape
    const = lambda shape: pl.BlockSpec(shape, lambda bb, i: (0, 0))
    row = lambda n: pl.BlockSpec((1, tm, n), lambda bb, i: (bb, i, 0))
    res = lambda d: pl.BlockSpec((1, d, tm // d, WIDTH), lambda bb, i: (bb, 0, i, 0))
    dils = [d for _, d in DILATED_BRANCHES]
    return pl.pallas_call(
        functools.partial(_even_out_kernel, tm=tm),
        grid=(b, t // tm),
        in_specs=[row(D_MODEL)] + [res(d) for d in dils for _ in range(2)] + [row(WIDTH)] * 4
        + [const((1, WIDTH)), const((1, WIDTH)), const((WIDTH, WIDTH)), const((D_MODEL, D_MODEL))],
        out_specs=row(D_MODEL),
        out_shape=jax.ShapeDtypeStruct((b, t, D_MODEL), F32),
        scratch_shapes=[pltpu.VMEM((4 * WIDTH // LANES, tm, LANES), F32)],
        compiler_params=_cparams("parallel", "parallel"),
        name="even_out",
    )(h, *branch, *rwkv_parts, lng, lnb, ones_bd, w)


def _mlp_kernel(h_ref, g_ref, w1_ref, w2_ref, out_ref, u_ref):
    f = pl.program_id(1)

    @pl.when(f == 0)
    def _():
        x = h_ref[...]
        u_ref[...] = _rms_rows(x, g_ref[...]).astype(BF16)
        out_ref[...] = x

    hdn = jnp.dot(u_ref[...], w1_ref[...], preferred_element_type=F32)
    hdn = jnp.square(jnp.maximum(hdn, 0.0))
    out_ref[...] += jnp.dot(hdn.astype(BF16), w2_ref[...], preferred_element_type=F32)


def _mlp(h2, g, w1, w2, tm=1024, tf=1024):
    m = h2.shape[0]
    return pl.pallas_call(
        _mlp_kernel,
        grid=(m // tm, D_FF // tf),
        in_specs=[pl.BlockSpec((tm, D_MODEL), lambda i, f: (i, 0)),
                  pl.BlockSpec((1, D_MODEL), lambda i, f: (0, 0)),
                  pl.BlockSpec((D_MODEL, tf), lambda i, f: (0, f)),
                  pl.BlockSpec((tf, D_MODEL), lambda i, f: (f, 0))],
        out_specs=pl.BlockSpec((tm, D_MODEL), lambda i, f: (i, 0)),
        out_shape=jax.ShapeDtypeStruct((m, D_MODEL), F32),
        scratch_shapes=[pltpu.VMEM((tm, D_MODEL), BF16)],
        compiler_params=_cparams("parallel", "arbitrary"),
        name="mlp",
    )(h2, g, w1, w2)


def _odd_in_kernel(x_ref, g_ref, wr_ref, ws_ref, wd_ref, ret_out, xz_out, dt_out):
    ub = _rms_rows(x_ref[...], g_ref[...]).astype(BF16)
    ret_out[...] = jnp.dot(ub, wr_ref[...], preferred_element_type=F32).astype(BF16)
    xz_out[...] = jnp.dot(ub, ws_ref[...], preferred_element_type=F32).astype(BF16)
    dt_out[...] = jnp.dot(ub, wd_ref[...], preferred_element_type=F32)


def _odd_in(x2, g, wr, ws, wd, tm=512):
    m = x2.shape[0]
    const = lambda shape: pl.BlockSpec(shape, lambda i: (0, 0))
    row = lambda n: pl.BlockSpec((tm, n), lambda i: (i, 0))
    n_xz = SSD_XBC + WIDTH
    return pl.pallas_call(
        _odd_in_kernel,
        grid=(m // tm,),
        in_specs=[row(D_MODEL), const((1, D_MODEL)), const((D_MODEL, 4 * WIDTH)),
                  const((D_MODEL, n_xz)), const((D_MODEL, LANES))],
        out_specs=[row(4 * WIDTH), row(n_xz), row(LANES)],
        out_shape=[jax.ShapeDtypeStruct((m, 4 * WIDTH), BF16),
                   jax.ShapeDtypeStruct((m, n_xz), BF16),
                   jax.ShapeDtypeStruct((m, LANES), F32)],
        compiler_params=_cparams("parallel"),
        name="odd_in",
    )(x2, g, wr, ws, wd)


def _rotary_pair(u, cos, sin_signed):
    first = (_iota(u.shape, 1) & (HEAD_DIM - 1)) < HEAD_DIM // 2
    swapped = jnp.where(first, pltpu.roll(u, LANES - HEAD_DIM // 2, 1),
                        pltpu.roll(u, HEAD_DIM // 2, 1))
    return u * cos + swapped * sin_signed


def _ret_kernel(q_ref, k_ref, v_ref, cos_ref, sin_ref, lg64_ref, lg128_ref, y_ref, s_ref,
                *, nc, reverse):
    c_len = LIN_CHUNK

    @pl.when(pl.program_id(1) == 0)
    def _():
        s_ref[...] = jnp.zeros_like(s_ref)

    l_io = _iota((c_len, 2 * c_len), 0)
    s_io = _iota((c_len, 2 * c_len), 1) & (c_len - 1)
    delta = (s_io - l_io) if reverse else (l_io - s_io)
    causal = delta >= 0
    delta_f = jnp.where(causal, delta, 0).astype(F32)
    pos = _iota((c_len, LANES), 0)
    if reverse:
        pos = c_len - 1 - pos
    pos = pos.astype(F32)
    r2 = _iota((LANES, LANES), 0)
    c2 = _iota((LANES, LANES), 1)
    same_head = (r2 & HEAD_DIM) == (c2 & HEAD_DIM)

    decay, from_start, to_end, whole = [], [], [], []
    for p in range(N_PAIRS):
        lg64 = lg64_ref[0, :, p * LANES:(p + 1) * LANES]
        lg128 = lg128_ref[0, :, 2 * p * c_len:2 * (p + 1) * c_len]
        decay.append(jnp.where(causal, jnp.exp(lg128 * delta_f), 0.0))
        from_start.append(jnp.exp(lg64 * (pos + 1.0)))
        to_end.append(jnp.exp(lg64 * (float(c_len - 1) - pos)))
        whole.append(jnp.exp(lg64 * float(c_len)))

    order = _scan_order(nc, reverse)
    inst = [(p, c) for c in order for p in range(N_PAIRS)]
    q, k, v = [], [], []
    for p, c in inst:
        rows = slice(c * c_len, (c + 1) * c_len)
        cols = slice(p * LANES, (p + 1) * LANES)
        cos = cos_ref[rows, :]
        sin = sin_ref[rows, :]
        q.append(_rotary_pair(q_ref[0, rows, cols].astype(F32), cos, sin))
        k.append(_rotary_pair(k_ref[0, rows, cols].astype(F32), cos, sin) * (HEAD_DIM ** -0.5))
        v.append(v_ref[0, rows, cols])
    sc = [(_mm_nt(q[i], _bd(k[i])) * decay[p]).astype(BF16) for i, (p, c) in enumerate(inst)]
    y_intra = [_mm(sc[i], _bd(v[i])) for i in range(len(inst))]
    upd = [jnp.where(same_head, _mm((k[i] * to_end[p]).T, v[i]), 0.0) for i, (p, c) in enumerate(inst)]
    qs = [(q[i] * from_start[p]).astype(BF16) for i, (p, c) in enumerate(inst)]

    s = [s_ref[p] for p in range(N_PAIRS)]
    for ci, c in enumerate(order):
        for p in range(N_PAIRS):
            i = ci * N_PAIRS + p
            y_ref[0, c * c_len:(c + 1) * c_len, p * LANES:(p + 1) * LANES] = (
                y_intra[i] + _mm(qs[i], s[p])).astype(BF16)
            s[p] = s[p] * whole[p] + upd[i]
    for p in range(N_PAIRS):
        s_ref[p] = s[p]


def _retention(ret, cos, sin, lg64, lg128, direction, tb=256):
    b, t, _ = ret.shape
    nt = t // tb
    reverse = direction == 1
    tmap = (lambda i: nt - 1 - i) if reverse else (lambda i: i)
    col = lambda blk: pl.BlockSpec((1, tb, WIDTH), lambda bb, i: (bb, tmap(i), blk))
    tab = pl.BlockSpec((tb, LANES), lambda bb, i: (tmap(i), 0))
    return pl.pallas_call(
        functools.partial(_ret_kernel, nc=tb // LIN_CHUNK, reverse=reverse),
        grid=(b, nt),
        in_specs=[col(0), col(1), col(2), tab, tab,
                  pl.BlockSpec((1, 1, WIDTH), lambda bb, i: (direction, 0, 0)),
                  pl.BlockSpec((1, 1, N_HEADS * LIN_CHUNK), lambda bb, i: (direction, 0, 0))],
        out_specs=col(0),
        out_shape=jax.ShapeDtypeStruct((b, t, WIDTH), BF16),
        scratch_shapes=[pltpu.VMEM((N_PAIRS, LANES, LANES), F32)],
        compiler_params=_cparams("parallel", "arbitrary"),
        name=f"retention_{'bwd' if reverse else 'fwd'}",
    )(ret, ret, ret, cos, sin, lg64, lg128)


def _ssd_conv_kernel(x_ref, xp_ref, xn_ref, w_ref, b_ref, xs_out, bc_out, *, tm):
    i = pl.program_id(1)
    nt = pl.num_programs(1)
    prev = jnp.where(i > 0, xp_ref[0].astype(F32), 0.0)
    nxt = jnp.where(i < nt - 1, xn_ref[0].astype(F32), 0.0)
    xe = jnp.concatenate([prev, x_ref[0].astype(F32), nxt], axis=0)
    ext = tm + 2 * HALO_ROWS
    pad = SSD_CONV // 2
    acc = jnp.zeros((tm, SSD_XBC), F32) + b_ref[...]
    for j in range(SSD_CONV):
        shift = (pad - j) % ext
        xj = xe if shift == 0 else pltpu.roll(xe, shift, 0)
        acc = acc + xj[HALO_ROWS:HALO_ROWS + tm] * w_ref[j:j + 1, :]
    y = _silu(acc).astype(BF16)
    xs_out[0] = y[:, :WIDTH]
    bc_out[0] = y[:, WIDTH:]


def _ssd_conv(xz, w, bias, tm=256):
    b, t, _ = xz.shape
    nt = t // tm
    per = tm // HALO_ROWS
    const = lambda shape: pl.BlockSpec(shape, lambda bb, i: (0, 0))
    main = pl.BlockSpec((1, tm, SSD_XBC), lambda bb, i: (bb, i, 0))
    prev = pl.BlockSpec((1, HALO_ROWS, SSD_XBC), lambda bb, i: (bb, jnp.maximum(i * per - 1, 0), 0))
    nxt = pl.BlockSpec((1, HALO_ROWS, SSD_XBC),
                       lambda bb, i: (bb, jnp.minimum((i + 1) * per, t // HALO_ROWS - 1), 0))
    out = pl.BlockSpec((1, tm, WIDTH), lambda bb, i: (bb, i, 0))
    return pl.pallas_call(
        functools.partial(_ssd_conv_kernel, tm=tm),
        grid=(b, nt),
        in_specs=[main, prev, nxt, const((8, SSD_XBC)), const((1, SSD_XBC))],
        out_specs=[out, out],
        out_shape=[jax.ShapeDtypeStruct((b, t, WIDTH), BF16)] * 2,
        compiler_params=_cparams("parallel", "parallel"),
        name="ssd_conv",
    )(xz, xz, xz, w, bias)


def _ssd_kernel(xs_ref, bc_ref, dtc_ref, dtr_ref, bias_c_ref, bias_r_ref, nega_c_ref, nega_r_ref,
                tri_ref, trit_ref, y_ref, s_ref, *, nc, reverse):
    c_len = LIN_CHUNK
    per_group = N_HEADS // SSD_GROUPS
    gw = WIDTH // SSD_GROUPS

    @pl.when(pl.program_id(1) == 0)
    def _():
        s_ref[...] = jnp.zeros_like(s_ref)

    tri = tri_ref[...]
    trit = trit_ref[...]
    dts_c = _softplus(dtc_ref[0, 0] + bias_c_ref[0])
    la_c = dts_c * nega_c_ref[0]
    dts_r = _softplus(dtr_ref[0, 0] + bias_r_ref[0])
    la_r = dts_r * nega_r_ref[0]
    l_io = _iota((c_len, c_len), 0)
    s_io = _iota((c_len, c_len), 1)
    causal = (s_io >= l_io) if reverse else (s_io <= l_io)
    lo = _lane_lo((c_len, LANES))

    def by_head128(cols, g):
        return jnp.concatenate([jnp.broadcast_to(cols[:, g * per_group + h:g * per_group + h + 1],
                                                 (c_len, LANES)) for h in range(per_group)], axis=1)

    def to64(wide):
        h = [wide[:, j * LANES:(j + 1) * LANES] for j in range(per_group)]
        return jnp.concatenate([jnp.where(lo, h[0], h[1]), jnp.where(lo, h[2], h[3])], axis=1)

    order = _scan_order(nc, reverse)
    inst = [(g, c) for c in order for g in range(SSD_GROUPS)]
    n = len(inst)
    rows_of = lambda c: slice(c * c_len, (c + 1) * c_len)
    cum = [_mm_exact_left(tri, by_head128(la_c[rows_of(c)], g)) for g, c in inst]
    cum_r = {c: _mm_exact_right(la_r[:, rows_of(c)], trit) for c in order}
    last = [u[0:1] if reverse else u[c_len - 1:c_len] for u in cum]
    bm = [bc_ref[0, rows_of(c), g * SSD_STATE:(g + 1) * SSD_STATE] for g, c in inst]
    cm = [bc_ref[0, rows_of(c), (SSD_GROUPS + g) * SSD_STATE:(SSD_GROUPS + g + 1) * SSD_STATE]
          for g, c in inst]
    xdt = [xs_ref[0, rows_of(c), g * gw:(g + 1) * gw].astype(F32) * to64(by_head128(dts_c[rows_of(c)], g))
           for g, c in inst]
    gm = [_mm_nt(cm[i], bm[i]) for i in range(n)]
    y_intra = []
    for i, (g, c) in enumerate(inst):
        parts = []
        for pp in range(per_group // 2):
            wide = []
            for hh in (2 * pp, 2 * pp + 1):
                head = g * per_group + hh
                diff = cum[i][:, hh * LANES:(hh + 1) * LANES] - cum_r[c][head:head + 1, :]
                dec = jnp.where(causal, jnp.exp(jnp.where(causal, diff, 0.0)), 0.0)
                wide.append((gm[i] * dec).astype(BF16))
            parts.append(_mm(jnp.concatenate(wide, axis=1), _bd(xdt[i][:, pp * LANES:(pp + 1) * LANES])))
        y_intra.append(jnp.concatenate(parts, axis=1))
    from_start = [to64(jnp.exp(u)) for u in cum]
    upd = [_mm(bm[i].astype(F32).T, xdt[i] * to64(jnp.exp(last[i] - cum[i]))) for i in range(n)]
    whole = [to64(jnp.exp(u)) for u in last]

    s = [s_ref[g] for g in range(SSD_GROUPS)]
    for ci, c in enumerate(order):
        for g in range(SSD_GROUPS):
            i = ci * SSD_GROUPS + g
            y_ref[0, rows_of(c), g * gw:(g + 1) * gw] = (
                y_intra[i] + _mm(cm[i], s[g]) * from_start[i]).astype(BF16)
            s[g] = s[g] * whole[i] + upd[i]
    for g in range(SSD_GROUPS):
        s_ref[g] = s[g]


def _ssd_scan(xs, bc, dtc, dtr, sp, direction, tb=256):
    b, t, _ = xs.shape
    nt = t // tb
    reverse = direction == 1
    tmap = (lambda i: nt - 1 - i) if reverse else (lambda i: i)
    row = pl.BlockSpec((1, tb, WIDTH), lambda bb, i: (bb, tmap(i), 0))
    par = lambda shape: pl.BlockSpec((1,) + shape, lambda bb, i: (direction, 0, 0))
    tri = pl.BlockSpec((LIN_CHUNK, LIN_CHUNK), lambda bb, i: (0, 0))
    return pl.pallas_call(
        functools.partial(_ssd_kernel, nc=tb // LIN_CHUNK, reverse=reverse),
        grid=(b, nt),
        in_specs=[row, row,
                  pl.BlockSpec((1, 1, tb, N_HEADS), lambda bb, i: (direction, bb, tmap(i), 0)),
                  pl.BlockSpec((1, 1, N_HEADS, tb), lambda bb, i: (direction, bb, 0, tmap(i))),
                  par((1, N_HEADS)), par((N_HEADS, 1)), par((1, N_HEADS)), par((N_HEADS, 1)),
                  tri, tri],
        out_specs=row,
        out_shape=jax.ShapeDtypeStruct((b, t, WIDTH), BF16),
        scratch_shapes=[pltpu.VMEM((SSD_GROUPS, SSD_STATE, WIDTH // SSD_GROUPS), F32)],
        compiler_params=_cparams("parallel", "arbitrary"),
        name=f"ssd_scan_{'bwd' if reverse else 'fwd'}",
    )(xs, bc, dtc, dtr, sp["bias_c"], sp["bias_r"], sp["nega_c"], sp["nega_r"],
      sp["tri_rev" if reverse else "tri_fwd"], sp["trit_rev" if reverse else "trit_fwd"])


def _odd_out_kernel(h_ref, rf, rb, gate, sf, sb, xs, z, gng, gnb, dsk, ng, ones_ref, w_ref, out_ref):
    f32 = lambda ref: ref[...].astype(F32)
    ones = ones_ref[...]
    inv = 1.0 / HEAD_DIM
    y = f32(rf) + f32(rb)
    yc = y - _group_sum(y, ones) * inv
    var = _group_sum(yc * yc, ones) * inv
    y_c = (yc * lax.rsqrt(var + GN_EPS) * gng[...] + gnb[...]) * _silu(f32(gate))
    yd = (f32(sf) + f32(sb) + dsk[...] * f32(xs)) * _silu(f32(z))
    gw = WIDTH // SSD_GROUPS
    halves = []
    for gi in range(SSD_GROUPS):
        yg = yd[:, gi * gw:(gi + 1) * gw]
        halves.append(yg * lax.rsqrt(jnp.mean(yg * yg, -1, keepdims=True) + NORM_EPS))
    y_d = jnp.concatenate(halves, axis=1) * ng[...]
    out_ref[...] = h_ref[...] + _mm(y_c, w_ref[0:WIDTH, :]) + _mm(y_d, w_ref[WIDTH:, :])


def _odd_out(h2, rf, rb, ret2, sf, sb, xs, xz2, gng, gnb, dsk, ng, ones_bd, w, tm=256):
    m = h2.shape[0]
    const = lambda shape: pl.BlockSpec(shape, lambda i: (0, 0))
    row = lambda n: pl.BlockSpec((tm, n), lambda i: (i, 0))
    vec = const((1, WIDTH))
    return pl.pallas_call(
        _odd_out_kernel,
        grid=(m // tm,),
        in_specs=[row(D_MODEL), row(WIDTH), row(WIDTH),
                  pl.BlockSpec((tm, WIDTH), lambda i: (i, 3)),
                  row(WIDTH), row(WIDTH), row(WIDTH),
                  pl.BlockSpec((tm, WIDTH), lambda i: (i, 2)),
                  vec, vec, vec, vec, const((WIDTH, WIDTH)), const((D_MODEL, D_MODEL))],
        out_specs=row(D_MODEL),
        out_shape=jax.ShapeDtypeStruct((m, D_MODEL), F32),
        compiler_params=_cparams("parallel"),
        name="odd_out",
    )(h2, rf, rb, ret2, sf, sb, xs, xz2, gng, gnb, dsk, ng, ones_bd, w)


def _tri(n, reverse):
    i = np.arange(n)
    m = (i[None, :] >= i[:, None]) if reverse else (i[None, :] <= i[:, None])
    return m.astype(np.float32)


def _prepare(norm_mix, norm_mlp, mlp_w1, mlp_w2, even_in_w, even_out_w, attn_q_gain, attn_k_gain,
             rwkv_mu_prev, rwkv_mu_next, rwkv_w0, rwkv_w2, rwkv_a0, rwkv_a2, rwkv_g2, rwkv_k_k,
             rwkv_k_a, rwkv_r_k, rwkv_ln_g, rwkv_ln_b, odd_in_w, odd_out_w, ret_decay_exp, ret_gn_g,
             ret_gn_b, ssd_conv_w, ssd_conv_b, ssd_dt_bias, ssd_a_log, ssd_d, ssd_norm_g):
    row = lambda v: v.reshape(1, -1).astype(F32)
    heads = np.arange(WIDTH) // HEAD_DIM
    ones_bd = jnp.asarray(heads[:, None] == heads[None, :], BF16)
    p = {"ones_bd": ones_bd, "norm_mix": norm_mix, "norm_mlp": norm_mlp,
         "mlp_w1": mlp_w1.astype(BF16), "mlp_w2": mlp_w2.astype(BF16)}

    w_in = even_in_w[0]
    p["even_wa"] = w_in[:, :3 * WIDTH].astype(BF16)
    p["even_wr"] = w_in[:, 3 * WIDTH:].astype(BF16)
    p["even_out_w"] = even_out_w[0].astype(BF16)
    p["q_gain"] = row(jnp.tile(attn_q_gain[0], N_HEADS))
    p["k_gain"] = row(jnp.tile(attn_k_gain[0], N_HEADS))

    def two_dir_lowrank(w):
        z = jnp.zeros_like(w[0])
        return jnp.concatenate([jnp.concatenate([w[0], z], 1), jnp.concatenate([z, w[1]], 1)], 0)

    p["rwkv"] = {
        "mu_prev": row(rwkv_mu_prev[0]), "mu_next": row(rwkv_mu_next[0]),
        "w0": row(rwkv_w0[0]), "w2": two_dir_lowrank(rwkv_w2[0]).astype(BF16),
        "a0": row(rwkv_a0[0]), "a2": two_dir_lowrank(rwkv_a2[0]).astype(BF16),
        "g2": rwkv_g2[0].astype(BF16), "k_k": row(rwkv_k_k[0]), "k_a": row(rwkv_k_a[0]),
        "r_k": row(rwkv_r_k[0]), "ones_bd": ones_bd,
    }
    p["rwkv_tri"] = [jnp.asarray(_tri(RWKV_CHUNK, False), BF16), jnp.asarray(_tri(RWKV_CHUNK, True), BF16)]
    p["rwkv_ln_g"] = row(rwkv_ln_g[0])
    p["rwkv_ln_b"] = row(rwkv_ln_b[0])

    w_odd = odd_in_w[0]
    p["odd_wr"] = w_odd[:, :4 * WIDTH].astype(BF16)
    z_w = w_odd[:, 4 * WIDTH:5 * WIDTH]
    xbc_w = w_odd[:, 5 * WIDTH:5 * WIDTH + SSD_XBC]
    dt_w = w_odd[:, 5 * WIDTH + SSD_XBC:]
    p["odd_ws"] = jnp.concatenate([xbc_w, z_w], axis=1).astype(BF16)
    p["odd_wd"] = jnp.concatenate([dt_w, jnp.zeros((D_MODEL, LANES - 2 * N_HEADS), F32)], 1).astype(BF16)
    p["odd_out_w"] = odd_out_w[0].astype(BF16)

    log_gamma = jnp.log1p(-jnp.exp2(-ret_decay_exp[0].astype(F32)))
    p["lg64"] = jnp.repeat(log_gamma, HEAD_DIM, axis=1).reshape(2, 1, WIDTH)
    p["lg128"] = jnp.repeat(log_gamma, LIN_CHUNK, axis=1).reshape(2, 1, N_HEADS * LIN_CHUNK)
    p["ret_gn_g"] = row(ret_gn_g[0])
    p["ret_gn_b"] = row(ret_gn_b[0])

    p["conv_w"] = jnp.concatenate([ssd_conv_w[0], jnp.zeros((8 - SSD_CONV, SSD_XBC), F32)], 0)
    p["conv_b"] = row(ssd_conv_b[0])
    bias = ssd_dt_bias[0].astype(F32)
    nega = -jnp.exp(ssd_a_log[0].astype(F32))
    p["ssd"] = {
        "bias_c": bias[:, None, :], "nega_c": nega[:, None, :],
        "bias_r": bias[:, :, None], "nega_r": nega[:, :, None],
        "tri_fwd": jnp.asarray(_tri(LIN_CHUNK, False), BF16),
        "tri_rev": jnp.asarray(_tri(LIN_CHUNK, True), BF16),
        "trit_fwd": jnp.asarray(_tri(LIN_CHUNK, False).T, BF16),
        "trit_rev": jnp.asarray(_tri(LIN_CHUNK, True).T, BF16),
    }
    p["ssd_d"] = row(jnp.repeat(ssd_d[0], HEAD_DIM))
    p["ssd_norm_g"] = row(ssd_norm_g[0])
    return p


def _rope_tables(t):
    half = HEAD_DIM // 2
    inv = ROPE_BASE ** (-jnp.arange(half, dtype=F32) / half)
    ang = jnp.arange(t, dtype=F32)[:, None] * inv[None, :]
    cos = jnp.tile(jnp.cos(ang), (1, LANES // half))
    sin = jnp.sin(ang)
    sin_signed = jnp.tile(jnp.concatenate([-sin, sin], axis=1), (1, LANES // HEAD_DIM))
    return cos, sin_signed


def _even_mixer(h2, b, t, p):
    h = h2.reshape(b, t, D_MODEL)
    *qkvs, rcols = _even_in(h, p["norm_mix"][0:1], p["even_wa"], p["even_wr"], p["ones_bd"],
                            p["q_gain"], p["k_gain"])
    branch = []
    for qkv, (_, dil) in zip(qkvs, DILATED_BRANCHES):
        branch.extend(_dil_attn(qkv, dil))
    r, v, kkn, g, bonus, lw, krep, bvec = _rwkv_prep(rcols, p["rwkv"])
    ys = [_rwkv_scan(r, v, kkn, lw, krep, bvec, p["rwkv_tri"][d], d) for d in range(2)]
    out = _even_out(h, branch, [ys[0], ys[1], bonus, g], p["rwkv_ln_g"], p["rwkv_ln_b"],
                    p["ones_bd"], p["even_out_w"])
    return out.reshape(b * t, D_MODEL)


def _odd_mixer(h2, b, t, p):
    ret2, xz2, dt2 = _odd_in(h2, p["norm_mix"][1:2], p["odd_wr"], p["odd_ws"], p["odd_wd"])
    ret = ret2.reshape(b, t, -1)
    xz = xz2.reshape(b, t, -1)
    cos, sin = _rope_tables(t)
    rets = [_retention(ret, cos, sin, p["lg64"], p["lg128"], d) for d in range(2)]
    xs, bc = _ssd_conv(xz, p["conv_w"], p["conv_b"])
    dt = dt2[:, :2 * N_HEADS].reshape(b, t, 2, N_HEADS)
    dtc = dt.transpose(2, 0, 1, 3)
    dtr = dt.transpose(2, 0, 3, 1)
    ssds = [_ssd_scan(xs, bc, dtc, dtr, p["ssd"], d) for d in range(2)]
    flat = lambda u: u.reshape(b * t, -1)
    return _odd_out(h2, flat(rets[0]), flat(rets[1]), ret2, flat(ssds[0]), flat(ssds[1]), flat(xs),
                    xz2, p["ret_gn_g"], p["ret_gn_b"], p["ssd_d"], p["ssd_norm_g"], p["ones_bd"],
                    p["odd_out_w"])


def _trunk(x, p):
    b, t, _ = x.shape
    h2 = x.reshape(b * t, D_MODEL)
    h2 = _even_mixer(h2, b, t, p)
    h2 = _mlp(h2, p["norm_mlp"][0:1], p["mlp_w1"][0], p["mlp_w2"][0])
    h2 = _odd_mixer(h2, b, t, p)
    h2 = _mlp(h2, p["norm_mlp"][1:2], p["mlp_w1"][1], p["mlp_w2"][1])
    return h2.reshape(b, t, D_MODEL)


def kernel(x_prompt, x_sample, norm_mix, norm_mlp, mlp_w1, mlp_w2, even_in_w, even_out_w, attn_q_gain, attn_k_gain, rwkv_mu_prev, rwkv_mu_next, rwkv_w0, rwkv_w2, rwkv_a0, rwkv_a2, rwkv_g2, rwkv_k_k, rwkv_k_a, rwkv_r_k, rwkv_ln_g, rwkv_ln_b, odd_in_w, odd_out_w, ret_decay_exp, ret_gn_g, ret_gn_b, ssd_conv_w, ssd_conv_b, ssd_dt_bias, ssd_a_log, ssd_d, ssd_norm_g):
    p = _prepare(norm_mix, norm_mlp, mlp_w1, mlp_w2, even_in_w, even_out_w, attn_q_gain, attn_k_gain,
                 rwkv_mu_prev, rwkv_mu_next, rwkv_w0, rwkv_w2, rwkv_a0, rwkv_a2, rwkv_g2, rwkv_k_k,
                 rwkv_k_a, rwkv_r_k, rwkv_ln_g, rwkv_ln_b, odd_in_w, odd_out_w, ret_decay_exp,
                 ret_gn_g, ret_gn_b, ssd_conv_w, ssd_conv_b, ssd_dt_bias, ssd_a_log, ssd_d, ssd_norm_g)
    return (_trunk(x_prompt, p), _trunk(x_sample, p))
```

```python
import functools
import math

import numpy as np
import jax
import jax.numpy as jnp
from jax import lax
from jax.experimental import pallas as pl
from jax.experimental.pallas import tpu as pltpu

F32 = jnp.float32
BF16 = jnp.bfloat16

D_MODEL = 1024
HEAD_DIM = 64
N_HEADS = 8
WIDTH = N_HEADS * HEAD_DIM
N_PAIRS = N_HEADS // 2
LANES = 128
HALO_ROWS = 16
DILATED_BRANCHES = ((128, 1), (512, 4), (2048, 16))
ATT_R = 64
RWKV_DECAY_RANK = 64
RWKV_ICL_RANK = 64
RWKV_GATE_RANK = 128
RWKV_COLS = 3 * WIDTH + 2 * RWKV_DECAY_RANK + 2 * RWKV_ICL_RANK + RWKV_GATE_RANK
SSD_STATE = 128
SSD_GROUPS = 2
SSD_CONV = 5
SSD_XBC = WIDTH + 2 * SSD_GROUPS * SSD_STATE
D_FF = 4 * D_MODEL
NORM_EPS = 1e-6
GN_EPS = 1e-5
RWKV_GN_EPS = 64e-5
ROPE_BASE = 10000.0
RWKV_CHUNK = 64
LIN_CHUNK = 128
NEG_BIG = -1e30
VMEM_LIMIT = 56 * 1024 * 1024


def _cparams(*sem):
    return pltpu.CompilerParams(dimension_semantics=sem, vmem_limit_bytes=VMEM_LIMIT)


def _iota(shape, dim):
    return lax.broadcasted_iota(jnp.int32, shape, dim)


def _lane_lo(shape):
    return (_iota(shape, len(shape) - 1) & HEAD_DIM) == 0


def _bd(x):
    lo = _lane_lo(x.shape)
    zero = jnp.zeros_like(x)
    return jnp.concatenate([jnp.where(lo, x, zero), jnp.where(lo, zero, x)], axis=0)


def _mm(a, b):
    return jnp.dot(a.astype(BF16), b.astype(BF16), preferred_element_type=F32)


def _mm_nt(a, b):
    return lax.dot_general(a.astype(BF16), b.astype(BF16), (((1,), (1,)), ((), ())),
                           preferred_element_type=F32)


def _split_bf16(x, terms):
    parts = []
    for _ in range(terms - 1):
        hi = x.astype(BF16)
        parts.append(hi)
        x = x - hi.astype(F32)
    parts.append(x.astype(BF16))
    return parts


def _mm_exact_left(t, x, terms=3):
    return sum(jnp.dot(t, piece, preferred_element_type=F32) for piece in _split_bf16(x, terms))


def _mm_exact_right(x, t, terms=3):
    return sum(jnp.dot(piece, t, preferred_element_type=F32) for piece in _split_bf16(x, terms))


def _group_sum(x, ones_bd):
    return jnp.dot(x.astype(BF16), ones_bd, preferred_element_type=F32)


def _rms_rows(x, g):
    return x * lax.rsqrt(jnp.mean(x * x, -1, keepdims=True) + NORM_EPS) * g


def _sigmoid(x):
    return 0.5 * jnp.tanh(0.5 * x) + 0.5


def _silu(x):
    return x * _sigmoid(x)


def _softplus(x):
    return jnp.maximum(x, 0.0) + jnp.log1p(jnp.exp(-jnp.abs(x)))


def _scan_order(nc, reverse):
    return list(range(nc - 1, -1, -1) if reverse else range(nc))


def _even_in_kernel(x_ref, g_ref, wa_ref, wr_ref, ones_ref, qg_ref, kg_ref,
                    qkv1_out, qkv4_out, qkv16_out, r_out, qkv_sc, *, tm):
    ub = _rms_rows(x_ref[0], g_ref[...]).astype(BF16)
    a = jnp.dot(ub, wa_ref[...], preferred_element_type=F32)
    q = a[:, :WIDTH]
    k = a[:, WIDTH:2 * WIDTH]
    ones = ones_ref[...]
    inv = 1.0 / HEAD_DIM
    qn = q * lax.rsqrt(_group_sum(q * q, ones) * inv + NORM_EPS) * qg_ref[...]
    kn = k * lax.rsqrt(_group_sum(k * k, ones) * inv + NORM_EPS) * kg_ref[...]
    qkv = jnp.concatenate([qn * (HEAD_DIM ** -0.5), kn, a[:, 2 * WIDTH:]], axis=1)
    qkv1_out[0, 0] = qkv.astype(BF16)
    n_tiles = 3 * WIDTH // LANES
    for c in range(n_tiles):
        qkv_sc[c] = qkv[:, c * LANES:(c + 1) * LANES]
    for dil, out in ((4, qkv4_out), (16, qkv16_out)):
        for rho in range(dil):
            for c in range(n_tiles):
                out[0, rho, :, c * LANES:(c + 1) * LANES] = (
                    qkv_sc[c, pl.ds(rho, tm // dil, stride=dil), :].astype(BF16))
    r_out[0] = jnp.dot(ub, wr_ref[...], preferred_element_type=F32).astype(BF16)


def _even_in(x, g, wa, wr, ones_bd, qg, kg, tm=512):
    b, t, _ = x.shape
    const = lambda shape: pl.BlockSpec(shape, lambda bb, i: (0, 0))
    row = lambda n: pl.BlockSpec((1, tm, n), lambda bb, i: (bb, i, 0))
    res = lambda d: pl.BlockSpec((1, d, tm // d, 3 * WIDTH), lambda bb, i: (bb, 0, i, 0))
    dils = [d for _, d in DILATED_BRANCHES]
    return pl.pallas_call(
        functools.partial(_even_in_kernel, tm=tm),
        grid=(b, t // tm),
        in_specs=[row(D_MODEL), const((1, D_MODEL)), const((D_MODEL, 3 * WIDTH)),
                  const((D_MODEL, RWKV_COLS)), const((WIDTH, WIDTH)),
                  const((1, WIDTH)), const((1, WIDTH))],
        out_specs=[res(d) for d in dils] + [row(RWKV_COLS)],
        out_shape=[jax.ShapeDtypeStruct((b, d, t // d, 3 * WIDTH), BF16) for d in dils]
        + [jax.ShapeDtypeStruct((b, t, RWKV_COLS), BF16)],
        scratch_shapes=[pltpu.VMEM((3 * WIDTH // LANES, tm, LANES), F32)],
        compiler_params=_cparams("parallel", "parallel"),
        name="even_in",
    )(x, g, wa, wr, ones_bd, qg, kg)


def _dil_attn_kernel(q_ref, k_ref, kl_ref, kr_ref, v_ref, vl_ref, vr_ref, o_ref, lse_ref,
                     kext, vext, *, tq, n_res, sub_len, dil):
    r = ATT_R
    i = pl.program_id(2)
    for z in range(n_res):
        kext[z, 0:r] = kl_ref[0, z]
        kext[z, r:r + tq] = k_ref[0, z]
        kext[z, r + tq:] = kr_ref[0, z]
        vext[z, 0:r] = vl_ref[0, z]
        vext[z, r:r + tq] = v_ref[0, z]
        vext[z, r + tq:] = vr_ref[0, z]

    c_io = _iota((r, 3 * r), 1)
    q_io = _iota((r, 3 * r), 0)
    rel = c_io - r - q_io
    near = jnp.abs(rel) <= r
    dist = jnp.abs(rel).astype(F32) * float(dil)
    lo = _lane_lo((r, LANES))
    near2 = jnp.concatenate([near, near], axis=0)
    dist2 = jnp.concatenate([dist, dist], axis=0)
    first_head = _iota((2 * r, 3 * r), 0) < r
    cols = [slice(p * LANES, (p + 1) * LANES) for p in range(N_PAIRS)]
    bias = [jnp.where(first_head, 2.0 ** (-(2 * p + 1)), 2.0 ** (-(2 * p + 2))) * dist2
            for p in range(N_PAIRS)]

    blocks = [(z, j) for z in range(n_res) for j in range(tq // r)]
    group = 2 if len(blocks) % 2 == 0 else 1
    for g0 in range(0, len(blocks), group):
        inst = [(z, j, p) for z, j in blocks[g0:g0 + group] for p in range(N_PAIRS)]
        valid = {}
        for z, j in blocks[g0:g0 + group]:
            kpos = i * tq + j * r + _iota((2 * r, 3 * r), 1) - r
            valid[j] = near2 & (kpos >= 0) & (kpos < sub_len)
        s = [jnp.where(valid[j],
                       _mm_nt(_bd(q_ref[0, z, j * r:(j + 1) * r, cols[p]]),
                              kext[z, j * r:(j + 3) * r, cols[p]]) - bias[p], NEG_BIG)
             for z, j, p in inst]
        m = [jnp.max(u, -1, keepdims=True) for u in s]
        e = [jnp.exp(u - mm) for u, mm in zip(s, m)]
        den = [jnp.sum(u, -1, keepdims=True) for u in e]
        pv = [_mm(e[n], vext[z, j * r:(j + 3) * r, cols[p]]) / den[n]
              for n, (z, j, p) in enumerate(inst)]
        lse = [jnp.broadcast_to(m[n] + jnp.log(den[n]), (2 * r, LANES)) for n in range(len(inst))]
        for n, (z, j, p) in enumerate(inst):
            o_ref[0, z, j * r:(j + 1) * r, cols[p]] = jnp.where(lo, pv[n][:r], pv[n][r:]).astype(BF16)
            lse_ref[0, z, j * r:(j + 1) * r, cols[p]] = jnp.where(lo, lse[n][:r], lse[n][r:])


def _dil_attn(qkv, dil, rows_per_step=512):
    b, _, sub_len, _ = qkv.shape
    tq = min(sub_len, rows_per_step)
    n_res = min(dil, rows_per_step // tq)
    nblk = sub_len // ATT_R
    per = tq // ATT_R
    main = lambda col: pl.BlockSpec((1, n_res, tq, WIDTH), lambda bb, rr, i: (bb, rr, i, col))
    left = lambda col: pl.BlockSpec((1, n_res, ATT_R, WIDTH),
                                    lambda bb, rr, i: (bb, rr, jnp.maximum(i * per - 1, 0), col))
    right = lambda col: pl.BlockSpec((1, n_res, ATT_R, WIDTH),
                                     lambda bb, rr, i: (bb, rr, jnp.minimum((i + 1) * per, nblk - 1), col))
    return pl.pallas_call(
        functools.partial(_dil_attn_kernel, tq=tq, n_res=n_res, sub_len=sub_len, dil=dil),
        grid=(b, dil // n_res, sub_len // tq),
        in_specs=[main(0), main(1), left(1), right(1), main(2), left(2), right(2)],
        out_specs=[main(0), main(0)],
        out_shape=[jax.ShapeDtypeStruct((b, dil, sub_len, WIDTH), BF16),
                   jax.ShapeDtypeStruct((b, dil, sub_len, WIDTH), F32)],
        scratch_shapes=[pltpu.VMEM((n_res, tq + 2 * ATT_R, WIDTH), BF16)] * 2,
        compiler_params=_cparams("parallel", "parallel", "parallel"),
        name=f"dil_attn_d{dil}",
    )(qkv, qkv, qkv, qkv, qkv, qkv, qkv)


def _rwkv_prep_kernel(x_ref, xp_ref, xn_ref, mup_ref, mun_ref, w0_ref, w2_ref, a0_ref, a2_ref,
                      g2_ref, kk_ref, ka_ref, rk_ref, ones_ref, trif_ref, trir_ref,
                      r_out, v_out, kkn_out, g_out, bonus_out, g_in_out, krep_out, b_out, *, tm):
    i = pl.program_id(1)
    nt = pl.num_programs(1)
    x = x_ref[0].astype(F32)
    prow = jnp.where(i > 0, xp_ref[0, HALO_ROWS - 1:HALO_ROWS, :].astype(F32), 0.0)
    nrow = jnp.where(i < nt - 1, xn_ref[0, 0:1, :].astype(F32), 0.0)
    rid = _iota((tm, 1), 0)
    prev = jnp.where(rid == 0, prow, pltpu.roll(x, 1, 0))
    nxt = jnp.where(rid == tm - 1, nrow, pltpu.roll(x, tm - 1, 0))
    xs = x + mup_ref[...] * (prev - x) + mun_ref[...] * (nxt - x)
    r = xs[:, 0:WIDTH]
    k = xs[:, WIDTH:2 * WIDTH]
    v = xs[:, 2 * WIDTH:3 * WIDTH]
    zw = xs[:, 3 * WIDTH:3 * WIDTH + LANES]
    za = xs[:, 3 * WIDTH + LANES:3 * WIDTH + 2 * LANES]
    zg = xs[:, 3 * WIDTH + 2 * LANES:]
    ones = ones_ref[...]
    kk = k * kk_ref[...]
    kkn = kk * lax.rsqrt(jnp.maximum(_group_sum(kk * kk, ones), 1e-24))
    g_out[0] = _mm(_sigmoid(zg), g2_ref[...]).astype(BF16)
    r_out[0] = r.astype(BF16)
    v_out[0] = v.astype(BF16)
    kkn_out[0] = kkn.astype(BF16)
    bonus_out[0] = (_group_sum(r * k * rk_ref[...], ones) * v).astype(BF16)
    wx = w0_ref[...] + _mm(jnp.tanh(zw), w2_ref[...])
    ax = a0_ref[...] + _mm(za, a2_ref[...])
    for d, tri_ref in enumerate((trif_ref, trir_ref)):
        cols = slice(d * WIDTH, (d + 1) * WIDTH)
        lw = -math.exp(-0.5) * _sigmoid(wx[:, cols])
        tri = tri_ref[...]
        for c in range(tm // RWKV_CHUNK):
            rows = slice(c * RWKV_CHUNK, (c + 1) * RWKV_CHUNK)
            g_in_out[d, 0, rows, :] = _mm_exact_left(tri, lw[rows], terms=2)
        a = _sigmoid(ax[:, cols])
        krep_out[d, 0] = (k * (1.0 + (a - 1.0) * ka_ref[...])).astype(BF16)
        b_out[d, 0] = (kkn * a).astype(BF16)


def _rwkv_prep(rcols, p, tri, tm=512):
    b, t, _ = rcols.shape
    nt = t // tm
    per = tm // HALO_ROWS
    const = lambda shape: pl.BlockSpec(shape, lambda bb, i: (0,) * len(shape))
    main = lambda n: pl.BlockSpec((1, tm, n), lambda bb, i: (bb, i, 0))
    dirs = pl.BlockSpec((2, 1, tm, WIDTH), lambda bb, i: (0, bb, i, 0))
    prev = pl.BlockSpec((1, HALO_ROWS, RWKV_COLS), lambda bb, i: (bb, jnp.maximum(i * per - 1, 0), 0))
    nxt = pl.BlockSpec((1, HALO_ROWS, RWKV_COLS),
                       lambda bb, i: (bb, jnp.minimum((i + 1) * per, t // HALO_ROWS - 1), 0))
    one = jax.ShapeDtypeStruct((b, t, WIDTH), BF16)
    two = lambda dt: jax.ShapeDtypeStruct((2, b, t, WIDTH), dt)
    return pl.pallas_call(
        functools.partial(_rwkv_prep_kernel, tm=tm),
        grid=(b, nt),
        in_specs=[main(RWKV_COLS), prev, nxt, const((1, RWKV_COLS)), const((1, RWKV_COLS)),
                  const((1, 2 * WIDTH)), const((LANES, 2 * WIDTH)),
                  const((1, 2 * WIDTH)), const((LANES, 2 * WIDTH)),
                  const((LANES, WIDTH)), const((1, WIDTH)), const((1, WIDTH)), const((1, WIDTH)),
                  const((WIDTH, WIDTH)), const((RWKV_CHUNK, RWKV_CHUNK)),
                  const((RWKV_CHUNK, RWKV_CHUNK))],
        out_specs=[main(WIDTH)] * 5 + [dirs] * 3,
        out_shape=[one] * 5 + [two(F32), two(BF16), two(BF16)],
        compiler_params=_cparams("parallel", "parallel"),
        name="rwkv_prep",
    )(rcols, rcols, rcols, p["mu_prev"], p["mu_next"], p["w0"], p["w2"], p["a0"], p["a2"],
      p["g2"], p["k_k"], p["k_a"], p["r_k"], p["ones_bd"], tri[0], tri[1])


def _rwkv_scan_kernel(r_ref, v_ref, kk_ref, g_ref, k_ref, b_ref, y_ref, s_ref, *, nc, reverse):
    c_len = RWKV_CHUNK

    @pl.when(pl.program_id(1) == 0)
    def _():
        s_ref[...] = jnp.zeros_like(s_ref)

    row = _iota((c_len, LANES), 0)
    li = _iota((c_len, LANES), 1) & (HEAD_DIM - 1)
    strict = (li > row) if reverse else (li < row)
    incl = (li >= row) if reverse else (li <= row)
    r2 = _iota((LANES, LANES), 0)
    c2 = _iota((LANES, LANES), 1)
    same_head = (r2 & HEAD_DIM) == (c2 & HEAD_DIM)
    diag = r2 == c2
    zeros = jnp.zeros((c_len, LANES), F32)

    order = _scan_order(nc, reverse)
    inst = [(p, c) for c in order for p in range(N_PAIRS)]
    n = len(inst)

    def tile(ref, p, c, lead=()):
        return ref[lead + (0, slice(c * c_len, (c + 1) * c_len), slice(p * LANES, (p + 1) * LANES))]

    g_in = [tile(g_ref, p, c, (0,)) for p, c in inst]
    g_tot = [g[0:1] if reverse else g[c_len - 1:c_len] for g in g_in]
    scan_first = row == (c_len - 1 if reverse else 0)
    g_ex = [jnp.where(scan_first, 0.0, pltpu.roll(g, c_len - 1 if reverse else 1, 0)) for g in g_in]
    v = [tile(v_ref, p, c) for p, c in inst]
    at, rt, w, bk_end = [], [], [], []
    for i, (p, c) in enumerate(inst):
        b = tile(b_ref, p, c, (0,)).astype(F32)
        k = tile(k_ref, p, c, (0,)).astype(F32)
        at.append(-tile(kk_ref, p, c).astype(F32) * jnp.exp(g_ex[i]))
        rt.append(tile(r_ref, p, c).astype(F32) * jnp.exp(g_in[i]))
        inv = jnp.exp(-g_in[i])
        to_end = jnp.exp(g_tot[i] - g_in[i])
        bk_end.append(jnp.concatenate([b * to_end, k * to_end], 0))
        w.append(_mm_nt(jnp.concatenate([at[i], rt[i]], 0),
                        jnp.concatenate([_bd(b * inv), _bd(k * inv)], 0)))

    bdv = [_bd(u) for u in v]
    m_ab = [jnp.where(strict, u[:c_len, :LANES], 0.0) for u in w]
    rhs = [jnp.concatenate([at[i], _mm(jnp.where(strict, w[i][:c_len, LANES:], 0.0), bdv[i])], axis=1)
           for i in range(n)]
    tinv = [jnp.where(li == row, 1.0, 0.0) + u for u in m_ab]
    mj = [u.astype(BF16) for u in m_ab]
    for j in range(1, 6):
        mj = [_mm(u, _bd(u)).astype(BF16) for u in mj]
        tinv = [tinv[i] + _mm(tinv[i], _bd(mj[i])) for i in range(n)]
    x = [_mm(tinv[i], _bd(rhs[i])) for i in range(n)]

    t1 = [_mm(jnp.where(incl, w[i][c_len:, :LANES], 0.0), _bd(x[i])) for i in range(n)]
    y0 = [t1[i][:, LANES:] + _mm(jnp.where(incl, w[i][c_len:, LANES:], 0.0), bdv[i]) for i in range(n)]
    rp = [(rt[i] + t1[i][:, :LANES]).astype(BF16) for i in range(n)]
    t2 = [_mm(bk_end[i].T,
              jnp.concatenate([x[i], jnp.concatenate([zeros, v[i].astype(F32)], axis=1)], 0))
          for i in range(n)]
    g_low = [jnp.where(same_head, u[:, :LANES], 0.0).astype(BF16) for u in t2]
    h_new = [jnp.where(same_head, u[:, LANES:], 0.0) for u in t2]
    g_col = [jnp.exp(jnp.sum(jnp.where(diag, jnp.broadcast_to(g, (LANES, LANES)), 0.0),
                             axis=1, keepdims=True)) for g in g_tot]

    s = [s_ref[p] for p in range(N_PAIRS)]
    for ci, c in enumerate(order):
        sb = [u.astype(BF16) for u in s]
        for p in range(N_PAIRS):
            i = ci * N_PAIRS + p
            y_ref[0, c * c_len:(c + 1) * c_len, p * LANES:(p + 1) * LANES] = (
                _mm(rp[i], sb[p]) + y0[i]).astype(BF16)
        s = [s[p] * g_col[ci * N_PAIRS + p] + _mm(g_low[ci * N_PAIRS + p], sb[p]) + h_new[ci * N_PAIRS + p]
             for p in range(N_PAIRS)]
    for p in range(N_PAIRS):
        s_ref[p] = s[p]


def _rwkv_scan(r, v, kkn, g_in, krep, bvec, direction, tb=256):
    b, t, _ = r.shape
    nt = t // tb
    reverse = direction == 1
    tmap = (lambda i: nt - 1 - i) if reverse else (lambda i: i)
    one = pl.BlockSpec((1, tb, WIDTH), lambda bb, i: (bb, tmap(i), 0))
    two = pl.BlockSpec((1, 1, tb, WIDTH), lambda bb, i: (direction, bb, tmap(i), 0))
    return pl.pallas_call(
        functools.partial(_rwkv_scan_kernel, nc=tb // RWKV_CHUNK, reverse=reverse),
        grid=(b, nt),
        in_specs=[one, one, one, two, two, two],
        out_specs=one,
        out_shape=jax.ShapeDtypeStruct((b, t, WIDTH), BF16),
        scratch_shapes=[pltpu.VMEM((N_PAIRS, LANES, LANES), F32)],
        compiler_params=_cparams("parallel", "arbitrary"),
        name=f"rwkv_scan_{'bwd' if reverse else 'fwd'}",
    )(r, v, kkn, g_in, krep, bvec)


def _even_out_kernel(h_ref, o1, l1, o4, l4, o16, l16, yf, yb, bonus, g, lng, lnb, ones_ref, w_ref,
                     out_ref, tok_sc, *, tm):
    n_tiles = WIDTH // LANES
    for slot, (dil, ref) in enumerate(((4, o4), (4, l4), (16, o16), (16, l16))):
        for rho in range(dil):
            for c in range(n_tiles):
                tok_sc[slot * n_tiles + c, pl.ds(rho, tm // dil, stride=dil), :] = (
                    ref[0, rho, :, c * LANES:(c + 1) * LANES].astype(F32))
    tok = lambda slot: jnp.concatenate([tok_sc[slot * n_tiles + c] for c in range(n_tiles)], axis=1)
    la, lb, lc = l1[0, 0], tok(1), tok(3)
    mx = jnp.maximum(jnp.maximum(la, lb), lc)
    wa, wb, wc = jnp.exp(la - mx), jnp.exp(lb - mx), jnp.exp(lc - mx)
    y_a = (wa * o1[0, 0].astype(F32) + wb * tok(0) + wc * tok(2)) / (wa + wb + wc)
    ones = ones_ref[...]
    inv = 1.0 / HEAD_DIM
    y = yf[0].astype(F32) + yb[0].astype(F32)
    yc = y - _group_sum(y, ones) * inv
    var = _group_sum(yc * yc, ones) * inv
    yn = yc * lax.rsqrt(var + RWKV_GN_EPS) * lng[...] + lnb[...]
    y_b = (yn + bonus[0].astype(F32)) * g[0].astype(F32)
    out_ref[0] = (h_ref[0] + _mm(y_a, w_ref[0:WIDTH, :]) + _mm(y_b, w_ref[WIDTH:, :]))


def _even_out(h, branch, rwkv_parts, lng, lnb, ones_bd, w, tm=512):
    b, t, _ = h.shape
    const = lambda shape: pl.BlockSpec(shape, lambda bb, i: (0, 0))
    row = lambda n: pl.BlockSpec((1, tm, n), lambda bb, i: (bb, i, 0))
    res = lambda d: pl.BlockSpec((1, d, tm // d, WIDTH), lambda bb, i: (bb, 0, i, 0))
    dils = [d for _, d in DILATED_BRANCHES]
    return pl.pallas_call(
        functools.partial(_even_out_kernel, tm=tm),
        grid=(b, t // tm),
        in_specs=[row(D_MODEL)] + [res(d) for d in dils for _ in range(2)] + [row(WIDTH)] * 4
        + [const((1, WIDTH)), const((1, WIDTH)), const((WIDTH, WIDTH)), const((D_MODEL, D_MODEL))],
        out_specs=row(D_MODEL),
        out_shape=jax.ShapeDtypeStruct((b, t, D_MODEL), F32),
        scratch_shapes=[pltpu.VMEM((4 * WIDTH // LANES, tm, LANES), F32)],
        compiler_params=_cparams("parallel", "parallel"),
        name="even_out",
    )(h, *branch, *rwkv_parts, lng, lnb, ones_bd, w)


def _mlp_kernel(h_ref, g_ref, w1_ref, w2_ref, out_ref, u_ref):
    f = pl.program_id(1)

    @pl.when(f == 0)
    def _():
        x = h_ref[...]
        u_ref[...] = _rms_rows(x, g_ref[...]).astype(BF16)
        out_ref[...] = x

    hdn = jnp.dot(u_ref[...], w1_ref[...], preferred_element_type=F32)
    hdn = jnp.square(jnp.maximum(hdn, 0.0))
    out_ref[...] += jnp.dot(hdn.astype(BF16), w2_ref[...], preferred_element_type=F32)


def _mlp(h2, g, w1, w2, tm=1024, tf=1024):
    m = h2.shape[0]
    return pl.pallas_call(
        _mlp_kernel,
        grid=(m // tm, D_FF // tf),
        in_specs=[pl.BlockSpec((tm, D_MODEL), lambda i, f: (i, 0)),
                  pl.BlockSpec((1, D_MODEL), lambda i, f: (0, 0)),
                  pl.BlockSpec((D_MODEL, tf), lambda i, f: (0, f)),
                  pl.BlockSpec((tf, D_MODEL), lambda i, f: (f, 0))],
        out_specs=pl.BlockSpec((tm, D_MODEL), lambda i, f: (i, 0)),
        out_shape=jax.ShapeDtypeStruct((m, D_MODEL), F32),
        scratch_shapes=[pltpu.VMEM((tm, D_MODEL), BF16)],
        compiler_params=_cparams("parallel", "arbitrary"),
        name="mlp",
    )(h2, g, w1, w2)


def _odd_in_kernel(x_ref, g_ref, wr_ref, ws_ref, wd_ref, ret_out, xz_out, dt_out):
    ub = _rms_rows(x_ref[...], g_ref[...]).astype(BF16)
    ret_out[...] = jnp.dot(ub, wr_ref[...], preferred_element_type=F32).astype(BF16)
    xz_out[...] = jnp.dot(ub, ws_ref[...], preferred_element_type=F32).astype(BF16)
    dt_out[...] = jnp.dot(ub, wd_ref[...], preferred_element_type=F32)


def _odd_in(x2, g, wr, ws, wd, tm=512):
    m = x2.shape[0]
    const = lambda shape: pl.BlockSpec(shape, lambda i: (0, 0))
    row = lambda n: pl.BlockSpec((tm, n), lambda i: (i, 0))
    n_xz = SSD_XBC + WIDTH
    return pl.pallas_call(
        _odd_in_kernel,
        grid=(m // tm,),
        in_specs=[row(D_MODEL), const((1, D_MODEL)), const((D_MODEL, 4 * WIDTH)),
                  const((D_MODEL, n_xz)), const((D_MODEL, LANES))],
        out_specs=[row(4 * WIDTH), row(n_xz), row(LANES)],
        out_shape=[jax.ShapeDtypeStruct((m, 4 * WIDTH), BF16),
                   jax.ShapeDtypeStruct((m, n_xz), BF16),
                   jax.ShapeDtypeStruct((m, LANES), F32)],
        compiler_params=_cparams("parallel"),
        name="odd_in",
    )(x2, g, wr, ws, wd)


def _rotary_pair(u, cos, sin_signed):
    first = (_iota(u.shape, 1) & (HEAD_DIM - 1)) < HEAD_DIM // 2
    swapped = jnp.where(first, pltpu.roll(u, LANES - HEAD_DIM // 2, 1),
                        pltpu.roll(u, HEAD_DIM // 2, 1))
    return u * cos + swapped * sin_signed


def _ret_kernel(q_ref, k_ref, v_ref, cos_ref, sin_ref, lg64_ref, lg128_ref, y_ref, s_ref,
                *, nc, reverse):
    c_len = LIN_CHUNK

    @pl.when(pl.program_id(1) == 0)
    def _():
        s_ref[...] = jnp.zeros_like(s_ref)

    l_io = _iota((c_len, 2 * c_len), 0)
    s_io = _iota((c_len, 2 * c_len), 1) & (c_len - 1)
    delta = (s_io - l_io) if reverse else (l_io - s_io)
    causal = delta >= 0
    delta_f = jnp.where(causal, delta, 0).astype(F32)
    pos = _iota((c_len, LANES), 0)
    if reverse:
        pos = c_len - 1 - pos
    pos = pos.astype(F32)
    r2 = _iota((LANES, LANES), 0)
    c2 = _iota((LANES, LANES), 1)
    same_head = (r2 & HEAD_DIM) == (c2 & HEAD_DIM)

    decay, from_start, to_end, whole = [], [], [], []
    for p in range(N_PAIRS):
        lg64 = lg64_ref[0, :, p * LANES:(p + 1) * LANES]
        lg128 = lg128_ref[0, :, 2 * p * c_len:2 * (p + 1) * c_len]
        decay.append(jnp.where(causal, jnp.exp(lg128 * delta_f), 0.0))
        from_start.append(jnp.exp(lg64 * (pos + 1.0)))
        to_end.append(jnp.exp(lg64 * (float(c_len - 1) - pos)))
        whole.append(jnp.exp(lg64 * float(c_len)))

    order = _scan_order(nc, reverse)
    inst = [(p, c) for c in order for p in range(N_PAIRS)]
    q, k, v = [], [], []
    for p, c in inst:
        rows = slice(c * c_len, (c + 1) * c_len)
        cols = slice(p * LANES, (p + 1) * LANES)
        cos = cos_ref[rows, :]
        sin = sin_ref[rows, :]
        q.append(_rotary_pair(q_ref[0, rows, cols].astype(F32), cos, sin))
        k.append(_rotary_pair(k_ref[0, rows, cols].astype(F32), cos, sin) * (HEAD_DIM ** -0.5))
        v.append(v_ref[0, rows, cols])
    sc = [(_mm_nt(q[i], _bd(k[i])) * decay[p]).astype(BF16) for i, (p, c) in enumerate(inst)]
    y_intra = [_mm(sc[i], _bd(v[i])) for i in range(len(inst))]
    upd = [jnp.where(same_head, _mm((k[i] * to_end[p]).T, v[i]), 0.0) for i, (p, c) in enumerate(inst)]
    qs = [(q[i] * from_start[p]).astype(BF16) for i, (p, c) in enumerate(inst)]

    s = [s_ref[p] for p in range(N_PAIRS)]
    for ci, c in enumerate(order):
        for p in range(N_PAIRS):
            i = ci * N_PAIRS + p
            y_ref[0, c * c_len:(c + 1) * c_len, p * LANES:(p + 1) * LANES] = (
                y_intra[i] + _mm(qs[i], s[p])).astype(BF16)
            s[p] = s[p] * whole[p] + upd[i]
    for p in range(N_PAIRS):
        s_ref[p] = s[p]


def _retention(ret, cos, sin, lg64, lg128, direction, tb=512):
    b, t, _ = ret.shape
    nt = t // tb
    reverse = direction == 1
    tmap = (lambda i: nt - 1 - i) if reverse else (lambda i: i)
    col = lambda blk: pl.BlockSpec((1, tb, WIDTH), lambda bb, i: (bb, tmap(i), blk))
    tab = pl.BlockSpec((tb, LANES), lambda bb, i: (tmap(i), 0))
    return pl.pallas_call(
        functools.partial(_ret_kernel, nc=tb // LIN_CHUNK, reverse=reverse),
        grid=(b, nt),
        in_specs=[col(0), col(1), col(2), tab, tab,
                  pl.BlockSpec((1, 1, WIDTH), lambda bb, i: (direction, 0, 0)),
                  pl.BlockSpec((1, 1, N_HEADS * LIN_CHUNK), lambda bb, i: (direction, 0, 0))],
        out_specs=col(0),
        out_shape=jax.ShapeDtypeStruct((b, t, WIDTH), BF16),
        scratch_shapes=[pltpu.VMEM((N_PAIRS, LANES, LANES), F32)],
        compiler_params=_cparams("parallel", "arbitrary"),
        name=f"retention_{'bwd' if reverse else 'fwd'}",
    )(ret, ret, ret, cos, sin, lg64, lg128)


def _ssd_conv_kernel(x_ref, xp_ref, xn_ref, w_ref, b_ref, xs_out, bc_out, *, tm):
    i = pl.program_id(1)
    nt = pl.num_programs(1)
    prev = jnp.where(i > 0, xp_ref[0].astype(F32), 0.0)
    nxt = jnp.where(i < nt - 1, xn_ref[0].astype(F32), 0.0)
    xe = jnp.concatenate([prev, x_ref[0].astype(F32), nxt], axis=0)
    ext = tm + 2 * HALO_ROWS
    pad = SSD_CONV // 2
    acc = jnp.zeros((tm, SSD_XBC), F32) + b_ref[...]
    for j in range(SSD_CONV):
        shift = (pad - j) % ext
        xj = xe if shift == 0 else pltpu.roll(xe, shift, 0)
        acc = acc + xj[HALO_ROWS:HALO_ROWS + tm] * w_ref[j:j + 1, :]
    y = _silu(acc).astype(BF16)
    xs_out[0] = y[:, :WIDTH]
    bc_out[0] = y[:, WIDTH:]


def _ssd_conv(xz, w, bias, tm=512):
    b, t, _ = xz.shape
    nt = t // tm
    per = tm // HALO_ROWS
    const = lambda shape: pl.BlockSpec(shape, lambda bb, i: (0, 0))
    main = pl.BlockSpec((1, tm, SSD_XBC), lambda bb, i: (bb, i, 0))
    prev = pl.BlockSpec((1, HALO_ROWS, SSD_XBC), lambda bb, i: (bb, jnp.maximum(i * per - 1, 0), 0))
    nxt = pl.BlockSpec((1, HALO_ROWS, SSD_XBC),
                       lambda bb, i: (bb, jnp.minimum((i + 1) * per, t // HALO_ROWS - 1), 0))
    out = pl.BlockSpec((1, tm, WIDTH), lambda bb, i: (bb, i, 0))
    return pl.pallas_call(
        functools.partial(_ssd_conv_kernel, tm=tm),
        grid=(b, nt),
        in_specs=[main, prev, nxt, const((8, SSD_XBC)), const((1, SSD_XBC))],
        out_specs=[out, out],
        out_shape=[jax.ShapeDtypeStruct((b, t, WIDTH), BF16)] * 2,
        compiler_params=_cparams("parallel", "parallel"),
        name="ssd_conv",
    )(xz, xz, xz, w, bias)


def _ssd_kernel(xs_ref, bc_ref, dtc_ref, dtr_ref, bias_c_ref, bias_r_ref, nega_c_ref, nega_r_ref,
                tri_ref, trit_ref, y_ref, s_ref, *, nc, reverse):
    c_len = LIN_CHUNK
    per_group = N_HEADS // SSD_GROUPS
    gw = WIDTH // SSD_GROUPS

    @pl.when(pl.program_id(1) == 0)
    def _():
        s_ref[...] = jnp.zeros_like(s_ref)

    tri = tri_ref[...]
    trit = trit_ref[...]
    dts_c = _softplus(dtc_ref[0, 0] + bias_c_ref[0])
    la_c = dts_c * nega_c_ref[0]
    dts_r = _softplus(dtr_ref[0, 0] + bias_r_ref[0])
    la_r = dts_r * nega_r_ref[0]
    l_io = _iota((c_len, c_len), 0)
    s_io = _iota((c_len, c_len), 1)
    causal = (s_io >= l_io) if reverse else (s_io <= l_io)
    lo = _lane_lo((c_len, LANES))

    def by_head128(cols, g):
        return jnp.concatenate([jnp.broadcast_to(cols[:, g * per_group + h:g * per_group + h + 1],
                                                 (c_len, LANES)) for h in range(per_group)], axis=1)

    def to64(wide):
        h = [wide[:, j * LANES:(j + 1) * LANES] for j in range(per_group)]
        return jnp.concatenate([jnp.where(lo, h[0], h[1]), jnp.where(lo, h[2], h[3])], axis=1)

    order = _scan_order(nc, reverse)
    inst = [(g, c) for c in order for g in range(SSD_GROUPS)]
    n = len(inst)
    rows_of = lambda c: slice(c * c_len, (c + 1) * c_len)
    cum = [_mm_exact_left(tri, by_head128(la_c[rows_of(c)], g)) for g, c in inst]
    cum_r = {c: _mm_exact_right(la_r[:, rows_of(c)], trit) for c in order}
    last = [u[0:1] if reverse else u[c_len - 1:c_len] for u in cum]
    bm = [bc_ref[0, rows_of(c), g * SSD_STATE:(g + 1) * SSD_STATE] for g, c in inst]
    cm = [bc_ref[0, rows_of(c), (SSD_GROUPS + g) * SSD_STATE:(SSD_GROUPS + g + 1) * SSD_STATE]
          for g, c in inst]
    xdt = [xs_ref[0, rows_of(c), g * gw:(g + 1) * gw].astype(F32) * to64(by_head128(dts_c[rows_of(c)], g))
           for g, c in inst]
    gm = [_mm_nt(cm[i], bm[i]) for i in range(n)]
    y_intra = []
    for i, (g, c) in enumerate(inst):
        parts = []
        for pp in range(per_group // 2):
            wide = []
            for hh in (2 * pp, 2 * pp + 1):
                head = g * per_group + hh
                diff = cum[i][:, hh * LANES:(hh + 1) * LANES] - cum_r[c][head:head + 1, :]
                dec = jnp.where(causal, jnp.exp(jnp.where(causal, diff, 0.0)), 0.0)
                wide.append((gm[i] * dec).astype(BF16))
            parts.append(_mm(jnp.concatenate(wide, axis=1), _bd(xdt[i][:, pp * LANES:(pp + 1) * LANES])))
        y_intra.append(jnp.concatenate(parts, axis=1))
    from_start = [to64(jnp.exp(u)) for u in cum]
    upd = [_mm(bm[i].astype(F32).T, xdt[i] * to64(jnp.exp(last[i] - cum[i]))) for i in range(n)]
    whole = [to64(jnp.exp(u)) for u in last]

    s = [s_ref[g] for g in range(SSD_GROUPS)]
    for ci, c in enumerate(order):
        for g in range(SSD_GROUPS):
            i = ci * SSD_GROUPS + g
            y_ref[0, rows_of(c), g * gw:(g + 1) * gw] = (
                y_intra[i] + _mm(cm[i], s[g]) * from_start[i]).astype(BF16)
            s[g] = s[g] * whole[i] + upd[i]
    for g in range(SSD_GROUPS):
        s_ref[g] = s[g]


def _ssd_scan(xs, bc, dtc, dtr, sp, direction, tb=512):
    b, t, _ = xs.shape
    nt = t // tb
    reverse = direction == 1
    tmap = (lambda i: nt - 1 - i) if reverse else (lambda i: i)
    row = pl.BlockSpec((1, tb, WIDTH), lambda bb, i: (bb, tmap(i), 0))
    par = lambda shape: pl.BlockSpec((1,) + shape, lambda bb, i: (direction, 0, 0))
    tri = pl.BlockSpec((LIN_CHUNK, LIN_CHUNK), lambda bb, i: (0, 0))
    return pl.pallas_call(
        functools.partial(_ssd_kernel, nc=tb // LIN_CHUNK, reverse=reverse),
        grid=(b, nt),
        in_specs=[row, row,
                  pl.BlockSpec((1, 1, tb, N_HEADS), lambda bb, i: (direction, bb, tmap(i), 0)),
                  pl.BlockSpec((1, 1, N_HEADS, tb), lambda bb, i: (direction, bb, 0, tmap(i))),
                  par((1, N_HEADS)), par((N_HEADS, 1)), par((1, N_HEADS)), par((N_HEADS, 1)),
                  tri, tri],
        out_specs=row,
        out_shape=jax.ShapeDtypeStruct((b, t, WIDTH), BF16),
        scratch_shapes=[pltpu.VMEM((SSD_GROUPS, SSD_STATE, WIDTH // SSD_GROUPS), F32)],
        compiler_params=_cparams("parallel", "arbitrary"),
        name=f"ssd_scan_{'bwd' if reverse else 'fwd'}",
    )(xs, bc, dtc, dtr, sp["bias_c"], sp["bias_r"], sp["nega_c"], sp["nega_r"],
      sp["tri_rev" if reverse else "tri_fwd"], sp["trit_rev" if reverse else "trit_fwd"])


def _odd_out_kernel(h_ref, rf, rb, gate, sf, sb, xs, z, gng, gnb, dsk, ng, ones_ref, w_ref, out_ref):
    f32 = lambda ref: ref[...].astype(F32)
    ones = ones_ref[...]
    inv = 1.0 / HEAD_DIM
    y = f32(rf) + f32(rb)
    yc = y - _group_sum(y, ones) * inv
    var = _group_sum(yc * yc, ones) * inv
    y_c = (yc * lax.rsqrt(var + GN_EPS) * gng[...] + gnb[...]) * _silu(f32(gate))
    yd = (f32(sf) + f32(sb) + dsk[...] * f32(xs)) * _silu(f32(z))
    gw = WIDTH // SSD_GROUPS
    halves = []
    for gi in range(SSD_GROUPS):
        yg = yd[:, gi * gw:(gi + 1) * gw]
        halves.append(yg * lax.rsqrt(jnp.mean(yg * yg, -1, keepdims=True) + NORM_EPS))
    y_d = jnp.concatenate(halves, axis=1) * ng[...]
    out_ref[...] = h_ref[...] + _mm(y_c, w_ref[0:WIDTH, :]) + _mm(y_d, w_ref[WIDTH:, :])


def _odd_out(h2, rf, rb, ret2, sf, sb, xs, xz2, gng, gnb, dsk, ng, ones_bd, w, tm=512):
    m = h2.shape[0]
    const = lambda shape: pl.BlockSpec(shape, lambda i: (0, 0))
    row = lambda n: pl.BlockSpec((tm, n), lambda i: (i, 0))
    vec = const((1, WIDTH))
    return pl.pallas_call(
        _odd_out_kernel,
        grid=(m // tm,),
        in_specs=[row(D_MODEL), row(WIDTH), row(WIDTH),
                  pl.BlockSpec((tm, WIDTH), lambda i: (i, 3)),
                  row(WIDTH), row(WIDTH), row(WIDTH),
                  pl.BlockSpec((tm, WIDTH), lambda i: (i, 2)),
                  vec, vec, vec, vec, const((WIDTH, WIDTH)), const((D_MODEL, D_MODEL))],
        out_specs=row(D_MODEL),
        out_shape=jax.ShapeDtypeStruct((m, D_MODEL), F32),
        compiler_params=_cparams("parallel"),
        name="odd_out",
    )(h2, rf, rb, ret2, sf, sb, xs, xz2, gng, gnb, dsk, ng, ones_bd, w)


def _tri(n, reverse):
    i = np.arange(n)
    m = (i[None, :] >= i[:, None]) if reverse else (i[None, :] <= i[:, None])
    return m.astype(np.float32)


def _prepare(norm_mix, norm_mlp, mlp_w1, mlp_w2, even_in_w, even_out_w, attn_q_gain, attn_k_gain,
             rwkv_mu_prev, rwkv_mu_next, rwkv_w0, rwkv_w2, rwkv_a0, rwkv_a2, rwkv_g2, rwkv_k_k,
             rwkv_k_a, rwkv_r_k, rwkv_ln_g, rwkv_ln_b, odd_in_w, odd_out_w, ret_decay_exp, ret_gn_g,
             ret_gn_b, ssd_conv_w, ssd_conv_b, ssd_dt_bias, ssd_a_log, ssd_d, ssd_norm_g):
    row = lambda v: v.reshape(1, -1).astype(F32)
    heads = np.arange(WIDTH) // HEAD_DIM
    ones_bd = jnp.asarray(heads[:, None] == heads[None, :], BF16)
    p = {"ones_bd": ones_bd, "norm_mix": norm_mix, "norm_mlp": norm_mlp,
         "mlp_w1": mlp_w1.astype(BF16), "mlp_w2": mlp_w2.astype(BF16)}

    w_in = even_in_w[0]
    p["even_wa"] = w_in[:, :3 * WIDTH].astype(BF16)
    p["even_wr"] = w_in[:, 3 * WIDTH:].astype(BF16)
    p["even_out_w"] = even_out_w[0].astype(BF16)
    p["q_gain"] = row(jnp.tile(attn_q_gain[0], N_HEADS))
    p["k_gain"] = row(jnp.tile(attn_k_gain[0], N_HEADS))

    def two_dir_lowrank(w):
        z = jnp.zeros_like(w[0])
        return jnp.concatenate([jnp.concatenate([w[0], z], 1), jnp.concatenate([z, w[1]], 1)], 0)

    p["rwkv"] = {
        "mu_prev": row(rwkv_mu_prev[0]), "mu_next": row(rwkv_mu_next[0]),
        "w0": row(rwkv_w0[0]), "w2": two_dir_lowrank(rwkv_w2[0]).astype(BF16),
        "a0": row(rwkv_a0[0]), "a2": two_dir_lowrank(rwkv_a2[0]).astype(BF16),
        "g2": rwkv_g2[0].astype(BF16), "k_k": row(rwkv_k_k[0]), "k_a": row(rwkv_k_a[0]),
        "r_k": row(rwkv_r_k[0]), "ones_bd": ones_bd,
    }
    p["rwkv_tri"] = [jnp.asarray(_tri(RWKV_CHUNK, False), BF16), jnp.asarray(_tri(RWKV_CHUNK, True), BF16)]
    p["rwkv_ln_g"] = row(rwkv_ln_g[0])
    p["rwkv_ln_b"] = row(rwkv_ln_b[0])

    w_odd = odd_in_w[0]
    p["odd_wr"] = w_odd[:, :4 * WIDTH].astype(BF16)
    z_w = w_odd[:, 4 * WIDTH:5 * WIDTH]
    xbc_w = w_odd[:, 5 * WIDTH:5 * WIDTH + SSD_XBC]
    dt_w = w_odd[:, 5 * WIDTH + SSD_XBC:]
    p["odd_ws"] = jnp.concatenate([xbc_w, z_w], axis=1).astype(BF16)
    p["odd_wd"] = jnp.concatenate([dt_w, jnp.zeros((D_MODEL, LANES - 2 * N_HEADS), F32)], 1).astype(BF16)
    p["odd_out_w"] = odd_out_w[0].astype(BF16)

    log_gamma = jnp.log1p(-jnp.exp2(-ret_decay_exp[0].astype(F32)))
    p["lg64"] = jnp.repeat(log_gamma, HEAD_DIM, axis=1).reshape(2, 1, WIDTH)
    p["lg128"] = jnp.repeat(log_gamma, LIN_CHUNK, axis=1).reshape(2, 1, N_HEADS * LIN_CHUNK)
    p["ret_gn_g"] = row(ret_gn_g[0])
    p["ret_gn_b"] = row(ret_gn_b[0])

    p["conv_w"] = jnp.concatenate([ssd_conv_w[0], jnp.zeros((8 - SSD_CONV, SSD_XBC), F32)], 0)
    p["conv_b"] = row(ssd_conv_b[0])
    bias = ssd_dt_bias[0].astype(F32)
    nega = -jnp.exp(ssd_a_log[0].astype(F32))
    p["ssd"] = {
        "bias_c": bias[:, None, :], "nega_c": nega[:, None, :],
        "bias_r": bias[:, :, None], "nega_r": nega[:, :, None],
        "tri_fwd": jnp.asarray(_tri(LIN_CHUNK, False), BF16),
        "tri_rev": jnp.asarray(_tri(LIN_CHUNK, True), BF16),
        "trit_fwd": jnp.asarray(_tri(LIN_CHUNK, False).T, BF16),
        "trit_rev": jnp.asarray(_tri(LIN_CHUNK, True).T, BF16),
    }
    p["ssd_d"] = row(jnp.repeat(ssd_d[0], HEAD_DIM))
    p["ssd_norm_g"] = row(ssd_norm_g[0])
    return p


def _rope_tables(t):
    half = HEAD_DIM // 2
    inv = ROPE_BASE ** (-jnp.arange(half, dtype=F32) / half)
    ang = jnp.arange(t, dtype=F32)[:, None] * inv[None, :]
    cos = jnp.tile(jnp.cos(ang), (1, LANES // half))
    sin = jnp.sin(ang)
    sin_signed = jnp.tile(jnp.concatenate([-sin, sin], axis=1), (1, LANES // HEAD_DIM))
    return cos, sin_signed


def _even_mixer(h2, b, t, p):
    h = h2.reshape(b, t, D_MODEL)
    *qkvs, rcols = _even_in(h, p["norm_mix"][0:1], p["even_wa"], p["even_wr"], p["ones_bd"],
                            p["q_gain"], p["k_gain"])
    branch = []
    for qkv, (_, dil) in zip(qkvs, DILATED_BRANCHES):
        branch.extend(_dil_attn(qkv, dil))
    r, v, kkn, g, bonus, g_in, krep, bvec = _rwkv_prep(rcols, p["rwkv"], p["rwkv_tri"])
    ys = [_rwkv_scan(r, v, kkn, g_in, krep, bvec, d) for d in range(2)]
    out = _even_out(h, branch, [ys[0], ys[1], bonus, g], p["rwkv_ln_g"], p["rwkv_ln_b"],
                    p["ones_bd"], p["even_out_w"])
    return out.reshape(b * t, D_MODEL)


def _odd_mixer(h2, b, t, p):
    ret2, xz2, dt2 = _odd_in(h2, p["norm_mix"][1:2], p["odd_wr"], p["odd_ws"], p["odd_wd"])
    ret = ret2.reshape(b, t, -1)
    xz = xz2.reshape(b, t, -1)
    cos, sin = _rope_tables(t)
    rets = [_retention(ret, cos, sin, p["lg64"], p["lg128"], d) for d in range(2)]
    xs, bc = _ssd_conv(xz, p["conv_w"], p["conv_b"])
    dt = dt2[:, :2 * N_HEADS].reshape(b, t, 2, N_HEADS)
    dtc = dt.transpose(2, 0, 1, 3)
    dtr = dt.transpose(2, 0, 3, 1)
    ssds = [_ssd_scan(xs, bc, dtc, dtr, p["ssd"], d) for d in range(2)]
    flat = lambda u: u.reshape(b * t, -1)
    return _odd_out(h2, flat(rets[0]), flat(rets[1]), ret2, flat(ssds[0]), flat(ssds[1]), flat(xs),
                    xz2, p["ret_gn_g"], p["ret_gn_b"], p["ssd_d"], p["ssd_norm_g"], p["ones_bd"],
                    p["odd_out_w"])


def _trunk(x, p):
    b, t, _ = x.shape
    h2 = x.reshape(b * t, D_MODEL)
    h2 = _even_mixer(h2, b, t, p)
    h2 = _mlp(h2, p["norm_mlp"][0:1], p["mlp_w1"][0], p["mlp_w2"][0])
    h2 = _odd_mixer(h2, b, t, p)
    h2 = _mlp(h2, p["norm_mlp"][1:2], p["mlp_w1"][1], p["mlp_w2"][1])
    return h2.reshape(b, t, D_MODEL)


def kernel(x_prompt, x_sample, norm_mix, norm_mlp, mlp_w1, mlp_w2, even_in_w, even_out_w, attn_q_gain, attn_k_gain, rwkv_mu_prev, rwkv_mu_next, rwkv_w0, rwkv_w2, rwkv_a0, rwkv_a2, rwkv_g2, rwkv_k_k, rwkv_k_a, rwkv_r_k, rwkv_ln_g, rwkv_ln_b, odd_in_w, odd_out_w, ret_decay_exp, ret_gn_g, ret_gn_b, ssd_conv_w, ssd_conv_b, ssd_dt_bias, ssd_a_log, ssd_d, ssd_norm_g):
    p = _prepare(norm_mix, norm_mlp, mlp_w1, mlp_w2, even_in_w, even_out_w, attn_q_gain, attn_k_gain,
                 rwkv_mu_prev, rwkv_mu_next, rwkv_w0, rwkv_w2, rwkv_a0, rwkv_a2, rwkv_g2, rwkv_k_k,
                 rwkv_k_a, rwkv_r_k, rwkv_ln_g, rwkv_ln_b, odd_in_w, odd_out_w, ret_decay_exp,
                 ret_gn_g, ret_gn_b, ssd_conv_w, ssd_conv_b, ssd_dt_bias, ssd_a_log, ssd_d, ssd_norm_g)
    return (_trunk(x_prompt, p), _trunk(x_sample, p))
```

```python
import functools
import math

import numpy as np
import jax
import jax.numpy as jnp
from jax import lax
from jax.experimental import pallas as pl
from jax.experimental.pallas import tpu as pltpu

F32 = jnp.float32
BF16 = jnp.bfloat16

D_MODEL = 1024
HEAD_DIM = 64
N_HEADS = 8
WIDTH = N_HEADS * HEAD_DIM
N_PAIRS = N_HEADS // 2
LANES = 128
HALO_ROWS = 16
DILATED_BRANCHES = ((128, 1), (512, 4), (2048, 16))
ATT_R = 64
RWKV_DECAY_RANK = 64
RWKV_ICL_RANK = 64
RWKV_GATE_RANK = 128
RWKV_COLS = 3 * WIDTH + 2 * RWKV_DECAY_RANK + 2 * RWKV_ICL_RANK + RWKV_GATE_RANK
SSD_STATE = 128
SSD_GROUPS = 2
SSD_CONV = 5
SSD_XBC = WIDTH + 2 * SSD_GROUPS * SSD_STATE
D_FF = 4 * D_MODEL
NORM_EPS = 1e-6
GN_EPS = 1e-5
RWKV_GN_EPS = 64e-5
ROPE_BASE = 10000.0
RWKV_CHUNK = 64
LIN_CHUNK = 128
NEG_BIG = -1e30
VMEM_LIMIT = 56 * 1024 * 1024


def _cparams(*sem):
    return pltpu.CompilerParams(dimension_semantics=sem, vmem_limit_bytes=VMEM_LIMIT)


def _iota(shape, dim):
    return lax.broadcasted_iota(jnp.int32, shape, dim)


def _lane_lo(shape):
    return (_iota(shape, len(shape) - 1) & HEAD_DIM) == 0


def _bd(x):
    lo = _lane_lo(x.shape)
    zero = jnp.zeros_like(x)
    return jnp.concatenate([jnp.where(lo, x, zero), jnp.where(lo, zero, x)], axis=0)


def _mm(a, b):
    return jnp.dot(a.astype(BF16), b.astype(BF16), preferred_element_type=F32)


def _mm_nt(a, b):
    return lax.dot_general(a.astype(BF16), b.astype(BF16), (((1,), (1,)), ((), ())),
                           preferred_element_type=F32)


def _split_bf16(x, terms):
    parts = []
    for _ in range(terms - 1):
        hi = x.astype(BF16)
        parts.append(hi)
        x = x - hi.astype(F32)
    parts.append(x.astype(BF16))
    return parts


def _mm_exact_left(t, x, terms=3):
    return sum(jnp.dot(t, piece, preferred_element_type=F32) for piece in _split_bf16(x, terms))


def _mm_exact_right(x, t, terms=3):
    return sum(jnp.dot(piece, t, preferred_element_type=F32) for piece in _split_bf16(x, terms))


def _group_sum(x, ones_bd):
    return jnp.dot(x.astype(BF16), ones_bd, preferred_element_type=F32)


def _rms_rows(x, g):
    return x * lax.rsqrt(jnp.mean(x * x, -1, keepdims=True) + NORM_EPS) * g


def _sigmoid(x):
    return 0.5 * jnp.tanh(0.5 * x) + 0.5


def _silu(x):
    return x * _sigmoid(x)


def _softplus(x):
    return jnp.maximum(x, 0.0) + jnp.log1p(jnp.exp(-jnp.abs(x)))


def _scan_order(nc, reverse):
    return list(range(nc - 1, -1, -1) if reverse else range(nc))


def _even_in_kernel(x_ref, g_ref, wa_ref, wr_ref, ones_ref, qg_ref, kg_ref, perm4_ref, perm16_ref,
                    qkv1_out, qkv4_out, qkv16_out, r_out, *, tm):
    ub = _rms_rows(x_ref[0], g_ref[...]).astype(BF16)
    a = jnp.dot(ub, wa_ref[...], preferred_element_type=F32)
    q = a[:, :WIDTH]
    k = a[:, WIDTH:2 * WIDTH]
    ones = ones_ref[...]
    inv = 1.0 / HEAD_DIM
    qn = q * lax.rsqrt(_group_sum(q * q, ones) * inv + NORM_EPS) * qg_ref[...]
    kn = k * lax.rsqrt(_group_sum(k * k, ones) * inv + NORM_EPS) * kg_ref[...]
    qkv = jnp.concatenate([qn * (HEAD_DIM ** -0.5), kn, a[:, 2 * WIDTH:]], axis=1).astype(BF16)
    qkv1_out[0, 0] = qkv
    for dil, perm_ref, out in ((4, perm4_ref, qkv4_out), (16, perm16_ref, qkv16_out)):
        perm = perm_ref[...]
        bs = perm.shape[0]
        per = bs // dil
        for blk in range(tm // bs):
            moved = jnp.dot(perm, qkv[blk * bs:(blk + 1) * bs],
                            preferred_element_type=F32).astype(BF16)
            for rho in range(dil):
                out[0, rho, blk * per:(blk + 1) * per, :] = moved[rho * per:(rho + 1) * per]
    r_out[0] = jnp.dot(ub, wr_ref[...], preferred_element_type=F32).astype(BF16)


def _even_in(x, g, wa, wr, ones_bd, qg, kg, tm=512):
    b, t, _ = x.shape
    const = lambda shape: pl.BlockSpec(shape, lambda bb, i: (0, 0))
    row = lambda n: pl.BlockSpec((1, tm, n), lambda bb, i: (bb, i, 0))
    res = lambda d: pl.BlockSpec((1, d, tm // d, 3 * WIDTH), lambda bb, i: (bb, 0, i, 0))
    dils = [d for _, d in DILATED_BRANCHES]

    def to_residue_major(d, bs):
        src = np.arange(bs).reshape(bs // d, d).T.reshape(-1)
        return jnp.asarray(src[:, None] == np.arange(bs)[None, :], BF16)

    blocks = {4: 2 * HALO_ROWS * 4, 16: HALO_ROWS * 16}

    return pl.pallas_call(
        functools.partial(_even_in_kernel, tm=tm),
        grid=(b, t // tm),
        in_specs=[row(D_MODEL), const((1, D_MODEL)), const((D_MODEL, 3 * WIDTH)),
                  const((D_MODEL, RWKV_COLS)), const((WIDTH, WIDTH)),
                  const((1, WIDTH)), const((1, WIDTH)),
                  const((blocks[4], blocks[4])), const((blocks[16], blocks[16]))],
        out_specs=[res(d) for d in dils] + [row(RWKV_COLS)],
        out_shape=[jax.ShapeDtypeStruct((b, d, t // d, 3 * WIDTH), BF16) for d in dils]
        + [jax.ShapeDtypeStruct((b, t, RWKV_COLS), BF16)],
        compiler_params=_cparams("parallel", "parallel"),
        name="even_in",
    )(x, g, wa, wr, ones_bd, qg, kg, to_residue_major(4, blocks[4]), to_residue_major(16, blocks[16]))


def _dil_attn_kernel(q_ref, k_ref, kl_ref, kr_ref, v_ref, vl_ref, vr_ref, o_ref, lse_ref,
                     kext, vext, *, tq, n_res, sub_len, dil):
    r = ATT_R
    i = pl.program_id(2)
    for z in range(n_res):
        kext[z, 0:r] = kl_ref[0, z]
        kext[z, r:r + tq] = k_ref[0, z]
        kext[z, r + tq:] = kr_ref[0, z]
        vext[z, 0:r] = vl_ref[0, z]
        vext[z, r:r + tq] = v_ref[0, z]
        vext[z, r + tq:] = vr_ref[0, z]

    c_io = _iota((r, 3 * r), 1)
    q_io = _iota((r, 3 * r), 0)
    rel = c_io - r - q_io
    near = jnp.abs(rel) <= r
    dist = jnp.abs(rel).astype(F32) * float(dil)
    lo = _lane_lo((r, LANES))
    near2 = jnp.concatenate([near, near], axis=0)
    dist2 = jnp.concatenate([dist, dist], axis=0)
    first_head = _iota((2 * r, 3 * r), 0) < r
    cols = [slice(p * LANES, (p + 1) * LANES) for p in range(N_PAIRS)]
    bias = [jnp.where(first_head, 2.0 ** (-(2 * p + 1)), 2.0 ** (-(2 * p + 2))) * dist2
            for p in range(N_PAIRS)]

    blocks = [(z, j) for z in range(n_res) for j in range(tq // r)]
    group = 2 if len(blocks) % 2 == 0 else 1
    for g0 in range(0, len(blocks), group):
        inst = [(z, j, p) for z, j in blocks[g0:g0 + group] for p in range(N_PAIRS)]
        valid = {}
        for z, j in blocks[g0:g0 + group]:
            kpos = i * tq + j * r + _iota((2 * r, 3 * r), 1) - r
            valid[j] = near2 & (kpos >= 0) & (kpos < sub_len)
        s = [jnp.where(valid[j],
                       _mm_nt(_bd(q_ref[0, z, j * r:(j + 1) * r, cols[p]]),
                              kext[z, j * r:(j + 3) * r, cols[p]]) - bias[p], NEG_BIG)
             for z, j, p in inst]
        m = [jnp.max(u, -1, keepdims=True) for u in s]
        e = [jnp.exp(u - mm) for u, mm in zip(s, m)]
        den = [jnp.sum(u, -1, keepdims=True) for u in e]
        pv = [_mm(e[n], vext[z, j * r:(j + 3) * r, cols[p]]) / den[n]
              for n, (z, j, p) in enumerate(inst)]
        lse = [jnp.broadcast_to(m[n] + jnp.log(den[n]), (2 * r, LANES)) for n in range(len(inst))]
        for n, (z, j, p) in enumerate(inst):
            o_ref[0, z, j * r:(j + 1) * r, cols[p]] = jnp.where(lo, pv[n][:r], pv[n][r:]).astype(BF16)
            lse_ref[0, z, j * r:(j + 1) * r, cols[p]] = jnp.where(lo, lse[n][:r], lse[n][r:])


def _dil_attn(qkv, dil, rows_per_step=512):
    b, _, sub_len, _ = qkv.shape
    tq = min(sub_len, rows_per_step)
    n_res = min(dil, rows_per_step // tq)
    nblk = sub_len // ATT_R
    per = tq // ATT_R
    main = lambda col: pl.BlockSpec((1, n_res, tq, WIDTH), lambda bb, rr, i: (bb, rr, i, col))
    left = lambda col: pl.BlockSpec((1, n_res, ATT_R, WIDTH),
                                    lambda bb, rr, i: (bb, rr, jnp.maximum(i * per - 1, 0), col))
    right = lambda col: pl.BlockSpec((1, n_res, ATT_R, WIDTH),
                                     lambda bb, rr, i: (bb, rr, jnp.minimum((i + 1) * per, nblk - 1), col))
    return pl.pallas_call(
        functools.partial(_dil_attn_kernel, tq=tq, n_res=n_res, sub_len=sub_len, dil=dil),
        grid=(b, dil // n_res, sub_len // tq),
        in_specs=[main(0), main(1), left(1), right(1), main(2), left(2), right(2)],
        out_specs=[main(0), main(0)],
        out_shape=[jax.ShapeDtypeStruct((b, dil, sub_len, WIDTH), BF16),
                   jax.ShapeDtypeStruct((b, dil, sub_len, WIDTH), F32)],
        scratch_shapes=[pltpu.VMEM((n_res, tq + 2 * ATT_R, WIDTH), BF16)] * 2,
        compiler_params=_cparams("parallel", "parallel", "parallel"),
        name=f"dil_attn_d{dil}",
    )(qkv, qkv, qkv, qkv, qkv, qkv, qkv)


def _rwkv_prep_kernel(x_ref, xp_ref, xn_ref, mup_ref, mun_ref, w0_ref, w2_ref, a0_ref, a2_ref,
                      g2_ref, kk_ref, ka_ref, rk_ref, ones_ref, trif_ref, trir_ref,
                      r_out, v_out, kkn_out, g_out, bonus_out, g_in_out, krep_out, b_out, *, tm):
    i = pl.program_id(1)
    nt = pl.num_programs(1)
    x = x_ref[0].astype(F32)
    prow = jnp.where(i > 0, xp_ref[0, HALO_ROWS - 1:HALO_ROWS, :].astype(F32), 0.0)
    nrow = jnp.where(i < nt - 1, xn_ref[0, 0:1, :].astype(F32), 0.0)
    rid = _iota((tm, 1), 0)
    prev = jnp.where(rid == 0, prow, pltpu.roll(x, 1, 0))
    nxt = jnp.where(rid == tm - 1, nrow, pltpu.roll(x, tm - 1, 0))
    xs = x + mup_ref[...] * (prev - x) + mun_ref[...] * (nxt - x)
    r = xs[:, 0:WIDTH]
    k = xs[:, WIDTH:2 * WIDTH]
    v = xs[:, 2 * WIDTH:3 * WIDTH]
    zw = xs[:, 3 * WIDTH:3 * WIDTH + LANES]
    za = xs[:, 3 * WIDTH + LANES:3 * WIDTH + 2 * LANES]
    zg = xs[:, 3 * WIDTH + 2 * LANES:]
    ones = ones_ref[...]
    kk = k * kk_ref[...]
    kkn = kk * lax.rsqrt(jnp.maximum(_group_sum(kk * kk, ones), 1e-24))
    g_out[0] = _mm(_sigmoid(zg), g2_ref[...]).astype(BF16)
    r_out[0] = r.astype(BF16)
    v_out[0] = v.astype(BF16)
    kkn_out[0] = kkn.astype(BF16)
    bonus_out[0] = (_group_sum(r * k * rk_ref[...], ones) * v).astype(BF16)
    wx = w0_ref[...] + _mm(jnp.tanh(zw), w2_ref[...])
    ax = a0_ref[...] + _mm(za, a2_ref[...])
    for d, tri_ref in enumerate((trif_ref, trir_ref)):
        cols = slice(d * WIDTH, (d + 1) * WIDTH)
        lw = -math.exp(-0.5) * _sigmoid(wx[:, cols])
        tri = tri_ref[...]
        for c in range(tm // RWKV_CHUNK):
            rows = slice(c * RWKV_CHUNK, (c + 1) * RWKV_CHUNK)
            g_in_out[d, 0, rows, :] = _mm_exact_left(tri, lw[rows], terms=2)
        a = _sigmoid(ax[:, cols])
        krep_out[d, 0] = (k * (1.0 + (a - 1.0) * ka_ref[...])).astype(BF16)
        b_out[d, 0] = (kkn * a).astype(BF16)


def _rwkv_prep(rcols, p, tri, tm=512):
    b, t, _ = rcols.shape
    nt = t // tm
    per = tm // HALO_ROWS
    const = lambda shape: pl.BlockSpec(shape, lambda bb, i: (0,) * len(shape))
    main = lambda n: pl.BlockSpec((1, tm, n), lambda bb, i: (bb, i, 0))
    dirs = pl.BlockSpec((2, 1, tm, WIDTH), lambda bb, i: (0, bb, i, 0))
    prev = pl.BlockSpec((1, HALO_ROWS, RWKV_COLS), lambda bb, i: (bb, jnp.maximum(i * per - 1, 0), 0))
    nxt = pl.BlockSpec((1, HALO_ROWS, RWKV_COLS),
                       lambda bb, i: (bb, jnp.minimum((i + 1) * per, t // HALO_ROWS - 1), 0))
    one = jax.ShapeDtypeStruct((b, t, WIDTH), BF16)
    two = lambda dt: jax.ShapeDtypeStruct((2, b, t, WIDTH), dt)
    return pl.pallas_call(
        functools.partial(_rwkv_prep_kernel, tm=tm),
        grid=(b, nt),
        in_specs=[main(RWKV_COLS), prev, nxt, const((1, RWKV_COLS)), const((1, RWKV_COLS)),
                  const((1, 2 * WIDTH)), const((LANES, 2 * WIDTH)),
                  const((1, 2 * WIDTH)), const((LANES, 2 * WIDTH)),
                  const((LANES, WIDTH)), const((1, WIDTH)), const((1, WIDTH)), const((1, WIDTH)),
                  const((WIDTH, WIDTH)), const((RWKV_CHUNK, RWKV_CHUNK)),
                  const((RWKV_CHUNK, RWKV_CHUNK))],
        out_specs=[main(WIDTH)] * 5 + [dirs] * 3,
        out_shape=[one] * 5 + [two(F32), two(BF16), two(BF16)],
        compiler_params=_cparams("parallel", "parallel"),
        name="rwkv_prep",
    )(rcols, rcols, rcols, p["mu_prev"], p["mu_next"], p["w0"], p["w2"], p["a0"], p["a2"],
      p["g2"], p["k_k"], p["k_a"], p["r_k"], p["ones_bd"], tri[0], tri[1])


def _rwkv_scan_kernel(r_ref, v_ref, kk_ref, g_ref, k_ref, b_ref, y_ref, s_ref, *, nc, reverse):
    c_len = RWKV_CHUNK

    @pl.when(pl.program_id(1) == 0)
    def _():
        s_ref[...] = jnp.zeros_like(s_ref)

    row = _iota((c_len, LANES), 0)
    li = _iota((c_len, LANES), 1) & (HEAD_DIM - 1)
    strict = (li > row) if reverse else (li < row)
    incl = (li >= row) if reverse else (li <= row)
    r2 = _iota((LANES, LANES), 0)
    c2 = _iota((LANES, LANES), 1)
    same_head = (r2 & HEAD_DIM) == (c2 & HEAD_DIM)
    diag = r2 == c2
    zeros = jnp.zeros((c_len, LANES), F32)

    order = _scan_order(nc, reverse)
    inst = [(p, c) for c in order for p in range(N_PAIRS)]
    n = len(inst)

    def tile(ref, p, c, lead=()):
        return ref[lead + (0, slice(c * c_len, (c + 1) * c_len), slice(p * LANES, (p + 1) * LANES))]

    g_in = [tile(g_ref, p, c, (0,)) for p, c in inst]
    g_tot = [g[0:1] if reverse else g[c_len - 1:c_len] for g in g_in]
    scan_first = row == (c_len - 1 if reverse else 0)
    g_ex = [jnp.where(scan_first, 0.0, pltpu.roll(g, c_len - 1 if reverse else 1, 0)) for g in g_in]
    v = [tile(v_ref, p, c) for p, c in inst]
    at, rt, w, bk_end = [], [], [], []
    for i, (p, c) in enumerate(inst):
        b = tile(b_ref, p, c, (0,)).astype(F32)
        k = tile(k_ref, p, c, (0,)).astype(F32)
        at.append(-tile(kk_ref, p, c).astype(F32) * jnp.exp(g_ex[i]))
        rt.append(tile(r_ref, p, c).astype(F32) * jnp.exp(g_in[i]))
        inv = jnp.exp(-g_in[i])
        to_end = jnp.exp(g_tot[i] - g_in[i])
        bk_end.append(jnp.concatenate([b * to_end, k * to_end], 0))
        w.append(_mm_nt(jnp.concatenate([at[i], rt[i]], 0),
                        jnp.concatenate([_bd(b * inv), _bd(k * inv)], 0)))

    bdv = [_bd(u) for u in v]
    m_ab = [jnp.where(strict, u[:c_len, :LANES], 0.0) for u in w]
    kv = [_mm(jnp.concatenate([jnp.where(strict, w[i][:c_len, LANES:], 0.0),
                               jnp.where(incl, w[i][c_len:, LANES:], 0.0)], 0), bdv[i])
          for i in range(n)]
    rhs = [jnp.concatenate([at[i], kv[i][:c_len]], axis=1) for i in range(n)]
    tinv = [jnp.where(li == row, 1.0, 0.0) + u for u in m_ab]
    mj = [u.astype(BF16) for u in m_ab]
    mj = [_mm(u, _bd(u)).astype(BF16) for u in mj]
    for j in range(1, 5):
        both = [_mm(jnp.concatenate([mj[i], tinv[i].astype(BF16)], 0), _bd(mj[i])) for i in range(n)]
        mj = [u[:c_len].astype(BF16) for u in both]
        tinv = [tinv[i] + both[i][c_len:] for i in range(n)]
    tinv = [tinv[i] + _mm(tinv[i], _bd(mj[i])) for i in range(n)]
    x = [_mm(tinv[i], _bd(rhs[i])) for i in range(n)]

    t1 = [_mm(jnp.where(incl, w[i][c_len:, :LANES], 0.0), _bd(x[i])) for i in range(n)]
    y0 = [t1[i][:, LANES:] + kv[i][c_len:] for i in range(n)]
    rp = [(rt[i] + t1[i][:, :LANES]).astype(BF16) for i in range(n)]
    t2 = [_mm(bk_end[i].T,
              jnp.concatenate([x[i], jnp.concatenate([zeros, v[i].astype(F32)], axis=1)], 0))
          for i in range(n)]
    g_low = [jnp.where(same_head, u[:, :LANES], 0.0).astype(BF16) for u in t2]
    h_new = [jnp.where(same_head, u[:, LANES:], 0.0) for u in t2]
    g_col = [jnp.exp(jnp.sum(jnp.where(diag, jnp.broadcast_to(g, (LANES, LANES)), 0.0),
                             axis=1, keepdims=True)) for g in g_tot]

    s = [s_ref[p] for p in range(N_PAIRS)]
    for ci, c in enumerate(order):
        ids = [ci * N_PAIRS + p for p in range(N_PAIRS)]
        both = [_mm(jnp.concatenate([rp[i], g_low[i]], 0), s[p]) for p, i in enumerate(ids)]
        for p, i in enumerate(ids):
            y_ref[0, c * c_len:(c + 1) * c_len, p * LANES:(p + 1) * LANES] = (
                both[p][:c_len] + y0[i]).astype(BF16)
        s = [s[p] * g_col[i] + both[p][c_len:] + h_new[i] for p, i in enumerate(ids)]
    for p in range(N_PAIRS):
        s_ref[p] = s[p]


def _rwkv_scan(r, v, kkn, g_in, krep, bvec, direction, tb=512):
    b, t, _ = r.shape
    nt = t // tb
    reverse = direction == 1
    tmap = (lambda i: nt - 1 - i) if reverse else (lambda i: i)
    one = pl.BlockSpec((1, tb, WIDTH), lambda bb, i: (bb, tmap(i), 0))
    two = pl.BlockSpec((1, 1, tb, WIDTH), lambda bb, i: (direction, bb, tmap(i), 0))
    return pl.pallas_call(
        functools.partial(_rwkv_scan_kernel, nc=tb // RWKV_CHUNK, reverse=reverse),
        grid=(b, nt),
        in_specs=[one, one, one, two, two, two],
        out_specs=one,
        out_shape=jax.ShapeDtypeStruct((b, t, WIDTH), BF16),
        scratch_shapes=[pltpu.VMEM((N_PAIRS, LANES, LANES), F32)],
        compiler_params=_cparams("parallel", "arbitrary"),
        name=f"rwkv_scan_{'bwd' if reverse else 'fwd'}",
    )(r, v, kkn, g_in, krep, bvec)


def _even_out_kernel(h_ref, o1, l1, o4, l4, o16, l16, yf, yb, bonus, g, lng, lnb, ones_ref, w_ref,
                     out_ref, tok_sc, *, tm):
    n_tiles = WIDTH // LANES
    for slot, (dil, ref) in enumerate(((4, o4), (4, l4), (16, o16), (16, l16))):
        for rho in range(dil):
            for c in range(n_tiles):
                tok_sc[slot * n_tiles + c, pl.ds(rho, tm // dil, stride=dil), :] = (
                    ref[0, rho, :, c * LANES:(c + 1) * LANES].astype(F32))
    tok = lambda slot: jnp.concatenate([tok_sc[slot * n_tiles + c] for c in range(n_tiles)], axis=1)
    la, lb, lc = l1[0, 0], tok(1), tok(3)
    mx = jnp.maximum(jnp.maximum(la, lb), lc)
    wa, wb, wc = jnp.exp(la - mx), jnp.exp(lb - mx), jnp.exp(lc - mx)
    y_a = (wa * o1[0, 0].astype(F32) + wb * tok(0) + wc * tok(2)) / (wa + wb + wc)
    ones = ones_ref[...]
    inv = 1.0 / HEAD_DIM
    y = yf[0].astype(F32) + yb[0].astype(F32)
    yc = y - _group_sum(y, ones) * inv
    var = _group_sum(yc * yc, ones) * inv
    yn = yc * lax.rsqrt(var + RWKV_GN_EPS) * lng[...] + lnb[...]
    y_b = (yn + bonus[0].astype(F32)) * g[0].astype(F32)
    out_ref[0] = (h_ref[0] + _mm(y_a, w_ref[0:WIDTH, :]) + _mm(y_b, w_ref[WIDTH:, :]))


def _even_out(h, branch, rwkv_parts, lng, lnb, ones_bd, w, tm=512):
    b, t, _ = h.shape
    const = lambda shape: pl.BlockSpec(shape, lambda bb, i: (0, 0))
    row = lambda n: pl.BlockSpec((1, tm, n), lambda bb, i: (bb, i, 0))
    res = lambda d: pl.BlockSpec((1, d, tm // d, WIDTH), lambda bb, i: (bb, 0, i, 0))
    dils = [d for _, d in DILATED_BRANCHES]
    return pl.pallas_call(
        functools.partial(_even_out_kernel, tm=tm),
        grid=(b, t // tm),
        in_specs=[row(D_MODEL)] + [res(d) for d in dils for _ in range(2)] + [row(WIDTH)] * 4
        + [const((1, WIDTH)), const((1, WIDTH)), const((WIDTH, WIDTH)), const((D_MODEL, D_MODEL))],
        out_specs=row(D_MODEL),
        out_shape=jax.ShapeDtypeStruct((b, t, D_MODEL), F32),
        scratch_shapes=[pltpu.VMEM((4 * WIDTH // LANES, tm, LANES), F32)],
        compiler_params=_cparams("parallel", "parallel"),
        name="even_out",
    )(h, *branch, *rwkv_parts, lng, lnb, ones_bd, w)


def _mlp_kernel(h_ref, g_ref, w1_ref, w2_ref, out_ref, u_ref):
    f = pl.program_id(1)

    @pl.when(f == 0)
    def _():
        x = h_ref[...]
        u_ref[...] = _rms_rows(x, g_ref[...]).astype(BF16)
        out_ref[...] = x

    hdn = jnp.dot(u_ref[...], w1_ref[...], preferred_element_type=F32)
    hdn = jnp.square(jnp.maximum(hdn, 0.0))
    out_ref[...] += jnp.dot(hdn.astype(BF16), w2_ref[...], preferred_element_type=F32)


def _mlp(h2, g, w1, w2, tm=1024, tf=1024):
    m = h2.shape[0]
    return pl.pallas_call(
        _mlp_kernel,
        grid=(m // tm, D_FF // tf),
        in_specs=[pl.BlockSpec((tm, D_MODEL), lambda i, f: (i, 0)),
                  pl.BlockSpec((1, D_MODEL), lambda i, f: (0, 0)),
                  pl.BlockSpec((D_MODEL, tf), lambda i, f: (0, f)),
                  pl.BlockSpec((tf, D_MODEL), lambda i, f: (f, 0))],
        out_specs=pl.BlockSpec((tm, D_MODEL), lambda i, f: (i, 0)),
        out_shape=jax.ShapeDtypeStruct((m, D_MODEL), F32),
        scratch_shapes=[pltpu.VMEM((tm, D_MODEL), BF16)],
        compiler_params=_cparams("parallel", "arbitrary"),
        name="mlp",
    )(h2, g, w1, w2)


def _rotary_pair(u, cos, sin_signed):
    first = (_iota(u.shape, 1) & (HEAD_DIM - 1)) < HEAD_DIM // 2
    swapped = jnp.where(first, pltpu.roll(u, LANES - HEAD_DIM // 2, 1),
                        pltpu.roll(u, HEAD_DIM // 2, 1))
    return u * cos + swapped * sin_signed


def _odd_in_kernel(x_ref, g_ref, wr_ref, ws_ref, wd_ref, cos_ref, sin_ref, ret_out, xz_out, dt_out):
    ub = _rms_rows(x_ref[...], g_ref[...]).astype(BF16)
    ret = jnp.dot(ub, wr_ref[...], preferred_element_type=F32)
    cos = cos_ref[...]
    sin = sin_ref[...]
    for c in range(2 * N_PAIRS):
        cols = slice(c * LANES, (c + 1) * LANES)
        scale = 1.0 if c < N_PAIRS else HEAD_DIM ** -0.5
        ret_out[:, cols] = (_rotary_pair(ret[:, cols], cos, sin) * scale).astype(BF16)
    ret_out[:, 2 * WIDTH:] = ret[:, 2 * WIDTH:].astype(BF16)
    xz_out[...] = jnp.dot(ub, ws_ref[...], preferred_element_type=F32).astype(BF16)
    dt_out[...] = jnp.dot(ub, wd_ref[...], preferred_element_type=F32)


def _odd_in(x2, g, wr, ws, wd, cos, sin, tm=512):
    m = x2.shape[0]
    per_seq = cos.shape[0] // tm
    const = lambda shape: pl.BlockSpec(shape, lambda i: (0, 0))
    row = lambda n: pl.BlockSpec((tm, n), lambda i: (i, 0))
    tab = pl.BlockSpec((tm, LANES), lambda i: (i % per_seq, 0))
    n_xz = SSD_XBC + WIDTH
    return pl.pallas_call(
        _odd_in_kernel,
        grid=(m // tm,),
        in_specs=[row(D_MODEL), const((1, D_MODEL)), const((D_MODEL, 4 * WIDTH)),
                  const((D_MODEL, n_xz)), const((D_MODEL, LANES)), tab, tab],
        out_specs=[row(4 * WIDTH), row(n_xz), row(LANES)],
        out_shape=[jax.ShapeDtypeStruct((m, 4 * WIDTH), BF16),
                   jax.ShapeDtypeStruct((m, n_xz), BF16),
                   jax.ShapeDtypeStruct((m, LANES), F32)],
        compiler_params=_cparams("parallel"),
        name="odd_in",
    )(x2, g, wr, ws, wd, cos, sin)


def _ret_kernel(q_ref, k_ref, v_ref, lg64_ref, lg128_ref, y_ref, s_ref,
                *, nc, reverse):
    c_len = LIN_CHUNK

    @pl.when(pl.program_id(1) == 0)
    def _():
        s_ref[...] = jnp.zeros_like(s_ref)

    l_io = _iota((c_len, 2 * c_len), 0)
    s_io = _iota((c_len, 2 * c_len), 1) & (c_len - 1)
    ahead = jnp.maximum(l_io - s_io, 0).astype(F32)
    behind = jnp.maximum(s_io - l_io, 0).astype(F32)
    pos = _iota((c_len, LANES), 0)
    if reverse:
        pos = c_len - 1 - pos
    pos = pos.astype(F32)
    r2 = _iota((LANES, LANES), 0)
    c2 = _iota((LANES, LANES), 1)
    same_head = (r2 & HEAD_DIM) == (c2 & HEAD_DIM)

    decay, from_start, to_end, whole = [], [], [], []
    for p in range(N_PAIRS):
        lg64 = lg64_ref[0, :, p * LANES:(p + 1) * LANES]
        if not reverse:
            wide = slice(2 * p * c_len, 2 * (p + 1) * c_len)
            decay.append(jnp.where(l_io >= s_io, jnp.exp(lg128_ref[0, :, wide] * ahead), 0.0)
                         + jnp.where(s_io >= l_io, jnp.exp(lg128_ref[1, :, wide] * behind), 0.0))
        from_start.append(jnp.exp(lg64 * (pos + 1.0)))
        to_end.append(jnp.exp(lg64 * (float(c_len - 1) - pos)))
        whole.append(jnp.exp(lg64 * float(c_len)))

    order = _scan_order(nc, reverse)
    inst = [(p, c) for c in order for p in range(N_PAIRS)]
    q, k, v = [], [], []
    for p, c in inst:
        rows = slice(c * c_len, (c + 1) * c_len)
        cols = slice(p * LANES, (p + 1) * LANES)
        q.append(q_ref[0, rows, cols])
        k.append(k_ref[0, rows, cols])
        v.append(v_ref[0, rows, cols])
    if not reverse:
        sc = [(_mm_nt(q[i], _bd(k[i])) * decay[p]).astype(BF16) for i, (p, c) in enumerate(inst)]
        y_intra = [_mm(sc[i], _bd(v[i])) for i in range(len(inst))]
    upd = [jnp.where(same_head, _mm((k[i] * to_end[p]).T, v[i]), 0.0) for i, (p, c) in enumerate(inst)]
    qs = [(q[i] * from_start[p]).astype(BF16) for i, (p, c) in enumerate(inst)]

    s = [s_ref[p] for p in range(N_PAIRS)]
    for ci, c in enumerate(order):
        for p in range(N_PAIRS):
            i = ci * N_PAIRS + p
            y = _mm(qs[i], s[p])
            if not reverse:
                y = y + y_intra[i]
            y_ref[0, c * c_len:(c + 1) * c_len, p * LANES:(p + 1) * LANES] = y.astype(BF16)
            s[p] = s[p] * whole[p] + upd[i]
    for p in range(N_PAIRS):
        s_ref[p] = s[p]


def _retention(ret, lg64, lg128, direction, tb=512):
    b, t, _ = ret.shape
    nt = t // tb
    reverse = direction == 1
    tmap = (lambda i: nt - 1 - i) if reverse else (lambda i: i)
    col = lambda blk: pl.BlockSpec((1, tb, WIDTH), lambda bb, i: (bb, tmap(i), blk))
    return pl.pallas_call(
        functools.partial(_ret_kernel, nc=tb // LIN_CHUNK, reverse=reverse),
        grid=(b, nt),
        in_specs=[col(0), col(1), col(2),
                  pl.BlockSpec((1, 1, WIDTH), lambda bb, i: (direction, 0, 0)),
                  pl.BlockSpec((2, 1, N_HEADS * LIN_CHUNK), lambda bb, i: (0, 0, 0))],
        out_specs=col(0),
        out_shape=jax.ShapeDtypeStruct((b, t, WIDTH), BF16),
        scratch_shapes=[pltpu.VMEM((N_PAIRS, LANES, LANES), F32)],
        compiler_params=_cparams("parallel", "arbitrary"),
        name=f"retention_{'bwd' if reverse else 'fwd'}",
    )(ret, ret, ret, lg64, lg128)


def _ssd_conv_kernel(x_ref, xp_ref, xn_ref, w_ref, b_ref, xs_out, bc_out, *, tm):
    i = pl.program_id(1)
    nt = pl.num_programs(1)
    prev = jnp.where(i > 0, xp_ref[0].astype(F32), 0.0)
    nxt = jnp.where(i < nt - 1, xn_ref[0].astype(F32), 0.0)
    xe = jnp.concatenate([prev, x_ref[0].astype(F32), nxt], axis=0)
    ext = tm + 2 * HALO_ROWS
    pad = SSD_CONV // 2
    acc = jnp.zeros((tm, SSD_XBC), F32) + b_ref[...]
    for j in range(SSD_CONV):
        shift = (pad - j) % ext
        xj = xe if shift == 0 else pltpu.roll(xe, shift, 0)
        acc = acc + xj[HALO_ROWS:HALO_ROWS + tm] * w_ref[j:j + 1, :]
    y = _silu(acc).astype(BF16)
    xs_out[0] = y[:, :WIDTH]
    bc_out[0] = y[:, WIDTH:]


def _ssd_conv(xz, w, bias, tm=512):
    b, t, _ = xz.shape
    nt = t // tm
    per = tm // HALO_ROWS
    const = lambda shape: pl.BlockSpec(shape, lambda bb, i: (0, 0))
    main = pl.BlockSpec((1, tm, SSD_XBC), lambda bb, i: (bb, i, 0))
    prev = pl.BlockSpec((1, HALO_ROWS, SSD_XBC), lambda bb, i: (bb, jnp.maximum(i * per - 1, 0), 0))
    nxt = pl.BlockSpec((1, HALO_ROWS, SSD_XBC),
                       lambda bb, i: (bb, jnp.minimum((i + 1) * per, t // HALO_ROWS - 1), 0))
    out = pl.BlockSpec((1, tm, WIDTH), lambda bb, i: (bb, i, 0))
    return pl.pallas_call(
        functools.partial(_ssd_conv_kernel, tm=tm),
        grid=(b, nt),
        in_specs=[main, prev, nxt, const((8, SSD_XBC)), const((1, SSD_XBC))],
        out_specs=[out, out],
        out_shape=[jax.ShapeDtypeStruct((b, t, WIDTH), BF16)] * 2,
        compiler_params=_cparams("parallel", "parallel"),
        name="ssd_conv",
    )(xz, xz, xz, w, bias)


def _ssd_kernel(xs_ref, bc_ref, dtc_ref, dtr_ref, bias_c_ref, bias_r_ref, nega_c_ref, nega_r_ref,
                tri_ref, trit_ref, y_ref, s_ref, *, nc, reverse):
    c_len = LIN_CHUNK
    per_group = N_HEADS // SSD_GROUPS
    gw = WIDTH // SSD_GROUPS

    @pl.when(pl.program_id(1) == 0)
    def _():
        s_ref[...] = jnp.zeros_like(s_ref)

    tri = tri_ref[...]
    trit = trit_ref[...]
    dts_c = _softplus(dtc_ref[0, 0] + bias_c_ref[0])
    la_c = dts_c * nega_c_ref[0]
    dts_r = _softplus(dtr_ref[0, 0] + bias_r_ref[0])
    la_r = dts_r * nega_r_ref[0]
    l_io = _iota((c_len, c_len), 0)
    s_io = _iota((c_len, c_len), 1)
    causal = (s_io >= l_io) if reverse else (s_io <= l_io)
    lo = _lane_lo((c_len, LANES))

    def by_head128(cols, g):
        return jnp.concatenate([jnp.broadcast_to(cols[:, g * per_group + h:g * per_group + h + 1],
                                                 (c_len, LANES)) for h in range(per_group)], axis=1)

    def to64(wide):
        h = [wide[:, j * LANES:(j + 1) * LANES] for j in range(per_group)]
        return jnp.concatenate([jnp.where(lo, h[0], h[1]), jnp.where(lo, h[2], h[3])], axis=1)

    order = _scan_order(nc, reverse)
    inst = [(g, c) for c in order for g in range(SSD_GROUPS)]
    n = len(inst)
    rows_of = lambda c: slice(c * c_len, (c + 1) * c_len)
    cum = [_mm_exact_left(tri, by_head128(la_c[rows_of(c)], g)) for g, c in inst]
    cum_r = {c: _mm_exact_right(la_r[:, rows_of(c)], trit) for c in order}
    last = [u[0:1] if reverse else u[c_len - 1:c_len] for u in cum]
    bm = [bc_ref[0, rows_of(c), g * SSD_STATE:(g + 1) * SSD_STATE] for g, c in inst]
    cm = [bc_ref[0, rows_of(c), (SSD_GROUPS + g) * SSD_STATE:(SSD_GROUPS + g + 1) * SSD_STATE]
          for g, c in inst]
    xdt = [xs_ref[0, rows_of(c), g * gw:(g + 1) * gw].astype(F32) * to64(by_head128(dts_c[rows_of(c)], g))
           for g, c in inst]
    gm = [_mm_nt(cm[i], bm[i]) for i in range(n)]
    y_intra = []
    for i, (g, c) in enumerate(inst):
        parts = []
        for pp in range(per_group // 2):
            wide = []
            for hh in (2 * pp, 2 * pp + 1):
                head = g * per_group + hh
                diff = cum[i][:, hh * LANES:(hh + 1) * LANES] - cum_r[c][head:head + 1, :]
                dec = jnp.where(causal, jnp.exp(jnp.where(causal, diff, 0.0)), 0.0)
                wide.append((gm[i] * dec).astype(BF16))
            parts.append(_mm(jnp.concatenate(wide, axis=1), _bd(xdt[i][:, pp * LANES:(pp + 1) * LANES])))
        y_intra.append(jnp.concatenate(parts, axis=1))
    from_start = [to64(jnp.exp(u)) for u in cum]
    upd = [_mm(bm[i].astype(F32).T, xdt[i] * to64(jnp.exp(last[i] - cum[i]))) for i in range(n)]
    whole = [to64(jnp.exp(u)) for u in last]

    s = [s_ref[g] for g in range(SSD_GROUPS)]
    for ci, c in enumerate(order):
        for g in range(SSD_GROUPS):
            i = ci * SSD_GROUPS + g
            y_ref[0, rows_of(c), g * gw:(g + 1) * gw] = (
                y_intra[i] + _mm(cm[i], s[g]) * from_start[i]).astype(BF16)
            s[g] = s[g] * whole[i] + upd[i]
    for g in range(SSD_GROUPS):
        s_ref[g] = s[g]


def _ssd_scan(xs, bc, dtc, dtr, sp, direction, tb=512):
    b, t, _ = xs.shape
    nt = t // tb
    reverse = direction == 1
    tmap = (lambda i: nt - 1 - i) if reverse else (lambda i: i)
    row = pl.BlockSpec((1, tb, WIDTH), lambda bb, i: (bb, tmap(i), 0))
    par = lambda shape: pl.BlockSpec((1,) + shape, lambda bb, i: (direction, 0, 0))
    tri = pl.BlockSpec((LIN_CHUNK, LIN_CHUNK), lambda bb, i: (0, 0))
    return pl.pallas_call(
        functools.partial(_ssd_kernel, nc=tb // LIN_CHUNK, reverse=reverse),
        grid=(b, nt),
        in_specs=[row, row,
                  pl.BlockSpec((1, 1, tb, N_HEADS), lambda bb, i: (direction, bb, tmap(i), 0)),
                  pl.BlockSpec((1, 1, N_HEADS, tb), lambda bb, i: (direction, bb, 0, tmap(i))),
                  par((1, N_HEADS)), par((N_HEADS, 1)), par((1, N_HEADS)), par((N_HEADS, 1)),
                  tri, tri],
        out_specs=row,
        out_shape=jax.ShapeDtypeStruct((b, t, WIDTH), BF16),
        scratch_shapes=[pltpu.VMEM((SSD_GROUPS, SSD_STATE, WIDTH // SSD_GROUPS), F32)],
        compiler_params=_cparams("parallel", "arbitrary"),
        name=f"ssd_scan_{'bwd' if reverse else 'fwd'}",
    )(xs, bc, dtc, dtr, sp["bias_c"], sp["bias_r"], sp["nega_c"], sp["nega_r"],
      sp["tri_rev" if reverse else "tri_fwd"], sp["trit_rev" if reverse else "trit_fwd"])


def _odd_out_kernel(h_ref, rf, rb, gate, sf, sb, xs, z, gng, gnb, dsk, ng, ones_ref, w_ref, out_ref):
    f32 = lambda ref: ref[...].astype(F32)
    ones = ones_ref[...]
    inv = 1.0 / HEAD_DIM
    y = f32(rf) + f32(rb)
    yc = y - _group_sum(y, ones) * inv
    var = _group_sum(yc * yc, ones) * inv
    y_c = (yc * lax.rsqrt(var + GN_EPS) * gng[...] + gnb[...]) * _silu(f32(gate))
    yd = (f32(sf) + f32(sb) + dsk[...] * f32(xs)) * _silu(f32(z))
    gw = WIDTH // SSD_GROUPS
    halves = []
    for gi in range(SSD_GROUPS):
        yg = yd[:, gi * gw:(gi + 1) * gw]
        halves.append(yg * lax.rsqrt(jnp.mean(yg * yg, -1, keepdims=True) + NORM_EPS))
    y_d = jnp.concatenate(halves, axis=1) * ng[...]
    out_ref[...] = h_ref[...] + _mm(y_c, w_ref[0:WIDTH, :]) + _mm(y_d, w_ref[WIDTH:, :])


def _odd_out(h2, rf, rb, ret2, sf, sb, xs, xz2, gng, gnb, dsk, ng, ones_bd, w, tm=512):
    m = h2.shape[0]
    const = lambda shape: pl.BlockSpec(shape, lambda i: (0, 0))
    row = lambda n: pl.BlockSpec((tm, n), lambda i: (i, 0))
    vec = const((1, WIDTH))
    return pl.pallas_call(
        _odd_out_kernel,
        grid=(m // tm,),
        in_specs=[row(D_MODEL), row(WIDTH), row(WIDTH),
                  pl.BlockSpec((tm, WIDTH), lambda i: (i, 3)),
                  row(WIDTH), row(WIDTH), row(WIDTH),
                  pl.BlockSpec((tm, WIDTH), lambda i: (i, 2)),
                  vec, vec, vec, vec, const((WIDTH, WIDTH)), const((D_MODEL, D_MODEL))],
        out_specs=row(D_MODEL),
        out_shape=jax.ShapeDtypeStruct((m, D_MODEL), F32),
        compiler_params=_cparams("parallel"),
        name="odd_out",
    )(h2, rf, rb, ret2, sf, sb, xs, xz2, gng, gnb, dsk, ng, ones_bd, w)


def _tri(n, reverse):
    i = np.arange(n)
    m = (i[None, :] >= i[:, None]) if reverse else (i[None, :] <= i[:, None])
    return m.astype(np.float32)


def _prepare(norm_mix, norm_mlp, mlp_w1, mlp_w2, even_in_w, even_out_w, attn_q_gain, attn_k_gain,
             rwkv_mu_prev, rwkv_mu_next, rwkv_w0, rwkv_w2, rwkv_a0, rwkv_a2, rwkv_g2, rwkv_k_k,
             rwkv_k_a, rwkv_r_k, rwkv_ln_g, rwkv_ln_b, odd_in_w, odd_out_w, ret_decay_exp, ret_gn_g,
             ret_gn_b, ssd_conv_w, ssd_conv_b, ssd_dt_bias, ssd_a_log, ssd_d, ssd_norm_g):
    row = lambda v: v.reshape(1, -1).astype(F32)
    heads = np.arange(WIDTH) // HEAD_DIM
    ones_bd = jnp.asarray(heads[:, None] == heads[None, :], BF16)
    p = {"ones_bd": ones_bd, "norm_mix": norm_mix, "norm_mlp": norm_mlp,
         "mlp_w1": mlp_w1.astype(BF16), "mlp_w2": mlp_w2.astype(BF16)}

    w_in = even_in_w[0]
    p["even_wa"] = w_in[:, :3 * WIDTH].astype(BF16)
    p["even_wr"] = w_in[:, 3 * WIDTH:].astype(BF16)
    p["even_out_w"] = even_out_w[0].astype(BF16)
    p["q_gain"] = row(jnp.tile(attn_q_gain[0], N_HEADS))
    p["k_gain"] = row(jnp.tile(attn_k_gain[0], N_HEADS))

    def two_dir_lowrank(w):
        z = jnp.zeros_like(w[0])
        return jnp.concatenate([jnp.concatenate([w[0], z], 1), jnp.concatenate([z, w[1]], 1)], 0)

    p["rwkv"] = {
        "mu_prev": row(rwkv_mu_prev[0]), "mu_next": row(rwkv_mu_next[0]),
        "w0": row(rwkv_w0[0]), "w2": two_dir_lowrank(rwkv_w2[0]).astype(BF16),
        "a0": row(rwkv_a0[0]), "a2": two_dir_lowrank(rwkv_a2[0]).astype(BF16),
        "g2": rwkv_g2[0].astype(BF16), "k_k": row(rwkv_k_k[0]), "k_a": row(rwkv_k_a[0]),
        "r_k": row(rwkv_r_k[0]), "ones_bd": ones_bd,
    }
    p["rwkv_tri"] = [jnp.asarray(_tri(RWKV_CHUNK, False), BF16), jnp.asarray(_tri(RWKV_CHUNK, True), BF16)]
    p["rwkv_ln_g"] = row(rwkv_ln_g[0])
    p["rwkv_ln_b"] = row(rwkv_ln_b[0])

    w_odd = odd_in_w[0]
    p["odd_wr"] = w_odd[:, :4 * WIDTH].astype(BF16)
    z_w = w_odd[:, 4 * WIDTH:5 * WIDTH]
    xbc_w = w_odd[:, 5 * WIDTH:5 * WIDTH + SSD_XBC]
    dt_w = w_odd[:, 5 * WIDTH + SSD_XBC:]
    p["odd_ws"] = jnp.concatenate([xbc_w, z_w], axis=1).astype(BF16)
    p["odd_wd"] = jnp.concatenate([dt_w, jnp.zeros((D_MODEL, LANES - 2 * N_HEADS), F32)], 1).astype(BF16)
    p["odd_out_w"] = odd_out_w[0].astype(BF16)

    log_gamma = jnp.log1p(-jnp.exp2(-ret_decay_exp[0].astype(F32)))
    p["lg64"] = jnp.repeat(log_gamma, HEAD_DIM, axis=1).reshape(2, 1, WIDTH)
    p["lg128"] = jnp.repeat(log_gamma, LIN_CHUNK, axis=1).reshape(2, 1, N_HEADS * LIN_CHUNK)
    p["ret_gn_g"] = row(ret_gn_g[0])
    p["ret_gn_b"] = row(ret_gn_b[0])

    p["conv_w"] = jnp.concatenate([ssd_conv_w[0], jnp.zeros((8 - SSD_CONV, SSD_XBC), F32)], 0)
    p["conv_b"] = row(ssd_conv_b[0])
    bias = ssd_dt_bias[0].astype(F32)
    nega = -jnp.exp(ssd_a_log[0].astype(F32))
    p["ssd"] = {
        "bias_c": bias[:, None, :], "nega_c": nega[:, None, :],
        "bias_r": bias[:, :, None], "nega_r": nega[:, :, None],
        "tri_fwd": jnp.asarray(_tri(LIN_CHUNK, False), BF16),
        "tri_rev": jnp.asarray(_tri(LIN_CHUNK, True), BF16),
        "trit_fwd": jnp.asarray(_tri(LIN_CHUNK, False).T, BF16),
        "trit_rev": jnp.asarray(_tri(LIN_CHUNK, True).T, BF16),
    }
    p["ssd_d"] = row(jnp.repeat(ssd_d[0], HEAD_DIM))
    p["ssd_norm_g"] = row(ssd_norm_g[0])
    return p


def _rope_tables(t):
    half = HEAD_DIM // 2
    inv = ROPE_BASE ** (-jnp.arange(half, dtype=F32) / half)
    ang = jnp.arange(t, dtype=F32)[:, None] * inv[None, :]
    cos = jnp.tile(jnp.cos(ang), (1, LANES // half))
    sin = jnp.sin(ang)
    sin_signed = jnp.tile(jnp.concatenate([-sin, sin], axis=1), (1, LANES // HEAD_DIM))
    return cos, sin_signed


def _even_mixer(h2, b, t, p):
    h = h2.reshape(b, t, D_MODEL)
    *qkvs, rcols = _even_in(h, p["norm_mix"][0:1], p["even_wa"], p["even_wr"], p["ones_bd"],
                            p["q_gain"], p["k_gain"])
    branch = []
    for qkv, (_, dil) in zip(qkvs, DILATED_BRANCHES):
        branch.extend(_dil_attn(qkv, dil))
    r, v, kkn, g, bonus, g_in, krep, bvec = _rwkv_prep(rcols, p["rwkv"], p["rwkv_tri"])
    ys = [_rwkv_scan(r, v, kkn, g_in, krep, bvec, d) for d in range(2)]
    out = _even_out(h, branch, [ys[0], ys[1], bonus, g], p["rwkv_ln_g"], p["rwkv_ln_b"],
                    p["ones_bd"], p["even_out_w"])
    return out.reshape(b * t, D_MODEL)


def _odd_mixer(h2, b, t, p):
    cos, sin = _rope_tables(t)
    ret2, xz2, dt2 = _odd_in(h2, p["norm_mix"][1:2], p["odd_wr"], p["odd_ws"], p["odd_wd"], cos, sin)
    ret = ret2.reshape(b, t, -1)
    xz = xz2.reshape(b, t, -1)
    rets = [_retention(ret, p["lg64"], p["lg128"], d) for d in range(2)]
    xs, bc = _ssd_conv(xz, p["conv_w"], p["conv_b"])
    dt = dt2[:, :2 * N_HEADS].reshape(b, t, 2, N_HEADS)
    dtc = dt.transpose(2, 0, 1, 3)
    dtr = dt.transpose(2, 0, 3, 1)
    ssds = [_ssd_scan(xs, bc, dtc, dtr, p["ssd"], d) for d in range(2)]
    flat = lambda u: u.reshape(b * t, -1)
    return _odd_out(h2, flat(rets[0]), flat(rets[1]), ret2, flat(ssds[0]), flat(ssds[1]), flat(xs),
                    xz2, p["ret_gn_g"], p["ret_gn_b"], p["ssd_d"], p["ssd_norm_g"], p["ones_bd"],
                    p["odd_out_w"])


def _trunk(x, p):
    b, t, _ = x.shape
    h2 = x.reshape(b * t, D_MODEL)
    h2 = _even_mixer(h2, b, t, p)
    h2 = _mlp(h2, p["norm_mlp"][0:1], p["mlp_w1"][0], p["mlp_w2"][0])
    h2 = _odd_mixer(h2, b, t, p)
    h2 = _mlp(h2, p["norm_mlp"][1:2], p["mlp_w1"][1], p["mlp_w2"][1])
    return h2.reshape(b, t, D_MODEL)


def kernel(x_prompt, x_sample, norm_mix, norm_mlp, mlp_w1, mlp_w2, even_in_w, even_out_w, attn_q_gain, attn_k_gain, rwkv_mu_prev, rwkv_mu_next, rwkv_w0, rwkv_w2, rwkv_a0, rwkv_a2, rwkv_g2, rwkv_k_k, rwkv_k_a, rwkv_r_k, rwkv_ln_g, rwkv_ln_b, odd_in_w, odd_out_w, ret_decay_exp, ret_gn_g, ret_gn_b, ssd_conv_w, ssd_conv_b, ssd_dt_bias, ssd_a_log, ssd_d, ssd_norm_g):
    p = _prepare(norm_mix, norm_mlp, mlp_w1, mlp_w2, even_in_w, even_out_w, attn_q_gain, attn_k_gain,
                 rwkv_mu_prev, rwkv_mu_next, rwkv_w0, rwkv_w2, rwkv_a0, rwkv_a2, rwkv_g2, rwkv_k_k,
                 rwkv_k_a, rwkv_r_k, rwkv_ln_g, rwkv_ln_b, odd_in_w, odd_out_w, ret_decay_exp,
                 ret_gn_g, ret_gn_b, ssd_conv_w, ssd_conv_b, ssd_dt_bias, ssd_a_log, ssd_d, ssd_norm_g)
    return (_trunk(x_prompt, p), _trunk(x_sample, p))
```

```python
import functools
import math

import numpy as np
import jax
import jax.numpy as jnp
from jax import lax
from jax.experimental import pallas as pl
from jax.experimental.pallas import tpu as pltpu

F32 = jnp.float32
BF16 = jnp.bfloat16

D_MODEL = 1024
HEAD_DIM = 64
N_HEADS = 8
WIDTH = N_HEADS * HEAD_DIM
N_PAIRS = N_HEADS // 2
LANES = 128
HALO_ROWS = 16
LSE_LANES = LANES // N_HEADS
DILATED_BRANCHES = ((128, 1), (512, 4), (2048, 16))
ATT_R = 64
RWKV_DECAY_RANK = 64
RWKV_ICL_RANK = 64
RWKV_GATE_RANK = 128
RWKV_COLS = 3 * WIDTH + 2 * RWKV_DECAY_RANK + 2 * RWKV_ICL_RANK + RWKV_GATE_RANK
SSD_STATE = 128
SSD_GROUPS = 2
SSD_CONV = 5
SSD_XBC = WIDTH + 2 * SSD_GROUPS * SSD_STATE
D_FF = 4 * D_MODEL
NORM_EPS = 1e-6
GN_EPS = 1e-5
RWKV_GN_EPS = 64e-5
ROPE_BASE = 10000.0
RWKV_CHUNK = 64
LIN_CHUNK = 128
NEG_BIG = -1e30
LOG2_E = math.log2(math.e)
VMEM_LIMIT = 56 * 1024 * 1024


def _cparams(*sem):
    return pltpu.CompilerParams(dimension_semantics=sem, vmem_limit_bytes=VMEM_LIMIT)


def _iota(shape, dim):
    return lax.broadcasted_iota(jnp.int32, shape, dim)


def _lane_lo(shape):
    return (_iota(shape, len(shape) - 1) & HEAD_DIM) == 0


def _bd(x):
    lo = _lane_lo(x.shape)
    zero = jnp.zeros_like(x)
    return jnp.concatenate([jnp.where(lo, x, zero), jnp.where(lo, zero, x)], axis=0)


def _mm(a, b):
    return jnp.dot(a.astype(BF16), b.astype(BF16), preferred_element_type=F32)


def _mm_nt(a, b):
    return lax.dot_general(a.astype(BF16), b.astype(BF16), (((1,), (1,)), ((), ())),
                           preferred_element_type=F32)


def _split_bf16(x, terms):
    parts = []
    for _ in range(terms - 1):
        hi = x.astype(BF16)
        parts.append(hi)
        x = x - hi.astype(F32)
    parts.append(x.astype(BF16))
    return parts


def _mm_exact_left(t, x, terms=3):
    return sum(jnp.dot(t, piece, preferred_element_type=F32) for piece in _split_bf16(x, terms))


def _mm_exact_right(x, t, terms=3):
    return sum(jnp.dot(piece, t, preferred_element_type=F32) for piece in _split_bf16(x, terms))


def _group_sum(x, ones_bd):
    return jnp.dot(x.astype(BF16), ones_bd, preferred_element_type=F32)


def _rms_rows(x, g):
    return x * lax.rsqrt(jnp.mean(x * x, -1, keepdims=True) + NORM_EPS) * g


def _sigmoid(x):
    return 0.5 * jnp.tanh(0.5 * x) + 0.5


def _silu(x):
    return x * _sigmoid(x)


def _softplus(x):
    return jnp.maximum(x, 0.0) + jnp.log1p(jnp.exp(-jnp.abs(x)))


def _scan_order(nc, reverse):
    return list(range(nc - 1, -1, -1) if reverse else range(nc))


def _even_in_kernel(x_ref, g_ref, wa_ref, wr_ref, ones_ref, qg_ref, kg_ref, perm4_ref, perm16_ref,
                    qkv1_out, qkv4_out, qkv16_out, r_out, *, tm):
    ub = _rms_rows(x_ref[0], g_ref[...]).astype(BF16)
    a = jnp.dot(ub, wa_ref[...], preferred_element_type=F32)
    q = a[:, :WIDTH]
    k = a[:, WIDTH:2 * WIDTH]
    ones = ones_ref[...]
    inv = 1.0 / HEAD_DIM
    qn = q * lax.rsqrt(_group_sum(q * q, ones) * inv + NORM_EPS) * qg_ref[...]
    kn = k * lax.rsqrt(_group_sum(k * k, ones) * inv + NORM_EPS) * kg_ref[...]
    qkv = jnp.concatenate([qn * (HEAD_DIM ** -0.5 * LOG2_E), kn, a[:, 2 * WIDTH:]], axis=1).astype(BF16)
    qkv1_out[0, 0] = qkv
    for dil, perm_ref, out in ((4, perm4_ref, qkv4_out), (16, perm16_ref, qkv16_out)):
        perm = perm_ref[...]
        bs = perm.shape[0]
        per = bs // dil
        for blk in range(tm // bs):
            moved = jnp.dot(perm, qkv[blk * bs:(blk + 1) * bs],
                            preferred_element_type=F32).astype(BF16)
            for rho in range(dil):
                out[0, rho, blk * per:(blk + 1) * per, :] = moved[rho * per:(rho + 1) * per]
    r_out[0] = jnp.dot(ub, wr_ref[...], preferred_element_type=F32).astype(BF16)


def _even_in(x, g, wa, wr, ones_bd, qg, kg, tm=512):
    b, t, _ = x.shape
    const = lambda shape: pl.BlockSpec(shape, lambda bb, i: (0, 0))
    row = lambda n: pl.BlockSpec((1, tm, n), lambda bb, i: (bb, i, 0))
    res = lambda d: pl.BlockSpec((1, d, tm // d, 3 * WIDTH), lambda bb, i: (bb, 0, i, 0))
    dils = [d for _, d in DILATED_BRANCHES]

    def to_residue_major(d, bs):
        src = np.arange(bs).reshape(bs // d, d).T.reshape(-1)
        return jnp.asarray(src[:, None] == np.arange(bs)[None, :], BF16)

    blocks = {4: 2 * HALO_ROWS * 4, 16: HALO_ROWS * 16}

    return pl.pallas_call(
        functools.partial(_even_in_kernel, tm=tm),
        grid=(b, t // tm),
        in_specs=[row(D_MODEL), const((1, D_MODEL)), const((D_MODEL, 3 * WIDTH)),
                  const((D_MODEL, RWKV_COLS)), const((WIDTH, WIDTH)),
                  const((1, WIDTH)), const((1, WIDTH)),
                  const((blocks[4], blocks[4])), const((blocks[16], blocks[16]))],
        out_specs=[res(d) for d in dils] + [row(RWKV_COLS)],
        out_shape=[jax.ShapeDtypeStruct((b, d, t // d, 3 * WIDTH), BF16) for d in dils]
        + [jax.ShapeDtypeStruct((b, t, RWKV_COLS), BF16)],
        compiler_params=_cparams("parallel", "parallel"),
        name="even_in",
    )(x, g, wa, wr, ones_bd, qg, kg, to_residue_major(4, blocks[4]), to_residue_major(16, blocks[16]))


def _dil_attn_kernel(q_ref, k_ref, kl_ref, kr_ref, v_ref, vl_ref, vr_ref, o_ref, lse_ref,
                     kext, vext, *, tq, n_res, sub_len, dil):
    r = ATT_R
    i = pl.program_id(2)
    for z in range(n_res):
        kext[z, 0:r] = kl_ref[0, z]
        kext[z, r:r + tq] = k_ref[0, z]
        kext[z, r + tq:] = kr_ref[0, z]
        vext[z, 0:r] = vl_ref[0, z]
        vext[z, r:r + tq] = v_ref[0, z]
        vext[z, r + tq:] = vr_ref[0, z]

    c_io = _iota((r, 3 * r), 1)
    q_io = _iota((r, 3 * r), 0)
    rel = c_io - r - q_io
    near = jnp.abs(rel) <= r
    dist = jnp.abs(rel).astype(F32) * float(dil)
    lo = _lane_lo((r, LANES))
    near2 = jnp.concatenate([near, near], axis=0)
    dist2 = jnp.concatenate([dist, dist], axis=0)
    first_head = _iota((2 * r, 3 * r), 0) < r
    cols = [slice(p * LANES, (p + 1) * LANES) for p in range(N_PAIRS)]
    bias = [jnp.where(first_head, 2.0 ** (-(2 * p + 1)), 2.0 ** (-(2 * p + 2))) * (LOG2_E * dist2)
            for p in range(N_PAIRS)]

    blocks = [(z, j) for z in range(n_res) for j in range(tq // r)]
    group = 2 if len(blocks) % 2 == 0 else 1
    for g0 in range(0, len(blocks), group):
        inst = [(z, j, p) for z, j in blocks[g0:g0 + group] for p in range(N_PAIRS)]
        valid = {}
        for z, j in blocks[g0:g0 + group]:
            kpos = i * tq + j * r + _iota((2 * r, 3 * r), 1) - r
            valid[j] = near2 & (kpos >= 0) & (kpos < sub_len)
        s = [jnp.where(valid[j],
                       _mm_nt(_bd(q_ref[0, z, j * r:(j + 1) * r, cols[p]]),
                              kext[z, j * r:(j + 3) * r, cols[p]]) - bias[p], NEG_BIG)
             for z, j, p in inst]
        m = [jnp.max(u, -1, keepdims=True) for u in s]
        e = [jnp.exp2(u - mm) for u, mm in zip(s, m)]
        den = [jnp.sum(u, -1, keepdims=True) for u in e]
        pv = [_mm(e[n], vext[z, j * r:(j + 3) * r, cols[p]]) / den[n]
              for n, (z, j, p) in enumerate(inst)]
        lse = [m[n] * (1.0 / LOG2_E) + jnp.log(den[n]) for n in range(len(inst))]
        for n, (z, j, p) in enumerate(inst):
            o_ref[0, z, j * r:(j + 1) * r, cols[p]] = jnp.where(lo, pv[n][:r], pv[n][r:]).astype(BF16)
        lane = _iota((r, LANES), 1)
        for bi, (z, j) in enumerate(blocks[g0:g0 + group]):
            tile = jnp.zeros((r, LANES), F32)
            for p in range(N_PAIRS):
                col = lse[bi * N_PAIRS + p]
                for hh in range(2):
                    h0 = (2 * p + hh) * LSE_LANES
                    tile = jnp.where((lane >= h0) & (lane < h0 + LSE_LANES),
                                     col[hh * r:(hh + 1) * r], tile)
            lse_ref[0, z, j * r:(j + 1) * r, :] = tile


def _dil_attn(qkv, dil, rows_per_step=512):
    b, _, sub_len, _ = qkv.shape
    tq = min(sub_len, rows_per_step)
    n_res = min(dil, rows_per_step // tq)
    nblk = sub_len // ATT_R
    per = tq // ATT_R
    main = lambda col: pl.BlockSpec((1, n_res, tq, WIDTH), lambda bb, rr, i: (bb, rr, i, col))
    left = lambda col: pl.BlockSpec((1, n_res, ATT_R, WIDTH),
                                    lambda bb, rr, i: (bb, rr, jnp.maximum(i * per - 1, 0), col))
    right = lambda col: pl.BlockSpec((1, n_res, ATT_R, WIDTH),
                                     lambda bb, rr, i: (bb, rr, jnp.minimum((i + 1) * per, nblk - 1), col))
    return pl.pallas_call(
        functools.partial(_dil_attn_kernel, tq=tq, n_res=n_res, sub_len=sub_len, dil=dil),
        grid=(b, dil // n_res, sub_len // tq),
        in_specs=[main(0), main(1), left(1), right(1), main(2), left(2), right(2)],
        out_specs=[main(0), pl.BlockSpec((1, n_res, tq, LANES), lambda bb, rr, i: (bb, rr, i, 0))],
        out_shape=[jax.ShapeDtypeStruct((b, dil, sub_len, WIDTH), BF16),
                   jax.ShapeDtypeStruct((b, dil, sub_len, LANES), F32)],
        scratch_shapes=[pltpu.VMEM((n_res, tq + 2 * ATT_R, WIDTH), BF16)] * 2,
        compiler_params=_cparams("parallel", "parallel", "parallel"),
        name=f"dil_attn_d{dil}",
    )(qkv, qkv, qkv, qkv, qkv, qkv, qkv)


def _rwkv_prep_kernel(x_ref, xp_ref, xn_ref, mup_ref, mun_ref, w0_ref, w2_ref, a0_ref, a2_ref,
                      g2_ref, kk_ref, ka_ref, rk_ref, ones_ref, trif_ref, trir_ref,
                      r_out, v_out, kkn_out, g_out, bonus_out, g_in_out, krep_out, b_out, *, tm):
    i = pl.program_id(1)
    nt = pl.num_programs(1)
    x = x_ref[0].astype(F32)
    prow = jnp.where(i > 0, xp_ref[0, HALO_ROWS - 1:HALO_ROWS, :].astype(F32), 0.0)
    nrow = jnp.where(i < nt - 1, xn_ref[0, 0:1, :].astype(F32), 0.0)
    rid = _iota((8, 1), 0)
    prev = pltpu.roll(x, 1, 0)
    prev = jnp.concatenate([jnp.where(rid == 0, prow, prev[:8]), prev[8:]], axis=0)
    nxt = pltpu.roll(x, tm - 1, 0)
    nxt = jnp.concatenate([nxt[:tm - 8], jnp.where(rid == 7, nrow, nxt[tm - 8:])], axis=0)
    xs = x + mup_ref[...] * (prev - x) + mun_ref[...] * (nxt - x)
    r = xs[:, 0:WIDTH]
    k = xs[:, WIDTH:2 * WIDTH]
    v = xs[:, 2 * WIDTH:3 * WIDTH]
    zw = xs[:, 3 * WIDTH:3 * WIDTH + LANES]
    za = xs[:, 3 * WIDTH + LANES:3 * WIDTH + 2 * LANES]
    zg = xs[:, 3 * WIDTH + 2 * LANES:]
    ones = ones_ref[...]
    kk = k * kk_ref[...]
    kkn = kk * lax.rsqrt(jnp.maximum(_group_sum(kk * kk, ones), 1e-24))
    g_out[0] = _mm(_sigmoid(zg), g2_ref[...]).astype(BF16)
    r_out[0] = r.astype(BF16)
    v_out[0] = v.astype(BF16)
    kkn_out[0] = kkn.astype(BF16)
    bonus_out[0] = (_group_sum(r * k * rk_ref[...], ones) * v).astype(BF16)
    wx = w0_ref[...] + _mm(jnp.tanh(zw), w2_ref[...])
    ax = a0_ref[...] + _mm(za, a2_ref[...])
    for d, tri_ref in enumerate((trif_ref, trir_ref)):
        cols = slice(d * WIDTH, (d + 1) * WIDTH)
        lw = -math.exp(-0.5) * _sigmoid(wx[:, cols])
        tri = tri_ref[...]
        for c in range(tm // RWKV_CHUNK):
            rows = slice(c * RWKV_CHUNK, (c + 1) * RWKV_CHUNK)
            g_in_out[d, 0, rows, :] = _mm_exact_left(tri, lw[rows], terms=2)
        a = _sigmoid(ax[:, cols])
        krep_out[d, 0] = (k * (1.0 + (a - 1.0) * ka_ref[...])).astype(BF16)
        b_out[d, 0] = (kkn * a).astype(BF16)


def _rwkv_prep(rcols, p, tri, tm=512):
    b, t, _ = rcols.shape
    nt = t // tm
    per = tm // HALO_ROWS
    const = lambda shape: pl.BlockSpec(shape, lambda bb, i: (0,) * len(shape))
    main = lambda n: pl.BlockSpec((1, tm, n), lambda bb, i: (bb, i, 0))
    dirs = pl.BlockSpec((2, 1, tm, WIDTH), lambda bb, i: (0, bb, i, 0))
    prev = pl.BlockSpec((1, HALO_ROWS, RWKV_COLS), lambda bb, i: (bb, jnp.maximum(i * per - 1, 0), 0))
    nxt = pl.BlockSpec((1, HALO_ROWS, RWKV_COLS),
                       lambda bb, i: (bb, jnp.minimum((i + 1) * per, t // HALO_ROWS - 1), 0))
    one = jax.ShapeDtypeStruct((b, t, WIDTH), BF16)
    two = lambda dt: jax.ShapeDtypeStruct((2, b, t, WIDTH), dt)
    return pl.pallas_call(
        functools.partial(_rwkv_prep_kernel, tm=tm),
        grid=(b, nt),
        in_specs=[main(RWKV_COLS), prev, nxt, const((1, RWKV_COLS)), const((1, RWKV_COLS)),
                  const((1, 2 * WIDTH)), const((LANES, 2 * WIDTH)),
                  const((1, 2 * WIDTH)), const((LANES, 2 * WIDTH)),
                  const((LANES, WIDTH)), const((1, WIDTH)), const((1, WIDTH)), const((1, WIDTH)),
                  const((WIDTH, WIDTH)), const((RWKV_CHUNK, RWKV_CHUNK)),
                  const((RWKV_CHUNK, RWKV_CHUNK))],
        out_specs=[main(WIDTH)] * 5 + [dirs] * 3,
        out_shape=[one] * 5 + [two(F32), two(BF16), two(BF16)],
        compiler_params=_cparams("parallel", "parallel"),
        name="rwkv_prep",
    )(rcols, rcols, rcols, p["mu_prev"], p["mu_next"], p["w0"], p["w2"], p["a0"], p["a2"],
      p["g2"], p["k_k"], p["k_a"], p["r_k"], p["ones_bd"], tri[0], tri[1])


def _rwkv_scan_kernel(r_ref, v_ref, kk_ref, g_ref, k_ref, b_ref, y_ref, s_ref, *, nc, reverse):
    c_len = RWKV_CHUNK

    @pl.when(pl.program_id(1) == 0)
    def _():
        s_ref[...] = jnp.zeros_like(s_ref)

    row = _iota((c_len, LANES), 0)
    li = _iota((c_len, LANES), 1) & (HEAD_DIM - 1)
    strict = (li > row) if reverse else (li < row)
    incl = (li >= row) if reverse else (li <= row)
    r2 = _iota((LANES, LANES), 0)
    c2 = _iota((LANES, LANES), 1)
    same_head = (r2 & HEAD_DIM) == (c2 & HEAD_DIM)
    diag = r2 == c2
    zeros = jnp.zeros((c_len, LANES), F32)

    order = _scan_order(nc, reverse)
    inst = [(p, c) for c in order for p in range(N_PAIRS)]
    n = len(inst)

    def tile(ref, p, c, lead=()):
        return ref[lead + (0, slice(c * c_len, (c + 1) * c_len), slice(p * LANES, (p + 1) * LANES))]

    g_in = [tile(g_ref, p, c, (0,)) for p, c in inst]
    g_tot = [g[0:1] if reverse else g[c_len - 1:c_len] for g in g_in]
    scan_first = row == (c_len - 1 if reverse else 0)
    g_ex = [jnp.where(scan_first, 0.0, pltpu.roll(g, c_len - 1 if reverse else 1, 0)) for g in g_in]
    v = [tile(v_ref, p, c) for p, c in inst]
    at, rt, w, bk_end = [], [], [], []
    for i, (p, c) in enumerate(inst):
        b = tile(b_ref, p, c, (0,)).astype(F32)
        k = tile(k_ref, p, c, (0,)).astype(F32)
        at.append(-tile(kk_ref, p, c).astype(F32) * jnp.exp(g_ex[i]))
        rt.append(tile(r_ref, p, c).astype(F32) * jnp.exp(g_in[i]))
        inv = jnp.exp(-g_in[i])
        to_end = jnp.exp(g_tot[i] - g_in[i])
        bk_end.append(jnp.concatenate([b * to_end, k * to_end], 0))
        w.append(_mm_nt(jnp.concatenate([at[i], rt[i]], 0),
                        jnp.concatenate([_bd(b * inv), _bd(k * inv)], 0)))

    bdv = [_bd(u) for u in v]
    m_ab = [jnp.where(strict, u[:c_len, :LANES], 0.0) for u in w]
    kv = [_mm(jnp.concatenate([jnp.where(strict, w[i][:c_len, LANES:], 0.0),
                               jnp.where(incl, w[i][c_len:, LANES:], 0.0)], 0), bdv[i])
          for i in range(n)]
    rhs = [jnp.concatenate([at[i], kv[i][:c_len]], axis=1) for i in range(n)]
    tinv = [jnp.where(li == row, 1.0, 0.0) + u for u in m_ab]
    mj = [u.astype(BF16) for u in m_ab]
    mj = [_mm(u, _bd(u)).astype(BF16) for u in mj]
    for j in range(1, 5):
        both = [_mm(jnp.concatenate([mj[i], tinv[i].astype(BF16)], 0), _bd(mj[i])) for i in range(n)]
        mj = [u[:c_len].astype(BF16) for u in both]
        tinv = [tinv[i] + both[i][c_len:] for i in range(n)]
    tinv = [tinv[i] + _mm(tinv[i], _bd(mj[i])) for i in range(n)]
    x = [_mm(tinv[i], _bd(rhs[i])) for i in range(n)]

    t1 = [_mm(jnp.where(incl, w[i][c_len:, :LANES], 0.0), _bd(x[i])) for i in range(n)]
    y0 = [t1[i][:, LANES:] + kv[i][c_len:] for i in range(n)]
    rp = [(rt[i] + t1[i][:, :LANES]).astype(BF16) for i in range(n)]
    t2 = [_mm(bk_end[i].T,
              jnp.concatenate([x[i], jnp.concatenate([zeros, v[i].astype(F32)], axis=1)], 0))
          for i in range(n)]
    g_low = [jnp.where(same_head, u[:, :LANES], 0.0).astype(BF16) for u in t2]
    h_new = [jnp.where(same_head, u[:, LANES:], 0.0) for u in t2]
    g_col = [jnp.exp(jnp.sum(jnp.where(diag, jnp.broadcast_to(g, (LANES, LANES)), 0.0),
                             axis=1, keepdims=True)) for g in g_tot]

    s = [s_ref[p] for p in range(N_PAIRS)]
    for ci, c in enumerate(order):
        ids = [ci * N_PAIRS + p for p in range(N_PAIRS)]
        both = [_mm(jnp.concatenate([rp[i], g_low[i]], 0), s[p]) for p, i in enumerate(ids)]
        for p, i in enumerate(ids):
            y_ref[0, c * c_len:(c + 1) * c_len, p * LANES:(p + 1) * LANES] = (
                both[p][:c_len] + y0[i]).astype(BF16)
        s = [s[p] * g_col[i] + both[p][c_len:] + h_new[i] for p, i in enumerate(ids)]
    for p in range(N_PAIRS):
        s_ref[p] = s[p]


def _rwkv_scan(r, v, kkn, g_in, krep, bvec, direction, tb=512):
    b, t, _ = r.shape
    nt = t // tb
    reverse = direction == 1
    tmap = (lambda i: nt - 1 - i) if reverse else (lambda i: i)
    one = pl.BlockSpec((1, tb, WIDTH), lambda bb, i: (bb, tmap(i), 0))
    two = pl.BlockSpec((1, 1, tb, WIDTH), lambda bb, i: (direction, bb, tmap(i), 0))
    return pl.pallas_call(
        functools.partial(_rwkv_scan_kernel, nc=tb // RWKV_CHUNK, reverse=reverse),
        grid=(b, nt),
        in_specs=[one, one, one, two, two, two],
        out_specs=one,
        out_shape=jax.ShapeDtypeStruct((b, t, WIDTH), BF16),
        scratch_shapes=[pltpu.VMEM((N_PAIRS, LANES, LANES), F32)],
        compiler_params=_cparams("parallel", "arbitrary"),
        name=f"rwkv_scan_{'bwd' if reverse else 'fwd'}",
    )(r, v, kkn, g_in, krep, bvec)


def _even_out_kernel(h_ref, o1, l1, o4, l4, o16, l16, yf, yb, bonus, g, lng, lnb, ones_ref, spread_ref,
                     w_ref, out_ref, tok_sc, lse_sc, *, tm):
    n_tiles = WIDTH // LANES
    for slot, (dil, o_ref, l_ref) in enumerate(((4, o4, l4), (16, o16, l16))):
        for rho in range(dil):
            rows = pl.ds(rho, tm // dil, stride=dil)
            lse_sc[slot, rows, :] = l_ref[0, rho]
            for c in range(n_tiles):
                tok_sc[slot * n_tiles + c, rows, :] = o_ref[0, rho, :, c * LANES:(c + 1) * LANES].astype(F32)
    tok = lambda slot: jnp.concatenate([tok_sc[slot * n_tiles + c] for c in range(n_tiles)], axis=1)
    la, lb, lc = l1[0, 0], lse_sc[0], lse_sc[1]
    mx = jnp.maximum(jnp.maximum(la, lb), lc)
    wa, wb, wc = jnp.exp(la - mx), jnp.exp(lb - mx), jnp.exp(lc - mx)
    norm = 1.0 / (wa + wb + wc)
    spread = lambda wgt: jnp.dot((wgt * norm).astype(BF16), spread_ref[...], preferred_element_type=F32)
    y_a = spread(wa) * o1[0, 0].astype(F32) + spread(wb) * tok(0) + spread(wc) * tok(1)
    ones = ones_ref[...]
    inv = 1.0 / HEAD_DIM
    y = yf[0].astype(F32) + yb[0].astype(F32)
    yc = y - _group_sum(y, ones) * inv
    var = _group_sum(yc * yc, ones) * inv
    yn = yc * lax.rsqrt(var + RWKV_GN_EPS) * lng[...] + lnb[...]
    y_b = (yn + bonus[0].astype(F32)) * g[0].astype(F32)
    out_ref[0] = (h_ref[0] + _mm(y_a, w_ref[0:WIDTH, :]) + _mm(y_b, w_ref[WIDTH:, :]))


def _even_out(h, branch, rwkv_parts, lng, lnb, ones_bd, w, tm=512):
    b, t, _ = h.shape
    const = lambda shape: pl.BlockSpec(shape, lambda bb, i: (0, 0))
    row = lambda n: pl.BlockSpec((1, tm, n), lambda bb, i: (bb, i, 0))
    res = lambda d, n: pl.BlockSpec((1, d, tm // d, n), lambda bb, i: (bb, 0, i, 0))
    dils = [d for _, d in DILATED_BRANCHES]
    spread = jnp.asarray(np.arange(LANES)[:, None] == (np.arange(WIDTH)[None, :] // HEAD_DIM) * LSE_LANES, BF16)
    return pl.pallas_call(
        functools.partial(_even_out_kernel, tm=tm),
        grid=(b, t // tm),
        in_specs=[row(D_MODEL)] + [res(d, n) for d in dils for n in (WIDTH, LANES)] + [row(WIDTH)] * 4
        + [const((1, WIDTH)), const((1, WIDTH)), const((WIDTH, WIDTH)), const((LANES, WIDTH)),
           const((D_MODEL, D_MODEL))],
        out_specs=row(D_MODEL),
        out_shape=jax.ShapeDtypeStruct((b, t, D_MODEL), F32),
        scratch_shapes=[pltpu.VMEM((2 * WIDTH // LANES, tm, LANES), F32), pltpu.VMEM((2, tm, LANES), F32)],
        compiler_params=_cparams("parallel", "parallel"),
        name="even_out",
    )(h, *branch, *rwkv_parts, lng, lnb, ones_bd, spread, w)


def _mlp_kernel(h_ref, g_ref, w1_ref, w2_ref, out_ref, u_ref):
    f = pl.program_id(1)

    @pl.when(f == 0)
    def _():
        x = h_ref[...]
        u_ref[...] = _rms_rows(x, g_ref[...]).astype(BF16)
        out_ref[...] = x

    hdn = jnp.dot(u_ref[...], w1_ref[...], preferred_element_type=F32)
    hdn = jnp.square(jnp.maximum(hdn, 0.0))
    out_ref[...] += jnp.dot(hdn.astype(BF16), w2_ref[...], preferred_element_type=F32)


def _mlp(h2, g, w1, w2, tm=1024, tf=1024):
    m = h2.shape[0]
    return pl.pallas_call(
        _mlp_kernel,
        grid=(m // tm, D_FF // tf),
        in_specs=[pl.BlockSpec((tm, D_MODEL), lambda i, f: (i, 0)),
                  pl.BlockSpec((1, D_MODEL), lambda i, f: (0, 0)),
                  pl.BlockSpec((D_MODEL, tf), lambda i, f: (0, f)),
                  pl.BlockSpec((tf, D_MODEL), lambda i, f: (f, 0))],
        out_specs=pl.BlockSpec((tm, D_MODEL), lambda i, f: (i, 0)),
        out_shape=jax.ShapeDtypeStruct((m, D_MODEL), F32),
        scratch_shapes=[pltpu.VMEM((tm, D_MODEL), BF16)],
        compiler_params=_cparams("parallel", "arbitrary"),
        name="mlp",
    )(h2, g, w1, w2)


def _rotary_pair(u, cos, sin_signed):
    first = (_iota(u.shape, 1) & (HEAD_DIM - 1)) < HEAD_DIM // 2
    swapped = jnp.where(first, pltpu.roll(u, LANES - HEAD_DIM // 2, 1),
                        pltpu.roll(u, HEAD_DIM // 2, 1))
    return u * cos + swapped * sin_signed


def _odd_in_kernel(x_ref, g_ref, wr_ref, ws_ref, wd_ref, cos_ref, sin_ref, ret_out, xz_out, dt_out):
    ub = _rms_rows(x_ref[...], g_ref[...]).astype(BF16)
    ret = jnp.dot(ub, wr_ref[...], preferred_element_type=F32)
    cos = cos_ref[...]
    sin = sin_ref[...]
    for c in range(2 * N_PAIRS):
        cols = slice(c * LANES, (c + 1) * LANES)
        scale = 1.0 if c < N_PAIRS else HEAD_DIM ** -0.5
        ret_out[:, cols] = (_rotary_pair(ret[:, cols], cos, sin) * scale).astype(BF16)
    ret_out[:, 2 * WIDTH:] = ret[:, 2 * WIDTH:].astype(BF16)
    xz_out[...] = jnp.dot(ub, ws_ref[...], preferred_element_type=F32).astype(BF16)
    dt_out[...] = jnp.dot(ub, wd_ref[...], preferred_element_type=F32)


def _odd_in(x2, g, wr, ws, wd, cos, sin, tm=512):
    m = x2.shape[0]
    per_seq = cos.shape[0] // tm
    const = lambda shape: pl.BlockSpec(shape, lambda i: (0, 0))
    row = lambda n: pl.BlockSpec((tm, n), lambda i: (i, 0))
    tab = pl.BlockSpec((tm, LANES), lambda i: (i % per_seq, 0))
    n_xz = SSD_XBC + WIDTH
    return pl.pallas_call(
        _odd_in_kernel,
        grid=(m // tm,),
        in_specs=[row(D_MODEL), const((1, D_MODEL)), const((D_MODEL, 4 * WIDTH)),
                  const((D_MODEL, n_xz)), const((D_MODEL, LANES)), tab, tab],
        out_specs=[row(4 * WIDTH), row(n_xz), row(LANES)],
        out_shape=[jax.ShapeDtypeStruct((m, 4 * WIDTH), BF16),
                   jax.ShapeDtypeStruct((m, n_xz), BF16),
                   jax.ShapeDtypeStruct((m, LANES), F32)],
        compiler_params=_cparams("parallel"),
        name="odd_in",
    )(x2, g, wr, ws, wd, cos, sin)


def _ret_kernel(q_ref, k_ref, v_ref, lg64_ref, lg128_ref, y_ref, s_ref,
                *, nc, reverse):
    c_len = LIN_CHUNK

    @pl.when(pl.program_id(1) == 0)
    def _():
        s_ref[...] = jnp.zeros_like(s_ref)

    l_io = _iota((c_len, 2 * c_len), 0)
    s_io = _iota((c_len, 2 * c_len), 1) & (c_len - 1)
    ahead = jnp.maximum(l_io - s_io, 0).astype(F32)
    behind = jnp.maximum(s_io - l_io, 0).astype(F32)
    pos = _iota((c_len, LANES), 0)
    if reverse:
        pos = c_len - 1 - pos
    pos = pos.astype(F32)
    r2 = _iota((LANES, LANES), 0)
    c2 = _iota((LANES, LANES), 1)
    same_head = (r2 & HEAD_DIM) == (c2 & HEAD_DIM)

    decay, from_start, to_end, whole = [], [], [], []
    for p in range(N_PAIRS):
        lg64 = lg64_ref[0, :, p * LANES:(p + 1) * LANES]
        if not reverse:
            wide = slice(2 * p * c_len, 2 * (p + 1) * c_len)
            decay.append(jnp.where(l_io >= s_io, jnp.exp(lg128_ref[0, :, wide] * ahead), 0.0)
                         + jnp.where(s_io >= l_io, jnp.exp(lg128_ref[1, :, wide] * behind), 0.0))
        from_start.append(jnp.exp(lg64 * (pos + 1.0)))
        to_end.append(jnp.exp(lg64 * (float(c_len - 1) - pos)))
        whole.append(jnp.exp(lg64 * float(c_len)))

    order = _scan_order(nc, reverse)
    inst = [(p, c) for c in order for p in range(N_PAIRS)]
    q, k, v = [], [], []
    for p, c in inst:
        rows = slice(c * c_len, (c + 1) * c_len)
        cols = slice(p * LANES, (p + 1) * LANES)
        q.append(q_ref[0, rows, cols])
        k.append(k_ref[0, rows, cols])
        v.append(v_ref[0, rows, cols])
    if not reverse:
        sc = [(_mm_nt(q[i], _bd(k[i])) * decay[p]).astype(BF16) for i, (p, c) in enumerate(inst)]
        y_intra = [_mm(sc[i], _bd(v[i])) for i in range(len(inst))]
    upd = [jnp.where(same_head, _mm((k[i] * to_end[p]).T, v[i]), 0.0) for i, (p, c) in enumerate(inst)]
    qs = [(q[i] * from_start[p]).astype(BF16) for i, (p, c) in enumerate(inst)]

    s = [s_ref[p] for p in range(N_PAIRS)]
    for ci, c in enumerate(order):
        for p in range(N_PAIRS):
            i = ci * N_PAIRS + p
            y = _mm(qs[i], s[p])
            if not reverse:
                y = y + y_intra[i]
            y_ref[0, c * c_len:(c + 1) * c_len, p * LANES:(p + 1) * LANES] = y.astype(BF16)
            s[p] = s[p] * whole[p] + upd[i]
    for p in range(N_PAIRS):
        s_ref[p] = s[p]


def _retention(ret, lg64, lg128, direction, tb=512):
    b, t, _ = ret.shape
    nt = t // tb
    reverse = direction == 1
    tmap = (lambda i: nt - 1 - i) if reverse else (lambda i: i)
    col = lambda blk: pl.BlockSpec((1, tb, WIDTH), lambda bb, i: (bb, tmap(i), blk))
    return pl.pallas_call(
        functools.partial(_ret_kernel, nc=tb // LIN_CHUNK, reverse=reverse),
        grid=(b, nt),
        in_specs=[col(0), col(1), col(2),
                  pl.BlockSpec((1, 1, WIDTH), lambda bb, i: (direction, 0, 0)),
                  pl.BlockSpec((2, 1, N_HEADS * LIN_CHUNK), lambda bb, i: (0, 0, 0))],
        out_specs=col(0),
        out_shape=jax.ShapeDtypeStruct((b, t, WIDTH), BF16),
        scratch_shapes=[pltpu.VMEM((N_PAIRS, LANES, LANES), F32)],
        compiler_params=_cparams("parallel", "arbitrary"),
        name=f"retention_{'bwd' if reverse else 'fwd'}",
    )(ret, ret, ret, lg64, lg128)


def _ssd_conv_kernel(x_ref, xp_ref, xn_ref, w_ref, b_ref, xs_out, bc_out, *, tm):
    i = pl.program_id(1)
    nt = pl.num_programs(1)
    prev = jnp.where(i > 0, xp_ref[0].astype(F32), 0.0)
    nxt = jnp.where(i < nt - 1, xn_ref[0].astype(F32), 0.0)
    xe = jnp.concatenate([prev, x_ref[0].astype(F32), nxt], axis=0)
    ext = tm + 2 * HALO_ROWS
    pad = SSD_CONV // 2
    acc = jnp.zeros((tm, SSD_XBC), F32) + b_ref[...]
    for j in range(SSD_CONV):
        shift = (pad - j) % ext
        xj = xe if shift == 0 else pltpu.roll(xe, shift, 0)
        acc = acc + xj[HALO_ROWS:HALO_ROWS + tm] * w_ref[j:j + 1, :]
    y = _silu(acc).astype(BF16)
    xs_out[0] = y[:, :WIDTH]
    bc_out[0] = y[:, WIDTH:]


def _ssd_conv(xz, w, bias, tm=512):
    b, t, _ = xz.shape
    nt = t // tm
    per = tm // HALO_ROWS
    const = lambda shape: pl.BlockSpec(shape, lambda bb, i: (0, 0))
    main = pl.BlockSpec((1, tm, SSD_XBC), lambda bb, i: (bb, i, 0))
    prev = pl.BlockSpec((1, HALO_ROWS, SSD_XBC), lambda bb, i: (bb, jnp.maximum(i * per - 1, 0), 0))
    nxt = pl.BlockSpec((1, HALO_ROWS, SSD_XBC),
                       lambda bb, i: (bb, jnp.minimum((i + 1) * per, t // HALO_ROWS - 1), 0))
    out = pl.BlockSpec((1, tm, WIDTH), lambda bb, i: (bb, i, 0))
    return pl.pallas_call(
        functools.partial(_ssd_conv_kernel, tm=tm),
        grid=(b, nt),
        in_specs=[main, prev, nxt, const((8, SSD_XBC)), const((1, SSD_XBC))],
        out_specs=[out, out],
        out_shape=[jax.ShapeDtypeStruct((b, t, WIDTH), BF16)] * 2,
        compiler_params=_cparams("parallel", "parallel"),
        name="ssd_conv",
    )(xz, xz, xz, w, bias)


def _ssd_kernel(xs_ref, bc_ref, dtc_ref, dtr_ref, bias_c_ref, bias_r_ref, nega_c_ref, nega_r_ref,
                tri_ref, trit_ref, y_ref, s_ref, *, nc, reverse):
    c_len = LIN_CHUNK
    per_group = N_HEADS // SSD_GROUPS
    gw = WIDTH // SSD_GROUPS

    @pl.when(pl.program_id(1) == 0)
    def _():
        s_ref[...] = jnp.zeros_like(s_ref)

    tri = tri_ref[...]
    trit = trit_ref[...]
    dts_c = _softplus(dtc_ref[0, 0] + bias_c_ref[0])
    la_c = dts_c * nega_c_ref[0]
    dts_r = _softplus(dtr_ref[0, 0] + bias_r_ref[0])
    la_r = dts_r * nega_r_ref[0]
    l_io = _iota((c_len, c_len), 0)
    s_io = _iota((c_len, c_len), 1)
    causal = (s_io >= l_io) if reverse else (s_io <= l_io)
    lo = _lane_lo((c_len, LANES))

    def by_head128(cols, g):
        return jnp.concatenate([jnp.broadcast_to(cols[:, g * per_group + h:g * per_group + h + 1],
                                                 (c_len, LANES)) for h in range(per_group)], axis=1)

    def to64(wide):
        h = [wide[:, j * LANES:(j + 1) * LANES] for j in range(per_group)]
        return jnp.concatenate([jnp.where(lo, h[0], h[1]), jnp.where(lo, h[2], h[3])], axis=1)

    order = _scan_order(nc, reverse)
    inst = [(g, c) for c in order for g in range(SSD_GROUPS)]
    n = len(inst)
    rows_of = lambda c: slice(c * c_len, (c + 1) * c_len)
    cum = [_mm_exact_left(tri, by_head128(la_c[rows_of(c)], g), terms=2) for g, c in inst]
    cum_r = {c: _mm_exact_right(la_r[:, rows_of(c)], trit, terms=2) for c in order}
    last = [u[0:1] if reverse else u[c_len - 1:c_len] for u in cum]
    bm = [bc_ref[0, rows_of(c), g * SSD_STATE:(g + 1) * SSD_STATE] for g, c in inst]
    cm = [bc_ref[0, rows_of(c), (SSD_GROUPS + g) * SSD_STATE:(SSD_GROUPS + g + 1) * SSD_STATE]
          for g, c in inst]
    xdt = [xs_ref[0, rows_of(c), g * gw:(g + 1) * gw].astype(F32) * to64(by_head128(dts_c[rows_of(c)], g))
           for g, c in inst]
    gm = [jnp.where(causal, _mm_nt(cm[i], bm[i]), 0.0) for i in range(n)]
    y_intra = []
    for i, (g, c) in enumerate(inst):
        parts = []
        for pp in range(per_group // 2):
            wide = []
            for hh in (2 * pp, 2 * pp + 1):
                head = g * per_group + hh
                diff = cum[i][:, hh * LANES:(hh + 1) * LANES] - cum_r[c][head:head + 1, :]
                wide.append((gm[i] * jnp.exp(jnp.minimum(diff, 0.0))).astype(BF16))
            parts.append(_mm(jnp.concatenate(wide, axis=1), _bd(xdt[i][:, pp * LANES:(pp + 1) * LANES])))
        y_intra.append(jnp.concatenate(parts, axis=1))
    from_start = [to64(jnp.exp(u)) for u in cum]
    upd = [_mm(bm[i].astype(F32).T, xdt[i] * to64(jnp.exp(last[i] - cum[i]))) for i in range(n)]
    whole = [to64(jnp.exp(u)) for u in last]

    s = [s_ref[g] for g in range(SSD_GROUPS)]
    for ci, c in enumerate(order):
        for g in range(SSD_GROUPS):
            i = ci * SSD_GROUPS + g
            y_ref[0, rows_of(c), g * gw:(g + 1) * gw] = (
                y_intra[i] + _mm(cm[i], s[g]) * from_start[i]).astype(BF16)
            s[g] = s[g] * whole[i] + upd[i]
    for g in range(SSD_GROUPS):
        s_ref[g] = s[g]


def _ssd_scan(xs, bc, dtc, dtr, sp, direction, tb=512):
    b, t, _ = xs.shape
    nt = t // tb
    reverse = direction == 1
    tmap = (lambda i: nt - 1 - i) if reverse else (lambda i: i)
    row = pl.BlockSpec((1, tb, WIDTH), lambda bb, i: (bb, tmap(i), 0))
    par = lambda shape: pl.BlockSpec((1,) + shape, lambda bb, i: (direction, 0, 0))
    tri = pl.BlockSpec((LIN_CHUNK, LIN_CHUNK), lambda bb, i: (0, 0))
    return pl.pallas_call(
        functools.partial(_ssd_kernel, nc=tb // LIN_CHUNK, reverse=reverse),
        grid=(b, nt),
        in_specs=[row, row,
                  pl.BlockSpec((1, 1, tb, N_HEADS), lambda bb, i: (direction, bb, tmap(i), 0)),
                  pl.BlockSpec((1, 1, N_HEADS, tb), lambda bb, i: (direction, bb, 0, tmap(i))),
                  par((1, N_HEADS)), par((N_HEADS, 1)), par((1, N_HEADS)), par((N_HEADS, 1)),
                  tri, tri],
        out_specs=row,
        out_shape=jax.ShapeDtypeStruct((b, t, WIDTH), BF16),
        scratch_shapes=[pltpu.VMEM((SSD_GROUPS, SSD_STATE, WIDTH // SSD_GROUPS), F32)],
        compiler_params=_cparams("parallel", "arbitrary"),
        name=f"ssd_scan_{'bwd' if reverse else 'fwd'}",
    )(xs, bc, dtc, dtr, sp["bias_c"], sp["bias_r"], sp["nega_c"], sp["nega_r"],
      sp["tri_rev" if reverse else "tri_fwd"], sp["trit_rev" if reverse else "trit_fwd"])


def _odd_out_kernel(h_ref, rf, rb, gate, sf, sb, xs, z, gng, gnb, dsk, ng, ones_ref, w_ref, out_ref):
    f32 = lambda ref: ref[...].astype(F32)
    ones = ones_ref[...]
    inv = 1.0 / HEAD_DIM
    y = f32(rf) + f32(rb)
    yc = y - _group_sum(y, ones) * inv
    var = _group_sum(yc * yc, ones) * inv
    y_c = (yc * lax.rsqrt(var + GN_EPS) * gng[...] + gnb[...]) * _silu(f32(gate))
    yd = (f32(sf) + f32(sb) + dsk[...] * f32(xs)) * _silu(f32(z))
    gw = WIDTH // SSD_GROUPS
    halves = []
    for gi in range(SSD_GROUPS):
        yg = yd[:, gi * gw:(gi + 1) * gw]
        halves.append(yg * lax.rsqrt(jnp.mean(yg * yg, -1, keepdims=True) + NORM_EPS))
    y_d = jnp.concatenate(halves, axis=1) * ng[...]
    out_ref[...] = h_ref[...] + _mm(y_c, w_ref[0:WIDTH, :]) + _mm(y_d, w_ref[WIDTH:, :])


def _odd_out(h2, rf, rb, ret2, sf, sb, xs, xz2, gng, gnb, dsk, ng, ones_bd, w, tm=512):
    m = h2.shape[0]
    const = lambda shape: pl.BlockSpec(shape, lambda i: (0, 0))
    row = lambda n: pl.BlockSpec((tm, n), lambda i: (i, 0))
    vec = const((1, WIDTH))
    return pl.pallas_call(
        _odd_out_kernel,
        grid=(m // tm,),
        in_specs=[row(D_MODEL), row(WIDTH), row(WIDTH),
                  pl.BlockSpec((tm, WIDTH), lambda i: (i, 3)),
                  row(WIDTH), row(WIDTH), row(WIDTH),
                  pl.BlockSpec((tm, WIDTH), lambda i: (i, 2)),
                  vec, vec, vec, vec, const((WIDTH, WIDTH)), const((D_MODEL, D_MODEL))],
        out_specs=row(D_MODEL),
        out_shape=jax.ShapeDtypeStruct((m, D_MODEL), F32),
        compiler_params=_cparams("parallel"),
        name="odd_out",
    )(h2, rf, rb, ret2, sf, sb, xs, xz2, gng, gnb, dsk, ng, ones_bd, w)


def _tri(n, reverse):
    i = np.arange(n)
    m = (i[None, :] >= i[:, None]) if reverse else (i[None, :] <= i[:, None])
    return m.astype(np.float32)


def _prepare(norm_mix, norm_mlp, mlp_w1, mlp_w2, even_in_w, even_out_w, attn_q_gain, attn_k_gain,
             rwkv_mu_prev, rwkv_mu_next, rwkv_w0, rwkv_w2, rwkv_a0, rwkv_a2, rwkv_g2, rwkv_k_k,
             rwkv_k_a, rwkv_r_k, rwkv_ln_g, rwkv_ln_b, odd_in_w, odd_out_w, ret_decay_exp, ret_gn_g,
             ret_gn_b, ssd_conv_w, ssd_conv_b, ssd_dt_bias, ssd_a_log, ssd_d, ssd_norm_g):
    row = lambda v: v.reshape(1, -1).astype(F32)
    heads = np.arange(WIDTH) // HEAD_DIM
    ones_bd = jnp.asarray(heads[:, None] == heads[None, :], BF16)
    p = {"ones_bd": ones_bd, "norm_mix": norm_mix, "norm_mlp": norm_mlp,
         "mlp_w1": mlp_w1.astype(BF16), "mlp_w2": mlp_w2.astype(BF16)}

    w_in = even_in_w[0]
    p["even_wa"] = w_in[:, :3 * WIDTH].astype(BF16)
    p["even_wr"] = w_in[:, 3 * WIDTH:].astype(BF16)
    p["even_out_w"] = even_out_w[0].astype(BF16)
    p["q_gain"] = row(jnp.tile(attn_q_gain[0], N_HEADS))
    p["k_gain"] = row(jnp.tile(attn_k_gain[0], N_HEADS))

    def two_dir_lowrank(w):
        z = jnp.zeros_like(w[0])
        return jnp.concatenate([jnp.concatenate([w[0], z], 1), jnp.concatenate([z, w[1]], 1)], 0)

    p["rwkv"] = {
        "mu_prev": row(rwkv_mu_prev[0]), "mu_next": row(rwkv_mu_next[0]),
        "w0": row(rwkv_w0[0]), "w2": two_dir_lowrank(rwkv_w2[0]).astype(BF16),
        "a0": row(rwkv_a0[0]), "a2": two_dir_lowrank(rwkv_a2[0]).astype(BF16),
        "g2": rwkv_g2[0].astype(BF16), "k_k": row(rwkv_k_k[0]), "k_a": row(rwkv_k_a[0]),
        "r_k": row(rwkv_r_k[0]), "ones_bd": ones_bd,
    }
    p["rwkv_tri"] = [jnp.asarray(_tri(RWKV_CHUNK, False), BF16), jnp.asarray(_tri(RWKV_CHUNK, True), BF16)]
    p["rwkv_ln_g"] = row(rwkv_ln_g[0])
    p["rwkv_ln_b"] = row(rwkv_ln_b[0])

    w_odd = odd_in_w[0]
    p["odd_wr"] = w_odd[:, :4 * WIDTH].astype(BF16)
    z_w = w_odd[:, 4 * WIDTH:5 * WIDTH]
    xbc_w = w_odd[:, 5 * WIDTH:5 * WIDTH + SSD_XBC]
    dt_w = w_odd[:, 5 * WIDTH + SSD_XBC:]
    p["odd_ws"] = jnp.concatenate([xbc_w, z_w], axis=1).astype(BF16)
    p["odd_wd"] = jnp.concatenate([dt_w, jnp.zeros((D_MODEL, LANES - 2 * N_HEADS), F32)], 1).astype(BF16)
    p["odd_out_w"] = odd_out_w[0].astype(BF16)

    log_gamma = jnp.log1p(-jnp.exp2(-ret_decay_exp[0].astype(F32)))
    p["lg64"] = jnp.repeat(log_gamma, HEAD_DIM, axis=1).reshape(2, 1, WIDTH)
    p["lg128"] = jnp.repeat(log_gamma, LIN_CHUNK, axis=1).reshape(2, 1, N_HEADS * LIN_CHUNK)
    p["ret_gn_g"] = row(ret_gn_g[0])
    p["ret_gn_b"] = row(ret_gn_b[0])

    p["conv_w"] = jnp.concatenate([ssd_conv_w[0], jnp.zeros((8 - SSD_CONV, SSD_XBC), F32)], 0)
    p["conv_b"] = row(ssd_conv_b[0])
    bias = ssd_dt_bias[0].astype(F32)
    nega = -jnp.exp(ssd_a_log[0].astype(F32))
    p["ssd"] = {
        "bias_c": bias[:, None, :], "nega_c": nega[:, None, :],
        "bias_r": bias[:, :, None], "nega_r": nega[:, :, None],
        "tri_fwd": jnp.asarray(_tri(LIN_CHUNK, False), BF16),
        "tri_rev": jnp.asarray(_tri(LIN_CHUNK, True), BF16),
        "trit_fwd": jnp.asarray(_tri(LIN_CHUNK, False).T, BF16),
        "trit_rev": jnp.asarray(_tri(LIN_CHUNK, True).T, BF16),
    }
    p["ssd_d"] = row(jnp.repeat(ssd_d[0], HEAD_DIM))
    p["ssd_norm_g"] = row(ssd_norm_g[0])
    return p


def _rope_tables(t):
    half = HEAD_DIM // 2
    inv = ROPE_BASE ** (-jnp.arange(half, dtype=F32) / half)
    ang = jnp.arange(t, dtype=F32)[:, None] * inv[None, :]
    cos = jnp.tile(jnp.cos(ang), (1, LANES // half))
    sin = jnp.sin(ang)
    sin_signed = jnp.tile(jnp.concatenate([-sin, sin], axis=1), (1, LANES // HEAD_DIM))
    return cos, sin_signed


def _even_mixer(h2, b, t, p):
    h = h2.reshape(b, t, D_MODEL)
    *qkvs, rcols = _even_in(h, p["norm_mix"][0:1], p["even_wa"], p["even_wr"], p["ones_bd"],
                            p["q_gain"], p["k_gain"])
    branch = []
    for qkv, (_, dil) in zip(qkvs, DILATED_BRANCHES):
        branch.extend(_dil_attn(qkv, dil))
    r, v, kkn, g, bonus, g_in, krep, bvec = _rwkv_prep(rcols, p["rwkv"], p["rwkv_tri"])
    ys = [_rwkv_scan(r, v, kkn, g_in, krep, bvec, d) for d in range(2)]
    out = _even_out(h, branch, [ys[0], ys[1], bonus, g], p["rwkv_ln_g"], p["rwkv_ln_b"],
                    p["ones_bd"], p["even_out_w"])
    return out.reshape(b * t, D_MODEL)


def _odd_mixer(h2, b, t, p):
    cos, sin = _rope_tables(t)
    ret2, xz2, dt2 = _odd_in(h2, p["norm_mix"][1:2], p["odd_wr"], p["odd_ws"], p["odd_wd"], cos, sin)
    ret = ret2.reshape(b, t, -1)
    xz = xz2.reshape(b, t, -1)
    rets = [_retention(ret, p["lg64"], p["lg128"], d) for d in range(2)]
    xs, bc = _ssd_conv(xz, p["conv_w"], p["conv_b"])
    dt = dt2[:, :2 * N_HEADS].reshape(b, t, 2, N_HEADS)
    dtc = dt.transpose(2, 0, 1, 3)
    dtr = dt.transpose(2, 0, 3, 1)
    ssds = [_ssd_scan(xs, bc, dtc, dtr, p["ssd"], d) for d in range(2)]
    flat = lambda u: u.reshape(b * t, -1)
    return _odd_out(h2, flat(rets[0]), flat(rets[1]), ret2, flat(ssds[0]), flat(ssds[1]), flat(xs),
                    xz2, p["ret_gn_g"], p["ret_gn_b"], p["ssd_d"], p["ssd_norm_g"], p["ones_bd"],
                    p["odd_out_w"])


def _trunk(x, p):
    b, t, _ = x.shape
    h2 = x.reshape(b * t, D_MODEL)
    h2 = _even_mixer(h2, b, t, p)
    h2 = _mlp(h2, p["norm_mlp"][0:1], p["mlp_w1"][0], p["mlp_w2"][0])
    h2 = _odd_mixer(h2, b, t, p)
    h2 = _mlp(h2, p["norm_mlp"][1:2], p["mlp_w1"][1], p["mlp_w2"][1])
    return h2.reshape(b, t, D_MODEL)


def kernel(x_prompt, x_sample, norm_mix, norm_mlp, mlp_w1, mlp_w2, even_in_w, even_out_w, attn_q_gain, attn_k_gain, rwkv_mu_prev, rwkv_mu_next, rwkv_w0, rwkv_w2, rwkv_a0, rwkv_a2, rwkv_g2, rwkv_k_k, rwkv_k_a, rwkv_r_k, rwkv_ln_g, rwkv_ln_b, odd_in_w, odd_out_w, ret_decay_exp, ret_gn_g, ret_gn_b, ssd_conv_w, ssd_conv_b, ssd_dt_bias, ssd_a_log, ssd_d, ssd_norm_g):
    p = _prepare(norm_mix, norm_mlp, mlp_w1, mlp_w2, even_in_w, even_out_w, attn_q_gain, attn_k_gain,
                 rwkv_mu_prev, rwkv_mu_next, rwkv_w0, rwkv_w2, rwkv_a0, rwkv_a2, rwkv_g2, rwkv_k_k,
                 rwkv_k_a, rwkv_r_k, rwkv_ln_g, rwkv_ln_b, odd_in_w, odd_out_w, ret_decay_exp,
                 ret_gn_g, ret_gn_b, ssd_conv_w, ssd_conv_b, ssd_dt_bias, ssd_a_log, ssd_d, ssd_norm_g)
    return (_trunk(x_prompt, p), _trunk(x_sample, p))
```

```python
import functools
import math

import numpy as np
import jax
import jax.numpy as jnp
from jax import lax
from jax.experimental import pallas as pl
from jax.experimental.pallas import tpu as pltpu

F32 = jnp.float32
BF16 = jnp.bfloat16

D_MODEL = 1024
HEAD_DIM = 64
N_HEADS = 8
WIDTH = N_HEADS * HEAD_DIM
N_PAIRS = N_HEADS // 2
LANES = 128
HALO_ROWS = 16
LSE_LANES = LANES // N_HEADS
DILATED_BRANCHES = ((128, 1), (512, 4), (2048, 16))
ATT_R = 64
RWKV_DECAY_RANK = 64
RWKV_ICL_RANK = 64
RWKV_GATE_RANK = 128
RWKV_COLS = 3 * WIDTH + 2 * RWKV_DECAY_RANK + 2 * RWKV_ICL_RANK + RWKV_GATE_RANK
SSD_STATE = 128
SSD_GROUPS = 2
SSD_CONV = 5
SSD_XBC = WIDTH + 2 * SSD_GROUPS * SSD_STATE
D_FF = 4 * D_MODEL
NORM_EPS = 1e-6
GN_EPS = 1e-5
RWKV_GN_EPS = 64e-5
ROPE_BASE = 10000.0
RWKV_CHUNK = 64
LIN_CHUNK = 128
NEG_BIG = -1e30
LOG2_E = math.log2(math.e)
VMEM_LIMIT = 56 * 1024 * 1024


def _cparams(*sem):
    return pltpu.CompilerParams(dimension_semantics=sem, vmem_limit_bytes=VMEM_LIMIT)


def _iota(shape, dim):
    return lax.broadcasted_iota(jnp.int32, shape, dim)


def _lane_lo(shape):
    return (_iota(shape, len(shape) - 1) & HEAD_DIM) == 0


def _bd(x):
    lo = _lane_lo(x.shape)
    zero = jnp.zeros_like(x)
    return jnp.concatenate([jnp.where(lo, x, zero), jnp.where(lo, zero, x)], axis=0)


def _mm(a, b):
    return jnp.dot(a.astype(BF16), b.astype(BF16), preferred_element_type=F32)


def _mm_nt(a, b):
    return lax.dot_general(a.astype(BF16), b.astype(BF16), (((1,), (1,)), ((), ())),
                           preferred_element_type=F32)


def _split_bf16(x, terms):
    parts = []
    for _ in range(terms - 1):
        hi = x.astype(BF16)
        parts.append(hi)
        x = x - hi.astype(F32)
    parts.append(x.astype(BF16))
    return parts


def _mm_exact_left(t, x, terms=3):
    return sum(jnp.dot(t, piece, preferred_element_type=F32) for piece in _split_bf16(x, terms))


def _mm_exact_right(x, t, terms=3):
    return sum(jnp.dot(piece, t, preferred_element_type=F32) for piece in _split_bf16(x, terms))


def _group_sum(x, ones_bd):
    return jnp.dot(x.astype(BF16), ones_bd, preferred_element_type=F32)


def _rms_rows(x, g):
    return x * lax.rsqrt(jnp.mean(x * x, -1, keepdims=True) + NORM_EPS) * g


def _sigmoid(x):
    return 0.5 * jnp.tanh(0.5 * x) + 0.5


def _silu(x):
    return x * _sigmoid(x)


def _softplus(x):
    return jnp.maximum(x, 0.0) + jnp.log1p(jnp.exp(-jnp.abs(x)))


def _scan_order(nc, reverse):
    return list(range(nc - 1, -1, -1) if reverse else range(nc))


def _even_in_kernel(x_ref, g_ref, wa_ref, wr_ref, ones_ref, qg_ref, kg_ref, perm4_ref, perm16_ref,
                    qkv1_out, qkv4_out, qkv16_out, r_out, *, tm):
    ub = _rms_rows(x_ref[0], g_ref[...]).astype(BF16)
    a = jnp.dot(ub, wa_ref[...], preferred_element_type=F32)
    q = a[:, :WIDTH]
    k = a[:, WIDTH:2 * WIDTH]
    ones = ones_ref[...]
    inv = 1.0 / HEAD_DIM
    qn = q * lax.rsqrt(_group_sum(q * q, ones) * inv + NORM_EPS) * qg_ref[...]
    kn = k * lax.rsqrt(_group_sum(k * k, ones) * inv + NORM_EPS) * kg_ref[...]
    qkv = jnp.concatenate([qn * (HEAD_DIM ** -0.5 * LOG2_E), kn, a[:, 2 * WIDTH:]], axis=1).astype(BF16)
    qkv1_out[0, 0] = qkv
    for dil, perm_ref, out in ((4, perm4_ref, qkv4_out), (16, perm16_ref, qkv16_out)):
        perm = perm_ref[...]
        bs = perm.shape[0]
        per = bs // dil
        for blk in range(tm // bs):
            moved = jnp.dot(perm, qkv[blk * bs:(blk + 1) * bs],
                            preferred_element_type=F32).astype(BF16)
            for rho in range(dil):
                out[0, rho, blk * per:(blk + 1) * per, :] = moved[rho * per:(rho + 1) * per]
    r_out[0] = jnp.dot(ub, wr_ref[...], preferred_element_type=F32).astype(BF16)


def _even_in(x, g, wa, wr, ones_bd, qg, kg, tm=512):
    b, t, _ = x.shape
    const = lambda shape: pl.BlockSpec(shape, lambda bb, i: (0, 0))
    row = lambda n: pl.BlockSpec((1, tm, n), lambda bb, i: (bb, i, 0))
    res = lambda d: pl.BlockSpec((1, d, tm // d, 3 * WIDTH), lambda bb, i: (bb, 0, i, 0))
    dils = [d for _, d in DILATED_BRANCHES]

    def to_residue_major(d, bs):
        src = np.arange(bs).reshape(bs // d, d).T.reshape(-1)
        return jnp.asarray(src[:, None] == np.arange(bs)[None, :], BF16)

    blocks = {4: 2 * HALO_ROWS * 4, 16: HALO_ROWS * 16}

    return pl.pallas_call(
        functools.partial(_even_in_kernel, tm=tm),
        grid=(b, t // tm),
        in_specs=[row(D_MODEL), const((1, D_MODEL)), const((D_MODEL, 3 * WIDTH)),
                  const((D_MODEL, RWKV_COLS)), const((WIDTH, WIDTH)),
                  const((1, WIDTH)), const((1, WIDTH)),
                  const((blocks[4], blocks[4])), const((blocks[16], blocks[16]))],
        out_specs=[res(d) for d in dils] + [row(RWKV_COLS)],
        out_shape=[jax.ShapeDtypeStruct((b, d, t // d, 3 * WIDTH), BF16) for d in dils]
        + [jax.ShapeDtypeStruct((b, t, RWKV_COLS), BF16)],
        compiler_params=_cparams("parallel", "parallel"),
        name="even_in",
    )(x, g, wa, wr, ones_bd, qg, kg, to_residue_major(4, blocks[4]), to_residue_major(16, blocks[16]))


def _dil_attn_kernel(q_ref, k_ref, kl_ref, kr_ref, v_ref, vl_ref, vr_ref, o_ref, lse_ref,
                     kext, vext, *, tq, n_res, sub_len, dil):
    r = ATT_R
    i = pl.program_id(2)
    for z in range(n_res):
        kext[z, 0:r] = kl_ref[0, z]
        kext[z, r:r + tq] = k_ref[0, z]
        kext[z, r + tq:] = kr_ref[0, z]
        vext[z, 0:r] = vl_ref[0, z]
        vext[z, r:r + tq] = v_ref[0, z]
        vext[z, r + tq:] = vr_ref[0, z]

    c_io = _iota((r, 3 * r), 1)
    q_io = _iota((r, 3 * r), 0)
    rel = c_io - r - q_io
    near = jnp.abs(rel) <= r
    dist = jnp.abs(rel).astype(F32) * float(dil)
    lo = _lane_lo((r, LANES))
    near2 = jnp.concatenate([near, near], axis=0)
    dist2 = jnp.concatenate([dist, dist], axis=0)
    first_head = _iota((2 * r, 3 * r), 0) < r
    cols = [slice(p * LANES, (p + 1) * LANES) for p in range(N_PAIRS)]
    bias = [jnp.where(first_head, 2.0 ** (-(2 * p + 1)), 2.0 ** (-(2 * p + 2))) * (LOG2_E * dist2)
            for p in range(N_PAIRS)]

    blocks = [(z, j) for z in range(n_res) for j in range(tq // r)]
    group = 2 if len(blocks) % 2 == 0 else 1
    for g0 in range(0, len(blocks), group):
        inst = [(z, j, p) for z, j in blocks[g0:g0 + group] for p in range(N_PAIRS)]
        valid = {}
        for z, j in blocks[g0:g0 + group]:
            kpos = i * tq + j * r + _iota((2 * r, 3 * r), 1) - r
            valid[j] = near2 & (kpos >= 0) & (kpos < sub_len)
        s = [jnp.where(valid[j],
                       _mm_nt(_bd(q_ref[0, z, j * r:(j + 1) * r, cols[p]]),
                              kext[z, j * r:(j + 3) * r, cols[p]]) - bias[p], NEG_BIG)
             for z, j, p in inst]
        m = [jnp.max(u, -1, keepdims=True) for u in s]
        e = [jnp.exp2(u - mm) for u, mm in zip(s, m)]
        den = [jnp.sum(u, -1, keepdims=True) for u in e]
        pv = [_mm(e[n], vext[z, j * r:(j + 3) * r, cols[p]]) / den[n]
              for n, (z, j, p) in enumerate(inst)]
        lse = [m[n] * (1.0 / LOG2_E) + jnp.log(den[n]) for n in range(len(inst))]
        for n, (z, j, p) in enumerate(inst):
            o_ref[0, z, j * r:(j + 1) * r, cols[p]] = jnp.where(lo, pv[n][:r], pv[n][r:]).astype(BF16)
        lane = _iota((r, LANES), 1)
        for bi, (z, j) in enumerate(blocks[g0:g0 + group]):
            tile = jnp.zeros((r, LANES), F32)
            for p in range(N_PAIRS):
                col = lse[bi * N_PAIRS + p]
                for hh in range(2):
                    h0 = (2 * p + hh) * LSE_LANES
                    tile = jnp.where((lane >= h0) & (lane < h0 + LSE_LANES),
                                     col[hh * r:(hh + 1) * r], tile)
            lse_ref[0, z, j * r:(j + 1) * r, :] = tile


def _dil_attn(qkv, dil, rows_per_step=1024):
    b, _, sub_len, _ = qkv.shape
    tq = min(sub_len, rows_per_step)
    n_res = min(dil, rows_per_step // tq)
    nblk = sub_len // ATT_R
    per = tq // ATT_R
    main = lambda col: pl.BlockSpec((1, n_res, tq, WIDTH), lambda bb, rr, i: (bb, rr, i, col))
    left = lambda col: pl.BlockSpec((1, n_res, ATT_R, WIDTH),
                                    lambda bb, rr, i: (bb, rr, jnp.maximum(i * per - 1, 0), col))
    right = lambda col: pl.BlockSpec((1, n_res, ATT_R, WIDTH),
                                     lambda bb, rr, i: (bb, rr, jnp.minimum((i + 1) * per, nblk - 1), col))
    return pl.pallas_call(
        functools.partial(_dil_attn_kernel, tq=tq, n_res=n_res, sub_len=sub_len, dil=dil),
        grid=(b, dil // n_res, sub_len // tq),
        in_specs=[main(0), main(1), left(1), right(1), main(2), left(2), right(2)],
        out_specs=[main(0), pl.BlockSpec((1, n_res, tq, LANES), lambda bb, rr, i: (bb, rr, i, 0))],
        out_shape=[jax.ShapeDtypeStruct((b, dil, sub_len, WIDTH), BF16),
                   jax.ShapeDtypeStruct((b, dil, sub_len, LANES), F32)],
        scratch_shapes=[pltpu.VMEM((n_res, tq + 2 * ATT_R, WIDTH), BF16)] * 2,
        compiler_params=_cparams("parallel", "parallel", "parallel"),
        name=f"dil_attn_d{dil}",
    )(qkv, qkv, qkv, qkv, qkv, qkv, qkv)


def _rwkv_prep_kernel(x_ref, xp_ref, xn_ref, mup_ref, mun_ref, w0_ref, w2_ref, a0_ref, a2_ref,
                      g2_ref, kk_ref, ka_ref, rk_ref, ones_ref, trif_ref, trir_ref,
                      r_out, v_out, kkn_out, g_out, bonus_out, g_in_out, krep_out, b_out, *, tm):
    i = pl.program_id(1)
    nt = pl.num_programs(1)
    x = x_ref[0].astype(F32)
    prow = jnp.where(i > 0, xp_ref[0, HALO_ROWS - 1:HALO_ROWS, :].astype(F32), 0.0)
    nrow = jnp.where(i < nt - 1, xn_ref[0, 0:1, :].astype(F32), 0.0)
    rid = _iota((8, 1), 0)
    prev = pltpu.roll(x, 1, 0)
    prev = jnp.concatenate([jnp.where(rid == 0, prow, prev[:8]), prev[8:]], axis=0)
    nxt = pltpu.roll(x, tm - 1, 0)
    nxt = jnp.concatenate([nxt[:tm - 8], jnp.where(rid == 7, nrow, nxt[tm - 8:])], axis=0)
    xs = x + mup_ref[...] * (prev - x) + mun_ref[...] * (nxt - x)
    r = xs[:, 0:WIDTH]
    k = xs[:, WIDTH:2 * WIDTH]
    v = xs[:, 2 * WIDTH:3 * WIDTH]
    zw = xs[:, 3 * WIDTH:3 * WIDTH + LANES]
    za = xs[:, 3 * WIDTH + LANES:3 * WIDTH + 2 * LANES]
    zg = xs[:, 3 * WIDTH + 2 * LANES:]
    ones = ones_ref[...]
    kk = k * kk_ref[...]
    kkn = kk * lax.rsqrt(jnp.maximum(_group_sum(kk * kk, ones), 1e-24))
    g_out[0] = _mm(_sigmoid(zg), g2_ref[...]).astype(BF16)
    r_out[0] = r.astype(BF16)
    v_out[0] = v.astype(BF16)
    kkn_out[0] = kkn.astype(BF16)
    bonus_out[0] = (_group_sum(r * k * rk_ref[...], ones) * v).astype(BF16)
    wx = w0_ref[...] + _mm(jnp.tanh(zw), w2_ref[...])
    ax = a0_ref[...] + _mm(za, a2_ref[...])
    for d, tri_ref in enumerate((trif_ref, trir_ref)):
        cols = slice(d * WIDTH, (d + 1) * WIDTH)
        lw = -math.exp(-0.5) * _sigmoid(wx[:, cols])
        tri = tri_ref[...]
        for c in range(tm // RWKV_CHUNK):
            rows = slice(c * RWKV_CHUNK, (c + 1) * RWKV_CHUNK)
            g_in_out[d, 0, rows, :] = _mm_exact_left(tri, lw[rows], terms=2)
        a = _sigmoid(ax[:, cols])
        krep_out[d, 0] = (k * (1.0 + (a - 1.0) * ka_ref[...])).astype(BF16)
        b_out[d, 0] = (kkn * a).astype(BF16)


def _rwkv_prep(rcols, p, tri, tm=512):
    b, t, _ = rcols.shape
    nt = t // tm
    per = tm // HALO_ROWS
    const = lambda shape: pl.BlockSpec(shape, lambda bb, i: (0,) * len(shape))
    main = lambda n: pl.BlockSpec((1, tm, n), lambda bb, i: (bb, i, 0))
    dirs = pl.BlockSpec((2, 1, tm, WIDTH), lambda bb, i: (0, bb, i, 0))
    prev = pl.BlockSpec((1, HALO_ROWS, RWKV_COLS), lambda bb, i: (bb, jnp.maximum(i * per - 1, 0), 0))
    nxt = pl.BlockSpec((1, HALO_ROWS, RWKV_COLS),
                       lambda bb, i: (bb, jnp.minimum((i + 1) * per, t // HALO_ROWS - 1), 0))
    one = jax.ShapeDtypeStruct((b, t, WIDTH), BF16)
    two = lambda dt: jax.ShapeDtypeStruct((2, b, t, WIDTH), dt)
    return pl.pallas_call(
        functools.partial(_rwkv_prep_kernel, tm=tm),
        grid=(b, nt),
        in_specs=[main(RWKV_COLS), prev, nxt, const((1, RWKV_COLS)), const((1, RWKV_COLS)),
                  const((1, 2 * WIDTH)), const((LANES, 2 * WIDTH)),
                  const((1, 2 * WIDTH)), const((LANES, 2 * WIDTH)),
                  const((LANES, WIDTH)), const((1, WIDTH)), const((1, WIDTH)), const((1, WIDTH)),
                  const((WIDTH, WIDTH)), const((RWKV_CHUNK, RWKV_CHUNK)),
                  const((RWKV_CHUNK, RWKV_CHUNK))],
        out_specs=[main(WIDTH)] * 5 + [dirs] * 3,
        out_shape=[one] * 5 + [two(F32), two(BF16), two(BF16)],
        compiler_params=_cparams("parallel", "parallel"),
        name="rwkv_prep",
    )(rcols, rcols, rcols, p["mu_prev"], p["mu_next"], p["w0"], p["w2"], p["a0"], p["a2"],
      p["g2"], p["k_k"], p["k_a"], p["r_k"], p["ones_bd"], tri[0], tri[1])


def _rwkv_scan_kernel(r_ref, v_ref, kk_ref, g_ref, k_ref, b_ref, y_ref, s_ref, *, nc, reverse):
    c_len = RWKV_CHUNK

    @pl.when(pl.program_id(1) == 0)
    def _():
        s_ref[...] = jnp.zeros_like(s_ref)

    row = _iota((c_len, LANES), 0)
    li = _iota((c_len, LANES), 1) & (HEAD_DIM - 1)
    strict = (li > row) if reverse else (li < row)
    incl = (li >= row) if reverse else (li <= row)
    r2 = _iota((LANES, LANES), 0)
    c2 = _iota((LANES, LANES), 1)
    same_head = (r2 & HEAD_DIM) == (c2 & HEAD_DIM)
    diag = r2 == c2
    zeros = jnp.zeros((c_len, LANES), F32)

    order = _scan_order(nc, reverse)
    inst = [(p, c) for c in order for p in range(N_PAIRS)]
    n = len(inst)

    def tile(ref, p, c, lead=()):
        return ref[lead + (0, slice(c * c_len, (c + 1) * c_len), slice(p * LANES, (p + 1) * LANES))]

    g_in = [tile(g_ref, p, c, (0,)) for p, c in inst]
    g_tot = [g[0:1] if reverse else g[c_len - 1:c_len] for g in g_in]
    scan_first = row == (c_len - 1 if reverse else 0)
    g_ex = [jnp.where(scan_first, 0.0, pltpu.roll(g, c_len - 1 if reverse else 1, 0)) for g in g_in]
    v = [tile(v_ref, p, c) for p, c in inst]
    at, rt, w, bk_end = [], [], [], []
    for i, (p, c) in enumerate(inst):
        b = tile(b_ref, p, c, (0,)).astype(F32)
        k = tile(k_ref, p, c, (0,)).astype(F32)
        at.append(-tile(kk_ref, p, c).astype(F32) * jnp.exp(g_ex[i]))
        rt.append(tile(r_ref, p, c).astype(F32) * jnp.exp(g_in[i]))
        inv = jnp.exp(-g_in[i])
        to_end = jnp.exp(g_tot[i] - g_in[i])
        bk_end.append(jnp.concatenate([b * to_end, k * to_end], 0))
        w.append(_mm_nt(jnp.concatenate([at[i], rt[i]], 0),
                        jnp.concatenate([_bd(b * inv), _bd(k * inv)], 0)))

    bdv = [_bd(u) for u in v]
    m_ab = [jnp.where(strict, u[:c_len, :LANES], 0.0) for u in w]
    kv = [_mm(jnp.concatenate([jnp.where(strict, w[i][:c_len, LANES:], 0.0),
                               jnp.where(incl, w[i][c_len:, LANES:], 0.0)], 0), bdv[i])
          for i in range(n)]
    rhs = [jnp.concatenate([at[i], kv[i][:c_len]], axis=1) for i in range(n)]
    tinv = [jnp.where(li == row, 1.0, 0.0) + u for u in m_ab]
    mj = [u.astype(BF16) for u in m_ab]
    mj = [_mm(u, _bd(u)).astype(BF16) for u in mj]
    for j in range(1, 5):
        both = [_mm(jnp.concatenate([mj[i], tinv[i].astype(BF16)], 0), _bd(mj[i])) for i in range(n)]
        mj = [u[:c_len].astype(BF16) for u in both]
        tinv = [tinv[i] + both[i][c_len:] for i in range(n)]
    tinv = [tinv[i] + _mm(tinv[i], _bd(mj[i])) for i in range(n)]
    x = [_mm(tinv[i], _bd(rhs[i])) for i in range(n)]

    t1 = [_mm(jnp.where(incl, w[i][c_len:, :LANES], 0.0), _bd(x[i])) for i in range(n)]
    y0 = [t1[i][:, LANES:] + kv[i][c_len:] for i in range(n)]
    rp = [(rt[i] + t1[i][:, :LANES]).astype(BF16) for i in range(n)]
    t2 = [_mm(bk_end[i].T,
              jnp.concatenate([x[i], jnp.concatenate([zeros, v[i].astype(F32)], axis=1)], 0))
          for i in range(n)]
    g_low = [jnp.where(same_head, u[:, :LANES], 0.0).astype(BF16) for u in t2]
    h_new = [jnp.where(same_head, u[:, LANES:], 0.0) for u in t2]
    g_col = [jnp.exp(jnp.sum(jnp.where(diag, jnp.broadcast_to(g, (LANES, LANES)), 0.0),
                             axis=1, keepdims=True)) for g in g_tot]

    s = [s_ref[p] for p in range(N_PAIRS)]
    for ci, c in enumerate(order):
        ids = [ci * N_PAIRS + p for p in range(N_PAIRS)]
        both = [_mm(jnp.concatenate([rp[i], g_low[i]], 0), s[p]) for p, i in enumerate(ids)]
        for p, i in enumerate(ids):
            y_ref[0, c * c_len:(c + 1) * c_len, p * LANES:(p + 1) * LANES] = (
                both[p][:c_len] + y0[i]).astype(BF16)
        s = [s[p] * g_col[i] + both[p][c_len:] + h_new[i] for p, i in enumerate(ids)]
    for p in range(N_PAIRS):
        s_ref[p] = s[p]


def _rwkv_scan(r, v, kkn, g_in, krep, bvec, direction, tb=512):
    b, t, _ = r.shape
    nt = t // tb
    reverse = direction == 1
    tmap = (lambda i: nt - 1 - i) if reverse else (lambda i: i)
    one = pl.BlockSpec((1, tb, WIDTH), lambda bb, i: (bb, tmap(i), 0))
    two = pl.BlockSpec((1, 1, tb, WIDTH), lambda bb, i: (direction, bb, tmap(i), 0))
    return pl.pallas_call(
        functools.partial(_rwkv_scan_kernel, nc=tb // RWKV_CHUNK, reverse=reverse),
        grid=(b, nt),
        in_specs=[one, one, one, two, two, two],
        out_specs=one,
        out_shape=jax.ShapeDtypeStruct((b, t, WIDTH), BF16),
        scratch_shapes=[pltpu.VMEM((N_PAIRS, LANES, LANES), F32)],
        compiler_params=_cparams("parallel", "arbitrary"),
        name=f"rwkv_scan_{'bwd' if reverse else 'fwd'}",
    )(r, v, kkn, g_in, krep, bvec)


def _even_out_kernel(h_ref, o1, l1, o4, l4, o16, l16, yf, yb, bonus, g, lng, lnb, ones_ref, spread_ref,
                     w_ref, out_ref, tok_sc, lse_sc, *, tm):
    n_tiles = WIDTH // LANES
    for slot, (dil, o_ref, l_ref) in enumerate(((4, o4, l4), (16, o16, l16))):
        for rho in range(dil):
            rows = pl.ds(rho, tm // dil, stride=dil)
            lse_sc[slot, rows, :] = l_ref[0, rho]
            for c in range(n_tiles):
                tok_sc[slot * n_tiles + c, rows, :] = o_ref[0, rho, :, c * LANES:(c + 1) * LANES].astype(F32)
    tok = lambda slot: jnp.concatenate([tok_sc[slot * n_tiles + c] for c in range(n_tiles)], axis=1)
    la, lb, lc = l1[0, 0], lse_sc[0], lse_sc[1]
    mx = jnp.maximum(jnp.maximum(la, lb), lc)
    wa, wb, wc = jnp.exp(la - mx), jnp.exp(lb - mx), jnp.exp(lc - mx)
    norm = 1.0 / (wa + wb + wc)
    spread = lambda wgt: jnp.dot((wgt * norm).astype(BF16), spread_ref[...], preferred_element_type=F32)
    y_a = spread(wa) * o1[0, 0].astype(F32) + spread(wb) * tok(0) + spread(wc) * tok(1)
    ones = ones_ref[...]
    inv = 1.0 / HEAD_DIM
    y = yf[0].astype(F32) + yb[0].astype(F32)
    yc = y - _group_sum(y, ones) * inv
    var = _group_sum(yc * yc, ones) * inv
    yn = yc * lax.rsqrt(var + RWKV_GN_EPS) * lng[...] + lnb[...]
    y_b = (yn + bonus[0].astype(F32)) * g[0].astype(F32)
    out_ref[0] = (h_ref[0] + _mm(y_a, w_ref[0:WIDTH, :]) + _mm(y_b, w_ref[WIDTH:, :]))


def _even_out(h, branch, rwkv_parts, lng, lnb, ones_bd, w, tm=512):
    b, t, _ = h.shape
    const = lambda shape: pl.BlockSpec(shape, lambda bb, i: (0, 0))
    row = lambda n: pl.BlockSpec((1, tm, n), lambda bb, i: (bb, i, 0))
    res = lambda d, n: pl.BlockSpec((1, d, tm // d, n), lambda bb, i: (bb, 0, i, 0))
    dils = [d for _, d in DILATED_BRANCHES]
    spread = jnp.asarray(np.arange(LANES)[:, None] == (np.arange(WIDTH)[None, :] // HEAD_DIM) * LSE_LANES, BF16)
    return pl.pallas_call(
        functools.partial(_even_out_kernel, tm=tm),
        grid=(b, t // tm),
        in_specs=[row(D_MODEL)] + [res(d, n) for d in dils for n in (WIDTH, LANES)] + [row(WIDTH)] * 4
        + [const((1, WIDTH)), const((1, WIDTH)), const((WIDTH, WIDTH)), const((LANES, WIDTH)),
           const((D_MODEL, D_MODEL))],
        out_specs=row(D_MODEL),
        out_shape=jax.ShapeDtypeStruct((b, t, D_MODEL), F32),
        scratch_shapes=[pltpu.VMEM((2 * WIDTH // LANES, tm, LANES), F32), pltpu.VMEM((2, tm, LANES), F32)],
        compiler_params=_cparams("parallel", "parallel"),
        name="even_out",
    )(h, *branch, *rwkv_parts, lng, lnb, ones_bd, spread, w)


def _mlp_kernel(h_ref, g_ref, w1_ref, w2_ref, out_ref, u_ref):
    f = pl.program_id(1)

    @pl.when(f == 0)
    def _():
        x = h_ref[...]
        u_ref[...] = _rms_rows(x, g_ref[...]).astype(BF16)
        out_ref[...] = x

    hdn = jnp.dot(u_ref[...], w1_ref[...], preferred_element_type=F32)
    hdn = jnp.square(jnp.maximum(hdn, 0.0))
    out_ref[...] += jnp.dot(hdn.astype(BF16), w2_ref[...], preferred_element_type=F32)


def _mlp(h2, g, w1, w2, tm=1024, tf=1024):
    m = h2.shape[0]
    return pl.pallas_call(
        _mlp_kernel,
        grid=(m // tm, D_FF // tf),
        in_specs=[pl.BlockSpec((tm, D_MODEL), lambda i, f: (i, 0)),
                  pl.BlockSpec((1, D_MODEL), lambda i, f: (0, 0)),
                  pl.BlockSpec((D_MODEL, tf), lambda i, f: (0, f)),
                  pl.BlockSpec((tf, D_MODEL), lambda i, f: (f, 0))],
        out_specs=pl.BlockSpec((tm, D_MODEL), lambda i, f: (i, 0)),
        out_shape=jax.ShapeDtypeStruct((m, D_MODEL), F32),
        scratch_shapes=[pltpu.VMEM((tm, D_MODEL), BF16)],
        compiler_params=_cparams("parallel", "arbitrary"),
        name="mlp",
    )(h2, g, w1, w2)


def _rotary_pair(u, cos, sin_signed):
    first = (_iota(u.shape, 1) & (HEAD_DIM - 1)) < HEAD_DIM // 2
    swapped = jnp.where(first, pltpu.roll(u, LANES - HEAD_DIM // 2, 1),
                        pltpu.roll(u, HEAD_DIM // 2, 1))
    return u * cos + swapped * sin_signed


def _odd_in_kernel(x_ref, g_ref, wr_ref, ws_ref, wd_ref, cos_ref, sin_ref, ret_out, xz_out, dt_out):
    ub = _rms_rows(x_ref[...], g_ref[...]).astype(BF16)
    ret = jnp.dot(ub, wr_ref[...], preferred_element_type=F32)
    cos = cos_ref[...]
    sin = sin_ref[...]
    for c in range(2 * N_PAIRS):
        cols = slice(c * LANES, (c + 1) * LANES)
        scale = 1.0 if c < N_PAIRS else HEAD_DIM ** -0.5
        ret_out[:, cols] = (_rotary_pair(ret[:, cols], cos, sin) * scale).astype(BF16)
    ret_out[:, 2 * WIDTH:] = ret[:, 2 * WIDTH:].astype(BF16)
    xz_out[...] = jnp.dot(ub, ws_ref[...], preferred_element_type=F32).astype(BF16)
    dt_out[...] = jnp.dot(ub, wd_ref[...], preferred_element_type=F32)


def _odd_in(x2, g, wr, ws, wd, cos, sin, tm=512):
    m = x2.shape[0]
    per_seq = cos.shape[0] // tm
    const = lambda shape: pl.BlockSpec(shape, lambda i: (0, 0))
    row = lambda n: pl.BlockSpec((tm, n), lambda i: (i, 0))
    tab = pl.BlockSpec((tm, LANES), lambda i: (i % per_seq, 0))
    n_xz = SSD_XBC + WIDTH
    return pl.pallas_call(
        _odd_in_kernel,
        grid=(m // tm,),
        in_specs=[row(D_MODEL), const((1, D_MODEL)), const((D_MODEL, 4 * WIDTH)),
                  const((D_MODEL, n_xz)), const((D_MODEL, LANES)), tab, tab],
        out_specs=[row(4 * WIDTH), row(n_xz), row(LANES)],
        out_shape=[jax.ShapeDtypeStruct((m, 4 * WIDTH), BF16),
                   jax.ShapeDtypeStruct((m, n_xz), BF16),
                   jax.ShapeDtypeStruct((m, LANES), F32)],
        compiler_params=_cparams("parallel"),
        name="odd_in",
    )(x2, g, wr, ws, wd, cos, sin)


def _ret_kernel(q_ref, k_ref, v_ref, lg64_ref, lg128_ref, y_ref, s_ref,
                *, nc, reverse):
    c_len = LIN_CHUNK

    @pl.when(pl.program_id(1) == 0)
    def _():
        s_ref[...] = jnp.zeros_like(s_ref)

    l_io = _iota((c_len, 2 * c_len), 0)
    s_io = _iota((c_len, 2 * c_len), 1) & (c_len - 1)
    ahead = jnp.maximum(l_io - s_io, 0).astype(F32)
    behind = jnp.maximum(s_io - l_io, 0).astype(F32)
    pos = _iota((c_len, LANES), 0)
    if reverse:
        pos = c_len - 1 - pos
    pos = pos.astype(F32)
    r2 = _iota((LANES, LANES), 0)
    c2 = _iota((LANES, LANES), 1)
    same_head = (r2 & HEAD_DIM) == (c2 & HEAD_DIM)

    decay, from_start, to_end, whole = [], [], [], []
    for p in range(N_PAIRS):
        lg64 = lg64_ref[0, :, p * LANES:(p + 1) * LANES]
        if not reverse:
            wide = slice(2 * p * c_len, 2 * (p + 1) * c_len)
            decay.append(jnp.where(l_io >= s_io, jnp.exp(lg128_ref[0, :, wide] * ahead), 0.0)
                         + jnp.where(s_io >= l_io, jnp.exp(lg128_ref[1, :, wide] * behind), 0.0))
        from_start.append(jnp.exp(lg64 * (pos + 1.0)))
        to_end.append(jnp.exp(lg64 * (float(c_len - 1) - pos)))
        whole.append(jnp.exp(lg64 * float(c_len)))

    order = _scan_order(nc, reverse)
    inst = [(p, c) for c in order for p in range(N_PAIRS)]
    q, k, v = [], [], []
    for p, c in inst:
        rows = slice(c * c_len, (c + 1) * c_len)
        cols = slice(p * LANES, (p + 1) * LANES)
        q.append(q_ref[0, rows, cols])
        k.append(k_ref[0, rows, cols])
        v.append(v_ref[0, rows, cols])
    if not reverse:
        sc = [(_mm_nt(q[i], _bd(k[i])) * decay[p]).astype(BF16) for i, (p, c) in enumerate(inst)]
        y_intra = [_mm(sc[i], _bd(v[i])) for i in range(len(inst))]
    upd = [jnp.where(same_head, _mm((k[i] * to_end[p]).T, v[i]), 0.0) for i, (p, c) in enumerate(inst)]
    qs = [(q[i] * from_start[p]).astype(BF16) for i, (p, c) in enumerate(inst)]

    s = [s_ref[p] for p in range(N_PAIRS)]
    for ci, c in enumerate(order):
        for p in range(N_PAIRS):
            i = ci * N_PAIRS + p
            y = _mm(qs[i], s[p])
            if not reverse:
                y = y + y_intra[i]
            y_ref[0, c * c_len:(c + 1) * c_len, p * LANES:(p + 1) * LANES] = y.astype(BF16)
            s[p] = s[p] * whole[p] + upd[i]
    for p in range(N_PAIRS):
        s_ref[p] = s[p]


def _retention(ret, lg64, lg128, direction, tb=1024):
    b, t, _ = ret.shape
    nt = t // tb
    reverse = direction == 1
    tmap = (lambda i: nt - 1 - i) if reverse else (lambda i: i)
    col = lambda blk: pl.BlockSpec((1, tb, WIDTH), lambda bb, i: (bb, tmap(i), blk))
    return pl.pallas_call(
        functools.partial(_ret_kernel, nc=tb // LIN_CHUNK, reverse=reverse),
        grid=(b, nt),
        in_specs=[col(0), col(1), col(2),
                  pl.BlockSpec((1, 1, WIDTH), lambda bb, i: (direction, 0, 0)),
                  pl.BlockSpec((2, 1, N_HEADS * LIN_CHUNK), lambda bb, i: (0, 0, 0))],
        out_specs=col(0),
        out_shape=jax.ShapeDtypeStruct((b, t, WIDTH), BF16),
        scratch_shapes=[pltpu.VMEM((N_PAIRS, LANES, LANES), F32)],
        compiler_params=_cparams("parallel", "arbitrary"),
        name=f"retention_{'bwd' if reverse else 'fwd'}",
    )(ret, ret, ret, lg64, lg128)


def _ssd_conv_kernel(x_ref, xp_ref, xn_ref, w_ref, b_ref, xs_out, bc_out, *, tm):
    i = pl.program_id(1)
    nt = pl.num_programs(1)
    prev = jnp.where(i > 0, xp_ref[0].astype(F32), 0.0)
    nxt = jnp.where(i < nt - 1, xn_ref[0].astype(F32), 0.0)
    xe = jnp.concatenate([prev, x_ref[0].astype(F32), nxt], axis=0)
    ext = tm + 2 * HALO_ROWS
    pad = SSD_CONV // 2
    acc = jnp.zeros((tm, SSD_XBC), F32) + b_ref[...]
    for j in range(SSD_CONV):
        shift = (pad - j) % ext
        xj = xe if shift == 0 else pltpu.roll(xe, shift, 0)
        acc = acc + xj[HALO_ROWS:HALO_ROWS + tm] * w_ref[j:j + 1, :]
    y = _silu(acc).astype(BF16)
    xs_out[0] = y[:, :WIDTH]
    bc_out[0] = y[:, WIDTH:]


def _ssd_conv(xz, w, bias, tm=512):
    b, t, _ = xz.shape
    nt = t // tm
    per = tm // HALO_ROWS
    const = lambda shape: pl.BlockSpec(shape, lambda bb, i: (0, 0))
    main = pl.BlockSpec((1, tm, SSD_XBC), lambda bb, i: (bb, i, 0))
    prev = pl.BlockSpec((1, HALO_ROWS, SSD_XBC), lambda bb, i: (bb, jnp.maximum(i * per - 1, 0), 0))
    nxt = pl.BlockSpec((1, HALO_ROWS, SSD_XBC),
                       lambda bb, i: (bb, jnp.minimum((i + 1) * per, t // HALO_ROWS - 1), 0))
    out = pl.BlockSpec((1, tm, WIDTH), lambda bb, i: (bb, i, 0))
    return pl.pallas_call(
        functools.partial(_ssd_conv_kernel, tm=tm),
        grid=(b, nt),
        in_specs=[main, prev, nxt, const((8, SSD_XBC)), const((1, SSD_XBC))],
        out_specs=[out, out],
        out_shape=[jax.ShapeDtypeStruct((b, t, WIDTH), BF16)] * 2,
        compiler_params=_cparams("parallel", "parallel"),
        name="ssd_conv",
    )(xz, xz, xz, w, bias)


def _ssd_kernel(xs_ref, bc_ref, dtc_ref, dtr_ref, bias_c_ref, bias_r_ref, nega_c_ref, nega_r_ref,
                tri_ref, trit_ref, y_ref, s_ref, *, nc, reverse):
    c_len = LIN_CHUNK
    per_group = N_HEADS // SSD_GROUPS
    gw = WIDTH // SSD_GROUPS

    @pl.when(pl.program_id(1) == 0)
    def _():
        s_ref[...] = jnp.zeros_like(s_ref)

    tri = tri_ref[...]
    trit = trit_ref[...]
    dts_c = _softplus(dtc_ref[0, 0] + bias_c_ref[0])
    la_c = dts_c * nega_c_ref[0]
    dts_r = _softplus(dtr_ref[0, 0] + bias_r_ref[0])
    la_r = dts_r * nega_r_ref[0]
    l_io = _iota((c_len, c_len), 0)
    s_io = _iota((c_len, c_len), 1)
    causal = (s_io >= l_io) if reverse else (s_io <= l_io)
    lo = _lane_lo((c_len, LANES))

    def by_head128(cols, g):
        return jnp.concatenate([jnp.broadcast_to(cols[:, g * per_group + h:g * per_group + h + 1],
                                                 (c_len, LANES)) for h in range(per_group)], axis=1)

    def to64(wide):
        h = [wide[:, j * LANES:(j + 1) * LANES] for j in range(per_group)]
        return jnp.concatenate([jnp.where(lo, h[0], h[1]), jnp.where(lo, h[2], h[3])], axis=1)

    order = _scan_order(nc, reverse)
    inst = [(g, c) for c in order for g in range(SSD_GROUPS)]
    n = len(inst)
    rows_of = lambda c: slice(c * c_len, (c + 1) * c_len)
    cum = [_mm_exact_left(tri, by_head128(la_c[rows_of(c)], g), terms=2) for g, c in inst]
    cum_r = {c: _mm_exact_right(la_r[:, rows_of(c)], trit, terms=2) for c in order}
    last = [u[0:1] if reverse else u[c_len - 1:c_len] for u in cum]
    bm = [bc_ref[0, rows_of(c), g * SSD_STATE:(g + 1) * SSD_STATE] for g, c in inst]
    cm = [bc_ref[0, rows_of(c), (SSD_GROUPS + g) * SSD_STATE:(SSD_GROUPS + g + 1) * SSD_STATE]
          for g, c in inst]
    xdt = [xs_ref[0, rows_of(c), g * gw:(g + 1) * gw].astype(F32) * to64(by_head128(dts_c[rows_of(c)], g))
           for g, c in inst]
    gm = [jnp.where(causal, _mm_nt(cm[i], bm[i]), 0.0) for i in range(n)]
    y_intra = []
    for i, (g, c) in enumerate(inst):
        parts = []
        for pp in range(per_group // 2):
            wide = []
            for hh in (2 * pp, 2 * pp + 1):
                head = g * per_group + hh
                diff = cum[i][:, hh * LANES:(hh + 1) * LANES] - cum_r[c][head:head + 1, :]
                wide.append((gm[i] * jnp.exp(jnp.minimum(diff, 0.0))).astype(BF16))
            parts.append(_mm(jnp.concatenate(wide, axis=1), _bd(xdt[i][:, pp * LANES:(pp + 1) * LANES])))
        y_intra.append(jnp.concatenate(parts, axis=1))
    from_start = [to64(jnp.exp(u)) for u in cum]
    upd = [_mm(bm[i].astype(F32).T, xdt[i] * to64(jnp.exp(last[i] - cum[i]))) for i in range(n)]
    whole = [to64(jnp.exp(u)) for u in last]

    s = [s_ref[g] for g in range(SSD_GROUPS)]
    for ci, c in enumerate(order):
        for g in range(SSD_GROUPS):
            i = ci * SSD_GROUPS + g
            y_ref[0, rows_of(c), g * gw:(g + 1) * gw] = (
                y_intra[i] + _mm(cm[i], s[g]) * from_start[i]).astype(BF16)
            s[g] = s[g] * whole[i] + upd[i]
    for g in range(SSD_GROUPS):
        s_ref[g] = s[g]


def _ssd_scan(xs, bc, dtc, dtr, sp, direction, tb=1024):
    b, t, _ = xs.shape
    nt = t // tb
    reverse = direction == 1
    tmap = (lambda i: nt - 1 - i) if reverse else (lambda i: i)
    row = pl.BlockSpec((1, tb, WIDTH), lambda bb, i: (bb, tmap(i), 0))
    par = lambda shape: pl.BlockSpec((1,) + shape, lambda bb, i: (direction, 0, 0))
    tri = pl.BlockSpec((LIN_CHUNK, LIN_CHUNK), lambda bb, i: (0, 0))
    return pl.pallas_call(
        functools.partial(_ssd_kernel, nc=tb // LIN_CHUNK, reverse=reverse),
        grid=(b, nt),
        in_specs=[row, row,
                  pl.BlockSpec((1, 1, tb, N_HEADS), lambda bb, i: (direction, bb, tmap(i), 0)),
                  pl.BlockSpec((1, 1, N_HEADS, tb), lambda bb, i: (direction, bb, 0, tmap(i))),
                  par((1, N_HEADS)), par((N_HEADS, 1)), par((1, N_HEADS)), par((N_HEADS, 1)),
                  tri, tri],
        out_specs=row,
        out_shape=jax.ShapeDtypeStruct((b, t, WIDTH), BF16),
        scratch_shapes=[pltpu.VMEM((SSD_GROUPS, SSD_STATE, WIDTH // SSD_GROUPS), F32)],
        compiler_params=_cparams("parallel", "arbitrary"),
        name=f"ssd_scan_{'bwd' if reverse else 'fwd'}",
    )(xs, bc, dtc, dtr, sp["bias_c"], sp["bias_r"], sp["nega_c"], sp["nega_r"],
      sp["tri_rev" if reverse else "tri_fwd"], sp["trit_rev" if reverse else "trit_fwd"])


def _odd_out_kernel(h_ref, rf, rb, gate, sf, sb, xs, z, gng, gnb, dsk, ng, ones_ref, w_ref, out_ref):
    f32 = lambda ref: ref[...].astype(F32)
    ones = ones_ref[...]
    inv = 1.0 / HEAD_DIM
    y = f32(rf) + f32(rb)
    yc = y - _group_sum(y, ones) * inv
    var = _group_sum(yc * yc, ones) * inv
    y_c = (yc * lax.rsqrt(var + GN_EPS) * gng[...] + gnb[...]) * _silu(f32(gate))
    yd = (f32(sf) + f32(sb) + dsk[...] * f32(xs)) * _silu(f32(z))
    gw = WIDTH // SSD_GROUPS
    halves = []
    for gi in range(SSD_GROUPS):
        yg = yd[:, gi * gw:(gi + 1) * gw]
        halves.append(yg * lax.rsqrt(jnp.mean(yg * yg, -1, keepdims=True) + NORM_EPS))
    y_d = jnp.concatenate(halves, axis=1) * ng[...]
    out_ref[...] = h_ref[...] + _mm(y_c, w_ref[0:WIDTH, :]) + _mm(y_d, w_ref[WIDTH:, :])


def _odd_out(h2, rf, rb, ret2, sf, sb, xs, xz2, gng, gnb, dsk, ng, ones_bd, w, tm=512):
    m = h2.shape[0]
    const = lambda shape: pl.BlockSpec(shape, lambda i: (0, 0))
    row = lambda n: pl.BlockSpec((tm, n), lambda i: (i, 0))
    vec = const((1, WIDTH))
    return pl.pallas_call(
        _odd_out_kernel,
        grid=(m // tm,),
        in_specs=[row(D_MODEL), row(WIDTH), row(WIDTH),
                  pl.BlockSpec((tm, WIDTH), lambda i: (i, 3)),
                  row(WIDTH), row(WIDTH), row(WIDTH),
                  pl.BlockSpec((tm, WIDTH), lambda i: (i, 2)),
                  vec, vec, vec, vec, const((WIDTH, WIDTH)), const((D_MODEL, D_MODEL))],
        out_specs=row(D_MODEL),
        out_shape=jax.ShapeDtypeStruct((m, D_MODEL), F32),
        compiler_params=_cparams("parallel"),
        name="odd_out",
    )(h2, rf, rb, ret2, sf, sb, xs, xz2, gng, gnb, dsk, ng, ones_bd, w)


def _tri(n, reverse):
    i = np.arange(n)
    m = (i[None, :] >= i[:, None]) if reverse else (i[None, :] <= i[:, None])
    return m.astype(np.float32)


def _prepare(norm_mix, norm_mlp, mlp_w1, mlp_w2, even_in_w, even_out_w, attn_q_gain, attn_k_gain,
             rwkv_mu_prev, rwkv_mu_next, rwkv_w0, rwkv_w2, rwkv_a0, rwkv_a2, rwkv_g2, rwkv_k_k,
             rwkv_k_a, rwkv_r_k, rwkv_ln_g, rwkv_ln_b, odd_in_w, odd_out_w, ret_decay_exp, ret_gn_g,
             ret_gn_b, ssd_conv_w, ssd_conv_b, ssd_dt_bias, ssd_a_log, ssd_d, ssd_norm_g):
    row = lambda v: v.reshape(1, -1).astype(F32)
    heads = np.arange(WIDTH) // HEAD_DIM
    ones_bd = jnp.asarray(heads[:, None] == heads[None, :], BF16)
    p = {"ones_bd": ones_bd, "norm_mix": norm_mix, "norm_mlp": norm_mlp,
         "mlp_w1": mlp_w1.astype(BF16), "mlp_w2": mlp_w2.astype(BF16)}

    w_in = even_in_w[0]
    p["even_wa"] = w_in[:, :3 * WIDTH].astype(BF16)
    p["even_wr"] = w_in[:, 3 * WIDTH:].astype(BF16)
    p["even_out_w"] = even_out_w[0].astype(BF16)
    p["q_gain"] = row(jnp.tile(attn_q_gain[0], N_HEADS))
    p["k_gain"] = row(jnp.tile(attn_k_gain[0], N_HEADS))

    def two_dir_lowrank(w):
        z = jnp.zeros_like(w[0])
        return jnp.concatenate([jnp.concatenate([w[0], z], 1), jnp.concatenate([z, w[1]], 1)], 0)

    p["rwkv"] = {
        "mu_prev": row(rwkv_mu_prev[0]), "mu_next": row(rwkv_mu_next[0]),
        "w0": row(rwkv_w0[0]), "w2": two_dir_lowrank(rwkv_w2[0]).astype(BF16),
        "a0": row(rwkv_a0[0]), "a2": two_dir_lowrank(rwkv_a2[0]).astype(BF16),
        "g2": rwkv_g2[0].astype(BF16), "k_k": row(rwkv_k_k[0]), "k_a": row(rwkv_k_a[0]),
        "r_k": row(rwkv_r_k[0]), "ones_bd": ones_bd,
    }
    p["rwkv_tri"] = [jnp.asarray(_tri(RWKV_CHUNK, False), BF16), jnp.asarray(_tri(RWKV_CHUNK, True), BF16)]
    p["rwkv_ln_g"] = row(rwkv_ln_g[0])
    p["rwkv_ln_b"] = row(rwkv_ln_b[0])

    w_odd = odd_in_w[0]
    p["odd_wr"] = w_odd[:, :4 * WIDTH].astype(BF16)
    z_w = w_odd[:, 4 * WIDTH:5 * WIDTH]
    xbc_w = w_odd[:, 5 * WIDTH:5 * WIDTH + SSD_XBC]
    dt_w = w_odd[:, 5 * WIDTH + SSD_XBC:]
    p["odd_ws"] = jnp.concatenate([xbc_w, z_w], axis=1).astype(BF16)
    p["odd_wd"] = jnp.concatenate([dt_w, jnp.zeros((D_MODEL, LANES - 2 * N_HEADS), F32)], 1).astype(BF16)
    p["odd_out_w"] = odd_out_w[0].astype(BF16)

    log_gamma = jnp.log1p(-jnp.exp2(-ret_decay_exp[0].astype(F32)))
    p["lg64"] = jnp.repeat(log_gamma, HEAD_DIM, axis=1).reshape(2, 1, WIDTH)
    p["lg128"] = jnp.repeat(log_gamma, LIN_CHUNK, axis=1).reshape(2, 1, N_HEADS * LIN_CHUNK)
    p["ret_gn_g"] = row(ret_gn_g[0])
    p["ret_gn_b"] = row(ret_gn_b[0])

    p["conv_w"] = jnp.concatenate([ssd_conv_w[0], jnp.zeros((8 - SSD_CONV, SSD_XBC), F32)], 0)
    p["conv_b"] = row(ssd_conv_b[0])
    bias = ssd_dt_bias[0].astype(F32)
    nega = -jnp.exp(ssd_a_log[0].astype(F32))
    p["ssd"] = {
        "bias_c": bias[:, None, :], "nega_c": nega[:, None, :],
        "bias_r": bias[:, :, None], "nega_r": nega[:, :, None],
        "tri_fwd": jnp.asarray(_tri(LIN_CHUNK, False), BF16),
        "tri_rev": jnp.asarray(_tri(LIN_CHUNK, True), BF16),
        "trit_fwd": jnp.asarray(_tri(LIN_CHUNK, False).T, BF16),
        "trit_rev": jnp.asarray(_tri(LIN_CHUNK, True).T, BF16),
    }
    p["ssd_d"] = row(jnp.repeat(ssd_d[0], HEAD_DIM))
    p["ssd_norm_g"] = row(ssd_norm_g[0])
    return p


def _rope_tables(t):
    half = HEAD_DIM // 2
    inv = ROPE_BASE ** (-jnp.arange(half, dtype=F32) / half)
    ang = jnp.arange(t, dtype=F32)[:, None] * inv[None, :]
    cos = jnp.tile(jnp.cos(ang), (1, LANES // half))
    sin = jnp.sin(ang)
    sin_signed = jnp.tile(jnp.concatenate([-sin, sin], axis=1), (1, LANES // HEAD_DIM))
    return cos, sin_signed


def _even_mixer(h2, b, t, p):
    h = h2.reshape(b, t, D_MODEL)
    *qkvs, rcols = _even_in(h, p["norm_mix"][0:1], p["even_wa"], p["even_wr"], p["ones_bd"],
                            p["q_gain"], p["k_gain"])
    branch = []
    for qkv, (_, dil) in zip(qkvs, DILATED_BRANCHES):
        branch.extend(_dil_attn(qkv, dil))
    r, v, kkn, g, bonus, g_in, krep, bvec = _rwkv_prep(rcols, p["rwkv"], p["rwkv_tri"])
    ys = [_rwkv_scan(r, v, kkn, g_in, krep, bvec, d) for d in range(2)]
    out = _even_out(h, branch, [ys[0], ys[1], bonus, g], p["rwkv_ln_g"], p["rwkv_ln_b"],
                    p["ones_bd"], p["even_out_w"])
    return out.reshape(b * t, D_MODEL)


def _odd_mixer(h2, b, t, p):
    cos, sin = _rope_tables(t)
    ret2, xz2, dt2 = _odd_in(h2, p["norm_mix"][1:2], p["odd_wr"], p["odd_ws"], p["odd_wd"], cos, sin)
    ret = ret2.reshape(b, t, -1)
    xz = xz2.reshape(b, t, -1)
    rets = [_retention(ret, p["lg64"], p["lg128"], d) for d in range(2)]
    xs, bc = _ssd_conv(xz, p["conv_w"], p["conv_b"])
    dt = dt2[:, :2 * N_HEADS].reshape(b, t, 2, N_HEADS)
    dtc = dt.transpose(2, 0, 1, 3)
    dtr = dt.transpose(2, 0, 3, 1)
    ssds = [_ssd_scan(xs, bc, dtc, dtr, p["ssd"], d) for d in range(2)]
    flat = lambda u: u.reshape(b * t, -1)
    return _odd_out(h2, flat(rets[0]), flat(rets[1]), ret2, flat(ssds[0]), flat(ssds[1]), flat(xs),
                    xz2, p["ret_gn_g"], p["ret_gn_b"], p["ssd_d"], p["ssd_norm_g"], p["ones_bd"],
                    p["odd_out_w"])


def _trunk(x, p):
    b, t, _ = x.shape
    h2 = x.reshape(b * t, D_MODEL)
    h2 = _even_mixer(h2, b, t, p)
    h2 = _mlp(h2, p["norm_mlp"][0:1], p["mlp_w1"][0], p["mlp_w2"][0])
    h2 = _odd_mixer(h2, b, t, p)
    h2 = _mlp(h2, p["norm_mlp"][1:2], p["mlp_w1"][1], p["mlp_w2"][1])
    return h2.reshape(b, t, D_MODEL)


def kernel(x_prompt, x_sample, norm_mix, norm_mlp, mlp_w1, mlp_w2, even_in_w, even_out_w, attn_q_gain, attn_k_gain, rwkv_mu_prev, rwkv_mu_next, rwkv_w0, rwkv_w2, rwkv_a0, rwkv_a2, rwkv_g2, rwkv_k_k, rwkv_k_a, rwkv_r_k, rwkv_ln_g, rwkv_ln_b, odd_in_w, odd_out_w, ret_decay_exp, ret_gn_g, ret_gn_b, ssd_conv_w, ssd_conv_b, ssd_dt_bias, ssd_a_log, ssd_d, ssd_norm_g):
    p = _prepare(norm_mix, norm_mlp, mlp_w1, mlp_w2, even_in_w, even_out_w, attn_q_gain, attn_k_gain,
                 rwkv_mu_prev, rwkv_mu_next, rwkv_w0, rwkv_w2, rwkv_a0, rwkv_a2, rwkv_g2, rwkv_k_k,
                 rwkv_k_a, rwkv_r_k, rwkv_ln_g, rwkv_ln_b, odd_in_w, odd_out_w, ret_decay_exp,
                 ret_gn_g, ret_gn_b, ssd_conv_w, ssd_conv_b, ssd_dt_bias, ssd_a_log, ssd_d, ssd_norm_g)
    return (_trunk(x_prompt, p), _trunk(x_sample, p))
```

```python
import functools
import math

import numpy as np
import jax
import jax.numpy as jnp
from jax import lax
from jax.experimental import pallas as pl
from jax.experimental.pallas import tpu as pltpu

F32 = jnp.float32
BF16 = jnp.bfloat16

D_MODEL = 1024
HEAD_DIM = 64
N_HEADS = 8
WIDTH = N_HEADS * HEAD_DIM
N_PAIRS = N_HEADS // 2
LANES = 128
HALO_ROWS = 16
LSE_LANES = LANES // N_HEADS
DILATED_BRANCHES = ((128, 1), (512, 4), (2048, 16))
ATT_R = 64
RWKV_DECAY_RANK = 64
RWKV_ICL_RANK = 64
RWKV_GATE_RANK = 128
RWKV_COLS = 3 * WIDTH + 2 * RWKV_DECAY_RANK + 2 * RWKV_ICL_RANK + RWKV_GATE_RANK
SSD_STATE = 128
SSD_GROUPS = 2
SSD_CONV = 5
SSD_XBC = WIDTH + 2 * SSD_GROUPS * SSD_STATE
D_FF = 4 * D_MODEL
NORM_EPS = 1e-6
GN_EPS = 1e-5
RWKV_GN_EPS = 64e-5
ROPE_BASE = 10000.0
RWKV_CHUNK = 64
LIN_CHUNK = 128
NEG_BIG = -1e30
LOG2_E = math.log2(math.e)
VMEM_LIMIT = 56 * 1024 * 1024


def _cparams(*sem):
    return pltpu.CompilerParams(dimension_semantics=sem, vmem_limit_bytes=VMEM_LIMIT)


def _iota(shape, dim):
    return lax.broadcasted_iota(jnp.int32, shape, dim)


def _lane_lo(shape):
    return (_iota(shape, len(shape) - 1) & HEAD_DIM) == 0


def _bd(x):
    lo = _lane_lo(x.shape)
    zero = jnp.zeros_like(x)
    return jnp.concatenate([jnp.where(lo, x, zero), jnp.where(lo, zero, x)], axis=0)


def _mm(a, b):
    return jnp.dot(a.astype(BF16), b.astype(BF16), preferred_element_type=F32)


def _mm_nt(a, b):
    return lax.dot_general(a.astype(BF16), b.astype(BF16), (((1,), (1,)), ((), ())),
                           preferred_element_type=F32)


def _split_bf16(x, terms):
    parts = []
    for _ in range(terms - 1):
        hi = x.astype(BF16)
        parts.append(hi)
        x = x - hi.astype(F32)
    parts.append(x.astype(BF16))
    return parts


def _mm_exact_left(t, x, terms=3):
    return sum(jnp.dot(t, piece, preferred_element_type=F32) for piece in _split_bf16(x, terms))


def _mm_exact_right(x, t, terms=3):
    return sum(jnp.dot(piece, t, preferred_element_type=F32) for piece in _split_bf16(x, terms))


def _group_sum(x, ones_bd):
    return jnp.dot(x.astype(BF16), ones_bd, preferred_element_type=F32)


def _rms_rows(x, g):
    return x * lax.rsqrt(jnp.mean(x * x, -1, keepdims=True) + NORM_EPS) * g


def _sigmoid(x):
    return 0.5 * jnp.tanh(0.5 * x) + 0.5


def _silu(x):
    return x * _sigmoid(x)


def _softplus(x):
    return jnp.maximum(x, 0.0) + jnp.log1p(jnp.exp(-jnp.abs(x)))


def _scan_order(nc, reverse):
    return list(range(nc - 1, -1, -1) if reverse else range(nc))


def _even_in_kernel(x_ref, g_ref, wa_ref, wr_ref, ones_ref, qg_ref, kg_ref, perm4_ref, perm16_ref,
                    qkv1_out, qkv4_out, qkv16_out, r_out, *, tm):
    ub = _rms_rows(x_ref[0], g_ref[...]).astype(BF16)
    a = jnp.dot(ub, wa_ref[...], preferred_element_type=F32)
    q = a[:, :WIDTH]
    k = a[:, WIDTH:2 * WIDTH]
    ones = ones_ref[...]
    inv = 1.0 / HEAD_DIM
    qn = q * lax.rsqrt(_group_sum(q * q, ones) * inv + NORM_EPS) * qg_ref[...]
    kn = k * lax.rsqrt(_group_sum(k * k, ones) * inv + NORM_EPS) * kg_ref[...]
    qkv = jnp.concatenate([qn * (HEAD_DIM ** -0.5 * LOG2_E), kn, a[:, 2 * WIDTH:]], axis=1).astype(BF16)
    qkv1_out[0, 0] = qkv
    for dil, perm_ref, out in ((4, perm4_ref, qkv4_out), (16, perm16_ref, qkv16_out)):
        perm = perm_ref[...]
        bs = perm.shape[0]
        per = bs // dil
        for blk in range(tm // bs):
            moved = jnp.dot(perm, qkv[blk * bs:(blk + 1) * bs],
                            preferred_element_type=F32).astype(BF16)
            for rho in range(dil):
                out[0, rho, blk * per:(blk + 1) * per, :] = moved[rho * per:(rho + 1) * per]
    r_out[0] = jnp.dot(ub, wr_ref[...], preferred_element_type=F32).astype(BF16)


def _even_in(x, g, wa, wr, ones_bd, qg, kg, tm=512):
    b, t, _ = x.shape
    const = lambda shape: pl.BlockSpec(shape, lambda bb, i: (0, 0))
    row = lambda n: pl.BlockSpec((1, tm, n), lambda bb, i: (bb, i, 0))
    res = lambda d: pl.BlockSpec((1, d, tm // d, 3 * WIDTH), lambda bb, i: (bb, 0, i, 0))
    dils = [d for _, d in DILATED_BRANCHES]

    def to_residue_major(d, bs):
        src = np.arange(bs).reshape(bs // d, d).T.reshape(-1)
        return jnp.asarray(src[:, None] == np.arange(bs)[None, :], BF16)

    blocks = {4: 2 * HALO_ROWS * 4, 16: HALO_ROWS * 16}

    return pl.pallas_call(
        functools.partial(_even_in_kernel, tm=tm),
        grid=(b, t // tm),
        in_specs=[row(D_MODEL), const((1, D_MODEL)), const((D_MODEL, 3 * WIDTH)),
                  const((D_MODEL, RWKV_COLS)), const((WIDTH, WIDTH)),
                  const((1, WIDTH)), const((1, WIDTH)),
                  const((blocks[4], blocks[4])), const((blocks[16], blocks[16]))],
        out_specs=[res(d) for d in dils] + [row(RWKV_COLS)],
        out_shape=[jax.ShapeDtypeStruct((b, d, t // d, 3 * WIDTH), BF16) for d in dils]
        + [jax.ShapeDtypeStruct((b, t, RWKV_COLS), BF16)],
        compiler_params=_cparams("parallel", "parallel"),
        name="even_in",
    )(x, g, wa, wr, ones_bd, qg, kg, to_residue_major(4, blocks[4]), to_residue_major(16, blocks[16]))


def _dil_attn_kernel(q_ref, k_ref, kl_ref, kr_ref, v_ref, vl_ref, vr_ref, o_ref, lse_ref,
                     kext, vext, *, tq, n_res, sub_len, dil):
    r = ATT_R
    i = pl.program_id(2)
    for z in range(n_res):
        kext[z, 0:r] = kl_ref[0, z]
        kext[z, r:r + tq] = k_ref[0, z]
        kext[z, r + tq:] = kr_ref[0, z]
        vext[z, 0:r] = vl_ref[0, z]
        vext[z, r:r + tq] = v_ref[0, z]
        vext[z, r + tq:] = vr_ref[0, z]

    c_io = _iota((r, 3 * r), 1)
    q_io = _iota((r, 3 * r), 0)
    rel = c_io - r - q_io
    near = jnp.abs(rel) <= r
    dist = jnp.abs(rel).astype(F32) * float(dil)
    lo = _lane_lo((r, LANES))
    near2 = jnp.concatenate([near, near], axis=0)
    dist2 = jnp.concatenate([dist, dist], axis=0)
    first_head = _iota((2 * r, 3 * r), 0) < r
    cols = [slice(p * LANES, (p + 1) * LANES) for p in range(N_PAIRS)]
    bias = [jnp.where(first_head, 2.0 ** (-(2 * p + 1)), 2.0 ** (-(2 * p + 2))) * (LOG2_E * dist2)
            for p in range(N_PAIRS)]

    blocks = [(z, j) for z in range(n_res) for j in range(tq // r)]
    group = 2 if len(blocks) % 2 == 0 else 1
    for g0 in range(0, len(blocks), group):
        inst = [(z, j, p) for z, j in blocks[g0:g0 + group] for p in range(N_PAIRS)]
        valid = {}
        for z, j in blocks[g0:g0 + group]:
            kpos = i * tq + j * r + _iota((2 * r, 3 * r), 1) - r
            valid[j] = near2 & (kpos >= 0) & (kpos < sub_len)
        s = [jnp.where(valid[j],
                       _mm_nt(_bd(q_ref[0, z, j * r:(j + 1) * r, cols[p]]),
                              kext[z, j * r:(j + 3) * r, cols[p]]) - bias[p], NEG_BIG)
             for z, j, p in inst]
        m = [jnp.max(u, -1, keepdims=True) for u in s]
        e = [jnp.exp2(u - mm) for u, mm in zip(s, m)]
        den = [jnp.sum(u, -1, keepdims=True) for u in e]
        pv = [_mm(e[n], vext[z, j * r:(j + 3) * r, cols[p]]) / den[n]
              for n, (z, j, p) in enumerate(inst)]
        lse = [m[n] * (1.0 / LOG2_E) + jnp.log(den[n]) for n in range(len(inst))]
        for n, (z, j, p) in enumerate(inst):
            o_ref[0, z, j * r:(j + 1) * r, cols[p]] = jnp.where(lo, pv[n][:r], pv[n][r:]).astype(BF16)
        lane = _iota((r, LANES), 1)
        for bi, (z, j) in enumerate(blocks[g0:g0 + group]):
            tile = jnp.zeros((r, LANES), F32)
            for p in range(N_PAIRS):
                col = lse[bi * N_PAIRS + p]
                for hh in range(2):
                    h0 = (2 * p + hh) * LSE_LANES
                    tile = jnp.where((lane >= h0) & (lane < h0 + LSE_LANES),
                                     col[hh * r:(hh + 1) * r], tile)
            lse_ref[0, z, j * r:(j + 1) * r, :] = tile


def _dil_attn(qkv, dil, rows_per_step=1024):
    b, _, sub_len, _ = qkv.shape
    tq = min(sub_len, rows_per_step)
    n_res = min(dil, rows_per_step // tq)
    nblk = sub_len // ATT_R
    per = tq // ATT_R
    main = lambda col: pl.BlockSpec((1, n_res, tq, WIDTH), lambda bb, rr, i: (bb, rr, i, col))
    left = lambda col: pl.BlockSpec((1, n_res, ATT_R, WIDTH),
                                    lambda bb, rr, i: (bb, rr, jnp.maximum(i * per - 1, 0), col))
    right = lambda col: pl.BlockSpec((1, n_res, ATT_R, WIDTH),
                                     lambda bb, rr, i: (bb, rr, jnp.minimum((i + 1) * per, nblk - 1), col))
    return pl.pallas_call(
        functools.partial(_dil_attn_kernel, tq=tq, n_res=n_res, sub_len=sub_len, dil=dil),
        grid=(b, dil // n_res, sub_len // tq),
        in_specs=[main(0), main(1), left(1), right(1), main(2), left(2), right(2)],
        out_specs=[main(0), pl.BlockSpec((1, n_res, tq, LANES), lambda bb, rr, i: (bb, rr, i, 0))],
        out_shape=[jax.ShapeDtypeStruct((b, dil, sub_len, WIDTH), BF16),
                   jax.ShapeDtypeStruct((b, dil, sub_len, LANES), F32)],
        scratch_shapes=[pltpu.VMEM((n_res, tq + 2 * ATT_R, WIDTH), BF16)] * 2,
        compiler_params=_cparams("parallel", "parallel", "parallel"),
        name=f"dil_attn_d{dil}",
    )(qkv, qkv, qkv, qkv, qkv, qkv, qkv)


def _rwkv_prep_kernel(x_ref, xp_ref, xn_ref, mup_ref, mun_ref, w0_ref, w2_ref, a0_ref, a2_ref,
                      g2_ref, kk_ref, ka_ref, rk_ref, ones_ref, trif_ref, trir_ref,
                      r_out, v_out, kkn_out, g_out, bonus_out, g_in_out, krep_out, b_out, *, tm):
    i = pl.program_id(1)
    nt = pl.num_programs(1)
    x = x_ref[0].astype(F32)
    prow = jnp.where(i > 0, xp_ref[0, HALO_ROWS - 1:HALO_ROWS, :].astype(F32), 0.0)
    nrow = jnp.where(i < nt - 1, xn_ref[0, 0:1, :].astype(F32), 0.0)
    rid = _iota((8, 1), 0)
    prev = pltpu.roll(x, 1, 0)
    prev = jnp.concatenate([jnp.where(rid == 0, prow, prev[:8]), prev[8:]], axis=0)
    nxt = pltpu.roll(x, tm - 1, 0)
    nxt = jnp.concatenate([nxt[:tm - 8], jnp.where(rid == 7, nrow, nxt[tm - 8:])], axis=0)
    xs = x + mup_ref[...] * (prev - x) + mun_ref[...] * (nxt - x)
    r = xs[:, 0:WIDTH]
    k = xs[:, WIDTH:2 * WIDTH]
    v = xs[:, 2 * WIDTH:3 * WIDTH]
    zw = xs[:, 3 * WIDTH:3 * WIDTH + LANES]
    za = xs[:, 3 * WIDTH + LANES:3 * WIDTH + 2 * LANES]
    zg = xs[:, 3 * WIDTH + 2 * LANES:]
    ones = ones_ref[...]
    kk = k * kk_ref[...]
    kkn = kk * lax.rsqrt(jnp.maximum(_group_sum(kk * kk, ones), 1e-24))
    g_out[0] = _mm(_sigmoid(zg), g2_ref[...]).astype(BF16)
    r_out[0] = r.astype(BF16)
    v_out[0] = v.astype(BF16)
    kkn_out[0] = kkn.astype(BF16)
    bonus_out[0] = (_group_sum(r * k * rk_ref[...], ones) * v).astype(BF16)
    wx = w0_ref[...] + _mm(jnp.tanh(zw), w2_ref[...])
    ax = a0_ref[...] + _mm(za, a2_ref[...])
    for d, tri_ref in enumerate((trif_ref, trir_ref)):
        cols = slice(d * WIDTH, (d + 1) * WIDTH)
        lw = -math.exp(-0.5) * _sigmoid(wx[:, cols])
        tri = tri_ref[...]
        for c in range(tm // RWKV_CHUNK):
            rows = slice(c * RWKV_CHUNK, (c + 1) * RWKV_CHUNK)
            g_in_out[d, 0, rows, :] = _mm_exact_left(tri, lw[rows], terms=2)
        a = _sigmoid(ax[:, cols])
        krep_out[d, 0] = (k * (1.0 + (a - 1.0) * ka_ref[...])).astype(BF16)
        b_out[d, 0] = (kkn * a).astype(BF16)


def _rwkv_prep(rcols, p, tri, tm=512):
    b, t, _ = rcols.shape
    nt = t // tm
    per = tm // HALO_ROWS
    const = lambda shape: pl.BlockSpec(shape, lambda bb, i: (0,) * len(shape))
    main = lambda n: pl.BlockSpec((1, tm, n), lambda bb, i: (bb, i, 0))
    dirs = pl.BlockSpec((2, 1, tm, WIDTH), lambda bb, i: (0, bb, i, 0))
    prev = pl.BlockSpec((1, HALO_ROWS, RWKV_COLS), lambda bb, i: (bb, jnp.maximum(i * per - 1, 0), 0))
    nxt = pl.BlockSpec((1, HALO_ROWS, RWKV_COLS),
                       lambda bb, i: (bb, jnp.minimum((i + 1) * per, t // HALO_ROWS - 1), 0))
    one = jax.ShapeDtypeStruct((b, t, WIDTH), BF16)
    two = lambda dt: jax.ShapeDtypeStruct((2, b, t, WIDTH), dt)
    return pl.pallas_call(
        functools.partial(_rwkv_prep_kernel, tm=tm),
        grid=(b, nt),
        in_specs=[main(RWKV_COLS), prev, nxt, const((1, RWKV_COLS)), const((1, RWKV_COLS)),
                  const((1, 2 * WIDTH)), const((LANES, 2 * WIDTH)),
                  const((1, 2 * WIDTH)), const((LANES, 2 * WIDTH)),
                  const((LANES, WIDTH)), const((1, WIDTH)), const((1, WIDTH)), const((1, WIDTH)),
                  const((WIDTH, WIDTH)), const((RWKV_CHUNK, RWKV_CHUNK)),
                  const((RWKV_CHUNK, RWKV_CHUNK))],
        out_specs=[main(WIDTH)] * 5 + [dirs] * 3,
        out_shape=[one] * 5 + [two(F32), two(BF16), two(BF16)],
        compiler_params=_cparams("parallel", "parallel"),
        name="rwkv_prep",
    )(rcols, rcols, rcols, p["mu_prev"], p["mu_next"], p["w0"], p["w2"], p["a0"], p["a2"],
      p["g2"], p["k_k"], p["k_a"], p["r_k"], p["ones_bd"], tri[0], tri[1])


def _rwkv_scan_kernel(r_ref, v_ref, kk_ref, g_ref, k_ref, b_ref, y_ref, s_ref, *, nc, reverse):
    c_len = RWKV_CHUNK

    @pl.when(pl.program_id(1) == 0)
    def _():
        s_ref[...] = jnp.zeros_like(s_ref)

    row = _iota((c_len, LANES), 0)
    li = _iota((c_len, LANES), 1) & (HEAD_DIM - 1)
    strict = (li > row) if reverse else (li < row)
    incl = (li >= row) if reverse else (li <= row)
    r2 = _iota((LANES, LANES), 0)
    c2 = _iota((LANES, LANES), 1)
    same_head = (r2 & HEAD_DIM) == (c2 & HEAD_DIM)
    diag = r2 == c2
    zeros = jnp.zeros((c_len, LANES), F32)

    order = _scan_order(nc, reverse)
    inst = [(p, c) for c in order for p in range(N_PAIRS)]
    n = len(inst)

    def tile(ref, p, c, lead=()):
        return ref[lead + (0, slice(c * c_len, (c + 1) * c_len), slice(p * LANES, (p + 1) * LANES))]

    g_in = [tile(g_ref, p, c, (0,)) for p, c in inst]
    g_tot = [g[0:1] if reverse else g[c_len - 1:c_len] for g in g_in]
    scan_first = row == (c_len - 1 if reverse else 0)
    g_ex = [jnp.where(scan_first, 0.0, pltpu.roll(g, c_len - 1 if reverse else 1, 0)) for g in g_in]
    v = [tile(v_ref, p, c) for p, c in inst]
    at, rt, w, bk_end = [], [], [], []
    for i, (p, c) in enumerate(inst):
        b = tile(b_ref, p, c, (0,)).astype(F32)
        k = tile(k_ref, p, c, (0,)).astype(F32)
        at.append(-tile(kk_ref, p, c).astype(F32) * jnp.exp(g_ex[i]))
        rt.append(tile(r_ref, p, c).astype(F32) * jnp.exp(g_in[i]))
        inv = jnp.exp(-g_in[i])
        to_end = jnp.exp(g_tot[i] - g_in[i])
        bk_end.append(jnp.concatenate([b * to_end, k * to_end], 0))
        w.append(_mm_nt(jnp.concatenate([at[i], rt[i]], 0),
                        jnp.concatenate([_bd(b * inv), _bd(k * inv)], 0)))

    bdv = [_bd(u) for u in v]
    m_ab = [jnp.where(strict, u[:c_len, :LANES], 0.0) for u in w]
    kv = [_mm(jnp.concatenate([jnp.where(strict, w[i][:c_len, LANES:], 0.0),
                               jnp.where(incl, w[i][c_len:, LANES:], 0.0)], 0), bdv[i])
          for i in range(n)]
    rhs = [jnp.concatenate([at[i], kv[i][:c_len]], axis=1) for i in range(n)]
    tinv = [jnp.where(li == row, 1.0, 0.0) + u for u in m_ab]
    mj = [u.astype(BF16) for u in m_ab]
    mj = [_mm(u, _bd(u)).astype(BF16) for u in mj]
    for j in range(1, 5):
        both = [_mm(jnp.concatenate([mj[i], tinv[i].astype(BF16)], 0), _bd(mj[i])) for i in range(n)]
        mj = [u[:c_len].astype(BF16) for u in both]
        tinv = [tinv[i] + both[i][c_len:] for i in range(n)]
    tinv = [tinv[i] + _mm(tinv[i], _bd(mj[i])) for i in range(n)]
    x = [_mm(tinv[i], _bd(rhs[i])) for i in range(n)]

    t1 = [_mm(jnp.where(incl, w[i][c_len:, :LANES], 0.0), _bd(x[i])) for i in range(n)]
    y0 = [t1[i][:, LANES:] + kv[i][c_len:] for i in range(n)]
    rp = [(rt[i] + t1[i][:, :LANES]).astype(BF16) for i in range(n)]
    t2 = [_mm(bk_end[i].T,
              jnp.concatenate([x[i], jnp.concatenate([zeros, v[i].astype(F32)], axis=1)], 0))
          for i in range(n)]
    g_low = [jnp.where(same_head, u[:, :LANES], 0.0).astype(BF16) for u in t2]
    h_new = [jnp.where(same_head, u[:, LANES:], 0.0) for u in t2]
    g_col = [jnp.exp(jnp.sum(jnp.where(diag, jnp.broadcast_to(g, (LANES, LANES)), 0.0),
                             axis=1, keepdims=True)) for g in g_tot]

    s = [s_ref[p] for p in range(N_PAIRS)]
    for ci, c in enumerate(order):
        ids = [ci * N_PAIRS + p for p in range(N_PAIRS)]
        both = [_mm(jnp.concatenate([rp[i], g_low[i]], 0), s[p]) for p, i in enumerate(ids)]
        for p, i in enumerate(ids):
            y_ref[0, c * c_len:(c + 1) * c_len, p * LANES:(p + 1) * LANES] = (
                both[p][:c_len] + y0[i]).astype(BF16)
        s = [s[p] * g_col[i] + both[p][c_len:] + h_new[i] for p, i in enumerate(ids)]
    for p in range(N_PAIRS):
        s_ref[p] = s[p]


def _rwkv_scan(r, v, kkn, g_in, krep, bvec, direction, tb=512):
    b, t, _ = r.shape
    nt = t // tb
    reverse = direction == 1
    tmap = (lambda i: nt - 1 - i) if reverse else (lambda i: i)
    one = pl.BlockSpec((1, tb, WIDTH), lambda bb, i: (bb, tmap(i), 0))
    two = pl.BlockSpec((1, 1, tb, WIDTH), lambda bb, i: (direction, bb, tmap(i), 0))
    return pl.pallas_call(
        functools.partial(_rwkv_scan_kernel, nc=tb // RWKV_CHUNK, reverse=reverse),
        grid=(b, nt),
        in_specs=[one, one, one, two, two, two],
        out_specs=one,
        out_shape=jax.ShapeDtypeStruct((b, t, WIDTH), BF16),
        scratch_shapes=[pltpu.VMEM((N_PAIRS, LANES, LANES), F32)],
        compiler_params=_cparams("parallel", "arbitrary"),
        name=f"rwkv_scan_{'bwd' if reverse else 'fwd'}",
    )(r, v, kkn, g_in, krep, bvec)


def _even_out_kernel(h_ref, o1, l1, o4, l4, o16, l16, yf, yb, bonus, g, lng, lnb, ones_ref, spread_ref,
                     w_ref, out_ref, tok_sc, lse_sc, *, tm):
    n_tiles = WIDTH // LANES
    for slot, (dil, o_ref, l_ref) in enumerate(((4, o4, l4), (16, o16, l16))):
        for rho in range(dil):
            rows = pl.ds(rho, tm // dil, stride=dil)
            lse_sc[slot, rows, :] = l_ref[0, rho]
            for c in range(n_tiles):
                tok_sc[slot * n_tiles + c, rows, :] = o_ref[0, rho, :, c * LANES:(c + 1) * LANES].astype(F32)
    tok = lambda slot: jnp.concatenate([tok_sc[slot * n_tiles + c] for c in range(n_tiles)], axis=1)
    la, lb, lc = l1[0, 0], lse_sc[0], lse_sc[1]
    mx = jnp.maximum(jnp.maximum(la, lb), lc)
    wa, wb, wc = jnp.exp(la - mx), jnp.exp(lb - mx), jnp.exp(lc - mx)
    norm = 1.0 / (wa + wb + wc)
    spread = lambda wgt: jnp.dot((wgt * norm).astype(BF16), spread_ref[...], preferred_element_type=F32)
    y_a = spread(wa) * o1[0, 0].astype(F32) + spread(wb) * tok(0) + spread(wc) * tok(1)
    ones = ones_ref[...]
    inv = 1.0 / HEAD_DIM
    y = yf[0].astype(F32) + yb[0].astype(F32)
    yc = y - _group_sum(y, ones) * inv
    var = _group_sum(yc * yc, ones) * inv
    yn = yc * lax.rsqrt(var + RWKV_GN_EPS) * lng[...] + lnb[...]
    y_b = (yn + bonus[0].astype(F32)) * g[0].astype(F32)
    out_ref[0] = (h_ref[0] + _mm(y_a, w_ref[0:WIDTH, :]) + _mm(y_b, w_ref[WIDTH:, :]))


def _even_out(h, branch, rwkv_parts, lng, lnb, ones_bd, w, tm=512):
    b, t, _ = h.shape
    const = lambda shape: pl.BlockSpec(shape, lambda bb, i: (0, 0))
    row = lambda n: pl.BlockSpec((1, tm, n), lambda bb, i: (bb, i, 0))
    res = lambda d, n: pl.BlockSpec((1, d, tm // d, n), lambda bb, i: (bb, 0, i, 0))
    dils = [d for _, d in DILATED_BRANCHES]
    spread = jnp.asarray(np.arange(LANES)[:, None] == (np.arange(WIDTH)[None, :] // HEAD_DIM) * LSE_LANES, BF16)
    return pl.pallas_call(
        functools.partial(_even_out_kernel, tm=tm),
        grid=(b, t // tm),
        in_specs=[row(D_MODEL)] + [res(d, n) for d in dils for n in (WIDTH, LANES)] + [row(WIDTH)] * 4
        + [const((1, WIDTH)), const((1, WIDTH)), const((WIDTH, WIDTH)), const((LANES, WIDTH)),
           const((D_MODEL, D_MODEL))],
        out_specs=row(D_MODEL),
        out_shape=jax.ShapeDtypeStruct((b, t, D_MODEL), F32),
        scratch_shapes=[pltpu.VMEM((2 * WIDTH // LANES, tm, LANES), F32), pltpu.VMEM((2, tm, LANES), F32)],
        compiler_params=_cparams("parallel", "parallel"),
        name="even_out",
    )(h, *branch, *rwkv_parts, lng, lnb, ones_bd, spread, w)


def _mlp_kernel(h_ref, g_ref, w1_ref, w2_ref, out_ref, u_ref):
    f = pl.program_id(1)

    @pl.when(f == 0)
    def _():
        x = h_ref[...]
        u_ref[...] = _rms_rows(x, g_ref[...]).astype(BF16)
        out_ref[...] = x

    hdn = jnp.dot(u_ref[...], w1_ref[...], preferred_element_type=F32)
    hdn = jnp.square(jnp.maximum(hdn, 0.0))
    out_ref[...] += jnp.dot(hdn.astype(BF16), w2_ref[...], preferred_element_type=F32)


def _mlp(h2, g, w1, w2, tm=1024, tf=2048):
    m = h2.shape[0]
    return pl.pallas_call(
        _mlp_kernel,
        grid=(m // tm, D_FF // tf),
        in_specs=[pl.BlockSpec((tm, D_MODEL), lambda i, f: (i, 0)),
                  pl.BlockSpec((1, D_MODEL), lambda i, f: (0, 0)),
                  pl.BlockSpec((D_MODEL, tf), lambda i, f: (0, f)),
                  pl.BlockSpec((tf, D_MODEL), lambda i, f: (f, 0))],
        out_specs=pl.BlockSpec((tm, D_MODEL), lambda i, f: (i, 0)),
        out_shape=jax.ShapeDtypeStruct((m, D_MODEL), F32),
        scratch_shapes=[pltpu.VMEM((tm, D_MODEL), BF16)],
        compiler_params=_cparams("parallel", "arbitrary"),
        name="mlp",
    )(h2, g, w1, w2)


def _rotary_pair(u, cos, sin_signed):
    first = (_iota(u.shape, 1) & (HEAD_DIM - 1)) < HEAD_DIM // 2
    swapped = jnp.where(first, pltpu.roll(u, LANES - HEAD_DIM // 2, 1),
                        pltpu.roll(u, HEAD_DIM // 2, 1))
    return u * cos + swapped * sin_signed


def _odd_in_kernel(x_ref, g_ref, wr_ref, ws_ref, wd_ref, cos_ref, sin_ref, ret_out, xz_out, dt_out):
    ub = _rms_rows(x_ref[...], g_ref[...]).astype(BF16)
    ret = jnp.dot(ub, wr_ref[...], preferred_element_type=F32)
    cos = cos_ref[...]
    sin = sin_ref[...]
    for c in range(2 * N_PAIRS):
        cols = slice(c * LANES, (c + 1) * LANES)
        scale = 1.0 if c < N_PAIRS else HEAD_DIM ** -0.5
        ret_out[:, cols] = (_rotary_pair(ret[:, cols], cos, sin) * scale).astype(BF16)
    ret_out[:, 2 * WIDTH:] = ret[:, 2 * WIDTH:].astype(BF16)
    xz_out[...] = jnp.dot(ub, ws_ref[...], preferred_element_type=F32).astype(BF16)
    dt_out[...] = jnp.dot(ub, wd_ref[...], preferred_element_type=F32)


def _odd_in(x2, g, wr, ws, wd, cos, sin, tm=512):
    m = x2.shape[0]
    per_seq = cos.shape[0] // tm
    const = lambda shape: pl.BlockSpec(shape, lambda i: (0, 0))
    row = lambda n: pl.BlockSpec((tm, n), lambda i: (i, 0))
    tab = pl.BlockSpec((tm, LANES), lambda i: (i % per_seq, 0))
    n_xz = SSD_XBC + WIDTH
    return pl.pallas_call(
        _odd_in_kernel,
        grid=(m // tm,),
        in_specs=[row(D_MODEL), const((1, D_MODEL)), const((D_MODEL, 4 * WIDTH)),
                  const((D_MODEL, n_xz)), const((D_MODEL, LANES)), tab, tab],
        out_specs=[row(4 * WIDTH), row(n_xz), row(LANES)],
        out_shape=[jax.ShapeDtypeStruct((m, 4 * WIDTH), BF16),
                   jax.ShapeDtypeStruct((m, n_xz), BF16),
                   jax.ShapeDtypeStruct((m, LANES), F32)],
        compiler_params=_cparams("parallel"),
        name="odd_in",
    )(x2, g, wr, ws, wd, cos, sin)


def _ret_kernel(q_ref, k_ref, v_ref, lg64_ref, lg128_ref, y_ref, s_ref,
                *, nc, reverse):
    c_len = LIN_CHUNK

    @pl.when(pl.program_id(1) == 0)
    def _():
        s_ref[...] = jnp.zeros_like(s_ref)

    l_io = _iota((c_len, 2 * c_len), 0)
    s_io = _iota((c_len, 2 * c_len), 1) & (c_len - 1)
    ahead = jnp.maximum(l_io - s_io, 0).astype(F32)
    behind = jnp.maximum(s_io - l_io, 0).astype(F32)
    pos = _iota((c_len, LANES), 0)
    if reverse:
        pos = c_len - 1 - pos
    pos = pos.astype(F32)
    r2 = _iota((LANES, LANES), 0)
    c2 = _iota((LANES, LANES), 1)
    same_head = (r2 & HEAD_DIM) == (c2 & HEAD_DIM)

    decay, from_start, to_end, whole = [], [], [], []
    for p in range(N_PAIRS):
        lg64 = lg64_ref[0, :, p * LANES:(p + 1) * LANES]
        if not reverse:
            wide = slice(2 * p * c_len, 2 * (p + 1) * c_len)
            decay.append(jnp.where(l_io >= s_io, jnp.exp(lg128_ref[0, :, wide] * ahead), 0.0)
                         + jnp.where(s_io >= l_io, jnp.exp(lg128_ref[1, :, wide] * behind), 0.0))
        from_start.append(jnp.exp(lg64 * (pos + 1.0)))
        to_end.append(jnp.exp(lg64 * (float(c_len - 1) - pos)))
        whole.append(jnp.exp(lg64 * float(c_len)))

    order = _scan_order(nc, reverse)
    inst = [(p, c) for c in order for p in range(N_PAIRS)]
    q, k, v = [], [], []
    for p, c in inst:
        rows = slice(c * c_len, (c + 1) * c_len)
        cols = slice(p * LANES, (p + 1) * LANES)
        q.append(q_ref[0, rows, cols])
        k.append(k_ref[0, rows, cols])
        v.append(v_ref[0, rows, cols])
    if not reverse:
        sc = [(_mm_nt(q[i], _bd(k[i])) * decay[p]).astype(BF16) for i, (p, c) in enumerate(inst)]
        y_intra = [_mm(sc[i], _bd(v[i])) for i in range(len(inst))]
    upd = [jnp.where(same_head, _mm((k[i] * to_end[p]).T, v[i]), 0.0) for i, (p, c) in enumerate(inst)]
    qs = [(q[i] * from_start[p]).astype(BF16) for i, (p, c) in enumerate(inst)]

    s = [s_ref[p] for p in range(N_PAIRS)]
    for ci, c in enumerate(order):
        for p in range(N_PAIRS):
            i = ci * N_PAIRS + p
            y = _mm(qs[i], s[p])
            if not reverse:
                y = y + y_intra[i]
            y_ref[0, c * c_len:(c + 1) * c_len, p * LANES:(p + 1) * LANES] = y.astype(BF16)
            s[p] = s[p] * whole[p] + upd[i]
    for p in range(N_PAIRS):
        s_ref[p] = s[p]


def _retention(ret, lg64, lg128, direction, tb=1024):
    b, t, _ = ret.shape
    nt = t // tb
    reverse = direction == 1
    tmap = (lambda i: nt - 1 - i) if reverse else (lambda i: i)
    col = lambda blk: pl.BlockSpec((1, tb, WIDTH), lambda bb, i: (bb, tmap(i), blk))
    return pl.pallas_call(
        functools.partial(_ret_kernel, nc=tb // LIN_CHUNK, reverse=reverse),
        grid=(b, nt),
        in_specs=[col(0), col(1), col(2),
                  pl.BlockSpec((1, 1, WIDTH), lambda bb, i: (direction, 0, 0)),
                  pl.BlockSpec((2, 1, N_HEADS * LIN_CHUNK), lambda bb, i: (0, 0, 0))],
        out_specs=col(0),
        out_shape=jax.ShapeDtypeStruct((b, t, WIDTH), BF16),
        scratch_shapes=[pltpu.VMEM((N_PAIRS, LANES, LANES), F32)],
        compiler_params=_cparams("parallel", "arbitrary"),
        name=f"retention_{'bwd' if reverse else 'fwd'}",
    )(ret, ret, ret, lg64, lg128)


def _ssd_conv_kernel(x_ref, xp_ref, xn_ref, w_ref, b_ref, xs_out, bc_out, *, tm):
    i = pl.program_id(1)
    nt = pl.num_programs(1)
    prev = jnp.where(i > 0, xp_ref[0].astype(F32), 0.0)
    nxt = jnp.where(i < nt - 1, xn_ref[0].astype(F32), 0.0)
    xe = jnp.concatenate([prev, x_ref[0].astype(F32), nxt], axis=0)
    ext = tm + 2 * HALO_ROWS
    pad = SSD_CONV // 2
    acc = jnp.zeros((tm, SSD_XBC), F32) + b_ref[...]
    for j in range(SSD_CONV):
        shift = (pad - j) % ext
        xj = xe if shift == 0 else pltpu.roll(xe, shift, 0)
        acc = acc + xj[HALO_ROWS:HALO_ROWS + tm] * w_ref[j:j + 1, :]
    y = _silu(acc).astype(BF16)
    xs_out[0] = y[:, :WIDTH]
    bc_out[0] = y[:, WIDTH:]


def _ssd_conv(xz, w, bias, tm=512):
    b, t, _ = xz.shape
    nt = t // tm
    per = tm // HALO_ROWS
    const = lambda shape: pl.BlockSpec(shape, lambda bb, i: (0, 0))
    main = pl.BlockSpec((1, tm, SSD_XBC), lambda bb, i: (bb, i, 0))
    prev = pl.BlockSpec((1, HALO_ROWS, SSD_XBC), lambda bb, i: (bb, jnp.maximum(i * per - 1, 0), 0))
    nxt = pl.BlockSpec((1, HALO_ROWS, SSD_XBC),
                       lambda bb, i: (bb, jnp.minimum((i + 1) * per, t // HALO_ROWS - 1), 0))
    out = pl.BlockSpec((1, tm, WIDTH), lambda bb, i: (bb, i, 0))
    return pl.pallas_call(
        functools.partial(_ssd_conv_kernel, tm=tm),
        grid=(b, nt),
        in_specs=[main, prev, nxt, const((8, SSD_XBC)), const((1, SSD_XBC))],
        out_specs=[out, out],
        out_shape=[jax.ShapeDtypeStruct((b, t, WIDTH), BF16)] * 2,
        compiler_params=_cparams("parallel", "parallel"),
        name="ssd_conv",
    )(xz, xz, xz, w, bias)


def _ssd_kernel(xs_ref, bc_ref, dtc_ref, dtr_ref, bias_c_ref, bias_r_ref, nega_c_ref, nega_r_ref,
                tri_ref, trit_ref, y_ref, s_ref, *, nc, reverse):
    c_len = LIN_CHUNK
    per_group = N_HEADS // SSD_GROUPS
    gw = WIDTH // SSD_GROUPS

    @pl.when(pl.program_id(1) == 0)
    def _():
        s_ref[...] = jnp.zeros_like(s_ref)

    tri = tri_ref[...]
    trit = trit_ref[...]
    dts_c = _softplus(dtc_ref[0, 0] + bias_c_ref[0])
    la_c = dts_c * nega_c_ref[0]
    dts_r = _softplus(dtr_ref[0, 0] + bias_r_ref[0])
    la_r = dts_r * nega_r_ref[0]
    l_io = _iota((c_len, c_len), 0)
    s_io = _iota((c_len, c_len), 1)
    causal = (s_io >= l_io) if reverse else (s_io <= l_io)
    lo = _lane_lo((c_len, LANES))

    def by_head128(cols, g):
        return jnp.concatenate([jnp.broadcast_to(cols[:, g * per_group + h:g * per_group + h + 1],
                                                 (c_len, LANES)) for h in range(per_group)], axis=1)

    def to64(wide):
        h = [wide[:, j * LANES:(j + 1) * LANES] for j in range(per_group)]
        return jnp.concatenate([jnp.where(lo, h[0], h[1]), jnp.where(lo, h[2], h[3])], axis=1)

    order = _scan_order(nc, reverse)
    inst = [(g, c) for c in order for g in range(SSD_GROUPS)]
    n = len(inst)
    rows_of = lambda c: slice(c * c_len, (c + 1) * c_len)
    cum = [_mm_exact_left(tri, by_head128(la_c[rows_of(c)], g), terms=2) for g, c in inst]
    cum_r = {c: _mm_exact_right(la_r[:, rows_of(c)], trit, terms=2) for c in order}
    last = [u[0:1] if reverse else u[c_len - 1:c_len] for u in cum]
    bm = [bc_ref[0, rows_of(c), g * SSD_STATE:(g + 1) * SSD_STATE] for g, c in inst]
    cm = [bc_ref[0, rows_of(c), (SSD_GROUPS + g) * SSD_STATE:(SSD_GROUPS + g + 1) * SSD_STATE]
          for g, c in inst]
    xdt = [xs_ref[0, rows_of(c), g * gw:(g + 1) * gw].astype(F32) * to64(by_head128(dts_c[rows_of(c)], g))
           for g, c in inst]
    gm = [jnp.where(causal, _mm_nt(cm[i], bm[i]), 0.0) for i in range(n)]
    y_intra = []
    for i, (g, c) in enumerate(inst):
        parts = []
        for pp in range(per_group // 2):
            wide = []
            for hh in (2 * pp, 2 * pp + 1):
                head = g * per_group + hh
                diff = cum[i][:, hh * LANES:(hh + 1) * LANES] - cum_r[c][head:head + 1, :]
                wide.append((gm[i] * jnp.exp(jnp.minimum(diff, 0.0))).astype(BF16))
            parts.append(_mm(jnp.concatenate(wide, axis=1), _bd(xdt[i][:, pp * LANES:(pp + 1) * LANES])))
        y_intra.append(jnp.concatenate(parts, axis=1))
    from_start = [to64(jnp.exp(u)) for u in cum]
    upd = [_mm(bm[i].astype(F32).T, xdt[i] * to64(jnp.exp(last[i] - cum[i]))) for i in range(n)]
    whole = [to64(jnp.exp(u)) for u in last]

    s = [s_ref[g] for g in range(SSD_GROUPS)]
    for ci, c in enumerate(order):
        for g in range(SSD_GROUPS):
            i = ci * SSD_GROUPS + g
            y_ref[0, rows_of(c), g * gw:(g + 1) * gw] = (
                y_intra[i] + _mm(cm[i], s[g]) * from_start[i]).astype(BF16)
            s[g] = s[g] * whole[i] + upd[i]
    for g in range(SSD_GROUPS):
        s_ref[g] = s[g]


def _ssd_scan(xs, bc, dtc, dtr, sp, direction, tb=1024):
    b, t, _ = xs.shape
    nt = t // tb
    reverse = direction == 1
    tmap = (lambda i: nt - 1 - i) if reverse else (lambda i: i)
    row = pl.BlockSpec((1, tb, WIDTH), lambda bb, i: (bb, tmap(i), 0))
    par = lambda shape: pl.BlockSpec((1,) + shape, lambda bb, i: (direction, 0, 0))
    tri = pl.BlockSpec((LIN_CHUNK, LIN_CHUNK), lambda bb, i: (0, 0))
    return pl.pallas_call(
        functools.partial(_ssd_kernel, nc=tb // LIN_CHUNK, reverse=reverse),
        grid=(b, nt),
        in_specs=[row, row,
                  pl.BlockSpec((1, 1, tb, N_HEADS), lambda bb, i: (direction, bb, tmap(i), 0)),
                  pl.BlockSpec((1, 1, N_HEADS, tb), lambda bb, i: (direction, bb, 0, tmap(i))),
                  par((1, N_HEADS)), par((N_HEADS, 1)), par((1, N_HEADS)), par((N_HEADS, 1)),
                  tri, tri],
        out_specs=row,
        out_shape=jax.ShapeDtypeStruct((b, t, WIDTH), BF16),
        scratch_shapes=[pltpu.VMEM((SSD_GROUPS, SSD_STATE, WIDTH // SSD_GROUPS), F32)],
        compiler_params=_cparams("parallel", "arbitrary"),
        name=f"ssd_scan_{'bwd' if reverse else 'fwd'}",
    )(xs, bc, dtc, dtr, sp["bias_c"], sp["bias_r"], sp["nega_c"], sp["nega_r"],
      sp["tri_rev" if reverse else "tri_fwd"], sp["trit_rev" if reverse else "trit_fwd"])


def _odd_out_kernel(h_ref, rf, rb, gate, sf, sb, xs, z, gng, gnb, dsk, ng, ones_ref, w_ref, out_ref):
    f32 = lambda ref: ref[...].astype(F32)
    ones = ones_ref[...]
    inv = 1.0 / HEAD_DIM
    y = f32(rf) + f32(rb)
    yc = y - _group_sum(y, ones) * inv
    var = _group_sum(yc * yc, ones) * inv
    y_c = (yc * lax.rsqrt(var + GN_EPS) * gng[...] + gnb[...]) * _silu(f32(gate))
    yd = (f32(sf) + f32(sb) + dsk[...] * f32(xs)) * _silu(f32(z))
    gw = WIDTH // SSD_GROUPS
    halves = []
    for gi in range(SSD_GROUPS):
        yg = yd[:, gi * gw:(gi + 1) * gw]
        halves.append(yg * lax.rsqrt(jnp.mean(yg * yg, -1, keepdims=True) + NORM_EPS))
    y_d = jnp.concatenate(halves, axis=1) * ng[...]
    out_ref[...] = h_ref[...] + _mm(y_c, w_ref[0:WIDTH, :]) + _mm(y_d, w_ref[WIDTH:, :])


def _odd_out(h2, rf, rb, ret2, sf, sb, xs, xz2, gng, gnb, dsk, ng, ones_bd, w, tm=512):
    m = h2.shape[0]
    const = lambda shape: pl.BlockSpec(shape, lambda i: (0, 0))
    row = lambda n: pl.BlockSpec((tm, n), lambda i: (i, 0))
    vec = const((1, WIDTH))
    return pl.pallas_call(
        _odd_out_kernel,
        grid=(m // tm,),
        in_specs=[row(D_MODEL), row(WIDTH), row(WIDTH),
                  pl.BlockSpec((tm, WIDTH), lambda i: (i, 3)),
                  row(WIDTH), row(WIDTH), row(WIDTH),
                  pl.BlockSpec((tm, WIDTH), lambda i: (i, 2)),
                  vec, vec, vec, vec, const((WIDTH, WIDTH)), const((D_MODEL, D_MODEL))],
        out_specs=row(D_MODEL),
        out_shape=jax.ShapeDtypeStruct((m, D_MODEL), F32),
        compiler_params=_cparams("parallel"),
        name="odd_out",
    )(h2, rf, rb, ret2, sf, sb, xs, xz2, gng, gnb, dsk, ng, ones_bd, w)


def _tri(n, reverse):
    i = np.arange(n)
    m = (i[None, :] >= i[:, None]) if reverse else (i[None, :] <= i[:, None])
    return m.astype(np.float32)


def _prepare(norm_mix, norm_mlp, mlp_w1, mlp_w2, even_in_w, even_out_w, attn_q_gain, attn_k_gain,
             rwkv_mu_prev, rwkv_mu_next, rwkv_w0, rwkv_w2, rwkv_a0, rwkv_a2, rwkv_g2, rwkv_k_k,
             rwkv_k_a, rwkv_r_k, rwkv_ln_g, rwkv_ln_b, odd_in_w, odd_out_w, ret_decay_exp, ret_gn_g,
             ret_gn_b, ssd_conv_w, ssd_conv_b, ssd_dt_bias, ssd_a_log, ssd_d, ssd_norm_g):
    row = lambda v: v.reshape(1, -1).astype(F32)
    heads = np.arange(WIDTH) // HEAD_DIM
    ones_bd = jnp.asarray(heads[:, None] == heads[None, :], BF16)
    p = {"ones_bd": ones_bd, "norm_mix": norm_mix, "norm_mlp": norm_mlp,
         "mlp_w1": mlp_w1.astype(BF16), "mlp_w2": mlp_w2.astype(BF16)}

    w_in = even_in_w[0]
    p["even_wa"] = w_in[:, :3 * WIDTH].astype(BF16)
    p["even_wr"] = w_in[:, 3 * WIDTH:].astype(BF16)
    p["even_out_w"] = even_out_w[0].astype(BF16)
    p["q_gain"] = row(jnp.tile(attn_q_gain[0], N_HEADS))
    p["k_gain"] = row(jnp.tile(attn_k_gain[0], N_HEADS))

    def two_dir_lowrank(w):
        z = jnp.zeros_like(w[0])
        return jnp.concatenate([jnp.concatenate([w[0], z], 1), jnp.concatenate([z, w[1]], 1)], 0)

    p["rwkv"] = {
        "mu_prev": row(rwkv_mu_prev[0]), "mu_next": row(rwkv_mu_next[0]),
        "w0": row(rwkv_w0[0]), "w2": two_dir_lowrank(rwkv_w2[0]).astype(BF16),
        "a0": row(rwkv_a0[0]), "a2": two_dir_lowrank(rwkv_a2[0]).astype(BF16),
        "g2": rwkv_g2[0].astype(BF16), "k_k": row(rwkv_k_k[0]), "k_a": row(rwkv_k_a[0]),
        "r_k": row(rwkv_r_k[0]), "ones_bd": ones_bd,
    }
    p["rwkv_tri"] = [jnp.asarray(_tri(RWKV_CHUNK, False), BF16), jnp.asarray(_tri(RWKV_CHUNK, True), BF16)]
    p["rwkv_ln_g"] = row(rwkv_ln_g[0])
    p["rwkv_ln_b"] = row(rwkv_ln_b[0])

    w_odd = odd_in_w[0]
    p["odd_wr"] = w_odd[:, :4 * WIDTH].astype(BF16)
    z_w = w_odd[:, 4 * WIDTH:5 * WIDTH]
    xbc_w = w_odd[:, 5 * WIDTH:5 * WIDTH + SSD_XBC]
    dt_w = w_odd[:, 5 * WIDTH + SSD_XBC:]
    p["odd_ws"] = jnp.concatenate([xbc_w, z_w], axis=1).astype(BF16)
    p["odd_wd"] = jnp.concatenate([dt_w, jnp.zeros((D_MODEL, LANES - 2 * N_HEADS), F32)], 1).astype(BF16)
    p["odd_out_w"] = odd_out_w[0].astype(BF16)

    log_gamma = jnp.log1p(-jnp.exp2(-ret_decay_exp[0].astype(F32)))
    p["lg64"] = jnp.repeat(log_gamma, HEAD_DIM, axis=1).reshape(2, 1, WIDTH)
    p["lg128"] = jnp.repeat(log_gamma, LIN_CHUNK, axis=1).reshape(2, 1, N_HEADS * LIN_CHUNK)
    p["ret_gn_g"] = row(ret_gn_g[0])
    p["ret_gn_b"] = row(ret_gn_b[0])

    p["conv_w"] = jnp.concatenate([ssd_conv_w[0], jnp.zeros((8 - SSD_CONV, SSD_XBC), F32)], 0)
    p["conv_b"] = row(ssd_conv_b[0])
    bias = ssd_dt_bias[0].astype(F32)
    nega = -jnp.exp(ssd_a_log[0].astype(F32))
    p["ssd"] = {
        "bias_c": bias[:, None, :], "nega_c": nega[:, None, :],
        "bias_r": bias[:, :, None], "nega_r": nega[:, :, None],
        "tri_fwd": jnp.asarray(_tri(LIN_CHUNK, False), BF16),
        "tri_rev": jnp.asarray(_tri(LIN_CHUNK, True), BF16),
        "trit_fwd": jnp.asarray(_tri(LIN_CHUNK, False).T, BF16),
        "trit_rev": jnp.asarray(_tri(LIN_CHUNK, True).T, BF16),
    }
    p["ssd_d"] = row(jnp.repeat(ssd_d[0], HEAD_DIM))
    p["ssd_norm_g"] = row(ssd_norm_g[0])
    return p


def _rope_tables(t):
    half = HEAD_DIM // 2
    inv = ROPE_BASE ** (-jnp.arange(half, dtype=F32) / half)
    ang = jnp.arange(t, dtype=F32)[:, None] * inv[None, :]
    cos = jnp.tile(jnp.cos(ang), (1, LANES // half))
    sin = jnp.sin(ang)
    sin_signed = jnp.tile(jnp.concatenate([-sin, sin], axis=1), (1, LANES // HEAD_DIM))
    return cos, sin_signed


def _even_mixer(h2, b, t, p):
    h = h2.reshape(b, t, D_MODEL)
    *qkvs, rcols = _even_in(h, p["norm_mix"][0:1], p["even_wa"], p["even_wr"], p["ones_bd"],
                            p["q_gain"], p["k_gain"])
    branch = []
    for qkv, (_, dil) in zip(qkvs, DILATED_BRANCHES):
        branch.extend(_dil_attn(qkv, dil))
    r, v, kkn, g, bonus, g_in, krep, bvec = _rwkv_prep(rcols, p["rwkv"], p["rwkv_tri"])
    ys = [_rwkv_scan(r, v, kkn, g_in, krep, bvec, d) for d in range(2)]
    out = _even_out(h, branch, [ys[0], ys[1], bonus, g], p["rwkv_ln_g"], p["rwkv_ln_b"],
                    p["ones_bd"], p["even_out_w"])
    return out.reshape(b * t, D_MODEL)


def _odd_mixer(h2, b, t, p):
    cos, sin = _rope_tables(t)
    ret2, xz2, dt2 = _odd_in(h2, p["norm_mix"][1:2], p["odd_wr"], p["odd_ws"], p["odd_wd"], cos, sin)
    ret = ret2.reshape(b, t, -1)
    xz = xz2.reshape(b, t, -1)
    rets = [_retention(ret, p["lg64"], p["lg128"], d) for d in range(2)]
    xs, bc = _ssd_conv(xz, p["conv_w"], p["conv_b"])
    dt = dt2[:, :2 * N_HEADS].reshape(b, t, 2, N_HEADS)
    dtc = dt.transpose(2, 0, 1, 3)
    dtr = dt.transpose(2, 0, 3, 1)
    ssds = [_ssd_scan(xs, bc, dtc, dtr, p["ssd"], d) for d in range(2)]
    flat = lambda u: u.reshape(b * t, -1)
    return _odd_out(h2, flat(rets[0]), flat(rets[1]), ret2, flat(ssds[0]), flat(ssds[1]), flat(xs),
                    xz2, p["ret_gn_g"], p["ret_gn_b"], p["ssd_d"], p["ssd_norm_g"], p["ones_bd"],
                    p["odd_out_w"])


def _trunk(x, p):
    b, t, _ = x.shape
    h2 = x.reshape(b * t, D_MODEL)
    h2 = _even_mixer(h2, b, t, p)
    h2 = _mlp(h2, p["norm_mlp"][0:1], p["mlp_w1"][0], p["mlp_w2"][0])
    h2 = _odd_mixer(h2, b, t, p)
    h2 = _mlp(h2, p["norm_mlp"][1:2], p["mlp_w1"][1], p["mlp_w2"][1])
    return h2.reshape(b, t, D_MODEL)


def kernel(x_prompt, x_sample, norm_mix, norm_mlp, mlp_w1, mlp_w2, even_in_w, even_out_w, attn_q_gain, attn_k_gain, rwkv_mu_prev, rwkv_mu_next, rwkv_w0, rwkv_w2, rwkv_a0, rwkv_a2, rwkv_g2, rwkv_k_k, rwkv_k_a, rwkv_r_k, rwkv_ln_g, rwkv_ln_b, odd_in_w, odd_out_w, ret_decay_exp, ret_gn_g, ret_gn_b, ssd_conv_w, ssd_conv_b, ssd_dt_bias, ssd_a_log, ssd_d, ssd_norm_g):
    p = _prepare(norm_mix, norm_mlp, mlp_w1, mlp_w2, even_in_w, even_out_w, attn_q_gain, attn_k_gain,
                 rwkv_mu_prev, rwkv_mu_next, rwkv_w0, rwkv_w2, rwkv_a0, rwkv_a2, rwkv_g2, rwkv_k_k,
                 rwkv_k_a, rwkv_r_k, rwkv_ln_g, rwkv_ln_b, odd_in_w, odd_out_w, ret_decay_exp,
                 ret_gn_g, ret_gn_b, ssd_conv_w, ssd_conv_b, ssd_dt_bias, ssd_a_log, ssd_d, ssd_norm_g)
    return (_trunk(x_prompt, p), _trunk(x_sample, p))
```

```python
import functools
import math

import numpy as np
import jax
import jax.numpy as jnp
from jax import lax
from jax.experimental import pallas as pl
from jax.experimental.pallas import tpu as pltpu

F32 = jnp.float32
BF16 = jnp.bfloat16

D_MODEL = 1024
HEAD_DIM = 64
N_HEADS = 8
WIDTH = N_HEADS * HEAD_DIM
N_PAIRS = N_HEADS // 2
LANES = 128
HALO_ROWS = 16
LSE_LANES = LANES // N_HEADS
DILATED_BRANCHES = ((128, 1), (512, 4), (2048, 16))
ATT_R = 64
assert all(window // (2 * dil) == ATT_R for window, dil in DILATED_BRANCHES)
assert [dil for _, dil in DILATED_BRANCHES] == [1, 4, 16]
RWKV_DECAY_RANK = 64
RWKV_ICL_RANK = 64
RWKV_GATE_RANK = 128
RWKV_COLS = 3 * WIDTH + 2 * RWKV_DECAY_RANK + 2 * RWKV_ICL_RANK + RWKV_GATE_RANK
SSD_STATE = 128
SSD_GROUPS = 2
SSD_CONV = 5
SSD_XBC = WIDTH + 2 * SSD_GROUPS * SSD_STATE
D_FF = 4 * D_MODEL
NORM_EPS = 1e-6
GN_EPS = 1e-5
RWKV_GN_EPS = 64e-5
ROPE_BASE = 10000.0
RWKV_CHUNK = 64
LIN_CHUNK = 128
NEG_BIG = -1e30
LOG2_E = math.log2(math.e)
VMEM_LIMIT = 56 * 1024 * 1024


def _cparams(*sem):
    return pltpu.CompilerParams(dimension_semantics=sem, vmem_limit_bytes=VMEM_LIMIT)


def _iota(shape, dim):
    return lax.broadcasted_iota(jnp.int32, shape, dim)


def _lane_lo(shape):
    return (_iota(shape, len(shape) - 1) & HEAD_DIM) == 0


def _bd(x):
    lo = _lane_lo(x.shape)
    zero = jnp.zeros_like(x)
    return jnp.concatenate([jnp.where(lo, x, zero), jnp.where(lo, zero, x)], axis=0)


def _mm(a, b):
    return jnp.dot(a.astype(BF16), b.astype(BF16), preferred_element_type=F32)


def _mm_nt(a, b):
    return lax.dot_general(a.astype(BF16), b.astype(BF16), (((1,), (1,)), ((), ())),
                           preferred_element_type=F32)


def _split_bf16(x, terms):
    parts = []
    for _ in range(terms - 1):
        hi = x.astype(BF16)
        parts.append(hi)
        x = x - hi.astype(F32)
    parts.append(x.astype(BF16))
    return parts


def _mm_exact_left(t, x, terms=3):
    return sum(jnp.dot(t, piece, preferred_element_type=F32) for piece in _split_bf16(x, terms))


def _mm_exact_right(x, t, terms=3):
    return sum(jnp.dot(piece, t, preferred_element_type=F32) for piece in _split_bf16(x, terms))


def _group_sum(x, ones_bd):
    return jnp.dot(x.astype(BF16), ones_bd, preferred_element_type=F32)


def _rms_rows(x, g):
    return x * lax.rsqrt(jnp.mean(x * x, -1, keepdims=True) + NORM_EPS) * g


def _sigmoid(x):
    return 0.5 * jnp.tanh(0.5 * x) + 0.5


def _silu(x):
    return x * _sigmoid(x)


def _softplus(x):
    return jnp.maximum(x, 0.0) + jnp.log1p(jnp.exp(-jnp.abs(x)))


def _scan_order(nc, reverse):
    return list(range(nc - 1, -1, -1) if reverse else range(nc))


def _even_in_kernel(x_ref, g_ref, wa_ref, wr_ref, ones_ref, qg_ref, kg_ref, perm4_ref, perm16_ref,
                    qkv1_out, qkv4_out, qkv16_out, r_out, *, tm):
    ub = _rms_rows(x_ref[0], g_ref[...]).astype(BF16)
    a = jnp.dot(ub, wa_ref[...], preferred_element_type=F32)
    q = a[:, :WIDTH]
    k = a[:, WIDTH:2 * WIDTH]
    ones = ones_ref[...]
    inv = 1.0 / HEAD_DIM
    qn = q * lax.rsqrt(_group_sum(q * q, ones) * inv + NORM_EPS) * qg_ref[...]
    kn = k * lax.rsqrt(_group_sum(k * k, ones) * inv + NORM_EPS) * kg_ref[...]
    qkv = jnp.concatenate([qn * (HEAD_DIM ** -0.5 * LOG2_E), kn, a[:, 2 * WIDTH:]], axis=1).astype(BF16)
    qkv1_out[0, 0] = qkv
    for dil, perm_ref, out in ((4, perm4_ref, qkv4_out), (16, perm16_ref, qkv16_out)):
        perm = perm_ref[...]
        bs = perm.shape[0]
        per = bs // dil
        for blk in range(tm // bs):
            moved = jnp.dot(perm, qkv[blk * bs:(blk + 1) * bs],
                            preferred_element_type=F32).astype(BF16)
            for rho in range(dil):
                out[0, rho, blk * per:(blk + 1) * per, :] = moved[rho * per:(rho + 1) * per]
    r_out[0] = jnp.dot(ub, wr_ref[...], preferred_element_type=F32).astype(BF16)


def _even_in(x, g, wa, wr, ones_bd, qg, kg, tm=512):
    b, t, _ = x.shape
    const = lambda shape: pl.BlockSpec(shape, lambda bb, i: (0, 0))
    row = lambda n: pl.BlockSpec((1, tm, n), lambda bb, i: (bb, i, 0))
    res = lambda d: pl.BlockSpec((1, d, tm // d, 3 * WIDTH), lambda bb, i: (bb, 0, i, 0))
    dils = [d for _, d in DILATED_BRANCHES]

    def to_residue_major(d, bs):
        src = np.arange(bs).reshape(bs // d, d).T.reshape(-1)
        return jnp.asarray(src[:, None] == np.arange(bs)[None, :], BF16)

    blocks = {4: 2 * HALO_ROWS * 4, 16: HALO_ROWS * 16}

    return pl.pallas_call(
        functools.partial(_even_in_kernel, tm=tm),
        grid=(b, t // tm),
        in_specs=[row(D_MODEL), const((1, D_MODEL)), const((D_MODEL, 3 * WIDTH)),
                  const((D_MODEL, RWKV_COLS)), const((WIDTH, WIDTH)),
                  const((1, WIDTH)), const((1, WIDTH)),
                  const((blocks[4], blocks[4])), const((blocks[16], blocks[16]))],
        out_specs=[res(d) for d in dils] + [row(RWKV_COLS)],
        out_shape=[jax.ShapeDtypeStruct((b, d, t // d, 3 * WIDTH), BF16) for d in dils]
        + [jax.ShapeDtypeStruct((b, t, RWKV_COLS), BF16)],
        compiler_params=_cparams("parallel", "parallel"),
        name="even_in",
    )(x, g, wa, wr, ones_bd, qg, kg, to_residue_major(4, blocks[4]), to_residue_major(16, blocks[16]))


def _dil_attn_kernel(q_ref, k_ref, kl_ref, kr_ref, v_ref, vl_ref, vr_ref, o_ref, lse_ref,
                     kext, vext, *, tq, n_res, sub_len, dil):
    r = ATT_R
    i = pl.program_id(2)
    for z in range(n_res):
        kext[z, 0:r] = kl_ref[0, z]
        kext[z, r:r + tq] = k_ref[0, z]
        kext[z, r + tq:] = kr_ref[0, z]
        vext[z, 0:r] = vl_ref[0, z]
        vext[z, r:r + tq] = v_ref[0, z]
        vext[z, r + tq:] = vr_ref[0, z]

    c_io = _iota((r, 3 * r), 1)
    q_io = _iota((r, 3 * r), 0)
    rel = c_io - r - q_io
    near = jnp.abs(rel) <= r
    dist = jnp.abs(rel).astype(F32) * float(dil)
    lo = _lane_lo((r, LANES))
    near2 = jnp.concatenate([near, near], axis=0)
    dist2 = jnp.concatenate([dist, dist], axis=0)
    first_head = _iota((2 * r, 3 * r), 0) < r
    cols = [slice(p * LANES, (p + 1) * LANES) for p in range(N_PAIRS)]
    bias = [jnp.where(first_head, 2.0 ** (-(2 * p + 1)), 2.0 ** (-(2 * p + 2))) * (LOG2_E * dist2)
            for p in range(N_PAIRS)]

    blocks = [(z, j) for z in range(n_res) for j in range(tq // r)]
    group = 2 if len(blocks) % 2 == 0 else 1
    for g0 in range(0, len(blocks), group):
        inst = [(z, j, p) for z, j in blocks[g0:g0 + group] for p in range(N_PAIRS)]
        valid = {}
        for z, j in blocks[g0:g0 + group]:
            kpos = i * tq + j * r + _iota((2 * r, 3 * r), 1) - r
            valid[j] = near2 & (kpos >= 0) & (kpos < sub_len)
        s = [jnp.where(valid[j],
                       _mm_nt(_bd(q_ref[0, z, j * r:(j + 1) * r, cols[p]]),
                              kext[z, j * r:(j + 3) * r, cols[p]]) - bias[p], NEG_BIG)
             for z, j, p in inst]
        m = [jnp.max(u, -1, keepdims=True) for u in s]
        e = [jnp.exp2(u - mm) for u, mm in zip(s, m)]
        den = [jnp.sum(u, -1, keepdims=True) for u in e]
        pv = [_mm(e[n], vext[z, j * r:(j + 3) * r, cols[p]]) / den[n]
              for n, (z, j, p) in enumerate(inst)]
        lse = [m[n] * (1.0 / LOG2_E) + jnp.log(den[n]) for n in range(len(inst))]
        for n, (z, j, p) in enumerate(inst):
            o_ref[0, z, j * r:(j + 1) * r, cols[p]] = jnp.where(lo, pv[n][:r], pv[n][r:]).astype(BF16)
        lane = _iota((r, LANES), 1)
        for bi, (z, j) in enumerate(blocks[g0:g0 + group]):
            tile = jnp.zeros((r, LANES), F32)
            for p in range(N_PAIRS):
                col = lse[bi * N_PAIRS + p]
                for hh in range(2):
                    h0 = (2 * p + hh) * LSE_LANES
                    tile = jnp.where((lane >= h0) & (lane < h0 + LSE_LANES),
                                     col[hh * r:(hh + 1) * r], tile)
            lse_ref[0, z, j * r:(j + 1) * r, :] = tile


def _dil_attn(qkv, dil, rows_per_step=1024):
    b, _, sub_len, _ = qkv.shape
    tq = min(sub_len, rows_per_step)
    n_res = min(dil, rows_per_step // tq)
    nblk = sub_len // ATT_R
    per = tq // ATT_R
    main = lambda col: pl.BlockSpec((1, n_res, tq, WIDTH), lambda bb, rr, i: (bb, rr, i, col))
    left = lambda col: pl.BlockSpec((1, n_res, ATT_R, WIDTH),
                                    lambda bb, rr, i: (bb, rr, jnp.maximum(i * per - 1, 0), col))
    right = lambda col: pl.BlockSpec((1, n_res, ATT_R, WIDTH),
                                     lambda bb, rr, i: (bb, rr, jnp.minimum((i + 1) * per, nblk - 1), col))
    return pl.pallas_call(
        functools.partial(_dil_attn_kernel, tq=tq, n_res=n_res, sub_len=sub_len, dil=dil),
        grid=(b, dil // n_res, sub_len // tq),
        in_specs=[main(0), main(1), left(1), right(1), main(2), left(2), right(2)],
        out_specs=[main(0), pl.BlockSpec((1, n_res, tq, LANES), lambda bb, rr, i: (bb, rr, i, 0))],
        out_shape=[jax.ShapeDtypeStruct((b, dil, sub_len, WIDTH), BF16),
                   jax.ShapeDtypeStruct((b, dil, sub_len, LANES), F32)],
        scratch_shapes=[pltpu.VMEM((n_res, tq + 2 * ATT_R, WIDTH), BF16)] * 2,
        compiler_params=_cparams("parallel", "parallel", "parallel"),
        name=f"dil_attn_d{dil}",
    )(qkv, qkv, qkv, qkv, qkv, qkv, qkv)


def _rwkv_prep_kernel(x_ref, xp_ref, xn_ref, mup_ref, mun_ref, w0_ref, w2_ref, a0_ref, a2_ref,
                      g2_ref, kk_ref, ka_ref, rk_ref, ones_ref, trif_ref, trir_ref,
                      r_out, v_out, kkn_out, g_out, bonus_out, g_in_out, krep_out, b_out, *, tm):
    i = pl.program_id(1)
    nt = pl.num_programs(1)
    x = x_ref[0].astype(F32)
    prow = jnp.where(i > 0, xp_ref[0, HALO_ROWS - 1:HALO_ROWS, :].astype(F32), 0.0)
    nrow = jnp.where(i < nt - 1, xn_ref[0, 0:1, :].astype(F32), 0.0)
    rid = _iota((8, 1), 0)
    prev = pltpu.roll(x, 1, 0)
    prev = jnp.concatenate([jnp.where(rid == 0, prow, prev[:8]), prev[8:]], axis=0)
    nxt = pltpu.roll(x, tm - 1, 0)
    nxt = jnp.concatenate([nxt[:tm - 8], jnp.where(rid == 7, nrow, nxt[tm - 8:])], axis=0)
    xs = x + mup_ref[...] * (prev - x) + mun_ref[...] * (nxt - x)
    r = xs[:, 0:WIDTH]
    k = xs[:, WIDTH:2 * WIDTH]
    v = xs[:, 2 * WIDTH:3 * WIDTH]
    zw = xs[:, 3 * WIDTH:3 * WIDTH + LANES]
    za = xs[:, 3 * WIDTH + LANES:3 * WIDTH + 2 * LANES]
    zg = xs[:, 3 * WIDTH + 2 * LANES:]
    ones = ones_ref[...]
    kk = k * kk_ref[...]
    kkn = kk * lax.rsqrt(jnp.maximum(_group_sum(kk * kk, ones), 1e-24))
    g_out[0] = _mm(_sigmoid(zg), g2_ref[...]).astype(BF16)
    r_out[0] = r.astype(BF16)
    v_out[0] = v.astype(BF16)
    kkn_out[0] = kkn.astype(BF16)
    bonus_out[0] = (_group_sum(r * k * rk_ref[...], ones) * v).astype(BF16)
    wx = w0_ref[...] + _mm(jnp.tanh(zw), w2_ref[...])
    ax = a0_ref[...] + _mm(za, a2_ref[...])
    for d, tri_ref in enumerate((trif_ref, trir_ref)):
        cols = slice(d * WIDTH, (d + 1) * WIDTH)
        lw = -math.exp(-0.5) * _sigmoid(wx[:, cols])
        tri = tri_ref[...]
        for c in range(tm // RWKV_CHUNK):
            rows = slice(c * RWKV_CHUNK, (c + 1) * RWKV_CHUNK)
            g_in_out[d, 0, rows, :] = _mm_exact_left(tri, lw[rows], terms=2)
        a = _sigmoid(ax[:, cols])
        krep_out[d, 0] = (k * (1.0 + (a - 1.0) * ka_ref[...])).astype(BF16)
        b_out[d, 0] = (kkn * a).astype(BF16)


def _rwkv_prep(rcols, p, tri, tm=512):
    b, t, _ = rcols.shape
    nt = t // tm
    per = tm // HALO_ROWS
    const = lambda shape: pl.BlockSpec(shape, lambda bb, i: (0,) * len(shape))
    main = lambda n: pl.BlockSpec((1, tm, n), lambda bb, i: (bb, i, 0))
    dirs = pl.BlockSpec((2, 1, tm, WIDTH), lambda bb, i: (0, bb, i, 0))
    prev = pl.BlockSpec((1, HALO_ROWS, RWKV_COLS), lambda bb, i: (bb, jnp.maximum(i * per - 1, 0), 0))
    nxt = pl.BlockSpec((1, HALO_ROWS, RWKV_COLS),
                       lambda bb, i: (bb, jnp.minimum((i + 1) * per, t // HALO_ROWS - 1), 0))
    one = jax.ShapeDtypeStruct((b, t, WIDTH), BF16)
    two = lambda dt: jax.ShapeDtypeStruct((2, b, t, WIDTH), dt)
    return pl.pallas_call(
        functools.partial(_rwkv_prep_kernel, tm=tm),
        grid=(b, nt),
        in_specs=[main(RWKV_COLS), prev, nxt, const((1, RWKV_COLS)), const((1, RWKV_COLS)),
                  const((1, 2 * WIDTH)), const((LANES, 2 * WIDTH)),
                  const((1, 2 * WIDTH)), const((LANES, 2 * WIDTH)),
                  const((LANES, WIDTH)), const((1, WIDTH)), const((1, WIDTH)), const((1, WIDTH)),
                  const((WIDTH, WIDTH)), const((RWKV_CHUNK, RWKV_CHUNK)),
                  const((RWKV_CHUNK, RWKV_CHUNK))],
        out_specs=[main(WIDTH)] * 5 + [dirs] * 3,
        out_shape=[one] * 5 + [two(F32), two(BF16), two(BF16)],
        compiler_params=_cparams("parallel", "parallel"),
        name="rwkv_prep",
    )(rcols, rcols, rcols, p["mu_prev"], p["mu_next"], p["w0"], p["w2"], p["a0"], p["a2"],
      p["g2"], p["k_k"], p["k_a"], p["r_k"], p["ones_bd"], tri[0], tri[1])


def _rwkv_scan_kernel(r_ref, v_ref, kk_ref, g_ref, k_ref, b_ref, y_ref, s_ref, *, nc, reverse):
    c_len = RWKV_CHUNK

    @pl.when(pl.program_id(1) == 0)
    def _():
        s_ref[...] = jnp.zeros_like(s_ref)

    row = _iota((c_len, LANES), 0)
    li = _iota((c_len, LANES), 1) & (HEAD_DIM - 1)
    strict = (li > row) if reverse else (li < row)
    incl = (li >= row) if reverse else (li <= row)
    r2 = _iota((LANES, LANES), 0)
    c2 = _iota((LANES, LANES), 1)
    same_head = (r2 & HEAD_DIM) == (c2 & HEAD_DIM)
    diag = r2 == c2
    zeros = jnp.zeros((c_len, LANES), F32)

    order = _scan_order(nc, reverse)
    inst = [(p, c) for c in order for p in range(N_PAIRS)]
    n = len(inst)

    def tile(ref, p, c, lead=()):
        return ref[lead + (0, slice(c * c_len, (c + 1) * c_len), slice(p * LANES, (p + 1) * LANES))]

    g_in = [tile(g_ref, p, c, (0,)) for p, c in inst]
    g_tot = [g[0:1] if reverse else g[c_len - 1:c_len] for g in g_in]
    scan_first = row == (c_len - 1 if reverse else 0)
    g_ex = [jnp.where(scan_first, 0.0, pltpu.roll(g, c_len - 1 if reverse else 1, 0)) for g in g_in]
    v = [tile(v_ref, p, c) for p, c in inst]
    at, rt, w, bk_end = [], [], [], []
    for i, (p, c) in enumerate(inst):
        b = tile(b_ref, p, c, (0,)).astype(F32)
        k = tile(k_ref, p, c, (0,)).astype(F32)
        at.append(-tile(kk_ref, p, c).astype(F32) * jnp.exp(g_ex[i]))
        rt.append(tile(r_ref, p, c).astype(F32) * jnp.exp(g_in[i]))
        inv = jnp.exp(-g_in[i])
        to_end = jnp.exp(g_tot[i] - g_in[i])
        bk_end.append(jnp.concatenate([b * to_end, k * to_end], 0))
        w.append(_mm_nt(jnp.concatenate([at[i], rt[i]], 0),
                        jnp.concatenate([_bd(b * inv), _bd(k * inv)], 0)))

    bdv = [_bd(u) for u in v]
    m_ab = [jnp.where(strict, u[:c_len, :LANES], 0.0) for u in w]
    kv = [_mm(jnp.concatenate([jnp.where(strict, w[i][:c_len, LANES:], 0.0),
                               jnp.where(incl, w[i][c_len:, LANES:], 0.0)], 0), bdv[i])
          for i in range(n)]
    rhs = [jnp.concatenate([at[i], kv[i][:c_len]], axis=1) for i in range(n)]
    tinv = [jnp.where(li == row, 1.0, 0.0) + u for u in m_ab]
    mj = [u.astype(BF16) for u in m_ab]
    mj = [_mm(u, _bd(u)).astype(BF16) for u in mj]
    for j in range(1, 5):
        both = [_mm(jnp.concatenate([mj[i], tinv[i].astype(BF16)], 0), _bd(mj[i])) for i in range(n)]
        mj = [u[:c_len].astype(BF16) for u in both]
        tinv = [tinv[i] + both[i][c_len:] for i in range(n)]
    tinv = [tinv[i] + _mm(tinv[i], _bd(mj[i])) for i in range(n)]
    x = [_mm(tinv[i], _bd(rhs[i])) for i in range(n)]

    t1 = [_mm(jnp.where(incl, w[i][c_len:, :LANES], 0.0), _bd(x[i])) for i in range(n)]
    y0 = [t1[i][:, LANES:] + kv[i][c_len:] for i in range(n)]
    rp = [(rt[i] + t1[i][:, :LANES]).astype(BF16) for i in range(n)]
    t2 = [_mm(bk_end[i].T,
              jnp.concatenate([x[i], jnp.concatenate([zeros, v[i].astype(F32)], axis=1)], 0))
          for i in range(n)]
    g_low = [jnp.where(same_head, u[:, :LANES], 0.0).astype(BF16) for u in t2]
    h_new = [jnp.where(same_head, u[:, LANES:], 0.0) for u in t2]
    g_col = [jnp.exp(jnp.sum(jnp.where(diag, jnp.broadcast_to(g, (LANES, LANES)), 0.0),
                             axis=1, keepdims=True)) for g in g_tot]

    s = [s_ref[p] for p in range(N_PAIRS)]
    for ci, c in enumerate(order):
        ids = [ci * N_PAIRS + p for p in range(N_PAIRS)]
        both = [_mm(jnp.concatenate([rp[i], g_low[i]], 0), s[p]) for p, i in enumerate(ids)]
        for p, i in enumerate(ids):
            y_ref[0, c * c_len:(c + 1) * c_len, p * LANES:(p + 1) * LANES] = (
                both[p][:c_len] + y0[i]).astype(BF16)
        s = [s[p] * g_col[i] + both[p][c_len:] + h_new[i] for p, i in enumerate(ids)]
    for p in range(N_PAIRS):
        s_ref[p] = s[p]


def _rwkv_scan(r, v, kkn, g_in, krep, bvec, direction, tb=1024):
    b, t, _ = r.shape
    nt = t // tb
    reverse = direction == 1
    tmap = (lambda i: nt - 1 - i) if reverse else (lambda i: i)
    one = pl.BlockSpec((1, tb, WIDTH), lambda bb, i: (bb, tmap(i), 0))
    two = pl.BlockSpec((1, 1, tb, WIDTH), lambda bb, i: (direction, bb, tmap(i), 0))
    return pl.pallas_call(
        functools.partial(_rwkv_scan_kernel, nc=tb // RWKV_CHUNK, reverse=reverse),
        grid=(b, nt),
        in_specs=[one, one, one, two, two, two],
        out_specs=one,
        out_shape=jax.ShapeDtypeStruct((b, t, WIDTH), BF16),
        scratch_shapes=[pltpu.VMEM((N_PAIRS, LANES, LANES), F32)],
        compiler_params=_cparams("parallel", "arbitrary"),
        name=f"rwkv_scan_{'bwd' if reverse else 'fwd'}",
    )(r, v, kkn, g_in, krep, bvec)


def _even_out_kernel(h_ref, o1, l1, o4, l4, o16, l16, yf, yb, bonus, g, lng, lnb, ones_ref, spread_ref,
                     w_ref, out_ref, tok_sc, lse_sc, *, tm):
    n_tiles = WIDTH // LANES
    for slot, (dil, o_ref, l_ref) in enumerate(((4, o4, l4), (16, o16, l16))):
        for rho in range(dil):
            rows = pl.ds(rho, tm // dil, stride=dil)
            lse_sc[slot, rows, :] = l_ref[0, rho]
            for c in range(n_tiles):
                tok_sc[slot * n_tiles + c, rows, :] = o_ref[0, rho, :, c * LANES:(c + 1) * LANES].astype(F32)
    tok = lambda slot: jnp.concatenate([tok_sc[slot * n_tiles + c] for c in range(n_tiles)], axis=1)
    la, lb, lc = l1[0, 0], lse_sc[0], lse_sc[1]
    mx = jnp.maximum(jnp.maximum(la, lb), lc)
    wa, wb, wc = jnp.exp(la - mx), jnp.exp(lb - mx), jnp.exp(lc - mx)
    norm = 1.0 / (wa + wb + wc)
    spread = lambda wgt: jnp.dot((wgt * norm).astype(BF16), spread_ref[...], preferred_element_type=F32)
    y_a = spread(wa) * o1[0, 0].astype(F32) + spread(wb) * tok(0) + spread(wc) * tok(1)
    ones = ones_ref[...]
    inv = 1.0 / HEAD_DIM
    y = yf[0].astype(F32) + yb[0].astype(F32)
    yc = y - _group_sum(y, ones) * inv
    var = _group_sum(yc * yc, ones) * inv
    yn = yc * lax.rsqrt(var + RWKV_GN_EPS) * lng[...] + lnb[...]
    y_b = (yn + bonus[0].astype(F32)) * g[0].astype(F32)
    out_ref[0] = (h_ref[0] + _mm(y_a, w_ref[0:WIDTH, :]) + _mm(y_b, w_ref[WIDTH:, :]))


def _even_out(h, branch, rwkv_parts, lng, lnb, ones_bd, w, tm=512):
    b, t, _ = h.shape
    const = lambda shape: pl.BlockSpec(shape, lambda bb, i: (0, 0))
    row = lambda n: pl.BlockSpec((1, tm, n), lambda bb, i: (bb, i, 0))
    res = lambda d, n: pl.BlockSpec((1, d, tm // d, n), lambda bb, i: (bb, 0, i, 0))
    dils = [d for _, d in DILATED_BRANCHES]
    spread = jnp.asarray(np.arange(LANES)[:, None] == (np.arange(WIDTH)[None, :] // HEAD_DIM) * LSE_LANES, BF16)
    return pl.pallas_call(
        functools.partial(_even_out_kernel, tm=tm),
        grid=(b, t // tm),
        in_specs=[row(D_MODEL)] + [res(d, n) for d in dils for n in (WIDTH, LANES)] + [row(WIDTH)] * 4
        + [const((1, WIDTH)), const((1, WIDTH)), const((WIDTH, WIDTH)), const((LANES, WIDTH)),
           const((D_MODEL, D_MODEL))],
        out_specs=row(D_MODEL),
        out_shape=jax.ShapeDtypeStruct((b, t, D_MODEL), F32),
        scratch_shapes=[pltpu.VMEM((2 * WIDTH // LANES, tm, LANES), F32), pltpu.VMEM((2, tm, LANES), F32)],
        compiler_params=_cparams("parallel", "parallel"),
        name="even_out",
    )(h, *branch, *rwkv_parts, lng, lnb, ones_bd, spread, w)


def _mlp_kernel(h_ref, g_ref, w1_ref, w2_ref, out_ref, u_ref):
    f = pl.program_id(1)

    @pl.when(f == 0)
    def _():
        x = h_ref[...]
        u_ref[...] = _rms_rows(x, g_ref[...]).astype(BF16)
        out_ref[...] = x

    hdn = jnp.dot(u_ref[...], w1_ref[...], preferred_element_type=F32)
    hdn = jnp.square(jnp.maximum(hdn, 0.0))
    out_ref[...] += jnp.dot(hdn.astype(BF16), w2_ref[...], preferred_element_type=F32)


def _mlp(h2, g, w1, w2, tm=1024, tf=2048):
    m = h2.shape[0]
    return pl.pallas_call(
        _mlp_kernel,
        grid=(m // tm, D_FF // tf),
        in_specs=[pl.BlockSpec((tm, D_MODEL), lambda i, f: (i, 0)),
                  pl.BlockSpec((1, D_MODEL), lambda i, f: (0, 0)),
                  pl.BlockSpec((D_MODEL, tf), lambda i, f: (0, f)),
                  pl.BlockSpec((tf, D_MODEL), lambda i, f: (f, 0))],
        out_specs=pl.BlockSpec((tm, D_MODEL), lambda i, f: (i, 0)),
        out_shape=jax.ShapeDtypeStruct((m, D_MODEL), F32),
        scratch_shapes=[pltpu.VMEM((tm, D_MODEL), BF16)],
        compiler_params=_cparams("parallel", "arbitrary"),
        name="mlp",
    )(h2, g, w1, w2)


def _rotary_pair(u, cos, sin_signed):
    first = (_iota(u.shape, 1) & (HEAD_DIM - 1)) < HEAD_DIM // 2
    swapped = jnp.where(first, pltpu.roll(u, LANES - HEAD_DIM // 2, 1),
                        pltpu.roll(u, HEAD_DIM // 2, 1))
    return u * cos + swapped * sin_signed


def _odd_in_kernel(x_ref, g_ref, wr_ref, ws_ref, wd_ref, cos_ref, sin_ref, ret_out, xz_out, dt_out):
    ub = _rms_rows(x_ref[...], g_ref[...]).astype(BF16)
    ret = jnp.dot(ub, wr_ref[...], preferred_element_type=F32)
    cos = cos_ref[...]
    sin = sin_ref[...]
    for c in range(2 * N_PAIRS):
        cols = slice(c * LANES, (c + 1) * LANES)
        scale = 1.0 if c < N_PAIRS else HEAD_DIM ** -0.5
        ret_out[:, cols] = (_rotary_pair(ret[:, cols], cos, sin) * scale).astype(BF16)
    ret_out[:, 2 * WIDTH:] = ret[:, 2 * WIDTH:].astype(BF16)
    xz_out[...] = jnp.dot(ub, ws_ref[...], preferred_element_type=F32).astype(BF16)
    dt_out[...] = jnp.dot(ub, wd_ref[...], preferred_element_type=F32)


def _odd_in(x2, g, wr, ws, wd, cos, sin, tm=512):
    m = x2.shape[0]
    per_seq = cos.shape[0] // tm
    const = lambda shape: pl.BlockSpec(shape, lambda i: (0, 0))
    row = lambda n: pl.BlockSpec((tm, n), lambda i: (i, 0))
    tab = pl.BlockSpec((tm, LANES), lambda i: (i % per_seq, 0))
    n_xz = SSD_XBC + WIDTH
    return pl.pallas_call(
        _odd_in_kernel,
        grid=(m // tm,),
        in_specs=[row(D_MODEL), const((1, D_MODEL)), const((D_MODEL, 4 * WIDTH)),
                  const((D_MODEL, n_xz)), const((D_MODEL, LANES)), tab, tab],
        out_specs=[row(4 * WIDTH), row(n_xz), row(LANES)],
        out_shape=[jax.ShapeDtypeStruct((m, 4 * WIDTH), BF16),
                   jax.ShapeDtypeStruct((m, n_xz), BF16),
                   jax.ShapeDtypeStruct((m, LANES), F32)],
        compiler_params=_cparams("parallel"),
        name="odd_in",
    )(x2, g, wr, ws, wd, cos, sin)


def _ret_kernel(q_ref, k_ref, v_ref, lg64_ref, lg128_ref, y_ref, s_ref,
                *, nc, reverse):
    c_len = LIN_CHUNK

    @pl.when(pl.program_id(1) == 0)
    def _():
        s_ref[...] = jnp.zeros_like(s_ref)

    l_io = _iota((c_len, 2 * c_len), 0)
    s_io = _iota((c_len, 2 * c_len), 1) & (c_len - 1)
    ahead = jnp.maximum(l_io - s_io, 0).astype(F32)
    behind = jnp.maximum(s_io - l_io, 0).astype(F32)
    pos = _iota((c_len, LANES), 0)
    if reverse:
        pos = c_len - 1 - pos
    pos = pos.astype(F32)
    r2 = _iota((LANES, LANES), 0)
    c2 = _iota((LANES, LANES), 1)
    same_head = (r2 & HEAD_DIM) == (c2 & HEAD_DIM)

    decay, from_start, to_end, whole = [], [], [], []
    for p in range(N_PAIRS):
        lg64 = lg64_ref[0, :, p * LANES:(p + 1) * LANES]
        if not reverse:
            wide = slice(2 * p * c_len, 2 * (p + 1) * c_len)
            decay.append(jnp.where(l_io >= s_io, jnp.exp(lg128_ref[0, :, wide] * ahead), 0.0)
                         + jnp.where(s_io >= l_io, jnp.exp(lg128_ref[1, :, wide] * behind), 0.0))
        from_start.append(jnp.exp(lg64 * (pos + 1.0)))
        to_end.append(jnp.exp(lg64 * (float(c_len - 1) - pos)))
        whole.append(jnp.exp(lg64 * float(c_len)))

    order = _scan_order(nc, reverse)
    inst = [(p, c) for c in order for p in range(N_PAIRS)]
    q, k, v = [], [], []
    for p, c in inst:
        rows = slice(c * c_len, (c + 1) * c_len)
        cols = slice(p * LANES, (p + 1) * LANES)
        q.append(q_ref[0, rows, cols])
        k.append(k_ref[0, rows, cols])
        v.append(v_ref[0, rows, cols])
    if not reverse:
        sc = [(_mm_nt(q[i], _bd(k[i])) * decay[p]).astype(BF16) for i, (p, c) in enumerate(inst)]
        y_intra = [_mm(sc[i], _bd(v[i])) for i in range(len(inst))]
    upd = [jnp.where(same_head, _mm((k[i] * to_end[p]).T, v[i]), 0.0) for i, (p, c) in enumerate(inst)]
    qs = [(q[i] * from_start[p]).astype(BF16) for i, (p, c) in enumerate(inst)]

    s = [s_ref[p] for p in range(N_PAIRS)]
    for ci, c in enumerate(order):
        for p in range(N_PAIRS):
            i = ci * N_PAIRS + p
            y = _mm(qs[i], s[p])
            if not reverse:
                y = y + y_intra[i]
            y_ref[0, c * c_len:(c + 1) * c_len, p * LANES:(p + 1) * LANES] = y.astype(BF16)
            s[p] = s[p] * whole[p] + upd[i]
    for p in range(N_PAIRS):
        s_ref[p] = s[p]


def _retention(ret, lg64, lg128, direction, tb=1024):
    b, t, _ = ret.shape
    nt = t // tb
    reverse = direction == 1
    tmap = (lambda i: nt - 1 - i) if reverse else (lambda i: i)
    col = lambda blk: pl.BlockSpec((1, tb, WIDTH), lambda bb, i: (bb, tmap(i), blk))
    return pl.pallas_call(
        functools.partial(_ret_kernel, nc=tb // LIN_CHUNK, reverse=reverse),
        grid=(b, nt),
        in_specs=[col(0), col(1), col(2),
                  pl.BlockSpec((1, 1, WIDTH), lambda bb, i: (direction, 0, 0)),
                  pl.BlockSpec((2, 1, N_HEADS * LIN_CHUNK), lambda bb, i: (0, 0, 0))],
        out_specs=col(0),
        out_shape=jax.ShapeDtypeStruct((b, t, WIDTH), BF16),
        scratch_shapes=[pltpu.VMEM((N_PAIRS, LANES, LANES), F32)],
        compiler_params=_cparams("parallel", "arbitrary"),
        name=f"retention_{'bwd' if reverse else 'fwd'}",
    )(ret, ret, ret, lg64, lg128)


def _ssd_conv_kernel(x_ref, xp_ref, xn_ref, w_ref, b_ref, xs_out, bc_out, *, tm):
    i = pl.program_id(1)
    nt = pl.num_programs(1)
    prev = jnp.where(i > 0, xp_ref[0].astype(F32), 0.0)
    nxt = jnp.where(i < nt - 1, xn_ref[0].astype(F32), 0.0)
    xe = jnp.concatenate([prev, x_ref[0].astype(F32), nxt], axis=0)
    ext = tm + 2 * HALO_ROWS
    pad = SSD_CONV // 2
    acc = jnp.zeros((tm, SSD_XBC), F32) + b_ref[...]
    for j in range(SSD_CONV):
        shift = (pad - j) % ext
        xj = xe if shift == 0 else pltpu.roll(xe, shift, 0)
        acc = acc + xj[HALO_ROWS:HALO_ROWS + tm] * w_ref[j:j + 1, :]
    y = _silu(acc).astype(BF16)
    xs_out[0] = y[:, :WIDTH]
    bc_out[0] = y[:, WIDTH:]


def _ssd_conv(xz, w, bias, tm=512):
    b, t, _ = xz.shape
    nt = t // tm
    per = tm // HALO_ROWS
    const = lambda shape: pl.BlockSpec(shape, lambda bb, i: (0, 0))
    main = pl.BlockSpec((1, tm, SSD_XBC), lambda bb, i: (bb, i, 0))
    prev = pl.BlockSpec((1, HALO_ROWS, SSD_XBC), lambda bb, i: (bb, jnp.maximum(i * per - 1, 0), 0))
    nxt = pl.BlockSpec((1, HALO_ROWS, SSD_XBC),
                       lambda bb, i: (bb, jnp.minimum((i + 1) * per, t // HALO_ROWS - 1), 0))
    out = pl.BlockSpec((1, tm, WIDTH), lambda bb, i: (bb, i, 0))
    return pl.pallas_call(
        functools.partial(_ssd_conv_kernel, tm=tm),
        grid=(b, nt),
        in_specs=[main, prev, nxt, const((8, SSD_XBC)), const((1, SSD_XBC))],
        out_specs=[out, out],
        out_shape=[jax.ShapeDtypeStruct((b, t, WIDTH), BF16)] * 2,
        compiler_params=_cparams("parallel", "parallel"),
        name="ssd_conv",
    )(xz, xz, xz, w, bias)


def _ssd_kernel(xs_ref, bc_ref, dtc_ref, dtr_ref, bias_c_ref, bias_r_ref, nega_c_ref, nega_r_ref,
                tri_ref, trit_ref, y_ref, s_ref, *, nc, reverse):
    c_len = LIN_CHUNK
    per_group = N_HEADS // SSD_GROUPS
    gw = WIDTH // SSD_GROUPS

    @pl.when(pl.program_id(1) == 0)
    def _():
        s_ref[...] = jnp.zeros_like(s_ref)

    tri = tri_ref[...]
    trit = trit_ref[...]
    dts_c = _softplus(dtc_ref[0, 0] + bias_c_ref[0])
    la_c = dts_c * nega_c_ref[0]
    dts_r = _softplus(dtr_ref[0, 0] + bias_r_ref[0])
    la_r = dts_r * nega_r_ref[0]
    l_io = _iota((c_len, c_len), 0)
    s_io = _iota((c_len, c_len), 1)
    causal = (s_io >= l_io) if reverse else (s_io <= l_io)
    lo = _lane_lo((c_len, LANES))

    def by_head128(cols, g):
        return jnp.concatenate([jnp.broadcast_to(cols[:, g * per_group + h:g * per_group + h + 1],
                                                 (c_len, LANES)) for h in range(per_group)], axis=1)

    def to64(wide):
        h = [wide[:, j * LANES:(j + 1) * LANES] for j in range(per_group)]
        return jnp.concatenate([jnp.where(lo, h[0], h[1]), jnp.where(lo, h[2], h[3])], axis=1)

    order = _scan_order(nc, reverse)
    inst = [(g, c) for c in order for g in range(SSD_GROUPS)]
    n = len(inst)
    rows_of = lambda c: slice(c * c_len, (c + 1) * c_len)
    cum = [_mm_exact_left(tri, by_head128(la_c[rows_of(c)], g), terms=2) for g, c in inst]
    cum_r = {c: _mm_exact_right(la_r[:, rows_of(c)], trit, terms=2) for c in order}
    last = [u[0:1] if reverse else u[c_len - 1:c_len] for u in cum]
    bm = [bc_ref[0, rows_of(c), g * SSD_STATE:(g + 1) * SSD_STATE] for g, c in inst]
    cm = [bc_ref[0, rows_of(c), (SSD_GROUPS + g) * SSD_STATE:(SSD_GROUPS + g + 1) * SSD_STATE]
          for g, c in inst]
    xdt = [xs_ref[0, rows_of(c), g * gw:(g + 1) * gw].astype(F32) * to64(by_head128(dts_c[rows_of(c)], g))
           for g, c in inst]
    gm = [jnp.where(causal, _mm_nt(cm[i], bm[i]), 0.0) for i in range(n)]
    y_intra = []
    for i, (g, c) in enumerate(inst):
        parts = []
        for pp in range(per_group // 2):
            wide = []
            for hh in (2 * pp, 2 * pp + 1):
                head = g * per_group + hh
                diff = cum[i][:, hh * LANES:(hh + 1) * LANES] - cum_r[c][head:head + 1, :]
                wide.append((gm[i] * jnp.exp(jnp.minimum(diff, 0.0))).astype(BF16))
            parts.append(_mm(jnp.concatenate(wide, axis=1), _bd(xdt[i][:, pp * LANES:(pp + 1) * LANES])))
        y_intra.append(jnp.concatenate(parts, axis=1))
    from_start = [to64(jnp.exp(u)) for u in cum]
    upd = [_mm(bm[i].astype(F32).T, xdt[i] * to64(jnp.exp(last[i] - cum[i]))) for i in range(n)]
    whole = [to64(jnp.exp(u)) for u in last]

    s = [s_ref[g] for g in range(SSD_GROUPS)]
    for ci, c in enumerate(order):
        for g in range(SSD_GROUPS):
            i = ci * SSD_GROUPS + g
            y_ref[0, rows_of(c), g * gw:(g + 1) * gw] = (
                y_intra[i] + _mm(cm[i], s[g]) * from_start[i]).astype(BF16)
            s[g] = s[g] * whole[i] + upd[i]
    for g in range(SSD_GROUPS):
        s_ref[g] = s[g]


def _ssd_scan(xs, bc, dtc, dtr, sp, direction, tb=1024):
    b, t, _ = xs.shape
    nt = t // tb
    reverse = direction == 1
    tmap = (lambda i: nt - 1 - i) if reverse else (lambda i: i)
    row = pl.BlockSpec((1, tb, WIDTH), lambda bb, i: (bb, tmap(i), 0))
    par = lambda shape: pl.BlockSpec((1,) + shape, lambda bb, i: (direction, 0, 0))
    tri = pl.BlockSpec((LIN_CHUNK, LIN_CHUNK), lambda bb, i: (0, 0))
    return pl.pallas_call(
        functools.partial(_ssd_kernel, nc=tb // LIN_CHUNK, reverse=reverse),
        grid=(b, nt),
        in_specs=[row, row,
                  pl.BlockSpec((1, 1, tb, N_HEADS), lambda bb, i: (direction, bb, tmap(i), 0)),
                  pl.BlockSpec((1, 1, N_HEADS, tb), lambda bb, i: (direction, bb, 0, tmap(i))),
                  par((1, N_HEADS)), par((N_HEADS, 1)), par((1, N_HEADS)), par((N_HEADS, 1)),
                  tri, tri],
        out_specs=row,
        out_shape=jax.ShapeDtypeStruct((b, t, WIDTH), BF16),
        scratch_shapes=[pltpu.VMEM((SSD_GROUPS, SSD_STATE, WIDTH // SSD_GROUPS), F32)],
        compiler_params=_cparams("parallel", "arbitrary"),
        name=f"ssd_scan_{'bwd' if reverse else 'fwd'}",
    )(xs, bc, dtc, dtr, sp["bias_c"], sp["bias_r"], sp["nega_c"], sp["nega_r"],
      sp["tri_rev" if reverse else "tri_fwd"], sp["trit_rev" if reverse else "trit_fwd"])


def _odd_out_kernel(h_ref, rf, rb, gate, sf, sb, xs, z, gng, gnb, dsk, ng, ones_ref, w_ref, out_ref):
    f32 = lambda ref: ref[...].astype(F32)
    ones = ones_ref[...]
    inv = 1.0 / HEAD_DIM
    y = f32(rf) + f32(rb)
    yc = y - _group_sum(y, ones) * inv
    var = _group_sum(yc * yc, ones) * inv
    y_c = (yc * lax.rsqrt(var + GN_EPS) * gng[...] + gnb[...]) * _silu(f32(gate))
    yd = (f32(sf) + f32(sb) + dsk[...] * f32(xs)) * _silu(f32(z))
    gw = WIDTH // SSD_GROUPS
    halves = []
    for gi in range(SSD_GROUPS):
        yg = yd[:, gi * gw:(gi + 1) * gw]
        halves.append(yg * lax.rsqrt(jnp.mean(yg * yg, -1, keepdims=True) + NORM_EPS))
    y_d = jnp.concatenate(halves, axis=1) * ng[...]
    out_ref[...] = h_ref[...] + _mm(y_c, w_ref[0:WIDTH, :]) + _mm(y_d, w_ref[WIDTH:, :])


def _odd_out(h2, rf, rb, ret2, sf, sb, xs, xz2, gng, gnb, dsk, ng, ones_bd, w, tm=512):
    m = h2.shape[0]
    const = lambda shape: pl.BlockSpec(shape, lambda i: (0, 0))
    row = lambda n: pl.BlockSpec((tm, n), lambda i: (i, 0))
    vec = const((1, WIDTH))
    return pl.pallas_call(
        _odd_out_kernel,
        grid=(m // tm,),
        in_specs=[row(D_MODEL), row(WIDTH), row(WIDTH),
                  pl.BlockSpec((tm, WIDTH), lambda i: (i, 3)),
                  row(WIDTH), row(WIDTH), row(WIDTH),
                  pl.BlockSpec((tm, WIDTH), lambda i: (i, 2)),
                  vec, vec, vec, vec, const((WIDTH, WIDTH)), const((D_MODEL, D_MODEL))],
        out_specs=row(D_MODEL),
        out_shape=jax.ShapeDtypeStruct((m, D_MODEL), F32),
        compiler_params=_cparams("parallel"),
        name="odd_out",
    )(h2, rf, rb, ret2, sf, sb, xs, xz2, gng, gnb, dsk, ng, ones_bd, w)


def _tri(n, reverse):
    i = np.arange(n)
    m = (i[None, :] >= i[:, None]) if reverse else (i[None, :] <= i[:, None])
    return m.astype(np.float32)


def _prepare(norm_mix, norm_mlp, mlp_w1, mlp_w2, even_in_w, even_out_w, attn_q_gain, attn_k_gain,
             rwkv_mu_prev, rwkv_mu_next, rwkv_w0, rwkv_w2, rwkv_a0, rwkv_a2, rwkv_g2, rwkv_k_k,
             rwkv_k_a, rwkv_r_k, rwkv_ln_g, rwkv_ln_b, odd_in_w, odd_out_w, ret_decay_exp, ret_gn_g,
             ret_gn_b, ssd_conv_w, ssd_conv_b, ssd_dt_bias, ssd_a_log, ssd_d, ssd_norm_g):
    row = lambda v: v.reshape(1, -1).astype(F32)
    heads = np.arange(WIDTH) // HEAD_DIM
    ones_bd = jnp.asarray(heads[:, None] == heads[None, :], BF16)
    p = {"ones_bd": ones_bd, "norm_mix": norm_mix, "norm_mlp": norm_mlp,
         "mlp_w1": mlp_w1.astype(BF16), "mlp_w2": mlp_w2.astype(BF16)}

    w_in = even_in_w[0]
    p["even_wa"] = w_in[:, :3 * WIDTH].astype(BF16)
    p["even_wr"] = w_in[:, 3 * WIDTH:].astype(BF16)
    p["even_out_w"] = even_out_w[0].astype(BF16)
    p["q_gain"] = row(jnp.tile(attn_q_gain[0], N_HEADS))
    p["k_gain"] = row(jnp.tile(attn_k_gain[0], N_HEADS))

    def two_dir_lowrank(w):
        z = jnp.zeros_like(w[0])
        return jnp.concatenate([jnp.concatenate([w[0], z], 1), jnp.concatenate([z, w[1]], 1)], 0)

    p["rwkv"] = {
        "mu_prev": row(rwkv_mu_prev[0]), "mu_next": row(rwkv_mu_next[0]),
        "w0": row(rwkv_w0[0]), "w2": two_dir_lowrank(rwkv_w2[0]).astype(BF16),
        "a0": row(rwkv_a0[0]), "a2": two_dir_lowrank(rwkv_a2[0]).astype(BF16),
        "g2": rwkv_g2[0].astype(BF16), "k_k": row(rwkv_k_k[0]), "k_a": row(rwkv_k_a[0]),
        "r_k": row(rwkv_r_k[0]), "ones_bd": ones_bd,
    }
    p["rwkv_tri"] = [jnp.asarray(_tri(RWKV_CHUNK, False), BF16), jnp.asarray(_tri(RWKV_CHUNK, True), BF16)]
    p["rwkv_ln_g"] = row(rwkv_ln_g[0])
    p["rwkv_ln_b"] = row(rwkv_ln_b[0])

    w_odd = odd_in_w[0]
    p["odd_wr"] = w_odd[:, :4 * WIDTH].astype(BF16)
    z_w = w_odd[:, 4 * WIDTH:5 * WIDTH]
    xbc_w = w_odd[:, 5 * WIDTH:5 * WIDTH + SSD_XBC]
    dt_w = w_odd[:, 5 * WIDTH + SSD_XBC:]
    p["odd_ws"] = jnp.concatenate([xbc_w, z_w], axis=1).astype(BF16)
    p["odd_wd"] = jnp.concatenate([dt_w, jnp.zeros((D_MODEL, LANES - 2 * N_HEADS), F32)], 1).astype(BF16)
    p["odd_out_w"] = odd_out_w[0].astype(BF16)

    log_gamma = jnp.log1p(-jnp.exp2(-ret_decay_exp[0].astype(F32)))
    p["lg64"] = jnp.repeat(log_gamma, HEAD_DIM, axis=1).reshape(2, 1, WIDTH)
    p["lg128"] = jnp.repeat(log_gamma, LIN_CHUNK, axis=1).reshape(2, 1, N_HEADS * LIN_CHUNK)
    p["ret_gn_g"] = row(ret_gn_g[0])
    p["ret_gn_b"] = row(ret_gn_b[0])

    p["conv_w"] = jnp.concatenate([ssd_conv_w[0], jnp.zeros((8 - SSD_CONV, SSD_XBC), F32)], 0)
    p["conv_b"] = row(ssd_conv_b[0])
    bias = ssd_dt_bias[0].astype(F32)
    nega = -jnp.exp(ssd_a_log[0].astype(F32))
    p["ssd"] = {
        "bias_c": bias[:, None, :], "nega_c": nega[:, None, :],
        "bias_r": bias[:, :, None], "nega_r": nega[:, :, None],
        "tri_fwd": jnp.asarray(_tri(LIN_CHUNK, False), BF16),
        "tri_rev": jnp.asarray(_tri(LIN_CHUNK, True), BF16),
        "trit_fwd": jnp.asarray(_tri(LIN_CHUNK, False).T, BF16),
        "trit_rev": jnp.asarray(_tri(LIN_CHUNK, True).T, BF16),
    }
    p["ssd_d"] = row(jnp.repeat(ssd_d[0], HEAD_DIM))
    p["ssd_norm_g"] = row(ssd_norm_g[0])
    return p


def _rope_tables(t):
    half = HEAD_DIM // 2
    inv = ROPE_BASE ** (-jnp.arange(half, dtype=F32) / half)
    ang = jnp.arange(t, dtype=F32)[:, None] * inv[None, :]
    cos = jnp.tile(jnp.cos(ang), (1, LANES // half))
    sin = jnp.sin(ang)
    sin_signed = jnp.tile(jnp.concatenate([-sin, sin], axis=1), (1, LANES // HEAD_DIM))
    return cos, sin_signed


def _even_mixer(h2, b, t, p):
    h = h2.reshape(b, t, D_MODEL)
    *qkvs, rcols = _even_in(h, p["norm_mix"][0:1], p["even_wa"], p["even_wr"], p["ones_bd"],
                            p["q_gain"], p["k_gain"])
    branch = []
    for qkv, (_, dil) in zip(qkvs, DILATED_BRANCHES):
        branch.extend(_dil_attn(qkv, dil))
    r, v, kkn, g, bonus, g_in, krep, bvec = _rwkv_prep(rcols, p["rwkv"], p["rwkv_tri"])
    ys = [_rwkv_scan(r, v, kkn, g_in, krep, bvec, d) for d in range(2)]
    out = _even_out(h, branch, [ys[0], ys[1], bonus, g], p["rwkv_ln_g"], p["rwkv_ln_b"],
                    p["ones_bd"], p["even_out_w"])
    return out.reshape(b * t, D_MODEL)


def _odd_mixer(h2, b, t, p):
    cos, sin = _rope_tables(t)
    ret2, xz2, dt2 = _odd_in(h2, p["norm_mix"][1:2], p["odd_wr"], p["odd_ws"], p["odd_wd"], cos, sin)
    ret = ret2.reshape(b, t, -1)
    xz = xz2.reshape(b, t, -1)
    rets = [_retention(ret, p["lg64"], p["lg128"], d) for d in range(2)]
    xs, bc = _ssd_conv(xz, p["conv_w"], p["conv_b"])
    dt = dt2[:, :2 * N_HEADS].reshape(b, t, 2, N_HEADS)
    dtc = dt.transpose(2, 0, 1, 3)
    dtr = dt.transpose(2, 0, 3, 1)
    ssds = [_ssd_scan(xs, bc, dtc, dtr, p["ssd"], d) for d in range(2)]
    flat = lambda u: u.reshape(b * t, -1)
    return _odd_out(h2, flat(rets[0]), flat(rets[1]), ret2, flat(ssds[0]), flat(ssds[1]), flat(xs),
                    xz2, p["ret_gn_g"], p["ret_gn_b"], p["ssd_d"], p["ssd_norm_g"], p["ones_bd"],
                    p["odd_out_w"])


def _trunk(x, p):
    b, t, _ = x.shape
    h2 = x.reshape(b * t, D_MODEL)
    h2 = _even_mixer(h2, b, t, p)
    h2 = _mlp(h2, p["norm_mlp"][0:1], p["mlp_w1"][0], p["mlp_w2"][0])
    h2 = _odd_mixer(h2, b, t, p)
    h2 = _mlp(h2, p["norm_mlp"][1:2], p["mlp_w1"][1], p["mlp_w2"][1])
    return h2.reshape(b, t, D_MODEL)


def kernel(x_prompt, x_sample, norm_mix, norm_mlp, mlp_w1, mlp_w2, even_in_w, even_out_w, attn_q_gain, attn_k_gain, rwkv_mu_prev, rwkv_mu_next, rwkv_w0, rwkv_w2, rwkv_a0, rwkv_a2, rwkv_g2, rwkv_k_k, rwkv_k_a, rwkv_r_k, rwkv_ln_g, rwkv_ln_b, odd_in_w, odd_out_w, ret_decay_exp, ret_gn_g, ret_gn_b, ssd_conv_w, ssd_conv_b, ssd_dt_bias, ssd_a_log, ssd_d, ssd_norm_g):
    p = _prepare(norm_mix, norm_mlp, mlp_w1, mlp_w2, even_in_w, even_out_w, attn_q_gain, attn_k_gain,
                 rwkv_mu_prev, rwkv_mu_next, rwkv_w0, rwkv_w2, rwkv_a0, rwkv_a2, rwkv_g2, rwkv_k_k,
                 rwkv_k_a, rwkv_r_k, rwkv_ln_g, rwkv_ln_b, odd_in_w, odd_out_w, ret_decay_exp,
                 ret_gn_g, ret_gn_b, ssd_conv_w, ssd_conv_b, ssd_dt_bias, ssd_a_log, ssd_d, ssd_norm_g)
    return (_trunk(x_prompt, p), _trunk(x_sample, p))
```

```python
import functools
import math

import numpy as np
import jax
import jax.numpy as jnp
from jax import lax
from jax.experimental import pallas as pl
from jax.experimental.pallas import tpu as pltpu

F32 = jnp.float32
BF16 = jnp.bfloat16

D_MODEL = 1024
HEAD_DIM = 64
N_HEADS = 8
WIDTH = N_HEADS * HEAD_DIM
N_PAIRS = N_HEADS // 2
LANES = 128
HALO_ROWS = 16
LSE_LANES = LANES // N_HEADS
DILATED_BRANCHES = ((128, 1), (512, 4), (2048, 16))
ATT_R = 64
assert all(window // (2 * dil) == ATT_R for window, dil in DILATED_BRANCHES)
assert [dil for _, dil in DILATED_BRANCHES] == [1, 4, 16]
RWKV_DECAY_RANK = 64
RWKV_ICL_RANK = 64
RWKV_GATE_RANK = 128
RWKV_COLS = 3 * WIDTH + 2 * RWKV_DECAY_RANK + 2 * RWKV_ICL_RANK + RWKV_GATE_RANK
SSD_STATE = 128
SSD_GROUPS = 2
SSD_CONV = 5
SSD_XBC = WIDTH + 2 * SSD_GROUPS * SSD_STATE
D_FF = 4 * D_MODEL
NORM_EPS = 1e-6
GN_EPS = 1e-5
RWKV_GN_EPS = 64e-5
ROPE_BASE = 10000.0
RWKV_CHUNK = 64
LIN_CHUNK = 128
NEG_BIG = -1e30
LOG2_E = math.log2(math.e)
VMEM_LIMIT = 56 * 1024 * 1024


def _cparams(*sem):
    return pltpu.CompilerParams(dimension_semantics=sem, vmem_limit_bytes=VMEM_LIMIT)


def _iota(shape, dim):
    return lax.broadcasted_iota(jnp.int32, shape, dim)


def _lane_lo(shape):
    return (_iota(shape, len(shape) - 1) & HEAD_DIM) == 0


def _bd(x):
    lo = _lane_lo(x.shape)
    zero = jnp.zeros_like(x)
    return jnp.concatenate([jnp.where(lo, x, zero), jnp.where(lo, zero, x)], axis=0)


def _mm(a, b):
    return jnp.dot(a.astype(BF16), b.astype(BF16), preferred_element_type=F32)


def _mm_nt(a, b):
    return lax.dot_general(a.astype(BF16), b.astype(BF16), (((1,), (1,)), ((), ())),
                           preferred_element_type=F32)


def _split_bf16(x, terms):
    parts = []
    for _ in range(terms - 1):
        hi = x.astype(BF16)
        parts.append(hi)
        x = x - hi.astype(F32)
    parts.append(x.astype(BF16))
    return parts


def _mm_exact_left(t, x, terms=3):
    return sum(jnp.dot(t, piece, preferred_element_type=F32) for piece in _split_bf16(x, terms))


def _mm_exact_right(x, t, terms=3):
    return sum(jnp.dot(piece, t, preferred_element_type=F32) for piece in _split_bf16(x, terms))


def _group_sum(x, ones_bd):
    return jnp.dot(x.astype(BF16), ones_bd, preferred_element_type=F32)


def _rms_rows(x, g):
    return x * lax.rsqrt(jnp.mean(x * x, -1, keepdims=True) + NORM_EPS) * g


def _sigmoid(x):
    return 0.5 * jnp.tanh(0.5 * x) + 0.5


def _silu(x):
    return x * _sigmoid(x)


def _softplus(x):
    return jnp.maximum(x, 0.0) + jnp.log1p(jnp.exp(-jnp.abs(x)))


def _scan_order(nc, reverse):
    return list(range(nc - 1, -1, -1) if reverse else range(nc))


def _even_in_kernel(x_ref, g_ref, wa_ref, wr_ref, ones_ref, qg_ref, kg_ref, perm4_ref, perm16_ref,
                    qkv1_out, qkv4_out, qkv16_out, r_out, *, tm):
    ub = _rms_rows(x_ref[0], g_ref[...]).astype(BF16)
    a = jnp.dot(ub, wa_ref[...], preferred_element_type=F32)
    q = a[:, :WIDTH]
    k = a[:, WIDTH:2 * WIDTH]
    ones = ones_ref[...]
    inv = 1.0 / HEAD_DIM
    qn = q * lax.rsqrt(_group_sum(q * q, ones) * inv + NORM_EPS) * qg_ref[...]
    kn = k * lax.rsqrt(_group_sum(k * k, ones) * inv + NORM_EPS) * kg_ref[...]
    qkv = jnp.concatenate([qn * (HEAD_DIM ** -0.5 * LOG2_E), kn, a[:, 2 * WIDTH:]], axis=1).astype(BF16)
    qkv1_out[0, 0] = qkv
    for dil, perm_ref, out in ((4, perm4_ref, qkv4_out), (16, perm16_ref, qkv16_out)):
        perm = perm_ref[...]
        bs = perm.shape[0]
        per = bs // dil
        for blk in range(tm // bs):
            moved = jnp.dot(perm, qkv[blk * bs:(blk + 1) * bs],
                            preferred_element_type=F32).astype(BF16)
            for rho in range(dil):
                out[0, rho, blk * per:(blk + 1) * per, :] = moved[rho * per:(rho + 1) * per]
    r_out[0] = jnp.dot(ub, wr_ref[...], preferred_element_type=F32).astype(BF16)


def _even_in(x, g, wa, wr, ones_bd, qg, kg, tm=512):
    b, t, _ = x.shape
    const = lambda shape: pl.BlockSpec(shape, lambda bb, i: (0, 0))
    row = lambda n: pl.BlockSpec((1, tm, n), lambda bb, i: (bb, i, 0))
    res = lambda d: pl.BlockSpec((1, d, tm // d, 3 * WIDTH), lambda bb, i: (bb, 0, i, 0))
    dils = [d for _, d in DILATED_BRANCHES]

    def to_residue_major(d, bs):
        src = np.arange(bs).reshape(bs // d, d).T.reshape(-1)
        return jnp.asarray(src[:, None] == np.arange(bs)[None, :], BF16)

    blocks = {4: 2 * HALO_ROWS * 4, 16: HALO_ROWS * 16}

    return pl.pallas_call(
        functools.partial(_even_in_kernel, tm=tm),
        grid=(b, t // tm),
        in_specs=[row(D_MODEL), const((1, D_MODEL)), const((D_MODEL, 3 * WIDTH)),
                  const((D_MODEL, RWKV_COLS)), const((WIDTH, WIDTH)),
                  const((1, WIDTH)), const((1, WIDTH)),
                  const((blocks[4], blocks[4])), const((blocks[16], blocks[16]))],
        out_specs=[res(d) for d in dils] + [row(RWKV_COLS)],
        out_shape=[jax.ShapeDtypeStruct((b, d, t // d, 3 * WIDTH), BF16) for d in dils]
        + [jax.ShapeDtypeStruct((b, t, RWKV_COLS), BF16)],
        compiler_params=_cparams("parallel", "parallel"),
        name="even_in",
    )(x, g, wa, wr, ones_bd, qg, kg, to_residue_major(4, blocks[4]), to_residue_major(16, blocks[16]))


def _dil_attn_kernel(q_ref, k_ref, kl_ref, kr_ref, v_ref, vl_ref, vr_ref, o_ref, lse_ref,
                     kext, vext, *, tq, n_res, sub_len, dil):
    r = ATT_R
    i = pl.program_id(2)
    for z in range(n_res):
        kext[z, 0:r] = kl_ref[0, z]
        kext[z, r:r + tq] = k_ref[0, z]
        kext[z, r + tq:] = kr_ref[0, z]
        vext[z, 0:r] = vl_ref[0, z]
        vext[z, r:r + tq] = v_ref[0, z]
        vext[z, r + tq:] = vr_ref[0, z]

    c_io = _iota((r, 3 * r), 1)
    q_io = _iota((r, 3 * r), 0)
    rel = c_io - r - q_io
    near = jnp.abs(rel) <= r
    dist = jnp.abs(rel).astype(F32) * float(dil)
    lo = _lane_lo((r, LANES))
    near2 = jnp.concatenate([near, near], axis=0)
    dist2 = jnp.concatenate([dist, dist], axis=0)
    first_head = _iota((2 * r, 3 * r), 0) < r
    cols = [slice(p * LANES, (p + 1) * LANES) for p in range(N_PAIRS)]
    bias = [jnp.where(first_head, 2.0 ** (-(2 * p + 1)), 2.0 ** (-(2 * p + 2))) * (LOG2_E * dist2)
            for p in range(N_PAIRS)]

    blocks = [(z, j) for z in range(n_res) for j in range(tq // r)]
    group = 2 if len(blocks) % 2 == 0 else 1
    for g0 in range(0, len(blocks), group):
        inst = [(z, j, p) for z, j in blocks[g0:g0 + group] for p in range(N_PAIRS)]
        valid = {}
        for z, j in blocks[g0:g0 + group]:
            kpos = i * tq + j * r + _iota((2 * r, 3 * r), 1) - r
            valid[j] = near2 & (kpos >= 0) & (kpos < sub_len)
        s = [jnp.where(valid[j],
                       _mm_nt(_bd(q_ref[0, z, j * r:(j + 1) * r, cols[p]]),
                              kext[z, j * r:(j + 3) * r, cols[p]]) - bias[p], NEG_BIG)
             for z, j, p in inst]
        m = [jnp.max(u, -1, keepdims=True) for u in s]
        e = [jnp.exp2(u - mm) for u, mm in zip(s, m)]
        den = [jnp.sum(u, -1, keepdims=True) for u in e]
        pv = [_mm(e[n], vext[z, j * r:(j + 3) * r, cols[p]]) / den[n]
              for n, (z, j, p) in enumerate(inst)]
        lse = [m[n] * (1.0 / LOG2_E) + jnp.log(den[n]) for n in range(len(inst))]
        for n, (z, j, p) in enumerate(inst):
            o_ref[0, z, j * r:(j + 1) * r, cols[p]] = jnp.where(lo, pv[n][:r], pv[n][r:]).astype(BF16)
        lane = _iota((r, LANES), 1)
        for bi, (z, j) in enumerate(blocks[g0:g0 + group]):
            tile = jnp.zeros((r, LANES), F32)
            for p in range(N_PAIRS):
                col = lse[bi * N_PAIRS + p]
                for hh in range(2):
                    h0 = (2 * p + hh) * LSE_LANES
                    tile = jnp.where((lane >= h0) & (lane < h0 + LSE_LANES),
                                     col[hh * r:(hh + 1) * r], tile)
            lse_ref[0, z, j * r:(j + 1) * r, :] = tile


def _dil_attn(qkv, dil, rows_per_step=1024):
    b, _, sub_len, _ = qkv.shape
    tq = min(sub_len, rows_per_step)
    n_res = min(dil, rows_per_step // tq)
    nblk = sub_len // ATT_R
    per = tq // ATT_R
    main = lambda col: pl.BlockSpec((1, n_res, tq, WIDTH), lambda bb, rr, i: (bb, rr, i, col))
    left = lambda col: pl.BlockSpec((1, n_res, ATT_R, WIDTH),
                                    lambda bb, rr, i: (bb, rr, jnp.maximum(i * per - 1, 0), col))
    right = lambda col: pl.BlockSpec((1, n_res, ATT_R, WIDTH),
                                     lambda bb, rr, i: (bb, rr, jnp.minimum((i + 1) * per, nblk - 1), col))
    return pl.pallas_call(
        functools.partial(_dil_attn_kernel, tq=tq, n_res=n_res, sub_len=sub_len, dil=dil),
        grid=(b, dil // n_res, sub_len // tq),
        in_specs=[main(0), main(1), left(1), right(1), main(2), left(2), right(2)],
        out_specs=[main(0), pl.BlockSpec((1, n_res, tq, LANES), lambda bb, rr, i: (bb, rr, i, 0))],
        out_shape=[jax.ShapeDtypeStruct((b, dil, sub_len, WIDTH), BF16),
                   jax.ShapeDtypeStruct((b, dil, sub_len, LANES), F32)],
        scratch_shapes=[pltpu.VMEM((n_res, tq + 2 * ATT_R, WIDTH), BF16)] * 2,
        compiler_params=_cparams("parallel", "parallel", "parallel"),
        name=f"dil_attn_d{dil}",
    )(qkv, qkv, qkv, qkv, qkv, qkv, qkv)


def _rwkv_prep_kernel(x_ref, xp_ref, xn_ref, mup_ref, mun_ref, w0_ref, w2_ref, a0_ref, a2_ref,
                      g2_ref, kk_ref, ka_ref, rk_ref, ones_ref, trif_ref, trir_ref,
                      r_out, v_out, kkn_out, g_out, bonus_out, g_in_out, krep_out, b_out, *, tm):
    i = pl.program_id(1)
    nt = pl.num_programs(1)
    x = x_ref[0].astype(F32)
    prow = jnp.where(i > 0, xp_ref[0, HALO_ROWS - 1:HALO_ROWS, :].astype(F32), 0.0)
    nrow = jnp.where(i < nt - 1, xn_ref[0, 0:1, :].astype(F32), 0.0)
    rid = _iota((8, 1), 0)
    prev = pltpu.roll(x, 1, 0)
    prev = jnp.concatenate([jnp.where(rid == 0, prow, prev[:8]), prev[8:]], axis=0)
    nxt = pltpu.roll(x, tm - 1, 0)
    nxt = jnp.concatenate([nxt[:tm - 8], jnp.where(rid == 7, nrow, nxt[tm - 8:])], axis=0)
    xs = x + mup_ref[...] * (prev - x) + mun_ref[...] * (nxt - x)
    r = xs[:, 0:WIDTH]
    k = xs[:, WIDTH:2 * WIDTH]
    v = xs[:, 2 * WIDTH:3 * WIDTH]
    zw = xs[:, 3 * WIDTH:3 * WIDTH + LANES]
    za = xs[:, 3 * WIDTH + LANES:3 * WIDTH + 2 * LANES]
    zg = xs[:, 3 * WIDTH + 2 * LANES:]
    ones = ones_ref[...]
    kk = k * kk_ref[...]
    kkn = kk * lax.rsqrt(jnp.maximum(_group_sum(kk * kk, ones), 1e-24))
    g_out[0] = _mm(_sigmoid(zg), g2_ref[...]).astype(BF16)
    r_out[0] = r.astype(BF16)
    v_out[0] = v.astype(BF16)
    kkn_out[0] = kkn.astype(BF16)
    bonus_out[0] = (_group_sum(r * k * rk_ref[...], ones) * v).astype(BF16)
    wx = w0_ref[...] + _mm(jnp.tanh(zw), w2_ref[...])
    ax = a0_ref[...] + _mm(za, a2_ref[...])
    for d, tri_ref in enumerate((trif_ref, trir_ref)):
        cols = slice(d * WIDTH, (d + 1) * WIDTH)
        lw = -math.exp(-0.5) * _sigmoid(wx[:, cols])
        tri = tri_ref[...]
        for c in range(tm // RWKV_CHUNK):
            rows = slice(c * RWKV_CHUNK, (c + 1) * RWKV_CHUNK)
            g_in_out[d, 0, rows, :] = _mm_exact_left(tri, lw[rows], terms=2)
        a = _sigmoid(ax[:, cols])
        krep_out[d, 0] = (k * (1.0 + (a - 1.0) * ka_ref[...])).astype(BF16)
        b_out[d, 0] = (kkn * a).astype(BF16)


def _rwkv_prep(rcols, p, tri, tm=512):
    b, t, _ = rcols.shape
    nt = t // tm
    per = tm // HALO_ROWS
    const = lambda shape: pl.BlockSpec(shape, lambda bb, i: (0,) * len(shape))
    main = lambda n: pl.BlockSpec((1, tm, n), lambda bb, i: (bb, i, 0))
    dirs = pl.BlockSpec((2, 1, tm, WIDTH), lambda bb, i: (0, bb, i, 0))
    prev = pl.BlockSpec((1, HALO_ROWS, RWKV_COLS), lambda bb, i: (bb, jnp.maximum(i * per - 1, 0), 0))
    nxt = pl.BlockSpec((1, HALO_ROWS, RWKV_COLS),
                       lambda bb, i: (bb, jnp.minimum((i + 1) * per, t // HALO_ROWS - 1), 0))
    one = jax.ShapeDtypeStruct((b, t, WIDTH), BF16)
    two = lambda dt: jax.ShapeDtypeStruct((2, b, t, WIDTH), dt)
    return pl.pallas_call(
        functools.partial(_rwkv_prep_kernel, tm=tm),
        grid=(b, nt),
        in_specs=[main(RWKV_COLS), prev, nxt, const((1, RWKV_COLS)), const((1, RWKV_COLS)),
                  const((1, 2 * WIDTH)), const((LANES, 2 * WIDTH)),
                  const((1, 2 * WIDTH)), const((LANES, 2 * WIDTH)),
                  const((LANES, WIDTH)), const((1, WIDTH)), const((1, WIDTH)), const((1, WIDTH)),
                  const((WIDTH, WIDTH)), const((RWKV_CHUNK, RWKV_CHUNK)),
                  const((RWKV_CHUNK, RWKV_CHUNK))],
        out_specs=[main(WIDTH)] * 5 + [dirs] * 3,
        out_shape=[one] * 5 + [two(F32), two(BF16), two(BF16)],
        compiler_params=_cparams("parallel", "parallel"),
        name="rwkv_prep",
    )(rcols, rcols, rcols, p["mu_prev"], p["mu_next"], p["w0"], p["w2"], p["a0"], p["a2"],
      p["g2"], p["k_k"], p["k_a"], p["r_k"], p["ones_bd"], tri[0], tri[1])


def _rwkv_scan_kernel(r_ref, v_ref, kk_ref, g_ref, k_ref, b_ref, y_ref, s_ref, *, nc, reverse):
    c_len = RWKV_CHUNK

    @pl.when(pl.program_id(1) == 0)
    def _():
        s_ref[...] = jnp.zeros_like(s_ref)

    row = _iota((c_len, LANES), 0)
    li = _iota((c_len, LANES), 1) & (HEAD_DIM - 1)
    strict = (li > row) if reverse else (li < row)
    incl = (li >= row) if reverse else (li <= row)
    r2 = _iota((LANES, LANES), 0)
    c2 = _iota((LANES, LANES), 1)
    same_head = (r2 & HEAD_DIM) == (c2 & HEAD_DIM)
    diag = r2 == c2
    zeros = jnp.zeros((c_len, LANES), F32)

    order = _scan_order(nc, reverse)
    inst = [(p, c) for c in order for p in range(N_PAIRS)]
    n = len(inst)

    def tile(ref, p, c, lead=()):
        return ref[lead + (0, slice(c * c_len, (c + 1) * c_len), slice(p * LANES, (p + 1) * LANES))]

    g_in = [tile(g_ref, p, c, (0,)) for p, c in inst]
    g_tot = [g[0:1] if reverse else g[c_len - 1:c_len] for g in g_in]
    scan_first = row == (c_len - 1 if reverse else 0)
    g_ex = [jnp.where(scan_first, 0.0, pltpu.roll(g, c_len - 1 if reverse else 1, 0)) for g in g_in]
    v = [tile(v_ref, p, c) for p, c in inst]
    at, rt, w, bk_end = [], [], [], []
    for i, (p, c) in enumerate(inst):
        b = tile(b_ref, p, c, (0,)).astype(F32)
        k = tile(k_ref, p, c, (0,)).astype(F32)
        at.append(-tile(kk_ref, p, c).astype(F32) * jnp.exp(g_ex[i]))
        rt.append(tile(r_ref, p, c).astype(F32) * jnp.exp(g_in[i]))
        inv = jnp.exp(-g_in[i])
        to_end = jnp.exp(g_tot[i] - g_in[i])
        bk_end.append(jnp.concatenate([b * to_end, k * to_end], 0))
        w.append(_mm_nt(jnp.concatenate([at[i], rt[i]], 0),
                        jnp.concatenate([_bd(b * inv), _bd(k * inv)], 0)))

    bdv = [_bd(u) for u in v]
    m_ab = [jnp.where(strict, u[:c_len, :LANES], 0.0) for u in w]
    kv = [_mm(jnp.concatenate([jnp.where(strict, w[i][:c_len, LANES:], 0.0),
                               jnp.where(incl, w[i][c_len:, LANES:], 0.0)], 0), bdv[i])
          for i in range(n)]
    rhs = [jnp.concatenate([at[i], kv[i][:c_len]], axis=1) for i in range(n)]
    tinv = [jnp.where(li == row, 1.0, 0.0) + u for u in m_ab]
    mj = [u.astype(BF16) for u in m_ab]
    mj = [_mm(u, _bd(u)).astype(BF16) for u in mj]
    for j in range(1, 5):
        both = [_mm(jnp.concatenate([mj[i], tinv[i].astype(BF16)], 0), _bd(mj[i])) for i in range(n)]
        mj = [u[:c_len].astype(BF16) for u in both]
        tinv = [tinv[i] + both[i][c_len:] for i in range(n)]
    tinv = [tinv[i] + _mm(tinv[i], _bd(mj[i])) for i in range(n)]
    x = [_mm(tinv[i], _bd(rhs[i])) for i in range(n)]

    t1 = [_mm(jnp.where(incl, w[i][c_len:, :LANES], 0.0), _bd(x[i])) for i in range(n)]
    y0 = [t1[i][:, LANES:] + kv[i][c_len:] for i in range(n)]
    rp = [(rt[i] + t1[i][:, :LANES]).astype(BF16) for i in range(n)]
    t2 = [_mm(bk_end[i].T,
              jnp.concatenate([x[i], jnp.concatenate([zeros, v[i].astype(F32)], axis=1)], 0))
          for i in range(n)]
    g_low = [jnp.where(same_head, u[:, :LANES], 0.0).astype(BF16) for u in t2]
    h_new = [jnp.where(same_head, u[:, LANES:], 0.0) for u in t2]
    g_col = [jnp.exp(jnp.sum(jnp.where(diag, jnp.broadcast_to(g, (LANES, LANES)), 0.0),
                             axis=1, keepdims=True)) for g in g_tot]

    s = [s_ref[p] for p in range(N_PAIRS)]
    for ci, c in enumerate(order):
        ids = [ci * N_PAIRS + p for p in range(N_PAIRS)]
        both = [_mm(jnp.concatenate([rp[i], g_low[i]], 0), s[p]) for p, i in enumerate(ids)]
        for p, i in enumerate(ids):
            y_ref[0, c * c_len:(c + 1) * c_len, p * LANES:(p + 1) * LANES] = (
                both[p][:c_len] + y0[i]).astype(BF16)
        s = [s[p] * g_col[i] + both[p][c_len:] + h_new[i] for p, i in enumerate(ids)]
    for p in range(N_PAIRS):
        s_ref[p] = s[p]


def _rwkv_scan(r, v, kkn, g_in, krep, bvec, direction, tb=1024):
    b, t, _ = r.shape
    nt = t // tb
    reverse = direction == 1
    tmap = (lambda i: nt - 1 - i) if reverse else (lambda i: i)
    one = pl.BlockSpec((1, tb, WIDTH), lambda bb, i: (bb, tmap(i), 0))
    two = pl.BlockSpec((1, 1, tb, WIDTH), lambda bb, i: (direction, bb, tmap(i), 0))
    return pl.pallas_call(
        functools.partial(_rwkv_scan_kernel, nc=tb // RWKV_CHUNK, reverse=reverse),
        grid=(b, nt),
        in_specs=[one, one, one, two, two, two],
        out_specs=one,
        out_shape=jax.ShapeDtypeStruct((b, t, WIDTH), BF16),
        scratch_shapes=[pltpu.VMEM((N_PAIRS, LANES, LANES), F32)],
        compiler_params=_cparams("parallel", "arbitrary"),
        name=f"rwkv_scan_{'bwd' if reverse else 'fwd'}",
    )(r, v, kkn, g_in, krep, bvec)


def _even_out_kernel(h_ref, o1, l1, o4, l4, o16, l16, yf, yb, bonus, g, lng, lnb, ones_ref, spread_ref,
                     w_ref, out_ref, tok_sc, lse_sc, *, tm):
    n_tiles = WIDTH // LANES
    for slot, (dil, o_ref, l_ref) in enumerate(((4, o4, l4), (16, o16, l16))):
        for rho in range(dil):
            rows = pl.ds(rho, tm // dil, stride=dil)
            lse_sc[slot, rows, :] = l_ref[0, rho]
            for c in range(n_tiles):
                tok_sc[slot * n_tiles + c, rows, :] = o_ref[0, rho, :, c * LANES:(c + 1) * LANES].astype(F32)
    tok = lambda slot: jnp.concatenate([tok_sc[slot * n_tiles + c] for c in range(n_tiles)], axis=1)
    la, lb, lc = l1[0, 0], lse_sc[0], lse_sc[1]
    mx = jnp.maximum(jnp.maximum(la, lb), lc)
    wa, wb, wc = jnp.exp(la - mx), jnp.exp(lb - mx), jnp.exp(lc - mx)
    norm = 1.0 / (wa + wb + wc)
    spread = lambda wgt: jnp.dot((wgt * norm).astype(BF16), spread_ref[...], preferred_element_type=F32)
    y_a = spread(wa) * o1[0, 0].astype(F32) + spread(wb) * tok(0) + spread(wc) * tok(1)
    ones = ones_ref[...]
    inv = 1.0 / HEAD_DIM
    y = yf[0].astype(F32) + yb[0].astype(F32)
    yc = y - _group_sum(y, ones) * inv
    var = _group_sum(yc * yc, ones) * inv
    yn = yc * lax.rsqrt(var + RWKV_GN_EPS) * lng[...] + lnb[...]
    y_b = (yn + bonus[0].astype(F32)) * g[0].astype(F32)
    out_ref[0] = (h_ref[0] + _mm(y_a, w_ref[0:WIDTH, :]) + _mm(y_b, w_ref[WIDTH:, :]))


def _even_out(h, branch, rwkv_parts, lng, lnb, ones_bd, w, tm=512):
    b, t, _ = h.shape
    const = lambda shape: pl.BlockSpec(shape, lambda bb, i: (0, 0))
    row = lambda n: pl.BlockSpec((1, tm, n), lambda bb, i: (bb, i, 0))
    res = lambda d, n: pl.BlockSpec((1, d, tm // d, n), lambda bb, i: (bb, 0, i, 0))
    dils = [d for _, d in DILATED_BRANCHES]
    spread = jnp.asarray(np.arange(LANES)[:, None] == (np.arange(WIDTH)[None, :] // HEAD_DIM) * LSE_LANES, BF16)
    return pl.pallas_call(
        functools.partial(_even_out_kernel, tm=tm),
        grid=(b, t // tm),
        in_specs=[row(D_MODEL)] + [res(d, n) for d in dils for n in (WIDTH, LANES)] + [row(WIDTH)] * 4
        + [const((1, WIDTH)), const((1, WIDTH)), const((WIDTH, WIDTH)), const((LANES, WIDTH)),
           const((D_MODEL, D_MODEL))],
        out_specs=row(D_MODEL),
        out_shape=jax.ShapeDtypeStruct((b, t, D_MODEL), F32),
        scratch_shapes=[pltpu.VMEM((2 * WIDTH // LANES, tm, LANES), F32), pltpu.VMEM((2, tm, LANES), F32)],
        compiler_params=_cparams("parallel", "parallel"),
        name="even_out",
    )(h, *branch, *rwkv_parts, lng, lnb, ones_bd, spread, w)


def _mlp_kernel(h_ref, g_ref, w1_ref, w2_ref, out_ref, u_ref):
    f = pl.program_id(1)

    @pl.when(f == 0)
    def _():
        x = h_ref[...]
        u_ref[...] = _rms_rows(x, g_ref[...]).astype(BF16)
        out_ref[...] = x

    hdn = jnp.dot(u_ref[...], w1_ref[...], preferred_element_type=F32)
    hdn = jnp.square(jnp.maximum(hdn, 0.0))
    out_ref[...] += jnp.dot(hdn.astype(BF16), w2_ref[...], preferred_element_type=F32)


def _mlp(h2, g, w1, w2, tm=1024, tf=2048):
    m = h2.shape[0]
    return pl.pallas_call(
        _mlp_kernel,
        grid=(m // tm, D_FF // tf),
        in_specs=[pl.BlockSpec((tm, D_MODEL), lambda i, f: (i, 0)),
                  pl.BlockSpec((1, D_MODEL), lambda i, f: (0, 0)),
                  pl.BlockSpec((D_MODEL, tf), lambda i, f: (0, f)),
                  pl.BlockSpec((tf, D_MODEL), lambda i, f: (f, 0))],
        out_specs=pl.BlockSpec((tm, D_MODEL), lambda i, f: (i, 0)),
        out_shape=jax.ShapeDtypeStruct((m, D_MODEL), F32),
        scratch_shapes=[pltpu.VMEM((tm, D_MODEL), BF16)],
        compiler_params=_cparams("parallel", "arbitrary"),
        name="mlp",
    )(h2, g, w1, w2)


def _rotary_pair(u, cos, sin_signed):
    first = (_iota(u.shape, 1) & (HEAD_DIM - 1)) < HEAD_DIM // 2
    swapped = jnp.where(first, pltpu.roll(u, LANES - HEAD_DIM // 2, 1),
                        pltpu.roll(u, HEAD_DIM // 2, 1))
    return u * cos + swapped * sin_signed


def _odd_in_kernel(x_ref, g_ref, wr_ref, ws_ref, wd_ref, cos_ref, sin_ref,
                   ret_out, xz_out, dt_out, dtt_out):
    ub = _rms_rows(x_ref[...], g_ref[...]).astype(BF16)
    ret = jnp.dot(ub, wr_ref[...], preferred_element_type=F32)
    cos = cos_ref[...]
    sin = sin_ref[...]
    for c in range(2 * N_PAIRS):
        cols = slice(c * LANES, (c + 1) * LANES)
        scale = 1.0 if c < N_PAIRS else HEAD_DIM ** -0.5
        ret_out[:, cols] = (_rotary_pair(ret[:, cols], cos, sin) * scale).astype(BF16)
    ret_out[:, 2 * WIDTH:] = ret[:, 2 * WIDTH:].astype(BF16)
    xz_out[...] = jnp.dot(ub, ws_ref[...], preferred_element_type=F32).astype(BF16)
    dt = jnp.dot(ub, wd_ref[...], preferred_element_type=F32)
    dt_out[...] = dt
    dtt_out[...] = dt.T[:2 * N_HEADS, :]


def _odd_in(x2, g, wr, ws, wd, cos, sin, tm=512):
    m = x2.shape[0]
    per_seq = cos.shape[0] // tm
    const = lambda shape: pl.BlockSpec(shape, lambda i: (0, 0))
    row = lambda n: pl.BlockSpec((tm, n), lambda i: (i, 0))
    tab = pl.BlockSpec((tm, LANES), lambda i: (i % per_seq, 0))
    n_xz = SSD_XBC + WIDTH
    return pl.pallas_call(
        _odd_in_kernel,
        grid=(m // tm,),
        in_specs=[row(D_MODEL), const((1, D_MODEL)), const((D_MODEL, 4 * WIDTH)),
                  const((D_MODEL, n_xz)), const((D_MODEL, LANES)), tab, tab],
        out_specs=[row(4 * WIDTH), row(n_xz), row(LANES),
                   pl.BlockSpec((2 * N_HEADS, tm), lambda i: (0, i))],
        out_shape=[jax.ShapeDtypeStruct((m, 4 * WIDTH), BF16),
                   jax.ShapeDtypeStruct((m, n_xz), BF16),
                   jax.ShapeDtypeStruct((m, LANES), F32),
                   jax.ShapeDtypeStruct((2 * N_HEADS, m), F32)],
        compiler_params=_cparams("parallel"),
        name="odd_in",
    )(x2, g, wr, ws, wd, cos, sin)


def _ret_kernel(q_ref, k_ref, v_ref, lg64_ref, lg128_ref, y_ref, s_ref,
                *, nc, reverse):
    c_len = LIN_CHUNK

    @pl.when(pl.program_id(1) == 0)
    def _():
        s_ref[...] = jnp.zeros_like(s_ref)

    l_io = _iota((c_len, 2 * c_len), 0)
    s_io = _iota((c_len, 2 * c_len), 1) & (c_len - 1)
    ahead = jnp.maximum(l_io - s_io, 0).astype(F32)
    behind = jnp.maximum(s_io - l_io, 0).astype(F32)
    pos = _iota((c_len, LANES), 0)
    if reverse:
        pos = c_len - 1 - pos
    pos = pos.astype(F32)
    r2 = _iota((LANES, LANES), 0)
    c2 = _iota((LANES, LANES), 1)
    same_head = (r2 & HEAD_DIM) == (c2 & HEAD_DIM)

    decay, from_start, to_end, whole = [], [], [], []
    for p in range(N_PAIRS):
        lg64 = lg64_ref[0, :, p * LANES:(p + 1) * LANES]
        if not reverse:
            wide = slice(2 * p * c_len, 2 * (p + 1) * c_len)
            decay.append(jnp.where(l_io >= s_io, jnp.exp(lg128_ref[0, :, wide] * ahead), 0.0)
                         + jnp.where(s_io >= l_io, jnp.exp(lg128_ref[1, :, wide] * behind), 0.0))
        from_start.append(jnp.exp(lg64 * (pos + 1.0)))
        to_end.append(jnp.exp(lg64 * (float(c_len - 1) - pos)))
        whole.append(jnp.exp(lg64 * float(c_len)))

    order = _scan_order(nc, reverse)
    inst = [(p, c) for c in order for p in range(N_PAIRS)]
    q, k, v = [], [], []
    for p, c in inst:
        rows = slice(c * c_len, (c + 1) * c_len)
        cols = slice(p * LANES, (p + 1) * LANES)
        q.append(q_ref[0, rows, cols])
        k.append(k_ref[0, rows, cols])
        v.append(v_ref[0, rows, cols])
    if not reverse:
        sc = [(_mm_nt(q[i], _bd(k[i])) * decay[p]).astype(BF16) for i, (p, c) in enumerate(inst)]
        y_intra = [_mm(sc[i], _bd(v[i])) for i in range(len(inst))]
    upd = [jnp.where(same_head, _mm((k[i] * to_end[p]).T, v[i]), 0.0) for i, (p, c) in enumerate(inst)]
    qs = [(q[i] * from_start[p]).astype(BF16) for i, (p, c) in enumerate(inst)]

    s = [s_ref[p] for p in range(N_PAIRS)]
    for ci, c in enumerate(order):
        for p in range(N_PAIRS):
            i = ci * N_PAIRS + p
            y = _mm(qs[i], s[p])
            if not reverse:
                y = y + y_intra[i]
            y_ref[0, c * c_len:(c + 1) * c_len, p * LANES:(p + 1) * LANES] = y.astype(BF16)
            s[p] = s[p] * whole[p] + upd[i]
    for p in range(N_PAIRS):
        s_ref[p] = s[p]


def _retention(ret, lg64, lg128, direction, tb=1024):
    b, t, _ = ret.shape
    nt = t // tb
    reverse = direction == 1
    tmap = (lambda i: nt - 1 - i) if reverse else (lambda i: i)
    col = lambda blk: pl.BlockSpec((1, tb, WIDTH), lambda bb, i: (bb, tmap(i), blk))
    return pl.pallas_call(
        functools.partial(_ret_kernel, nc=tb // LIN_CHUNK, reverse=reverse),
        grid=(b, nt),
        in_specs=[col(0), col(1), col(2),
                  pl.BlockSpec((1, 1, WIDTH), lambda bb, i: (direction, 0, 0)),
                  pl.BlockSpec((2, 1, N_HEADS * LIN_CHUNK), lambda bb, i: (0, 0, 0))],
        out_specs=col(0),
        out_shape=jax.ShapeDtypeStruct((b, t, WIDTH), BF16),
        scratch_shapes=[pltpu.VMEM((N_PAIRS, LANES, LANES), F32)],
        compiler_params=_cparams("parallel", "arbitrary"),
        name=f"retention_{'bwd' if reverse else 'fwd'}",
    )(ret, ret, ret, lg64, lg128)


def _ssd_conv_kernel(x_ref, xp_ref, xn_ref, w_ref, b_ref, xs_out, bc_out, *, tm):
    i = pl.program_id(1)
    nt = pl.num_programs(1)
    prev = jnp.where(i > 0, xp_ref[0].astype(F32), 0.0)
    nxt = jnp.where(i < nt - 1, xn_ref[0].astype(F32), 0.0)
    xe = jnp.concatenate([prev, x_ref[0].astype(F32), nxt], axis=0)
    ext = tm + 2 * HALO_ROWS
    pad = SSD_CONV // 2
    acc = jnp.zeros((tm, SSD_XBC), F32) + b_ref[...]
    for j in range(SSD_CONV):
        shift = (pad - j) % ext
        xj = xe if shift == 0 else pltpu.roll(xe, shift, 0)
        acc = acc + xj[HALO_ROWS:HALO_ROWS + tm] * w_ref[j:j + 1, :]
    y = _silu(acc).astype(BF16)
    xs_out[0] = y[:, :WIDTH]
    bc_out[0] = y[:, WIDTH:]


def _ssd_conv(xz, w, bias, tm=512):
    b, t, _ = xz.shape
    nt = t // tm
    per = tm // HALO_ROWS
    const = lambda shape: pl.BlockSpec(shape, lambda bb, i: (0, 0))
    main = pl.BlockSpec((1, tm, SSD_XBC), lambda bb, i: (bb, i, 0))
    prev = pl.BlockSpec((1, HALO_ROWS, SSD_XBC), lambda bb, i: (bb, jnp.maximum(i * per - 1, 0), 0))
    nxt = pl.BlockSpec((1, HALO_ROWS, SSD_XBC),
                       lambda bb, i: (bb, jnp.minimum((i + 1) * per, t // HALO_ROWS - 1), 0))
    out = pl.BlockSpec((1, tm, WIDTH), lambda bb, i: (bb, i, 0))
    return pl.pallas_call(
        functools.partial(_ssd_conv_kernel, tm=tm),
        grid=(b, nt),
        in_specs=[main, prev, nxt, const((8, SSD_XBC)), const((1, SSD_XBC))],
        out_specs=[out, out],
        out_shape=[jax.ShapeDtypeStruct((b, t, WIDTH), BF16)] * 2,
        compiler_params=_cparams("parallel", "parallel"),
        name="ssd_conv",
    )(xz, xz, xz, w, bias)


def _ssd_kernel(xs_ref, bc_ref, dtc_ref, dtr_ref, bias_c_ref, bias_r_ref, nega_c_ref, nega_r_ref,
                tri_ref, trit_ref, y_ref, s_ref, *, nc, reverse):
    c_len = LIN_CHUNK
    per_group = N_HEADS // SSD_GROUPS
    gw = WIDTH // SSD_GROUPS

    @pl.when(pl.program_id(1) == 0)
    def _():
        s_ref[...] = jnp.zeros_like(s_ref)

    tri = tri_ref[...]
    trit = trit_ref[...]
    own = slice(N_HEADS, 2 * N_HEADS) if reverse else slice(0, N_HEADS)
    dts_c = _softplus(dtc_ref[0][:, own] + bias_c_ref[0])
    la_c = dts_c * nega_c_ref[0]
    dts_r = _softplus(dtr_ref[0] + bias_r_ref[0])
    la_r = dts_r * nega_r_ref[0]
    l_io = _iota((c_len, c_len), 0)
    s_io = _iota((c_len, c_len), 1)
    causal = (s_io >= l_io) if reverse else (s_io <= l_io)
    lo = _lane_lo((c_len, LANES))

    def by_head128(cols, g):
        return jnp.concatenate([jnp.broadcast_to(cols[:, g * per_group + h:g * per_group + h + 1],
                                                 (c_len, LANES)) for h in range(per_group)], axis=1)

    def to64(wide):
        h = [wide[:, j * LANES:(j + 1) * LANES] for j in range(per_group)]
        return jnp.concatenate([jnp.where(lo, h[0], h[1]), jnp.where(lo, h[2], h[3])], axis=1)

    order = _scan_order(nc, reverse)
    inst = [(g, c) for c in order for g in range(SSD_GROUPS)]
    n = len(inst)
    rows_of = lambda c: slice(c * c_len, (c + 1) * c_len)
    cum = [_mm_exact_left(tri, by_head128(la_c[rows_of(c)], g), terms=2) for g, c in inst]
    cum_r = {c: _mm_exact_right(la_r[:, rows_of(c)], trit, terms=2) for c in order}
    last = [u[0:1] if reverse else u[c_len - 1:c_len] for u in cum]
    bm = [bc_ref[0, rows_of(c), g * SSD_STATE:(g + 1) * SSD_STATE] for g, c in inst]
    cm = [bc_ref[0, rows_of(c), (SSD_GROUPS + g) * SSD_STATE:(SSD_GROUPS + g + 1) * SSD_STATE]
          for g, c in inst]
    xdt = [xs_ref[0, rows_of(c), g * gw:(g + 1) * gw].astype(F32) * to64(by_head128(dts_c[rows_of(c)], g))
           for g, c in inst]
    gm = [jnp.where(causal, _mm_nt(cm[i], bm[i]), 0.0) for i in range(n)]
    y_intra = []
    for i, (g, c) in enumerate(inst):
        parts = []
        for pp in range(per_group // 2):
            wide = []
            for hh in (2 * pp, 2 * pp + 1):
                head = g * per_group + hh
                diff = cum[i][:, hh * LANES:(hh + 1) * LANES] - cum_r[c][head:head + 1, :]
                wide.append((gm[i] * jnp.exp(jnp.minimum(diff, 0.0))).astype(BF16))
            parts.append(_mm(jnp.concatenate(wide, axis=1), _bd(xdt[i][:, pp * LANES:(pp + 1) * LANES])))
        y_intra.append(jnp.concatenate(parts, axis=1))
    from_start = [to64(jnp.exp(u)) for u in cum]
    upd = [_mm(bm[i].astype(F32).T, xdt[i] * to64(jnp.exp(last[i] - cum[i]))) for i in range(n)]
    whole = [to64(jnp.exp(u)) for u in last]

    s = [s_ref[g] for g in range(SSD_GROUPS)]
    for ci, c in enumerate(order):
        for g in range(SSD_GROUPS):
            i = ci * SSD_GROUPS + g
            y_ref[0, rows_of(c), g * gw:(g + 1) * gw] = (
                y_intra[i] + _mm(cm[i], s[g]) * from_start[i]).astype(BF16)
            s[g] = s[g] * whole[i] + upd[i]
    for g in range(SSD_GROUPS):
        s_ref[g] = s[g]


def _ssd_scan(xs, bc, dtc, dtr, sp, direction, tb=1024):
    b, t, _ = xs.shape
    nt = t // tb
    reverse = direction == 1
    tmap = (lambda i: nt - 1 - i) if reverse else (lambda i: i)
    row = pl.BlockSpec((1, tb, WIDTH), lambda bb, i: (bb, tmap(i), 0))
    par = lambda shape: pl.BlockSpec((1,) + shape, lambda bb, i: (direction, 0, 0))
    tri = pl.BlockSpec((LIN_CHUNK, LIN_CHUNK), lambda bb, i: (0, 0))
    return pl.pallas_call(
        functools.partial(_ssd_kernel, nc=tb // LIN_CHUNK, reverse=reverse),
        grid=(b, nt),
        in_specs=[row, row,
                  pl.BlockSpec((1, tb, LANES), lambda bb, i: (bb, tmap(i), 0)),
                  pl.BlockSpec((1, N_HEADS, tb), lambda bb, i: (direction, 0, bb * nt + tmap(i))),
                  par((1, N_HEADS)), par((N_HEADS, 1)), par((1, N_HEADS)), par((N_HEADS, 1)),
                  tri, tri],
        out_specs=row,
        out_shape=jax.ShapeDtypeStruct((b, t, WIDTH), BF16),
        scratch_shapes=[pltpu.VMEM((SSD_GROUPS, SSD_STATE, WIDTH // SSD_GROUPS), F32)],
        compiler_params=_cparams("parallel", "arbitrary"),
        name=f"ssd_scan_{'bwd' if reverse else 'fwd'}",
    )(xs, bc, dtc, dtr, sp["bias_c"], sp["bias_r"], sp["nega_c"], sp["nega_r"],
      sp["tri_rev" if reverse else "tri_fwd"], sp["trit_rev" if reverse else "trit_fwd"])


def _odd_out_kernel(h_ref, rf, rb, gate, sf, sb, xs, z, gng, gnb, dsk, ng, ones_ref, w_ref, out_ref):
    f32 = lambda ref: ref[...].astype(F32)
    ones = ones_ref[...]
    inv = 1.0 / HEAD_DIM
    y = f32(rf) + f32(rb)
    yc = y - _group_sum(y, ones) * inv
    var = _group_sum(yc * yc, ones) * inv
    y_c = (yc * lax.rsqrt(var + GN_EPS) * gng[...] + gnb[...]) * _silu(f32(gate))
    yd = (f32(sf) + f32(sb) + dsk[...] * f32(xs)) * _silu(f32(z))
    gw = WIDTH // SSD_GROUPS
    halves = []
    for gi in range(SSD_GROUPS):
        yg = yd[:, gi * gw:(gi + 1) * gw]
        halves.append(yg * lax.rsqrt(jnp.mean(yg * yg, -1, keepdims=True) + NORM_EPS))
    y_d = jnp.concatenate(halves, axis=1) * ng[...]
    out_ref[...] = h_ref[...] + _mm(y_c, w_ref[0:WIDTH, :]) + _mm(y_d, w_ref[WIDTH:, :])


def _odd_out(h2, rf, rb, ret2, sf, sb, xs, xz2, gng, gnb, dsk, ng, ones_bd, w, tm=512):
    m = h2.shape[0]
    const = lambda shape: pl.BlockSpec(shape, lambda i: (0, 0))
    row = lambda n: pl.BlockSpec((tm, n), lambda i: (i, 0))
    vec = const((1, WIDTH))
    return pl.pallas_call(
        _odd_out_kernel,
        grid=(m // tm,),
        in_specs=[row(D_MODEL), row(WIDTH), row(WIDTH),
                  pl.BlockSpec((tm, WIDTH), lambda i: (i, 3)),
                  row(WIDTH), row(WIDTH), row(WIDTH),
                  pl.BlockSpec((tm, WIDTH), lambda i: (i, 2)),
                  vec, vec, vec, vec, const((WIDTH, WIDTH)), const((D_MODEL, D_MODEL))],
        out_specs=row(D_MODEL),
        out_shape=jax.ShapeDtypeStruct((m, D_MODEL), F32),
        compiler_params=_cparams("parallel"),
        name="odd_out",
    )(h2, rf, rb, ret2, sf, sb, xs, xz2, gng, gnb, dsk, ng, ones_bd, w)


def _tri(n, reverse):
    i = np.arange(n)
    m = (i[None, :] >= i[:, None]) if reverse else (i[None, :] <= i[:, None])
    return m.astype(np.float32)


def _prepare(norm_mix, norm_mlp, mlp_w1, mlp_w2, even_in_w, even_out_w, attn_q_gain, attn_k_gain,
             rwkv_mu_prev, rwkv_mu_next, rwkv_w0, rwkv_w2, rwkv_a0, rwkv_a2, rwkv_g2, rwkv_k_k,
             rwkv_k_a, rwkv_r_k, rwkv_ln_g, rwkv_ln_b, odd_in_w, odd_out_w, ret_decay_exp, ret_gn_g,
             ret_gn_b, ssd_conv_w, ssd_conv_b, ssd_dt_bias, ssd_a_log, ssd_d, ssd_norm_g):
    row = lambda v: v.reshape(1, -1).astype(F32)
    heads = np.arange(WIDTH) // HEAD_DIM
    ones_bd = jnp.asarray(heads[:, None] == heads[None, :], BF16)
    p = {"ones_bd": ones_bd, "norm_mix": norm_mix, "norm_mlp": norm_mlp,
         "mlp_w1": mlp_w1.astype(BF16), "mlp_w2": mlp_w2.astype(BF16)}

    w_in = even_in_w[0]
    p["even_wa"] = w_in[:, :3 * WIDTH].astype(BF16)
    p["even_wr"] = w_in[:, 3 * WIDTH:].astype(BF16)
    p["even_out_w"] = even_out_w[0].astype(BF16)
    p["q_gain"] = row(jnp.tile(attn_q_gain[0], N_HEADS))
    p["k_gain"] = row(jnp.tile(attn_k_gain[0], N_HEADS))

    def two_dir_lowrank(w):
        z = jnp.zeros_like(w[0])
        return jnp.concatenate([jnp.concatenate([w[0], z], 1), jnp.concatenate([z, w[1]], 1)], 0)

    p["rwkv"] = {
        "mu_prev": row(rwkv_mu_prev[0]), "mu_next": row(rwkv_mu_next[0]),
        "w0": row(rwkv_w0[0]), "w2": two_dir_lowrank(rwkv_w2[0]).astype(BF16),
        "a0": row(rwkv_a0[0]), "a2": two_dir_lowrank(rwkv_a2[0]).astype(BF16),
        "g2": rwkv_g2[0].astype(BF16), "k_k": row(rwkv_k_k[0]), "k_a": row(rwkv_k_a[0]),
        "r_k": row(rwkv_r_k[0]), "ones_bd": ones_bd,
    }
    p["rwkv_tri"] = [jnp.asarray(_tri(RWKV_CHUNK, False), BF16), jnp.asarray(_tri(RWKV_CHUNK, True), BF16)]
    p["rwkv_ln_g"] = row(rwkv_ln_g[0])
    p["rwkv_ln_b"] = row(rwkv_ln_b[0])

    w_odd = odd_in_w[0]
    p["odd_wr"] = w_odd[:, :4 * WIDTH].astype(BF16)
    z_w = w_odd[:, 4 * WIDTH:5 * WIDTH]
    xbc_w = w_odd[:, 5 * WIDTH:5 * WIDTH + SSD_XBC]
    dt_w = w_odd[:, 5 * WIDTH + SSD_XBC:]
    p["odd_ws"] = jnp.concatenate([xbc_w, z_w], axis=1).astype(BF16)
    p["odd_wd"] = jnp.concatenate([dt_w, jnp.zeros((D_MODEL, LANES - 2 * N_HEADS), F32)], 1).astype(BF16)
    p["odd_out_w"] = odd_out_w[0].astype(BF16)

    log_gamma = jnp.log1p(-jnp.exp2(-ret_decay_exp[0].astype(F32)))
    p["lg64"] = jnp.repeat(log_gamma, HEAD_DIM, axis=1).reshape(2, 1, WIDTH)
    p["lg128"] = jnp.repeat(log_gamma, LIN_CHUNK, axis=1).reshape(2, 1, N_HEADS * LIN_CHUNK)
    p["ret_gn_g"] = row(ret_gn_g[0])
    p["ret_gn_b"] = row(ret_gn_b[0])

    p["conv_w"] = jnp.concatenate([ssd_conv_w[0], jnp.zeros((8 - SSD_CONV, SSD_XBC), F32)], 0)
    p["conv_b"] = row(ssd_conv_b[0])
    bias = ssd_dt_bias[0].astype(F32)
    nega = -jnp.exp(ssd_a_log[0].astype(F32))
    p["ssd"] = {
        "bias_c": bias[:, None, :], "nega_c": nega[:, None, :],
        "bias_r": bias[:, :, None], "nega_r": nega[:, :, None],
        "tri_fwd": jnp.asarray(_tri(LIN_CHUNK, False), BF16),
        "tri_rev": jnp.asarray(_tri(LIN_CHUNK, True), BF16),
        "trit_fwd": jnp.asarray(_tri(LIN_CHUNK, False).T, BF16),
        "trit_rev": jnp.asarray(_tri(LIN_CHUNK, True).T, BF16),
    }
    p["ssd_d"] = row(jnp.repeat(ssd_d[0], HEAD_DIM))
    p["ssd_norm_g"] = row(ssd_norm_g[0])
    return p


def _rope_tables(t):
    half = HEAD_DIM // 2
    inv = ROPE_BASE ** (-jnp.arange(half, dtype=F32) / half)
    ang = jnp.arange(t, dtype=F32)[:, None] * inv[None, :]
    cos = jnp.tile(jnp.cos(ang), (1, LANES // half))
    sin = jnp.sin(ang)
    sin_signed = jnp.tile(jnp.concatenate([-sin, sin], axis=1), (1, LANES // HEAD_DIM))
    return cos, sin_signed


def _even_mixer(h2, b, t, p):
    h = h2.reshape(b, t, D_MODEL)
    *qkvs, rcols = _even_in(h, p["norm_mix"][0:1], p["even_wa"], p["even_wr"], p["ones_bd"],
                            p["q_gain"], p["k_gain"])
    branch = []
    for qkv, (_, dil) in zip(qkvs, DILATED_BRANCHES):
        branch.extend(_dil_attn(qkv, dil))
    r, v, kkn, g, bonus, g_in, krep, bvec = _rwkv_prep(rcols, p["rwkv"], p["rwkv_tri"])
    ys = [_rwkv_scan(r, v, kkn, g_in, krep, bvec, d) for d in range(2)]
    out = _even_out(h, branch, [ys[0], ys[1], bonus, g], p["rwkv_ln_g"], p["rwkv_ln_b"],
                    p["ones_bd"], p["even_out_w"])
    return out.reshape(b * t, D_MODEL)


def _odd_mixer(h2, b, t, p):
    cos, sin = _rope_tables(t)
    ret2, xz2, dt2, dtt = _odd_in(h2, p["norm_mix"][1:2], p["odd_wr"], p["odd_ws"], p["odd_wd"], cos, sin)
    ret = ret2.reshape(b, t, -1)
    xz = xz2.reshape(b, t, -1)
    rets = [_retention(ret, p["lg64"], p["lg128"], d) for d in range(2)]
    xs, bc = _ssd_conv(xz, p["conv_w"], p["conv_b"])
    dtc = dt2.reshape(b, t, LANES)
    dtr = dtt.reshape(2, N_HEADS, b * t)
    ssds = [_ssd_scan(xs, bc, dtc, dtr, p["ssd"], d) for d in range(2)]
    flat = lambda u: u.reshape(b * t, -1)
    return _odd_out(h2, flat(rets[0]), flat(rets[1]), ret2, flat(ssds[0]), flat(ssds[1]), flat(xs),
                    xz2, p["ret_gn_g"], p["ret_gn_b"], p["ssd_d"], p["ssd_norm_g"], p["ones_bd"],
                    p["odd_out_w"])


def _trunk(x, p):
    b, t, _ = x.shape
    h2 = x.reshape(b * t, D_MODEL)
    h2 = _even_mixer(h2, b, t, p)
    h2 = _mlp(h2, p["norm_mlp"][0:1], p["mlp_w1"][0], p["mlp_w2"][0])
    h2 = _odd_mixer(h2, b, t, p)
    h2 = _mlp(h2, p["norm_mlp"][1:2], p["mlp_w1"][1], p["mlp_w2"][1])
    return h2.reshape(b, t, D_MODEL)


def kernel(x_prompt, x_sample, norm_mix, norm_mlp, mlp_w1, mlp_w2, even_in_w, even_out_w, attn_q_gain, attn_k_gain, rwkv_mu_prev, rwkv_mu_next, rwkv_w0, rwkv_w2, rwkv_a0, rwkv_a2, rwkv_g2, rwkv_k_k, rwkv_k_a, rwkv_r_k, rwkv_ln_g, rwkv_ln_b, odd_in_w, odd_out_w, ret_decay_exp, ret_gn_g, ret_gn_b, ssd_conv_w, ssd_conv_b, ssd_dt_bias, ssd_a_log, ssd_d, ssd_norm_g):
    p = _prepare(norm_mix, norm_mlp, mlp_w1, mlp_w2, even_in_w, even_out_w, attn_q_gain, attn_k_gain,
                 rwkv_mu_prev, rwkv_mu_next, rwkv_w0, rwkv_w2, rwkv_a0, rwkv_a2, rwkv_g2, rwkv_k_k,
                 rwkv_k_a, rwkv_r_k, rwkv_ln_g, rwkv_ln_b, odd_in_w, odd_out_w, ret_decay_exp,
                 ret_gn_g, ret_gn_b, ssd_conv_w, ssd_conv_b, ssd_dt_bias, ssd_a_log, ssd_d, ssd_norm_g)
    return (_trunk(x_prompt, p), _trunk(x_sample, p))
```

```python
import functools
import math

import numpy as np
import jax
import jax.numpy as jnp
from jax import lax
from jax.experimental import pallas as pl
from jax.experimental.pallas import tpu as pltpu

F32 = jnp.float32
BF16 = jnp.bfloat16

D_MODEL = 1024
HEAD_DIM = 64
N_HEADS = 8
WIDTH = N_HEADS * HEAD_DIM
N_PAIRS = N_HEADS // 2
LANES = 128
HALO_ROWS = 16
LSE_LANES = LANES // N_HEADS
DILATED_BRANCHES = ((128, 1), (512, 4), (2048, 16))
ATT_R = 64
assert all(window // (2 * dil) == ATT_R for window, dil in DILATED_BRANCHES)
assert [dil for _, dil in DILATED_BRANCHES] == [1, 4, 16]
RWKV_DECAY_RANK = 64
RWKV_ICL_RANK = 64
RWKV_GATE_RANK = 128
RWKV_COLS = 3 * WIDTH + 2 * RWKV_DECAY_RANK + 2 * RWKV_ICL_RANK + RWKV_GATE_RANK
SSD_STATE = 128
SSD_GROUPS = 2
SSD_CONV = 5
SSD_XBC = WIDTH + 2 * SSD_GROUPS * SSD_STATE
D_FF = 4 * D_MODEL
NORM_EPS = 1e-6
GN_EPS = 1e-5
RWKV_GN_EPS = 64e-5
ROPE_BASE = 10000.0
RWKV_CHUNK = 64
LIN_CHUNK = 128
NEG_BIG = -1e30
LOG2_E = math.log2(math.e)
VMEM_LIMIT = 56 * 1024 * 1024


def _cparams(*sem):
    return pltpu.CompilerParams(dimension_semantics=sem, vmem_limit_bytes=VMEM_LIMIT)


def _iota(shape, dim):
    return lax.broadcasted_iota(jnp.int32, shape, dim)


def _lane_lo(shape):
    return (_iota(shape, len(shape) - 1) & HEAD_DIM) == 0


def _bd(x):
    lo = _lane_lo(x.shape)
    zero = jnp.zeros_like(x)
    return jnp.concatenate([jnp.where(lo, x, zero), jnp.where(lo, zero, x)], axis=0)


def _mm(a, b):
    return jnp.dot(a.astype(BF16), b.astype(BF16), preferred_element_type=F32)


def _mm_nt(a, b):
    return lax.dot_general(a.astype(BF16), b.astype(BF16), (((1,), (1,)), ((), ())),
                           preferred_element_type=F32)


def _split_bf16(x, terms):
    parts = []
    for _ in range(terms - 1):
        hi = x.astype(BF16)
        parts.append(hi)
        x = x - hi.astype(F32)
    parts.append(x.astype(BF16))
    return parts


def _mm_exact_left(t, x, terms=3):
    return sum(jnp.dot(t, piece, preferred_element_type=F32) for piece in _split_bf16(x, terms))


def _mm_exact_right(x, t, terms=3):
    return sum(jnp.dot(piece, t, preferred_element_type=F32) for piece in _split_bf16(x, terms))


def _group_sum(x, ones_bd):
    return jnp.dot(x.astype(BF16), ones_bd, preferred_element_type=F32)


def _rms_rows(x, g):
    return x * lax.rsqrt(jnp.mean(x * x, -1, keepdims=True) + NORM_EPS) * g


def _sigmoid(x):
    return 0.5 * jnp.tanh(0.5 * x) + 0.5


def _silu(x):
    return x * _sigmoid(x)


def _softplus(x):
    return jnp.maximum(x, 0.0) + jnp.log1p(jnp.exp(-jnp.abs(x)))


def _scan_order(nc, reverse):
    return list(range(nc - 1, -1, -1) if reverse else range(nc))


def _even_in_kernel(x_ref, g_ref, wa_ref, wr_ref, ones_ref, qg_ref, kg_ref, perm4_ref, perm16_ref,
                    qkv1_out, qkv4_out, qkv16_out, r_out, *, tm):
    ub = _rms_rows(x_ref[0], g_ref[...]).astype(BF16)
    a = jnp.dot(ub, wa_ref[...], preferred_element_type=F32)
    q = a[:, :WIDTH]
    k = a[:, WIDTH:2 * WIDTH]
    ones = ones_ref[...]
    inv = 1.0 / HEAD_DIM
    qn = q * lax.rsqrt(_group_sum(q * q, ones) * inv + NORM_EPS) * qg_ref[...]
    kn = k * lax.rsqrt(_group_sum(k * k, ones) * inv + NORM_EPS) * kg_ref[...]
    qkv = jnp.concatenate([qn * (HEAD_DIM ** -0.5 * LOG2_E), kn, a[:, 2 * WIDTH:]], axis=1).astype(BF16)
    qkv1_out[0, 0] = qkv
    for dil, perm_ref, out in ((4, perm4_ref, qkv4_out), (16, perm16_ref, qkv16_out)):
        perm = perm_ref[...]
        bs = perm.shape[0]
        per = bs // dil
        for blk in range(tm // bs):
            moved = jnp.dot(perm, qkv[blk * bs:(blk + 1) * bs],
                            preferred_element_type=F32).astype(BF16)
            for rho in range(dil):
                out[0, rho, blk * per:(blk + 1) * per, :] = moved[rho * per:(rho + 1) * per]
    r_out[0] = jnp.dot(ub, wr_ref[...], preferred_element_type=F32).astype(BF16)


def _even_in(x, g, wa, wr, ones_bd, qg, kg, tm=512):
    b, t, _ = x.shape
    const = lambda shape: pl.BlockSpec(shape, lambda bb, i: (0, 0))
    row = lambda n: pl.BlockSpec((1, tm, n), lambda bb, i: (bb, i, 0))
    res = lambda d: pl.BlockSpec((1, d, tm // d, 3 * WIDTH), lambda bb, i: (bb, 0, i, 0))
    dils = [d for _, d in DILATED_BRANCHES]

    def to_residue_major(d, bs):
        src = np.arange(bs).reshape(bs // d, d).T.reshape(-1)
        return jnp.asarray(src[:, None] == np.arange(bs)[None, :], BF16)

    blocks = {4: 2 * HALO_ROWS * 4, 16: HALO_ROWS * 16}

    return pl.pallas_call(
        functools.partial(_even_in_kernel, tm=tm),
        grid=(b, t // tm),
        in_specs=[row(D_MODEL), const((1, D_MODEL)), const((D_MODEL, 3 * WIDTH)),
                  const((D_MODEL, RWKV_COLS)), const((WIDTH, WIDTH)),
                  const((1, WIDTH)), const((1, WIDTH)),
                  const((blocks[4], blocks[4])), const((blocks[16], blocks[16]))],
        out_specs=[res(d) for d in dils] + [row(RWKV_COLS)],
        out_shape=[jax.ShapeDtypeStruct((b, d, t // d, 3 * WIDTH), BF16) for d in dils]
        + [jax.ShapeDtypeStruct((b, t, RWKV_COLS), BF16)],
        compiler_params=_cparams("parallel", "parallel"),
        name="even_in",
    )(x, g, wa, wr, ones_bd, qg, kg, to_residue_major(4, blocks[4]), to_residue_major(16, blocks[16]))


def _dil_attn_kernel(q_ref, k_ref, kl_ref, kr_ref, v_ref, vl_ref, vr_ref, o_ref, lse_ref,
                     kext, vext, *, tq, n_res, sub_len, dil):
    r = ATT_R
    i = pl.program_id(2)
    for z in range(n_res):
        kext[z, 0:r] = kl_ref[0, z]
        kext[z, r:r + tq] = k_ref[0, z]
        kext[z, r + tq:] = kr_ref[0, z]
        vext[z, 0:r] = vl_ref[0, z]
        vext[z, r:r + tq] = v_ref[0, z]
        vext[z, r + tq:] = vr_ref[0, z]

    c_io = _iota((r, 3 * r), 1)
    q_io = _iota((r, 3 * r), 0)
    rel = c_io - r - q_io
    near = jnp.abs(rel) <= r
    dist = jnp.abs(rel).astype(F32) * float(dil)
    lo = _lane_lo((r, LANES))
    near2 = jnp.concatenate([near, near], axis=0)
    dist2 = jnp.concatenate([dist, dist], axis=0)
    first_head = _iota((2 * r, 3 * r), 0) < r
    cols = [slice(p * LANES, (p + 1) * LANES) for p in range(N_PAIRS)]
    bias = [jnp.where(first_head, 2.0 ** (-(2 * p + 1)), 2.0 ** (-(2 * p + 2))) * (LOG2_E * dist2)
            for p in range(N_PAIRS)]

    blocks = [(z, j) for z in range(n_res) for j in range(tq // r)]
    group = 2 if len(blocks) % 2 == 0 else 1
    for g0 in range(0, len(blocks), group):
        inst = [(z, j, p) for z, j in blocks[g0:g0 + group] for p in range(N_PAIRS)]
        valid = {}
        for z, j in blocks[g0:g0 + group]:
            kpos = i * tq + j * r + _iota((2 * r, 3 * r), 1) - r
            valid[j] = near2 & (kpos >= 0) & (kpos < sub_len)
        s = [jnp.where(valid[j],
                       _mm_nt(_bd(q_ref[0, z, j * r:(j + 1) * r, cols[p]]),
                              kext[z, j * r:(j + 3) * r, cols[p]]) - bias[p], NEG_BIG)
             for z, j, p in inst]
        m = [jnp.max(u, -1, keepdims=True) for u in s]
        e = [jnp.exp2(u - mm) for u, mm in zip(s, m)]
        den = [jnp.sum(u, -1, keepdims=True) for u in e]
        pv = [_mm(e[n], vext[z, j * r:(j + 3) * r, cols[p]]) / den[n]
              for n, (z, j, p) in enumerate(inst)]
        lse = [m[n] * (1.0 / LOG2_E) + jnp.log(den[n]) for n in range(len(inst))]
        for n, (z, j, p) in enumerate(inst):
            o_ref[0, z, j * r:(j + 1) * r, cols[p]] = jnp.where(lo, pv[n][:r], pv[n][r:]).astype(BF16)
        lane = _iota((r, LANES), 1)
        for bi, (z, j) in enumerate(blocks[g0:g0 + group]):
            tile = jnp.zeros((r, LANES), F32)
            for p in range(N_PAIRS):
                col = lse[bi * N_PAIRS + p]
                for hh in range(2):
                    h0 = (2 * p + hh) * LSE_LANES
                    tile = jnp.where((lane >= h0) & (lane < h0 + LSE_LANES),
                                     col[hh * r:(hh + 1) * r], tile)
            lse_ref[0, z, j * r:(j + 1) * r, :] = tile


def _dil_attn(qkv, dil, rows_per_step=1024):
    b, _, sub_len, _ = qkv.shape
    tq = min(sub_len, rows_per_step)
    n_res = min(dil, rows_per_step // tq)
    nblk = sub_len // ATT_R
    per = tq // ATT_R
    main = lambda col: pl.BlockSpec((1, n_res, tq, WIDTH), lambda bb, rr, i: (bb, rr, i, col))
    left = lambda col: pl.BlockSpec((1, n_res, ATT_R, WIDTH),
                                    lambda bb, rr, i: (bb, rr, jnp.maximum(i * per - 1, 0), col))
    right = lambda col: pl.BlockSpec((1, n_res, ATT_R, WIDTH),
                                     lambda bb, rr, i: (bb, rr, jnp.minimum((i + 1) * per, nblk - 1), col))
    return pl.pallas_call(
        functools.partial(_dil_attn_kernel, tq=tq, n_res=n_res, sub_len=sub_len, dil=dil),
        grid=(b, dil // n_res, sub_len // tq),
        in_specs=[main(0), main(1), left(1), right(1), main(2), left(2), right(2)],
        out_specs=[main(0), pl.BlockSpec((1, n_res, tq, LANES), lambda bb, rr, i: (bb, rr, i, 0))],
        out_shape=[jax.ShapeDtypeStruct((b, dil, sub_len, WIDTH), BF16),
                   jax.ShapeDtypeStruct((b, dil, sub_len, LANES), F32)],
        scratch_shapes=[pltpu.VMEM((n_res, tq + 2 * ATT_R, WIDTH), BF16)] * 2,
        compiler_params=_cparams("parallel", "parallel", "parallel"),
        name=f"dil_attn_d{dil}",
    )(qkv, qkv, qkv, qkv, qkv, qkv, qkv)


def _rwkv_prep_kernel(x_ref, xp_ref, xn_ref, mup_ref, mun_ref, w0_ref, w2_ref, a0_ref, a2_ref,
                      g2_ref, kk_ref, ka_ref, rk_ref, ones_ref, trif_ref, trir_ref,
                      r_out, v_out, kkn_out, g_out, bonus_out, g_in_out, krep_out, b_out, *, tm):
    i = pl.program_id(1)
    nt = pl.num_programs(1)
    x = x_ref[0].astype(F32)
    prow = jnp.where(i > 0, xp_ref[0, HALO_ROWS - 1:HALO_ROWS, :].astype(F32), 0.0)
    nrow = jnp.where(i < nt - 1, xn_ref[0, 0:1, :].astype(F32), 0.0)
    rid = _iota((8, 1), 0)
    prev = pltpu.roll(x, 1, 0)
    prev = jnp.concatenate([jnp.where(rid == 0, prow, prev[:8]), prev[8:]], axis=0)
    nxt = pltpu.roll(x, tm - 1, 0)
    nxt = jnp.concatenate([nxt[:tm - 8], jnp.where(rid == 7, nrow, nxt[tm - 8:])], axis=0)
    xs = x + mup_ref[...] * (prev - x) + mun_ref[...] * (nxt - x)
    r = xs[:, 0:WIDTH]
    k = xs[:, WIDTH:2 * WIDTH]
    v = xs[:, 2 * WIDTH:3 * WIDTH]
    zw = xs[:, 3 * WIDTH:3 * WIDTH + LANES]
    za = xs[:, 3 * WIDTH + LANES:3 * WIDTH + 2 * LANES]
    zg = xs[:, 3 * WIDTH + 2 * LANES:]
    ones = ones_ref[...]
    kk = k * kk_ref[...]
    kkn = kk * lax.rsqrt(jnp.maximum(_group_sum(kk * kk, ones), 1e-24))
    g_out[0] = _mm(_sigmoid(zg), g2_ref[...]).astype(BF16)
    r_out[0] = r.astype(BF16)
    v_out[0] = v.astype(BF16)
    kkn_out[0] = kkn.astype(BF16)
    bonus_out[0] = (_group_sum(r * k * rk_ref[...], ones) * v).astype(BF16)
    wx = w0_ref[...] + _mm(jnp.tanh(zw), w2_ref[...])
    ax = a0_ref[...] + _mm(za, a2_ref[...])
    for d, tri_ref in enumerate((trif_ref, trir_ref)):
        cols = slice(d * WIDTH, (d + 1) * WIDTH)
        lw = -math.exp(-0.5) * _sigmoid(wx[:, cols])
        tri = tri_ref[...]
        for c in range(tm // RWKV_CHUNK):
            rows = slice(c * RWKV_CHUNK, (c + 1) * RWKV_CHUNK)
            g_in_out[d, 0, rows, :] = _mm_exact_left(tri, lw[rows], terms=2)
        a = _sigmoid(ax[:, cols])
        krep_out[d, 0] = (k * (1.0 + (a - 1.0) * ka_ref[...])).astype(BF16)
        b_out[d, 0] = (kkn * a).astype(BF16)


def _rwkv_prep(rcols, p, tri, tm=512):
    b, t, _ = rcols.shape
    nt = t // tm
    per = tm // HALO_ROWS
    const = lambda shape: pl.BlockSpec(shape, lambda bb, i: (0,) * len(shape))
    main = lambda n: pl.BlockSpec((1, tm, n), lambda bb, i: (bb, i, 0))
    dirs = pl.BlockSpec((2, 1, tm, WIDTH), lambda bb, i: (0, bb, i, 0))
    prev = pl.BlockSpec((1, HALO_ROWS, RWKV_COLS), lambda bb, i: (bb, jnp.maximum(i * per - 1, 0), 0))
    nxt = pl.BlockSpec((1, HALO_ROWS, RWKV_COLS),
                       lambda bb, i: (bb, jnp.minimum((i + 1) * per, t // HALO_ROWS - 1), 0))
    one = jax.ShapeDtypeStruct((b, t, WIDTH), BF16)
    two = lambda dt: jax.ShapeDtypeStruct((2, b, t, WIDTH), dt)
    return pl.pallas_call(
        functools.partial(_rwkv_prep_kernel, tm=tm),
        grid=(b, nt),
        in_specs=[main(RWKV_COLS), prev, nxt, const((1, RWKV_COLS)), const((1, RWKV_COLS)),
                  const((1, 2 * WIDTH)), const((LANES, 2 * WIDTH)),
                  const((1, 2 * WIDTH)), const((LANES, 2 * WIDTH)),
                  const((LANES, WIDTH)), const((1, WIDTH)), const((1, WIDTH)), const((1, WIDTH)),
                  const((WIDTH, WIDTH)), const((RWKV_CHUNK, RWKV_CHUNK)),
                  const((RWKV_CHUNK, RWKV_CHUNK))],
        out_specs=[main(WIDTH)] * 5 + [dirs] * 3,
        out_shape=[one] * 5 + [two(F32), two(BF16), two(BF16)],
        compiler_params=_cparams("parallel", "parallel"),
        name="rwkv_prep",
    )(rcols, rcols, rcols, p["mu_prev"], p["mu_next"], p["w0"], p["w2"], p["a0"], p["a2"],
      p["g2"], p["k_k"], p["k_a"], p["r_k"], p["ones_bd"], tri[0], tri[1])


def _rwkv_scan_kernel(r_ref, v_ref, kk_ref, g_ref, k_ref, b_ref, *rest, nc, reverse):
    c_len = RWKV_CHUNK
    add_ref, y_ref, s_ref = rest if reverse else (None,) + rest

    @pl.when(pl.program_id(1) == 0)
    def _():
        s_ref[...] = jnp.zeros_like(s_ref)

    row = _iota((c_len, LANES), 0)
    li = _iota((c_len, LANES), 1) & (HEAD_DIM - 1)
    strict = (li > row) if reverse else (li < row)
    incl = (li >= row) if reverse else (li <= row)
    r2 = _iota((LANES, LANES), 0)
    c2 = _iota((LANES, LANES), 1)
    same_head = (r2 & HEAD_DIM) == (c2 & HEAD_DIM)
    diag = r2 == c2
    zeros = jnp.zeros((c_len, LANES), F32)

    order = _scan_order(nc, reverse)
    inst = [(p, c) for c in order for p in range(N_PAIRS)]
    n = len(inst)

    def tile(ref, p, c, lead=()):
        return ref[lead + (0, slice(c * c_len, (c + 1) * c_len), slice(p * LANES, (p + 1) * LANES))]

    g_in = [tile(g_ref, p, c, (0,)) for p, c in inst]
    g_tot = [g[0:1] if reverse else g[c_len - 1:c_len] for g in g_in]
    scan_first = row == (c_len - 1 if reverse else 0)
    g_ex = [jnp.where(scan_first, 0.0, pltpu.roll(g, c_len - 1 if reverse else 1, 0)) for g in g_in]
    v = [tile(v_ref, p, c) for p, c in inst]
    at, rt, w, bk_end = [], [], [], []
    for i, (p, c) in enumerate(inst):
        b = tile(b_ref, p, c, (0,)).astype(F32)
        k = tile(k_ref, p, c, (0,)).astype(F32)
        at.append(-tile(kk_ref, p, c).astype(F32) * jnp.exp(g_ex[i]))
        rt.append(tile(r_ref, p, c).astype(F32) * jnp.exp(g_in[i]))
        inv = jnp.exp(-g_in[i])
        to_end = jnp.exp(g_tot[i] - g_in[i])
        bk_end.append(jnp.concatenate([b * to_end, k * to_end], 0))
        w.append(_mm_nt(jnp.concatenate([at[i], rt[i]], 0),
                        jnp.concatenate([_bd(b * inv), _bd(k * inv)], 0)))

    bdv = [_bd(u) for u in v]
    m_ab = [jnp.where(strict, u[:c_len, :LANES], 0.0) for u in w]
    kv = [_mm(jnp.concatenate([jnp.where(strict, w[i][:c_len, LANES:], 0.0),
                               jnp.where(incl, w[i][c_len:, LANES:], 0.0)], 0), bdv[i])
          for i in range(n)]
    rhs = [jnp.concatenate([at[i], kv[i][:c_len]], axis=1) for i in range(n)]
    tinv = [jnp.where(li == row, 1.0, 0.0) + u for u in m_ab]
    mj = [u.astype(BF16) for u in m_ab]
    mj = [_mm(u, _bd(u)).astype(BF16) for u in mj]
    for j in range(1, 5):
        both = [_mm(jnp.concatenate([mj[i], tinv[i].astype(BF16)], 0), _bd(mj[i])) for i in range(n)]
        mj = [u[:c_len].astype(BF16) for u in both]
        tinv = [tinv[i] + both[i][c_len:] for i in range(n)]
    tinv = [tinv[i] + _mm(tinv[i], _bd(mj[i])) for i in range(n)]
    x = [_mm(tinv[i], _bd(rhs[i])) for i in range(n)]

    t1 = [_mm(jnp.where(incl, w[i][c_len:, :LANES], 0.0), _bd(x[i])) for i in range(n)]
    y0 = [t1[i][:, LANES:] + kv[i][c_len:] for i in range(n)]
    rp = [(rt[i] + t1[i][:, :LANES]).astype(BF16) for i in range(n)]
    t2 = [_mm(bk_end[i].T,
              jnp.concatenate([x[i], jnp.concatenate([zeros, v[i].astype(F32)], axis=1)], 0))
          for i in range(n)]
    g_low = [jnp.where(same_head, u[:, :LANES], 0.0).astype(BF16) for u in t2]
    h_new = [jnp.where(same_head, u[:, LANES:], 0.0) for u in t2]
    g_col = [jnp.exp(jnp.sum(jnp.where(diag, jnp.broadcast_to(g, (LANES, LANES)), 0.0),
                             axis=1, keepdims=True)) for g in g_tot]

    s = [s_ref[p] for p in range(N_PAIRS)]
    for ci, c in enumerate(order):
        ids = [ci * N_PAIRS + p for p in range(N_PAIRS)]
        both = [_mm(jnp.concatenate([rp[i], g_low[i]], 0), s[p]) for p, i in enumerate(ids)]
        for p, i in enumerate(ids):
            where = (0, slice(c * c_len, (c + 1) * c_len), slice(p * LANES, (p + 1) * LANES))
            y = both[p][:c_len] + y0[i]
            if reverse:
                y = y + add_ref[where].astype(F32)
            y_ref[where] = y.astype(BF16)
        s = [s[p] * g_col[i] + both[p][c_len:] + h_new[i] for p, i in enumerate(ids)]
    for p in range(N_PAIRS):
        s_ref[p] = s[p]


def _rwkv_scan(r, v, kkn, g_in, krep, bvec, direction, other=(), tb=1024):
    b, t, _ = r.shape
    nt = t // tb
    reverse = direction == 1
    tmap = (lambda i: nt - 1 - i) if reverse else (lambda i: i)
    one = pl.BlockSpec((1, tb, WIDTH), lambda bb, i: (bb, tmap(i), 0))
    two = pl.BlockSpec((1, 1, tb, WIDTH), lambda bb, i: (direction, bb, tmap(i), 0))
    return pl.pallas_call(
        functools.partial(_rwkv_scan_kernel, nc=tb // RWKV_CHUNK, reverse=reverse),
        grid=(b, nt),
        in_specs=[one, one, one, two, two, two] + [one] * len(other),
        out_specs=one,
        out_shape=jax.ShapeDtypeStruct((b, t, WIDTH), BF16),
        scratch_shapes=[pltpu.VMEM((N_PAIRS, LANES, LANES), F32)],
        compiler_params=_cparams("parallel", "arbitrary"),
        name=f"rwkv_scan_{'bwd' if reverse else 'fwd'}",
    )(r, v, kkn, g_in, krep, bvec, *other)


def _even_out_kernel(h_ref, o1, l1, o4, l4, o16, l16, y_both, bonus, g, lng, lnb, ones_ref, spread_ref,
                     w_ref, out_ref, tok_sc, lse_sc, *, tm):
    n_tiles = WIDTH // LANES
    for slot, (dil, o_ref, l_ref) in enumerate(((4, o4, l4), (16, o16, l16))):
        for rho in range(dil):
            rows = pl.ds(rho, tm // dil, stride=dil)
            lse_sc[slot, rows, :] = l_ref[0, rho]
            for c in range(n_tiles):
                tok_sc[slot * n_tiles + c, rows, :] = o_ref[0, rho, :, c * LANES:(c + 1) * LANES].astype(F32)
    tok = lambda slot: jnp.concatenate([tok_sc[slot * n_tiles + c] for c in range(n_tiles)], axis=1)
    la, lb, lc = l1[0, 0], lse_sc[0], lse_sc[1]
    mx = jnp.maximum(jnp.maximum(la, lb), lc)
    wa, wb, wc = jnp.exp(la - mx), jnp.exp(lb - mx), jnp.exp(lc - mx)
    norm = 1.0 / (wa + wb + wc)
    spread = lambda wgt: jnp.dot((wgt * norm).astype(BF16), spread_ref[...], preferred_element_type=F32)
    y_a = spread(wa) * o1[0, 0].astype(F32) + spread(wb) * tok(0) + spread(wc) * tok(1)
    ones = ones_ref[...]
    inv = 1.0 / HEAD_DIM
    y = y_both[0].astype(F32)
    yc = y - _group_sum(y, ones) * inv
    var = _group_sum(yc * yc, ones) * inv
    yn = yc * lax.rsqrt(var + RWKV_GN_EPS) * lng[...] + lnb[...]
    y_b = (yn + bonus[0].astype(F32)) * g[0].astype(F32)
    out_ref[0] = (h_ref[0] + _mm(y_a, w_ref[0:WIDTH, :]) + _mm(y_b, w_ref[WIDTH:, :]))


def _even_out(h, branch, rwkv_parts, lng, lnb, ones_bd, w, tm=512):
    b, t, _ = h.shape
    const = lambda shape: pl.BlockSpec(shape, lambda bb, i: (0, 0))
    row = lambda n: pl.BlockSpec((1, tm, n), lambda bb, i: (bb, i, 0))
    res = lambda d, n: pl.BlockSpec((1, d, tm // d, n), lambda bb, i: (bb, 0, i, 0))
    dils = [d for _, d in DILATED_BRANCHES]
    spread = jnp.asarray(np.arange(LANES)[:, None] == (np.arange(WIDTH)[None, :] // HEAD_DIM) * LSE_LANES, BF16)
    return pl.pallas_call(
        functools.partial(_even_out_kernel, tm=tm),
        grid=(b, t // tm),
        in_specs=[row(D_MODEL)] + [res(d, n) for d in dils for n in (WIDTH, LANES)] + [row(WIDTH)] * 3
        + [const((1, WIDTH)), const((1, WIDTH)), const((WIDTH, WIDTH)), const((LANES, WIDTH)),
           const((D_MODEL, D_MODEL))],
        out_specs=row(D_MODEL),
        out_shape=jax.ShapeDtypeStruct((b, t, D_MODEL), F32),
        scratch_shapes=[pltpu.VMEM((2 * WIDTH // LANES, tm, LANES), F32), pltpu.VMEM((2, tm, LANES), F32)],
        compiler_params=_cparams("parallel", "parallel"),
        name="even_out",
    )(h, *branch, *rwkv_parts, lng, lnb, ones_bd, spread, w)


def _mlp_kernel(h_ref, g_ref, w1_ref, w2_ref, out_ref, u_ref):
    f = pl.program_id(1)

    @pl.when(f == 0)
    def _():
        x = h_ref[...]
        u_ref[...] = _rms_rows(x, g_ref[...]).astype(BF16)
        out_ref[...] = x

    hdn = jnp.dot(u_ref[...], w1_ref[...], preferred_element_type=F32)
    hdn = jnp.square(jnp.maximum(hdn, 0.0))
    out_ref[...] += jnp.dot(hdn.astype(BF16), w2_ref[...], preferred_element_type=F32)


def _mlp(h2, g, w1, w2, tm=1024, tf=2048):
    m = h2.shape[0]
    return pl.pallas_call(
        _mlp_kernel,
        grid=(m // tm, D_FF // tf),
        in_specs=[pl.BlockSpec((tm, D_MODEL), lambda i, f: (i, 0)),
                  pl.BlockSpec((1, D_MODEL), lambda i, f: (0, 0)),
                  pl.BlockSpec((D_MODEL, tf), lambda i, f: (0, f)),
                  pl.BlockSpec((tf, D_MODEL), lambda i, f: (f, 0))],
        out_specs=pl.BlockSpec((tm, D_MODEL), lambda i, f: (i, 0)),
        out_shape=jax.ShapeDtypeStruct((m, D_MODEL), F32),
        scratch_shapes=[pltpu.VMEM((tm, D_MODEL), BF16)],
        compiler_params=_cparams("parallel", "arbitrary"),
        name="mlp",
    )(h2, g, w1, w2)


def _rotary_pair(u, cos, sin_signed):
    first = (_iota(u.shape, 1) & (HEAD_DIM - 1)) < HEAD_DIM // 2
    swapped = jnp.where(first, pltpu.roll(u, LANES - HEAD_DIM // 2, 1),
                        pltpu.roll(u, HEAD_DIM // 2, 1))
    return u * cos + swapped * sin_signed


def _odd_in_kernel(x_ref, g_ref, wr_ref, ws_ref, wd_ref, cos_ref, sin_ref,
                   ret_out, xz_out, dt_out, dtt_out):
    ub = _rms_rows(x_ref[...], g_ref[...]).astype(BF16)
    ret = jnp.dot(ub, wr_ref[...], preferred_element_type=F32)
    cos = cos_ref[...]
    sin = sin_ref[...]
    for c in range(2 * N_PAIRS):
        cols = slice(c * LANES, (c + 1) * LANES)
        scale = 1.0 if c < N_PAIRS else HEAD_DIM ** -0.5
        ret_out[:, cols] = (_rotary_pair(ret[:, cols], cos, sin) * scale).astype(BF16)
    ret_out[:, 2 * WIDTH:] = ret[:, 2 * WIDTH:].astype(BF16)
    xz_out[...] = jnp.dot(ub, ws_ref[...], preferred_element_type=F32).astype(BF16)
    dt = jnp.dot(ub, wd_ref[...], preferred_element_type=F32)
    dt_out[...] = dt
    dtt_out[...] = dt.T[:2 * N_HEADS, :]


def _odd_in(x2, g, wr, ws, wd, cos, sin, tm=512):
    m = x2.shape[0]
    per_seq = cos.shape[0] // tm
    const = lambda shape: pl.BlockSpec(shape, lambda i: (0, 0))
    row = lambda n: pl.BlockSpec((tm, n), lambda i: (i, 0))
    tab = pl.BlockSpec((tm, LANES), lambda i: (i % per_seq, 0))
    n_xz = SSD_XBC + WIDTH
    return pl.pallas_call(
        _odd_in_kernel,
        grid=(m // tm,),
        in_specs=[row(D_MODEL), const((1, D_MODEL)), const((D_MODEL, 4 * WIDTH)),
                  const((D_MODEL, n_xz)), const((D_MODEL, LANES)), tab, tab],
        out_specs=[row(4 * WIDTH), row(n_xz), row(LANES),
                   pl.BlockSpec((2 * N_HEADS, tm), lambda i: (0, i))],
        out_shape=[jax.ShapeDtypeStruct((m, 4 * WIDTH), BF16),
                   jax.ShapeDtypeStruct((m, n_xz), BF16),
                   jax.ShapeDtypeStruct((m, LANES), F32),
                   jax.ShapeDtypeStruct((2 * N_HEADS, m), F32)],
        compiler_params=_cparams("parallel"),
        name="odd_in",
    )(x2, g, wr, ws, wd, cos, sin)


def _ret_kernel(q_ref, k_ref, v_ref, lg64_ref, lg128_ref, *rest, nc, reverse):
    c_len = LIN_CHUNK
    add_ref, y_ref, s_ref = rest if reverse else (None,) + rest

    @pl.when(pl.program_id(1) == 0)
    def _():
        s_ref[...] = jnp.zeros_like(s_ref)

    l_io = _iota((c_len, 2 * c_len), 0)
    s_io = _iota((c_len, 2 * c_len), 1) & (c_len - 1)
    ahead = jnp.maximum(l_io - s_io, 0).astype(F32)
    behind = jnp.maximum(s_io - l_io, 0).astype(F32)
    pos = _iota((c_len, LANES), 0)
    if reverse:
        pos = c_len - 1 - pos
    pos = pos.astype(F32)
    r2 = _iota((LANES, LANES), 0)
    c2 = _iota((LANES, LANES), 1)
    same_head = (r2 & HEAD_DIM) == (c2 & HEAD_DIM)

    decay, from_start, to_end, whole = [], [], [], []
    for p in range(N_PAIRS):
        lg64 = lg64_ref[0, :, p * LANES:(p + 1) * LANES]
        if not reverse:
            wide = slice(2 * p * c_len, 2 * (p + 1) * c_len)
            decay.append(jnp.where(l_io >= s_io, jnp.exp(lg128_ref[0, :, wide] * ahead), 0.0)
                         + jnp.where(s_io >= l_io, jnp.exp(lg128_ref[1, :, wide] * behind), 0.0))
        from_start.append(jnp.exp(lg64 * (pos + 1.0)))
        to_end.append(jnp.exp(lg64 * (float(c_len - 1) - pos)))
        whole.append(jnp.exp(lg64 * float(c_len)))

    order = _scan_order(nc, reverse)
    inst = [(p, c) for c in order for p in range(N_PAIRS)]
    q, k, v = [], [], []
    for p, c in inst:
        rows = slice(c * c_len, (c + 1) * c_len)
        cols = slice(p * LANES, (p + 1) * LANES)
        q.append(q_ref[0, rows, cols])
        k.append(k_ref[0, rows, cols])
        v.append(v_ref[0, rows, cols])
    if not reverse:
        sc = [(_mm_nt(q[i], _bd(k[i])) * decay[p]).astype(BF16) for i, (p, c) in enumerate(inst)]
        y_intra = [_mm(sc[i], _bd(v[i])) for i in range(len(inst))]
    upd = [jnp.where(same_head, _mm((k[i] * to_end[p]).T, v[i]), 0.0) for i, (p, c) in enumerate(inst)]
    qs = [(q[i] * from_start[p]).astype(BF16) for i, (p, c) in enumerate(inst)]

    s = [s_ref[p] for p in range(N_PAIRS)]
    for ci, c in enumerate(order):
        for p in range(N_PAIRS):
            i = ci * N_PAIRS + p
            where = (0, slice(c * c_len, (c + 1) * c_len), slice(p * LANES, (p + 1) * LANES))
            y = _mm(qs[i], s[p])
            y = y + (add_ref[where].astype(F32) if reverse else y_intra[i])
            y_ref[where] = y.astype(BF16)
            s[p] = s[p] * whole[p] + upd[i]
    for p in range(N_PAIRS):
        s_ref[p] = s[p]


def _retention(ret, lg64, lg128, direction, other=(), tb=1024):
    b, t, _ = ret.shape
    nt = t // tb
    reverse = direction == 1
    tmap = (lambda i: nt - 1 - i) if reverse else (lambda i: i)
    col = lambda blk: pl.BlockSpec((1, tb, WIDTH), lambda bb, i: (bb, tmap(i), blk))
    return pl.pallas_call(
        functools.partial(_ret_kernel, nc=tb // LIN_CHUNK, reverse=reverse),
        grid=(b, nt),
        in_specs=[col(0), col(1), col(2),
                  pl.BlockSpec((1, 1, WIDTH), lambda bb, i: (direction, 0, 0)),
                  pl.BlockSpec((2, 1, N_HEADS * LIN_CHUNK), lambda bb, i: (0, 0, 0))]
        + [col(0)] * len(other),
        out_specs=col(0),
        out_shape=jax.ShapeDtypeStruct((b, t, WIDTH), BF16),
        scratch_shapes=[pltpu.VMEM((N_PAIRS, LANES, LANES), F32)],
        compiler_params=_cparams("parallel", "arbitrary"),
        name=f"retention_{'bwd' if reverse else 'fwd'}",
    )(ret, ret, ret, lg64, lg128, *other)


def _ssd_conv_kernel(x_ref, xp_ref, xn_ref, w_ref, b_ref, xs_out, bc_out, *, tm):
    i = pl.program_id(1)
    nt = pl.num_programs(1)
    prev = jnp.where(i > 0, xp_ref[0].astype(F32), 0.0)
    nxt = jnp.where(i < nt - 1, xn_ref[0].astype(F32), 0.0)
    xe = jnp.concatenate([prev, x_ref[0].astype(F32), nxt], axis=0)
    ext = tm + 2 * HALO_ROWS
    pad = SSD_CONV // 2
    acc = jnp.zeros((tm, SSD_XBC), F32) + b_ref[...]
    for j in range(SSD_CONV):
        shift = (pad - j) % ext
        xj = xe if shift == 0 else pltpu.roll(xe, shift, 0)
        acc = acc + xj[HALO_ROWS:HALO_ROWS + tm] * w_ref[j:j + 1, :]
    y = _silu(acc).astype(BF16)
    xs_out[0] = y[:, :WIDTH]
    bc_out[0] = y[:, WIDTH:]


def _ssd_conv(xz, w, bias, tm=512):
    b, t, _ = xz.shape
    nt = t // tm
    per = tm // HALO_ROWS
    const = lambda shape: pl.BlockSpec(shape, lambda bb, i: (0, 0))
    main = pl.BlockSpec((1, tm, SSD_XBC), lambda bb, i: (bb, i, 0))
    prev = pl.BlockSpec((1, HALO_ROWS, SSD_XBC), lambda bb, i: (bb, jnp.maximum(i * per - 1, 0), 0))
    nxt = pl.BlockSpec((1, HALO_ROWS, SSD_XBC),
                       lambda bb, i: (bb, jnp.minimum((i + 1) * per, t // HALO_ROWS - 1), 0))
    out = pl.BlockSpec((1, tm, WIDTH), lambda bb, i: (bb, i, 0))
    return pl.pallas_call(
        functools.partial(_ssd_conv_kernel, tm=tm),
        grid=(b, nt),
        in_specs=[main, prev, nxt, const((8, SSD_XBC)), const((1, SSD_XBC))],
        out_specs=[out, out],
        out_shape=[jax.ShapeDtypeStruct((b, t, WIDTH), BF16)] * 2,
        compiler_params=_cparams("parallel", "parallel"),
        name="ssd_conv",
    )(xz, xz, xz, w, bias)


def _ssd_kernel(xs_ref, bc_ref, dtc_ref, dtr_ref, bias_c_ref, bias_r_ref, nega_c_ref, nega_r_ref,
                tri_ref, trit_ref, *rest, nc, reverse):
    c_len = LIN_CHUNK
    add_ref, y_ref, s_ref = rest if reverse else (None,) + rest
    per_group = N_HEADS // SSD_GROUPS
    gw = WIDTH // SSD_GROUPS

    @pl.when(pl.program_id(1) == 0)
    def _():
        s_ref[...] = jnp.zeros_like(s_ref)

    tri = tri_ref[...]
    trit = trit_ref[...]
    own = slice(N_HEADS, 2 * N_HEADS) if reverse else slice(0, N_HEADS)
    dts_c = _softplus(dtc_ref[0][:, own] + bias_c_ref[0])
    la_c = dts_c * nega_c_ref[0]
    dts_r = _softplus(dtr_ref[0] + bias_r_ref[0])
    la_r = dts_r * nega_r_ref[0]
    l_io = _iota((c_len, c_len), 0)
    s_io = _iota((c_len, c_len), 1)
    causal = (s_io >= l_io) if reverse else (s_io <= l_io)
    lo = _lane_lo((c_len, LANES))

    def by_head128(cols, g):
        return jnp.concatenate([jnp.broadcast_to(cols[:, g * per_group + h:g * per_group + h + 1],
                                                 (c_len, LANES)) for h in range(per_group)], axis=1)

    def to64(wide):
        h = [wide[:, j * LANES:(j + 1) * LANES] for j in range(per_group)]
        return jnp.concatenate([jnp.where(lo, h[0], h[1]), jnp.where(lo, h[2], h[3])], axis=1)

    order = _scan_order(nc, reverse)
    inst = [(g, c) for c in order for g in range(SSD_GROUPS)]
    n = len(inst)
    rows_of = lambda c: slice(c * c_len, (c + 1) * c_len)
    cum = [_mm_exact_left(tri, by_head128(la_c[rows_of(c)], g), terms=2) for g, c in inst]
    cum_r = {c: _mm_exact_right(la_r[:, rows_of(c)], trit, terms=2) for c in order}
    last = [u[0:1] if reverse else u[c_len - 1:c_len] for u in cum]
    bm = [bc_ref[0, rows_of(c), g * SSD_STATE:(g + 1) * SSD_STATE] for g, c in inst]
    cm = [bc_ref[0, rows_of(c), (SSD_GROUPS + g) * SSD_STATE:(SSD_GROUPS + g + 1) * SSD_STATE]
          for g, c in inst]
    xdt = [xs_ref[0, rows_of(c), g * gw:(g + 1) * gw].astype(F32) * to64(by_head128(dts_c[rows_of(c)], g))
           for g, c in inst]
    gm = [jnp.where(causal, _mm_nt(cm[i], bm[i]), 0.0) for i in range(n)]
    y_intra = []
    for i, (g, c) in enumerate(inst):
        parts = []
        for pp in range(per_group // 2):
            wide = []
            for hh in (2 * pp, 2 * pp + 1):
                head = g * per_group + hh
                diff = cum[i][:, hh * LANES:(hh + 1) * LANES] - cum_r[c][head:head + 1, :]
                wide.append((gm[i] * jnp.exp(jnp.minimum(diff, 0.0))).astype(BF16))
            parts.append(_mm(jnp.concatenate(wide, axis=1), _bd(xdt[i][:, pp * LANES:(pp + 1) * LANES])))
        y_intra.append(jnp.concatenate(parts, axis=1))
    from_start = [to64(jnp.exp(u)) for u in cum]
    upd = [_mm(bm[i].astype(F32).T, xdt[i] * to64(jnp.exp(last[i] - cum[i]))) for i in range(n)]
    whole = [to64(jnp.exp(u)) for u in last]

    s = [s_ref[g] for g in range(SSD_GROUPS)]
    for ci, c in enumerate(order):
        for g in range(SSD_GROUPS):
            i = ci * SSD_GROUPS + g
            where = (0, rows_of(c), slice(g * gw, (g + 1) * gw))
            y = y_intra[i] + _mm(cm[i], s[g]) * from_start[i]
            if reverse:
                y = y + add_ref[where].astype(F32)
            y_ref[where] = y.astype(BF16)
            s[g] = s[g] * whole[i] + upd[i]
    for g in range(SSD_GROUPS):
        s_ref[g] = s[g]


def _ssd_scan(xs, bc, dtc, dtr, sp, direction, other=(), tb=1024):
    b, t, _ = xs.shape
    nt = t // tb
    reverse = direction == 1
    tmap = (lambda i: nt - 1 - i) if reverse else (lambda i: i)
    row = pl.BlockSpec((1, tb, WIDTH), lambda bb, i: (bb, tmap(i), 0))
    par = lambda shape: pl.BlockSpec((1,) + shape, lambda bb, i: (direction, 0, 0))
    tri = pl.BlockSpec((LIN_CHUNK, LIN_CHUNK), lambda bb, i: (0, 0))
    return pl.pallas_call(
        functools.partial(_ssd_kernel, nc=tb // LIN_CHUNK, reverse=reverse),
        grid=(b, nt),
        in_specs=[row, row,
                  pl.BlockSpec((1, tb, LANES), lambda bb, i: (bb, tmap(i), 0)),
                  pl.BlockSpec((1, N_HEADS, tb), lambda bb, i: (direction, 0, bb * nt + tmap(i))),
                  par((1, N_HEADS)), par((N_HEADS, 1)), par((1, N_HEADS)), par((N_HEADS, 1)),
                  tri, tri] + [row] * len(other),
        out_specs=row,
        out_shape=jax.ShapeDtypeStruct((b, t, WIDTH), BF16),
        scratch_shapes=[pltpu.VMEM((SSD_GROUPS, SSD_STATE, WIDTH // SSD_GROUPS), F32)],
        compiler_params=_cparams("parallel", "arbitrary"),
        name=f"ssd_scan_{'bwd' if reverse else 'fwd'}",
    )(xs, bc, dtc, dtr, sp["bias_c"], sp["bias_r"], sp["nega_c"], sp["nega_r"],
      sp["tri_rev" if reverse else "tri_fwd"], sp["trit_rev" if reverse else "trit_fwd"], *other)


def _odd_out_kernel(h_ref, ret_y, gate, ssd_y, xs, z, gng, gnb, dsk, ng, ones_ref, w_ref, out_ref):
    f32 = lambda ref: ref[...].astype(F32)
    ones = ones_ref[...]
    inv = 1.0 / HEAD_DIM
    y = f32(ret_y)
    yc = y - _group_sum(y, ones) * inv
    var = _group_sum(yc * yc, ones) * inv
    y_c = (yc * lax.rsqrt(var + GN_EPS) * gng[...] + gnb[...]) * _silu(f32(gate))
    yd = (f32(ssd_y) + dsk[...] * f32(xs)) * _silu(f32(z))
    gw = WIDTH // SSD_GROUPS
    halves = []
    for gi in range(SSD_GROUPS):
        yg = yd[:, gi * gw:(gi + 1) * gw]
        halves.append(yg * lax.rsqrt(jnp.mean(yg * yg, -1, keepdims=True) + NORM_EPS))
    y_d = jnp.concatenate(halves, axis=1) * ng[...]
    out_ref[...] = h_ref[...] + _mm(y_c, w_ref[0:WIDTH, :]) + _mm(y_d, w_ref[WIDTH:, :])


def _odd_out(h2, ret_y, ret2, ssd_y, xs, xz2, gng, gnb, dsk, ng, ones_bd, w, tm=512):
    m = h2.shape[0]
    const = lambda shape: pl.BlockSpec(shape, lambda i: (0, 0))
    row = lambda n: pl.BlockSpec((tm, n), lambda i: (i, 0))
    vec = const((1, WIDTH))
    return pl.pallas_call(
        _odd_out_kernel,
        grid=(m // tm,),
        in_specs=[row(D_MODEL), row(WIDTH),
                  pl.BlockSpec((tm, WIDTH), lambda i: (i, 3)),
                  row(WIDTH), row(WIDTH),
                  pl.BlockSpec((tm, WIDTH), lambda i: (i, 2)),
                  vec, vec, vec, vec, const((WIDTH, WIDTH)), const((D_MODEL, D_MODEL))],
        out_specs=row(D_MODEL),
        out_shape=jax.ShapeDtypeStruct((m, D_MODEL), F32),
        compiler_params=_cparams("parallel"),
        name="odd_out",
    )(h2, ret_y, ret2, ssd_y, xs, xz2, gng, gnb, dsk, ng, ones_bd, w)


def _tri(n, reverse):
    i = np.arange(n)
    m = (i[None, :] >= i[:, None]) if reverse else (i[None, :] <= i[:, None])
    return m.astype(np.float32)


def _prepare(norm_mix, norm_mlp, mlp_w1, mlp_w2, even_in_w, even_out_w, attn_q_gain, attn_k_gain,
             rwkv_mu_prev, rwkv_mu_next, rwkv_w0, rwkv_w2, rwkv_a0, rwkv_a2, rwkv_g2, rwkv_k_k,
             rwkv_k_a, rwkv_r_k, rwkv_ln_g, rwkv_ln_b, odd_in_w, odd_out_w, ret_decay_exp, ret_gn_g,
             ret_gn_b, ssd_conv_w, ssd_conv_b, ssd_dt_bias, ssd_a_log, ssd_d, ssd_norm_g):
    row = lambda v: v.reshape(1, -1).astype(F32)
    heads = np.arange(WIDTH) // HEAD_DIM
    ones_bd = jnp.asarray(heads[:, None] == heads[None, :], BF16)
    p = {"ones_bd": ones_bd, "norm_mix": norm_mix, "norm_mlp": norm_mlp,
         "mlp_w1": mlp_w1.astype(BF16), "mlp_w2": mlp_w2.astype(BF16)}

    w_in = even_in_w[0]
    p["even_wa"] = w_in[:, :3 * WIDTH].astype(BF16)
    p["even_wr"] = w_in[:, 3 * WIDTH:].astype(BF16)
    p["even_out_w"] = even_out_w[0].astype(BF16)
    p["q_gain"] = row(jnp.tile(attn_q_gain[0], N_HEADS))
    p["k_gain"] = row(jnp.tile(attn_k_gain[0], N_HEADS))

    def two_dir_lowrank(w):
        z = jnp.zeros_like(w[0])
        return jnp.concatenate([jnp.concatenate([w[0], z], 1), jnp.concatenate([z, w[1]], 1)], 0)

    p["rwkv"] = {
        "mu_prev": row(rwkv_mu_prev[0]), "mu_next": row(rwkv_mu_next[0]),
        "w0": row(rwkv_w0[0]), "w2": two_dir_lowrank(rwkv_w2[0]).astype(BF16),
        "a0": row(rwkv_a0[0]), "a2": two_dir_lowrank(rwkv_a2[0]).astype(BF16),
        "g2": rwkv_g2[0].astype(BF16), "k_k": row(rwkv_k_k[0]), "k_a": row(rwkv_k_a[0]),
        "r_k": row(rwkv_r_k[0]), "ones_bd": ones_bd,
    }
    p["rwkv_tri"] = [jnp.asarray(_tri(RWKV_CHUNK, False), BF16), jnp.asarray(_tri(RWKV_CHUNK, True), BF16)]
    p["rwkv_ln_g"] = row(rwkv_ln_g[0])
    p["rwkv_ln_b"] = row(rwkv_ln_b[0])

    w_odd = odd_in_w[0]
    p["odd_wr"] = w_odd[:, :4 * WIDTH].astype(BF16)
    z_w = w_odd[:, 4 * WIDTH:5 * WIDTH]
    xbc_w = w_odd[:, 5 * WIDTH:5 * WIDTH + SSD_XBC]
    dt_w = w_odd[:, 5 * WIDTH + SSD_XBC:]
    p["odd_ws"] = jnp.concatenate([xbc_w, z_w], axis=1).astype(BF16)
    p["odd_wd"] = jnp.concatenate([dt_w, jnp.zeros((D_MODEL, LANES - 2 * N_HEADS), F32)], 1).astype(BF16)
    p["odd_out_w"] = odd_out_w[0].astype(BF16)

    log_gamma = jnp.log1p(-jnp.exp2(-ret_decay_exp[0].astype(F32)))
    p["lg64"] = jnp.repeat(log_gamma, HEAD_DIM, axis=1).reshape(2, 1, WIDTH)
    p["lg128"] = jnp.repeat(log_gamma, LIN_CHUNK, axis=1).reshape(2, 1, N_HEADS * LIN_CHUNK)
    p["ret_gn_g"] = row(ret_gn_g[0])
    p["ret_gn_b"] = row(ret_gn_b[0])

    p["conv_w"] = jnp.concatenate([ssd_conv_w[0], jnp.zeros((8 - SSD_CONV, SSD_XBC), F32)], 0)
    p["conv_b"] = row(ssd_conv_b[0])
    bias = ssd_dt_bias[0].astype(F32)
    nega = -jnp.exp(ssd_a_log[0].astype(F32))
    p["ssd"] = {
        "bias_c": bias[:, None, :], "nega_c": nega[:, None, :],
        "bias_r": bias[:, :, None], "nega_r": nega[:, :, None],
        "tri_fwd": jnp.asarray(_tri(LIN_CHUNK, False), BF16),
        "tri_rev": jnp.asarray(_tri(LIN_CHUNK, True), BF16),
        "trit_fwd": jnp.asarray(_tri(LIN_CHUNK, False).T, BF16),
        "trit_rev": jnp.asarray(_tri(LIN_CHUNK, True).T, BF16),
    }
    p["ssd_d"] = row(jnp.repeat(ssd_d[0], HEAD_DIM))
    p["ssd_norm_g"] = row(ssd_norm_g[0])
    return p


def _rope_tables(t):
    half = HEAD_DIM // 2
    inv = ROPE_BASE ** (-jnp.arange(half, dtype=F32) / half)
    ang = jnp.arange(t, dtype=F32)[:, None] * inv[None, :]
    cos = jnp.tile(jnp.cos(ang), (1, LANES // half))
    sin = jnp.sin(ang)
    sin_signed = jnp.tile(jnp.concatenate([-sin, sin], axis=1), (1, LANES // HEAD_DIM))
    return cos, sin_signed


def _even_mixer(h2, b, t, p):
    h = h2.reshape(b, t, D_MODEL)
    *qkvs, rcols = _even_in(h, p["norm_mix"][0:1], p["even_wa"], p["even_wr"], p["ones_bd"],
                            p["q_gain"], p["k_gain"])
    branch = []
    for qkv, (_, dil) in zip(qkvs, DILATED_BRANCHES):
        branch.extend(_dil_attn(qkv, dil))
    r, v, kkn, g, bonus, g_in, krep, bvec = _rwkv_prep(rcols, p["rwkv"], p["rwkv_tri"])
    y_fwd = _rwkv_scan(r, v, kkn, g_in, krep, bvec, 0)
    y_both = _rwkv_scan(r, v, kkn, g_in, krep, bvec, 1, other=(y_fwd,))
    out = _even_out(h, branch, [y_both, bonus, g], p["rwkv_ln_g"], p["rwkv_ln_b"],
                    p["ones_bd"], p["even_out_w"])
    return out.reshape(b * t, D_MODEL)


def _odd_mixer(h2, b, t, p):
    cos, sin = _rope_tables(t)
    ret2, xz2, dt2, dtt = _odd_in(h2, p["norm_mix"][1:2], p["odd_wr"], p["odd_ws"], p["odd_wd"], cos, sin)
    ret = ret2.reshape(b, t, -1)
    xz = xz2.reshape(b, t, -1)
    ret_fwd = _retention(ret, p["lg64"], p["lg128"], 0)
    ret_y = _retention(ret, p["lg64"], p["lg128"], 1, other=(ret_fwd,))
    xs, bc = _ssd_conv(xz, p["conv_w"], p["conv_b"])
    dtc = dt2.reshape(b, t, LANES)
    dtr = dtt.reshape(2, N_HEADS, b * t)
    ssd_fwd = _ssd_scan(xs, bc, dtc, dtr, p["ssd"], 0)
    ssd_y = _ssd_scan(xs, bc, dtc, dtr, p["ssd"], 1, other=(ssd_fwd,))
    flat = lambda u: u.reshape(b * t, -1)
    return _odd_out(h2, flat(ret_y), ret2, flat(ssd_y), flat(xs),
                    xz2, p["ret_gn_g"], p["ret_gn_b"], p["ssd_d"], p["ssd_norm_g"], p["ones_bd"],
                    p["odd_out_w"])


def _trunk(x, p):
    b, t, _ = x.shape
    h2 = x.reshape(b * t, D_MODEL)
    h2 = _even_mixer(h2, b, t, p)
    h2 = _mlp(h2, p["norm_mlp"][0:1], p["mlp_w1"][0], p["mlp_w2"][0])
    h2 = _odd_mixer(h2, b, t, p)
    h2 = _mlp(h2, p["norm_mlp"][1:2], p["mlp_w1"][1], p["mlp_w2"][1])
    return h2.reshape(b, t, D_MODEL)


def kernel(x_prompt, x_sample, norm_mix, norm_mlp, mlp_w1, mlp_w2, even_in_w, even_out_w, attn_q_gain, attn_k_gain, rwkv_mu_prev, rwkv_mu_next, rwkv_w0, rwkv_w2, rwkv_a0, rwkv_a2, rwkv_g2, rwkv_k_k, rwkv_k_a, rwkv_r_k, rwkv_ln_g, rwkv_ln_b, odd_in_w, odd_out_w, ret_decay_exp, ret_gn_g, ret_gn_b, ssd_conv_w, ssd_conv_b, ssd_dt_bias, ssd_a_log, ssd_d, ssd_norm_g):
    p = _prepare(norm_mix, norm_mlp, mlp_w1, mlp_w2, even_in_w, even_out_w, attn_q_gain, attn_k_gain,
                 rwkv_mu_prev, rwkv_mu_next, rwkv_w0, rwkv_w2, rwkv_a0, rwkv_a2, rwkv_g2, rwkv_k_k,
                 rwkv_k_a, rwkv_r_k, rwkv_ln_g, rwkv_ln_b, odd_in_w, odd_out_w, ret_decay_exp,
                 ret_gn_g, ret_gn_b, ssd_conv_w, ssd_conv_b, ssd_dt_bias, ssd_a_log, ssd_d, ssd_norm_g)
    return (_trunk(x_prompt, p), _trunk(x_sample, p))
```
